```python
import jax, jax.numpy as jnp
from jax import lax
import numpy as np

D_MODEL = 1024
BATCH = 8
SEQ = 2048
DEPTH = 1
DEC_BATCH = 128
DEC_SEQ = 1
PAST_LEN = 16384
PAGE_SIZE = 128

HEAD_DIM = 64
RWKV_WIDTH = D_MODEL // 2
RWKV_HEADS = RWKV_WIDTH // HEAD_DIM
DECAY_LORA = 64
AAA_LORA = 64
GATE_LORA = 128
RWKV_PROJ = 3 * RWKV_WIDTH + DECAY_LORA + AAA_LORA + GATE_LORA
LRU_WIDTH = D_MODEL // 2
LRU_BLOCKS = 8
LRU_BLOCK = LRU_WIDTH // LRU_BLOCKS
LRU_CONV = 4
LRU_C = 8.0
IN_COLS = RWKV_PROJ + LRU_WIDTH + 2 * D_MODEL
D_FF = 2816
FFN_CONV = 3
NORM_EPS = 1e-6
LN_X_EPS = 64e-5

kernel_name = 'rwkv7_rglru_gated_hybrid_step'


def rmsnorm(x, g):
    x32 = x.astype(jnp.float32)
    y = x32 * lax.rsqrt(jnp.mean(x32 * x32, axis=-1, keepdims=True) + NORM_EPS)
    return (y * g.astype(jnp.float32)).astype(x.dtype)


def causal_dwconv(u, buf, w, b):
    width = w.shape[0]
    t = u.shape[1]
    full = jnp.concatenate([buf.astype(u.dtype), u], axis=1)
    y = b + sum(full[:, j:j + t] * w[j] for j in range(width))
    return y, full[:, t:]


def wkv7_scan(r, decay, k, v, aa, bb, s0):
    def step(s, inp):
        r_t, w_t, k_t, v_t, a_t, b_t = inp
        sa = jnp.einsum('bhvk,bhk->bhv', s, a_t)
        s = s * w_t[:, :, None, :] + sa[..., None] * b_t[:, :, None, :] + v_t[..., None] * k_t[:, :, None, :]
        return s, jnp.einsum('bhvk,bhk->bhv', s, r_t)
    xs = tuple(jnp.swapaxes(t.astype(jnp.float32), 0, 1) for t in (r, decay, k, v, aa, bb))
    s, ys = lax.scan(step, s0.astype(jnp.float32), xs)
    return jnp.swapaxes(ys, 0, 1), s


def rglru_scan(a, u, h0):
    def step(h, inp):
        a_t, u_t = inp
        h = a_t * h + u_t
        return h, h
    xs = (jnp.swapaxes(a, 0, 1), jnp.swapaxes(u, 0, 1))
    h, hs = lax.scan(step, h0.astype(jnp.float32), xs)
    return jnp.swapaxes(hs, 0, 1), h


def rwkv7_branch(p, shift_buf, s0, mu, w0, w_up, a0, a_up, g_up, k_k, k_a, r_k, lnx_w, lnx_b, w_out):
    b_, t_, _ = p.shape
    prev = jnp.concatenate([shift_buf.astype(p.dtype), p], axis=1)[:, :t_]
    new_shift = p[:, t_ - 1:]
    m = p + mu * (prev - p)
    cut = np.cumsum([RWKV_WIDTH, RWKV_WIDTH, RWKV_WIDTH, DECAY_LORA, AAA_LORA])
    r, k, v, wd, ad, gd = jnp.split(m, cut, axis=-1)
    w = -jax.nn.softplus(-(w0 + jnp.tanh(wd) @ w_up)) - 0.5
    decay = jnp.exp(-jnp.exp(w.astype(jnp.float32)))
    a = jax.nn.sigmoid(a0 + ad @ a_up)
    g = jax.nn.sigmoid(gd) @ g_up
    heads = lambda z: z.reshape(b_, t_, RWKV_HEADS, HEAD_DIM)
    kk = heads(k * k_k).astype(jnp.float32)
    kk = kk / jnp.maximum(jnp.sqrt(jnp.sum(kk * kk, axis=-1, keepdims=True)), 1e-12)
    k = k * (1.0 + (a - 1.0) * k_a)
    rh, kh, vh = heads(r), heads(k), heads(v)
    y, s = wkv7_scan(rh, heads(decay), kh, vh, -kk, kk * heads(a).astype(jnp.float32), s0)
    mean = jnp.mean(y, axis=-1, keepdims=True)
    var = jnp.mean(jnp.square(y - mean), axis=-1, keepdims=True)
    yn = ((y - mean) * lax.rsqrt(var + LN_X_EPS)).reshape(b_, t_, RWKV_WIDTH) * lnx_w + lnx_b
    bonus = (jnp.sum(rh * kh * r_k, axis=-1, keepdims=True) * vh).reshape(b_, t_, RWKV_WIDTH)
    out = ((yn.astype(p.dtype) + bonus) * g) @ w_out
    return out, new_shift, s


def rglru_branch(xb, conv_buf, h0, conv_w, conv_b, gx_w, gx_b, ga_w, ga_b, lam, w_out):
    b_, t_, _ = xb.shape
    xc, new_conv = causal_dwconv(xb, conv_buf, conv_w, conv_b)
    blk = xc.reshape(b_, t_, LRU_BLOCKS, LRU_BLOCK)
    gate_x = jax.nn.sigmoid(jnp.einsum('btni,nij->btnj', blk, gx_w).reshape(b_, t_, LRU_WIDTH) + gx_b)
    gate_a = jax.nn.sigmoid(jnp.einsum('btni,nij->btnj', blk, ga_w).reshape(b_, t_, LRU_WIDTH) + ga_b)
    log_a = -LRU_C * gate_a.astype(jnp.float32) * jax.nn.softplus(-lam.astype(jnp.float32))
    a = jnp.exp(log_a)
    u = jnp.sqrt(-jnp.expm1(2.0 * log_a)) * (gate_x * xc).astype(jnp.float32)
    hs, h = rglru_scan(a, u, h0)
    return hs.astype(xb.dtype) @ w_out, new_conv, h


def conv_ffn(h, buf, up, conv_w, conv_b, down):
    u, new_buf = causal_dwconv(h @ up, buf, conv_w, conv_b)
    gate, val = jnp.split(u, [D_FF], axis=-1)
    return (jax.nn.gelu(gate, approximate=True) * val) @ down, new_buf


def trunk_layer(x, shift_buf, wkv0, lru_buf, h0, ffn_buf,
                norm_pre_mix, norm_post_mix, norm_pre_ffn, norm_post_ffn, w_in,
                rwkv_mu, rwkv_w0, rwkv_w_up, rwkv_a0, rwkv_a_up, rwkv_g_up, rwkv_k_k, rwkv_k_a,
                rwkv_r_k, rwkv_lnx_w, rwkv_lnx_b, rwkv_w_out,
                lru_conv_w, lru_conv_b, lru_gx_w, lru_gx_b, lru_ga_w, lru_ga_b, lru_lambda, lru_w_out,
                w_o, ffn_up, ffn_conv_w, ffn_conv_b, ffn_down):
    xn = rmsnorm(x, norm_pre_mix)
    proj = xn @ w_in
    p_rwkv, xb, gates = jnp.split(proj, [RWKV_PROJ, RWKV_PROJ + LRU_WIDTH], axis=-1)
    oa, new_shift, wkv = rwkv7_branch(p_rwkv, shift_buf, wkv0, rwkv_mu, rwkv_w0, rwkv_w_up, rwkv_a0,
                                      rwkv_a_up, rwkv_g_up, rwkv_k_k, rwkv_k_a, rwkv_r_k,
                                      rwkv_lnx_w, rwkv_lnx_b, rwkv_w_out)
    ob, new_lru_buf, h = rglru_branch(xb, lru_buf, h0, lru_conv_w, lru_conv_b, lru_gx_w, lru_gx_b,
                                      lru_ga_w, lru_ga_b, lru_lambda, lru_w_out)
    ga, gb = jnp.split(jax.nn.sigmoid(gates), 2, axis=-1)
    mix = (ga * oa + gb * ob) @ w_o
    x = x + rmsnorm(mix, norm_post_mix)
    f, new_ffn_buf = conv_ffn(rmsnorm(x, norm_pre_ffn), ffn_buf, ffn_up, ffn_conv_w, ffn_conv_b, ffn_down)
    x = x + rmsnorm(f, norm_post_ffn)
    return x, (new_shift, wkv, new_lru_buf, h, new_ffn_buf)


def setup_inputs(seed: int = 0) -> dict:
    key = jax.random.key(seed)
    ks = iter(jax.random.split(key, 48))
    f32 = jnp.float32
    nrm = lambda shape, scale: jax.random.normal(next(ks), shape, f32) * scale
    L = DEPTH
    W = RWKV_WIDTH
    lam_u = jax.random.uniform(next(ks), (L, LRU_WIDTH), f32, minval=0.9, maxval=0.999)
    return {
        'x_prompt': nrm((BATCH, SEQ, D_MODEL), 1.0),
        'x_sample': nrm((DEC_BATCH, DEC_SEQ, D_MODEL), 1.0),
        'state_rwkv_shift': nrm((L, DEC_BATCH, 1, RWKV_PROJ), 1.0),
        'state_rwkv_wkv': nrm((L, DEC_BATCH, RWKV_HEADS, HEAD_DIM, HEAD_DIM), 0.5),
        'state_lru_conv': nrm((L, DEC_BATCH, LRU_CONV - 1, LRU_WIDTH), 1.0),
        'state_lru_h': nrm((L, DEC_BATCH, LRU_WIDTH), 0.5),
        'state_ffn_conv': nrm((L, DEC_BATCH, FFN_CONV - 1, 2 * D_FF), 1.0),
        'norm_pre_mix': 1.0 + nrm((L, D_MODEL), 0.02),
        'norm_post_mix': 1.0 + nrm((L, D_MODEL), 0.02),
        'norm_pre_ffn': 1.0 + nrm((L, D_MODEL), 0.02),
        'norm_post_ffn': 1.0 + nrm((L, D_MODEL), 0.02),
        'w_in': nrm((L, D_MODEL, IN_COLS), D_MODEL ** -0.5),
        'rwkv_mu': jax.random.uniform(next(ks), (L, RWKV_PROJ), f32),
        'rwkv_w0': jax.random.uniform(next(ks), (L, W), f32, minval=-6.0, maxval=0.0),
        'rwkv_w_up': nrm((L, DECAY_LORA, W), 0.1),
        'rwkv_a0': nrm((L, W), 0.1),
        'rwkv_a_up': nrm((L, AAA_LORA, W), 0.5 * AAA_LORA ** -0.5),
        'rwkv_g_up': nrm((L, GATE_LORA, W), GATE_LORA ** -0.5),
        'rwkv_k_k': 0.85 + nrm((L, W), 0.02),
        'rwkv_k_a': 1.0 + nrm((L, W), 0.02),
        'rwkv_r_k': nrm((L, RWKV_HEADS, HEAD_DIM), 0.1),
        'rwkv_lnx_w': 1.0 + nrm((L, W), 0.02),
        'rwkv_lnx_b': nrm((L, W), 0.02),
        'rwkv_w_out': nrm((L, W, D_MODEL), W ** -0.5),
        'lru_conv_w': nrm((L, LRU_CONV, LRU_WIDTH), LRU_CONV ** -0.5),
        'lru_conv_b': nrm((L, LRU_WIDTH), 0.02),
        'lru_gx_w': nrm((L, LRU_BLOCKS, LRU_BLOCK, LRU_BLOCK), LRU_BLOCK ** -0.5),
        'lru_gx_b': nrm((L, LRU_WIDTH), 0.02),
        'lru_ga_w': nrm((L, LRU_BLOCKS, LRU_BLOCK, LRU_BLOCK), LRU_BLOCK ** -0.5),
        'lru_ga_b': nrm((L, LRU_WIDTH), 0.02),
        'lru_lambda': jnp.log(lam_u) - jnp.log1p(-lam_u),
        'lru_w_out': nrm((L, LRU_WIDTH, D_MODEL), LRU_WIDTH ** -0.5),
        'w_o': nrm((L, D_MODEL, D_MODEL), D_MODEL ** -0.5),
        'ffn_up': nrm((L, D_MODEL, 2 * D_FF), D_MODEL ** -0.5),
        'ffn_conv_w': nrm((L, FFN_CONV, 2 * D_FF), FFN_CONV ** -0.5),
        'ffn_conv_b': nrm((L, 2 * D_FF), 0.02),
        'ffn_down': nrm((L, D_FF, D_MODEL), D_FF ** -0.5),
    }


def reference(x_prompt, x_sample, state_rwkv_shift, state_rwkv_wkv, state_lru_conv, state_lru_h, state_ffn_conv,
              norm_pre_mix, norm_post_mix, norm_pre_ffn, norm_post_ffn, w_in,
              rwkv_mu, rwkv_w0, rwkv_w_up, rwkv_a0, rwkv_a_up, rwkv_g_up, rwkv_k_k, rwkv_k_a,
              rwkv_r_k, rwkv_lnx_w, rwkv_lnx_b, rwkv_w_out,
              lru_conv_w, lru_conv_b, lru_gx_w, lru_gx_b, lru_ga_w, lru_ga_b, lru_lambda, lru_w_out,
              w_o, ffn_up, ffn_conv_w, ffn_conv_b, ffn_down):
    params = (norm_pre_mix, norm_post_mix, norm_pre_ffn, norm_post_ffn, w_in,
              rwkv_mu, rwkv_w0, rwkv_w_up, rwkv_a0, rwkv_a_up, rwkv_g_up, rwkv_k_k, rwkv_k_a,
              rwkv_r_k, rwkv_lnx_w, rwkv_lnx_b, rwkv_w_out,
              lru_conv_w, lru_conv_b, lru_gx_w, lru_gx_b, lru_ga_w, lru_ga_b, lru_lambda, lru_w_out,
              w_o, ffn_up, ffn_conv_w, ffn_conv_b, ffn_down)
    bp = x_prompt.shape[0]
    dt = x_prompt.dtype
    zero_states = (jnp.zeros((bp, 1, RWKV_PROJ), dt),
                   jnp.zeros((bp, RWKV_HEADS, HEAD_DIM, HEAD_DIM), jnp.float32),
                   jnp.zeros((bp, LRU_CONV - 1, LRU_WIDTH), dt),
                   jnp.zeros((bp, LRU_WIDTH), jnp.float32),
                   jnp.zeros((bp, FFN_CONV - 1, 2 * D_FF), dt))
    yp, ys = x_prompt, x_sample
    new_p, new_s = [], []
    for l in range(DEPTH):
        lp = [p[l] for p in params]
        yp, st_p = trunk_layer(yp, *zero_states, *lp)
        ys, st_s = trunk_layer(ys, state_rwkv_shift[l], state_rwkv_wkv[l], state_lru_conv[l],
                               state_lru_h[l], state_ffn_conv[l], *lp)
        new_p.append(st_p)
        new_s.append(st_s)
    stk = lambda lst, i: jnp.stack([s[i] for s in lst], axis=0)
    return (yp, ys,
            stk(new_p, 0), stk(new_p, 1), stk(new_p, 2), stk(new_p, 3), stk(new_p, 4),
            stk(new_s, 0), stk(new_s, 1), stk(new_s, 2), stk(new_s, 3), stk(new_s, 4))
```

```python
import functools

import jax
import jax.numpy as jnp
from jax import lax
from jax.experimental import pallas as pl
from jax.experimental.pallas import tpu as pltpu

F32 = jnp.float32
BF16 = jnp.bfloat16

NORM_EPS = 1e-6
LN_X_EPS = 64e-5
LRU_C = 8.0
HEAD_DIM = 64
LANES = 128
SUBLANES = 8
CHUNK = 64
FF_COLS = 256
VMEM_LIMIT = 56 * 1024 * 1024

_NN = (((1,), (0,)), ((), ()))
_NT = (((1,), (1,)), ((), ()))
_TN = (((0,), (0,)), ((), ()))


def _rms(x, g):
    return x * lax.rsqrt(jnp.mean(x * x, axis=-1, keepdims=True) + NORM_EPS) * g


def _softplus(x):
    return jnp.maximum(x, 0.0) + jnp.log1p(jnp.exp(-jnp.abs(x)))


def _sigmoid(x):
    return 1.0 / (1.0 + jnp.exp(-x))


def _expm1(x):
    u = jnp.exp(x)
    um1 = u - 1.0
    edge = (um1 == 0.0) | (um1 == -1.0)
    r = um1 * x / jnp.where(edge, 1.0, jnp.log(u))
    return jnp.where(um1 == 0.0, x, jnp.where(um1 == -1.0, -1.0, r))


def _gelu_tanh(x):
    return 0.5 * x * (1.0 + jnp.tanh(0.7978845608028654 * (x + 0.044715 * (x * x * x))))


def _mm(a, b):
    return jnp.dot(a.astype(BF16), b.astype(BF16), preferred_element_type=F32)


def _split2(x):
    hi = x.astype(BF16)
    lo = (x - hi.astype(F32)).astype(BF16)
    return hi, lo


def _mm3(a, b, dims=_NN):
    ah, al = _split2(a)
    bh, bl = _split2(b)
    d = lambda x, y: lax.dot_general(x, y, dims, preferred_element_type=F32)
    return d(ah, bh) + (d(ah, bl) + d(al, bh))


def _segsum(x, seg):
    hi, lo = _split2(x)
    return (jnp.dot(hi, seg, preferred_element_type=F32) + jnp.dot(lo, seg, preferred_element_type=F32))


def _cumsum_rows(tri, x):
    h1 = x.astype(BF16)
    r1 = x - h1.astype(F32)
    h2 = r1.astype(BF16)
    h3 = (r1 - h2.astype(F32)).astype(BF16)
    d = lambda y: jnp.dot(tri, y, preferred_element_type=F32)
    return d(h1) + (d(h2) + d(h3))


def _shift_rows(x, tail, j):
    xs = pltpu.roll(x, j, axis=0)
    ts = pltpu.roll(tail, j, axis=0)
    row = lax.broadcasted_iota(jnp.int32, ts.shape, 0)
    head = jnp.where(row < j, ts, xs[0:SUBLANES])
    if x.shape[0] == SUBLANES:
        return head
    return jnp.concatenate([head, xs[SUBLANES:]], axis=0)


def _const(shape):
    n = len(shape)
    return pl.BlockSpec(shape, lambda *_: (0,) * n)


def _params(sem):
    return pltpu.CompilerParams(dimension_semantics=sem, vmem_limit_bytes=VMEM_LIMIT)


def _proj_kernel(x_ref, g_ref, w_ref, p_ref, xb_ref, gt_ref, *, splits):
    xn = _rms(x_ref[...], g_ref[...]).astype(BF16)
    c0, c1, c2 = splits
    p_ref[...] = jnp.dot(xn, w_ref[:, 0:c0], preferred_element_type=F32)
    xb_ref[...] = jnp.dot(xn, w_ref[:, c0:c0 + c1], preferred_element_type=F32)
    gt_ref[...] = jnp.dot(xn, w_ref[:, c0 + c1:c0 + c1 + c2], preferred_element_type=F32)


def _proj(x, g, w_in16, splits, tm):
    n, d = x.shape
    c0, c1, c2 = splits
    return pl.pallas_call(
        functools.partial(_proj_kernel, splits=splits),
        grid=(n // tm,),
        in_specs=[pl.BlockSpec((tm, d), lambda i: (i, 0)), _const((1, d)), _const(w_in16.shape)],
        out_specs=[pl.BlockSpec((tm, c0), lambda i: (i, 0)),
                   pl.BlockSpec((tm, c1), lambda i: (i, 0)),
                   pl.BlockSpec((tm, c2), lambda i: (i, 0))],
        out_shape=[jax.ShapeDtypeStruct((n, c0), F32), jax.ShapeDtypeStruct((n, c1), F32),
                   jax.ShapeDtypeStruct((n, c2), F32)],
        compiler_params=_params(("arbitrary",)),
    )(x, g, w_in16)


def _rwkv_pre(p, prev, mu, w0, lora2, a0, g_up, k_k, k_a, seg):
    w = w0.shape[-1]
    m = p + mu * (prev - p)
    r, k, v = m[:, 0:w], m[:, w:2 * w], m[:, 2 * w:3 * w]
    z = m[:, 3 * w:3 * w + LANES]
    lane = lax.broadcasted_iota(jnp.int32, z.shape, 1)
    lor = _mm(jnp.where(lane < HEAD_DIM, jnp.tanh(z), z), lora2)
    wlog = -_softplus(-(w0 + lor[:, 0:w])) - 0.5
    lw = -jnp.exp(wlog)
    a_sig = _sigmoid(a0 + lor[:, w:2 * w])
    g = _mm(_sigmoid(m[:, 3 * w + LANES:3 * w + 2 * LANES]), g_up)
    kk = k * k_k
    kkn = kk / jnp.maximum(jnp.sqrt(_segsum(kk * kk, seg)), 1e-12)
    k2 = k * (1.0 + (a_sig - 1.0) * k_a)
    return r, lw, k2, v, -kkn, kkn * a_sig, g


def _rwkv_post(y, r, k2, v, g, r_k, lnx_w, lnx_b, w_out, seg):
    inv = 1.0 / HEAD_DIM
    mean = _segsum(y, seg) * inv
    d = y - mean
    var = _segsum(d * d, seg) * inv
    yn = d * lax.rsqrt(var + LN_X_EPS) * lnx_w + lnx_b
    bonus = _segsum(r * k2 * r_k, seg) * v
    return _mm((yn + bonus) * g, w_out)


def _wkv_pair(rt, at, bt, kt, v, wl, s, m0, m1, strict, incl, eye2, bdmask):
    ln = rt.shape[0]
    m0b = m0 > 0.5
    swap = lambda x: pltpu.roll(x, HEAD_DIM, axis=1)
    bd = lambda x: jnp.concatenate([x * m0, x * m1], axis=0)
    x4 = jnp.concatenate([at * m0, at * m1, rt * m0, rt * m1], axis=0)
    zs = jnp.concatenate([bt, kt], axis=0)
    gm = _mm3(x4, zs, _NT)
    ga0 = jnp.where(strict, gm[0:ln], 0.0)
    ga1 = jnp.where(strict, gm[ln:2 * ln], 0.0)
    gr0 = jnp.where(incl, gm[2 * ln:3 * ln], 0.0)
    gr1 = jnp.where(incl, gm[3 * ln:4 * ln], 0.0)
    n_ab = jnp.where(m0b, ga0, swap(ga1))
    n_ak = jnp.where(m0b, swap(ga0), ga1)
    n_rb = jnp.where(m0b, gr0, swap(gr1))
    n_rk = jnp.where(m0b, swap(gr0), gr1)
    tinv = eye2 + n_ab
    npow = n_ab
    steps = (ln - 1).bit_length() - 1
    for _ in range(steps):
        npow = _mm3(npow, bd(npow))
        tinv = tinv + _mm3(tinv, bd(npow))
    ar = _mm3(jnp.concatenate([at, rt], axis=0), s, _NT)
    bdv = bd(v)
    u = _mm3(tinv, bd(ar[0:ln] + _mm3(n_ak, bdv)))
    y = ar[ln:2 * ln] + _mm3(n_rb, bd(u)) + _mm3(n_rk, bdv)
    ds = _mm3(jnp.concatenate([u, v], axis=0), zs, _TN)
    s_new = (s + ds * bdmask) * wl
    return y, s_new


def _rwkv_prompt_kernel(p_ref, mu_ref, w0_ref, lora_ref, a0_ref, gup_ref, kk_ref, ka_ref, rk_ref,
                        lnw_ref, lnb_ref, wout_ref, seg_ref, tri_ref,
                        oa_ref, tail_ref, wkv_ref, prev_scr, s_scr):
    t = pl.program_id(1)

    @pl.when(t == 0)
    def _init():
        prev_scr[...] = jnp.zeros_like(prev_scr)
        s_scr[...] = jnp.zeros_like(s_scr)

    p = p_ref[...]
    ln = p.shape[0]
    row = lax.broadcasted_iota(jnp.int32, p.shape, 0)
    prev = jnp.where(row == 0, prev_scr[SUBLANES - 1:SUBLANES, :], pltpu.roll(p, 1, axis=0))
    prev_scr[...] = p[ln - SUBLANES:ln, :]
    tail_ref[0] = p[ln - SUBLANES:ln, :]

    seg = seg_ref[...]
    r, lw, k2, v, aa, bb, g = _rwkv_pre(p, prev, mu_ref[...], w0_ref[...], lora_ref[...], a0_ref[...],
                                        gup_ref[...], kk_ref[...], ka_ref[...], seg)
    c = _cumsum_rows(tri_ref[...], lw)
    e_c = jnp.exp(c)
    e_n = jnp.exp(-c)
    rt = r * e_c
    at = aa * jnp.exp(c - lw)
    bt = bb * e_n
    kt = k2 * e_n
    wl = e_c[ln - 1:ln, :]

    ri = lax.broadcasted_iota(jnp.int32, (ln, LANES), 0)
    ci = lax.broadcasted_iota(jnp.int32, (ln, LANES), 1)
    cj = jnp.where(ci < HEAD_DIM, ci, ci - HEAD_DIM)
    m0 = (ci < HEAD_DIM).astype(F32)
    m1 = 1.0 - m0
    strict = cj < ri
    incl = cj <= ri
    eye2 = (cj == ri).astype(F32)
    r2 = lax.broadcasted_iota(jnp.int32, (LANES, LANES), 0)
    c2 = lax.broadcasted_iota(jnp.int32, (LANES, LANES), 1)
    bdmask = ((r2 < HEAD_DIM) == (c2 < HEAD_DIM)).astype(F32)

    ys = []
    for pr in range(s_scr.shape[0]):
        sl = slice(pr * LANES, (pr + 1) * LANES)
        y, s_new = _wkv_pair(rt[:, sl], at[:, sl], bt[:, sl], kt[:, sl], v[:, sl], wl[:, sl], s_scr[pr],
                             m0, m1, strict, incl, eye2, bdmask)
        s_scr[pr] = s_new
        ys.append(y)
    y = jnp.concatenate(ys, axis=1)
    oa_ref[...] = _rwkv_post(y, r, k2, v, g, rk_ref[...], lnw_ref[...], lnb_ref[...], wout_ref[...], seg)

    @pl.when(t == pl.num_programs(1) - 1)
    def _fin():
        for pr in range(s_scr.shape[0]):
            s = s_scr[pr]
            wkv_ref[0, 2 * pr] = s[0:HEAD_DIM, 0:HEAD_DIM]
            wkv_ref[0, 2 * pr + 1] = pltpu.roll(s[HEAD_DIM:LANES, :], HEAD_DIM, axis=1)[:, 0:HEAD_DIM]


def _rwkv_prompt(p, rw, b, t):
    n, cp = p.shape
    w = rw["w0"].shape[-1]
    heads = w // HEAD_DIM
    nt = t // CHUNK
    dm = rw["w_out"].shape[-1]
    names = ("mu", "w0", "lora2", "a0", "g_up", "k_k", "k_a", "r_k", "lnx_w", "lnx_b", "w_out", "seg", "tri")
    return pl.pallas_call(
        _rwkv_prompt_kernel,
        grid=(b, nt),
        in_specs=[pl.BlockSpec((CHUNK, cp), lambda i, j: (i * nt + j, 0))] + [_const(rw[k].shape) for k in names],
        out_specs=[pl.BlockSpec((CHUNK, dm), lambda i, j: (i * nt + j, 0)),
                   pl.BlockSpec((1, SUBLANES, cp), lambda i, j: (i, 0, 0)),
                   pl.BlockSpec((1, heads, HEAD_DIM, HEAD_DIM), lambda i, j: (i, 0, 0, 0))],
        out_shape=[jax.ShapeDtypeStruct((n, dm), F32),
                   jax.ShapeDtypeStruct((b, SUBLANES, cp), F32),
                   jax.ShapeDtypeStruct((b, heads, HEAD_DIM, HEAD_DIM), F32)],
        scratch_shapes=[pltpu.VMEM((SUBLANES, cp), F32), pltpu.VMEM((heads // 2, LANES, LANES), F32)],
        compiler_params=_params(("arbitrary", "arbitrary")),
    )(p, *[rw[k] for k in names])


def _rwkv_sample_pre_kernel(p_ref, prev_ref, mu_ref, w0_ref, lora_ref, a0_ref, gup_ref, kk_ref, ka_ref, seg_ref,
                            r_ref, w_ref, k_ref, v_ref, a_ref, b_ref, g_ref):
    r, lw, k2, v, aa, bb, g = _rwkv_pre(p_ref[...], prev_ref[...], mu_ref[...], w0_ref[...], lora_ref[...],
                                        a0_ref[...], gup_ref[...], kk_ref[...], ka_ref[...], seg_ref[...])
    r_ref[...] = r
    w_ref[...] = jnp.exp(lw)
    k_ref[...] = k2
    v_ref[...] = v
    a_ref[...] = aa
    b_ref[...] = bb
    g_ref[...] = g


def _rwkv_sample_step_kernel(s_ref, r_ref, w_ref, k_ref, v_ref, a_ref, b_ref, y_ref, so_ref):
    s = s_ref[...]
    sa = jnp.sum(s * a_ref[...], axis=-1, keepdims=True)
    s_new = s * w_ref[...] + sa * b_ref[...] + v_ref[...] * k_ref[...]
    so_ref[...] = s_new
    y_ref[...] = jnp.sum(s_new * r_ref[...], axis=-1, keepdims=True)


def _rwkv_sample_post_kernel(y_ref, r_ref, k_ref, v_ref, g_ref, rk_ref, lnw_ref, lnb_ref, wout_ref, seg_ref, oa_ref):
    oa_ref[...] = _rwkv_post(y_ref[...], r_ref[...], k_ref[...], v_ref[...], g_ref[...], rk_ref[...],
                             lnw_ref[...], lnb_ref[...], wout_ref[...], seg_ref[...])


def _rwkv_sample(p, shift, wkv, rw, bt):
    n, cp = p.shape
    w = rw["w0"].shape[-1]
    heads = w // HEAD_DIM
    dm = rw["w_out"].shape[-1]
    pre_names = ("mu", "w0", "lora2", "a0", "g_up", "k_k", "k_a", "seg")
    vec = jax.ShapeDtypeStruct((n, w), F32)
    r, dec, k2, v, aa, bb, g = pl.pallas_call(
        _rwkv_sample_pre_kernel,
        grid=(1,),
        in_specs=[_const((n, cp)), _const((n, cp))] + [_const(rw[k].shape) for k in pre_names],
        out_specs=[_const((n, w))] * 7,
        out_shape=[vec] * 7,
        compiler_params=_params(("arbitrary",)),
    )(p, shift, *[rw[k] for k in pre_names])

    rowv = lambda x: x.reshape(n, heads, 1, HEAD_DIM)
    row_spec = pl.BlockSpec((bt, heads, 1, HEAD_DIM), lambda i: (i, 0, 0, 0))
    col_spec = pl.BlockSpec((bt, heads, HEAD_DIM, 1), lambda i: (i, 0, 0, 0))
    st_spec = pl.BlockSpec((bt, heads, HEAD_DIM, HEAD_DIM), lambda i: (i, 0, 0, 0))
    y, wkv_new = pl.pallas_call(
        _rwkv_sample_step_kernel,
        grid=(n // bt,),
        in_specs=[st_spec, row_spec, row_spec, row_spec, col_spec, row_spec, row_spec],
        out_specs=[col_spec, st_spec],
        out_shape=[jax.ShapeDtypeStruct((n, heads, HEAD_DIM, 1), F32),
                   jax.ShapeDtypeStruct((n, heads, HEAD_DIM, HEAD_DIM), F32)],
        compiler_params=_params(("arbitrary",)),
    )(wkv, rowv(r), rowv(dec), rowv(k2), v.reshape(n, heads, HEAD_DIM, 1), rowv(aa), rowv(bb))

    post_names = ("r_k", "lnx_w", "lnx_b", "w_out", "seg")
    oa = pl.pallas_call(
        _rwkv_sample_post_kernel,
        grid=(1,),
        in_specs=[_const((n, w))] * 5 + [_const(rw[k].shape) for k in post_names],
        out_specs=_const((n, dm)),
        out_shape=jax.ShapeDtypeStruct((n, dm), F32),
        compiler_params=_params(("arbitrary",)),
    )(y.reshape(n, w), r, k2, v, g, *[rw[k] for k in post_names])
    return oa, wkv_new


def _lru_gates(xc, gw, gb, lam):
    c = xc.shape[-1]
    gates = _mm(xc, gw) + gb
    gx = _sigmoid(gates[:, 0:c])
    ga = _sigmoid(gates[:, c:2 * c])
    log_a = -LRU_C * ga * _softplus(-lam)
    a = jnp.exp(log_a)
    u = jnp.sqrt(-_expm1(2.0 * log_a)) * (gx * xc)
    return a, u


def _lru_pre_kernel(x_ref, cw_ref, cb_ref, gw_ref, gb_ref, lam_ref, a_ref, u_ref, tail_ref, tail_scr):
    @pl.when(pl.program_id(1) == 0)
    def _init():
        tail_scr[...] = jnp.zeros_like(tail_scr)

    x = x_ref[...]
    rows = x.shape[0]
    tail = tail_scr[...]
    width = cw_ref.shape[0]
    xc = cb_ref[...] + cw_ref[width - 1:width, :] * x
    for j in range(1, width):
        xc = xc + cw_ref[width - 1 - j:width - j, :] * _shift_rows(x, tail, j)
    tail_scr[...] = x[rows - SUBLANES:rows, :]
    tail_ref[0] = x[rows - SUBLANES:rows, :]
    a, u = _lru_gates(xc, gw_ref[...], gb_ref[...], lam_ref[...])
    a_ref[...] = a
    u_ref[...] = u


def _lru_pre(xb, lw, b, t, tc):
    n, c = xb.shape
    nt = t // tc
    names = ("conv_w", "conv_b", "gate_w", "gate_b", "lam")
    tb = jax.ShapeDtypeStruct((t, b * c), F32)
    tb_spec = pl.BlockSpec((tc, c), lambda i, j: (j, i))
    return pl.pallas_call(
        _lru_pre_kernel,
        grid=(b, nt),
        in_specs=[pl.BlockSpec((tc, c), lambda i, j: (i * nt + j, 0))] + [_const(lw[k].shape) for k in names],
        out_specs=[tb_spec, tb_spec, pl.BlockSpec((1, SUBLANES, c), lambda i, j: (i, 0, 0))],
        out_shape=[tb, tb, jax.ShapeDtypeStruct((b, SUBLANES, c), F32)],
        scratch_shapes=[pltpu.VMEM((SUBLANES, c), F32)],
        compiler_params=_params(("arbitrary", "arbitrary")),
    )(xb, *[lw[k] for k in names])


def _lru_scan_kernel(a_ref, u_ref, hs_ref, h_ref, h_scr):
    @pl.when(pl.program_id(0) == 0)
    def _init():
        h_scr[...] = jnp.zeros_like(h_scr)

    def body(i, h):
        h = a_ref[i] * h + u_ref[i]
        hs_ref[i] = h
        return h

    h = lax.fori_loop(0, a_ref.shape[0], body, h_scr[...], unroll=8)
    h_scr[...] = h
    h_ref[...] = h


def _lru_scan(a, u, b, tt):
    t = a.shape[0]
    c = a.shape[1] // b
    a3, u3 = a.reshape(t, b, c), u.reshape(t, b, c)
    spec = pl.BlockSpec((tt, b, c), lambda i: (i, 0, 0))
    hs, h = pl.pallas_call(
        _lru_scan_kernel,
        grid=(t // tt,),
        in_specs=[spec, spec],
        out_specs=[spec, _const((b, c))],
        out_shape=[jax.ShapeDtypeStruct((t, b, c), F32), jax.ShapeDtypeStruct((b, c), F32)],
        scratch_shapes=[pltpu.VMEM((b, c), F32)],
        compiler_params=_params(("arbitrary",)),
    )(a3, u3)
    return hs.reshape(t, b * c), h


def _lru_sample_kernel(x_ref, buf_ref, h0_ref, cw_ref, cb_ref, gw_ref, gb_ref, lam_ref, h_ref, nb_ref):
    x = x_ref[...]
    c = x.shape[-1]
    width = cw_ref.shape[0]
    xc = cb_ref[...] + cw_ref[width - 1:width, :] * x
    for j in range(width - 1):
        xc = xc + cw_ref[j:j + 1, :] * buf_ref[:, j * c:(j + 1) * c]
    a, u = _lru_gates(xc, gw_ref[...], gb_ref[...], lam_ref[...])
    h_ref[...] = a * h0_ref[...] + u
    nb_ref[:, 0:(width - 2) * c] = buf_ref[:, c:(width - 1) * c]
    nb_ref[:, (width - 2) * c:(width - 1) * c] = x


def _lru_sample(xb, buf, h0, lw):
    n, c = xb.shape
    names = ("conv_w", "conv_b", "gate_w", "gate_b", "lam")
    return pl.pallas_call(
        _lru_sample_kernel,
        grid=(1,),
        in_specs=[_const(xb.shape), _const(buf.shape), _const(h0.shape)] + [_const(lw[k].shape) for k in names],
        out_specs=[_const((n, c)), _const(buf.shape)],
        out_shape=[jax.ShapeDtypeStruct((n, c), F32), jax.ShapeDtypeStruct(buf.shape, F32)],
        compiler_params=_params(("arbitrary",)),
    )(xb, buf, h0, *[lw[k] for k in names])


def _mix_kernel(x_ref, oa_ref, hs_ref, gt_ref, lwo_ref, wo_ref, g_ref, o_ref):
    d = x_ref.shape[-1]
    ob = _mm(hs_ref[...], lwo_ref[...])
    ga = _sigmoid(gt_ref[:, 0:d])
    gb = _sigmoid(gt_ref[:, d:2 * d])
    mix = _mm(ga * oa_ref[...] + gb * ob, wo_ref[...])
    o_ref[...] = x_ref[...] + _rms(mix, g_ref[...])


def _mix(x, oa, hs, gates, lwo16, wo16, g, b, t, tm):
    n, d = x.shape
    c = hs.shape[1] // b
    nt = t // tm
    rows = lambda w: pl.BlockSpec((tm, w), lambda i, j: (i * nt + j, 0))
    return pl.pallas_call(
        _mix_kernel,
        grid=(b, nt),
        in_specs=[rows(d), rows(d), pl.BlockSpec((tm, c), lambda i, j: (j, i)), rows(2 * d),
                  _const(lwo16.shape), _const(wo16.shape), _const(g.shape)],
        out_specs=rows(d),
        out_shape=jax.ShapeDtypeStruct((n, d), F32),
        compiler_params=_params(("arbitrary", "arbitrary")),
    )(x, oa, hs, gates, lwo16, wo16, g)


def _ffn_body(x1, gpre, gpost, up_ref, cw_ref, cb_ref, down_ref, hist, keep):
    dff = down_ref.shape[0]
    hn = _rms(x1, gpre).astype(BF16)
    f = jnp.zeros(x1.shape, F32)
    for c0 in range(0, dff, FF_COLS):
        halves = []
        for off in (c0, dff + c0):
            cols = slice(off, off + FF_COLS)
            u = jnp.dot(hn, up_ref[:, cols], preferred_element_type=F32)
            u1, u2 = hist(u, cols)
            halves.append(cb_ref[:, cols] + cw_ref[2:3, cols] * u + cw_ref[1:2, cols] * u1 + cw_ref[0:1, cols] * u2)
            keep(u, cols)
        act = _gelu_tanh(halves[0]) * halves[1]
        f = f + jnp.dot(act.astype(BF16), down_ref[c0:c0 + FF_COLS, :], preferred_element_type=F32)
    return x1 + _rms(f, gpost)


def _ffn_prompt_kernel(x_ref, gpre_ref, gpost_ref, up_ref, cw_ref, cb_ref, down_ref, y_ref, tail_ref, tail_scr):
    @pl.when(pl.program_id(1) == 0)
    def _init():
        tail_scr[...] = jnp.zeros_like(tail_scr)

    rows = x_ref.shape[0]

    def hist(u, cols):
        tail = tail_scr[:, cols]
        return _shift_rows(u, tail, 1), _shift_rows(u, tail, 2)

    def keep(u, cols):
        tail_scr[:, cols] = u[rows - SUBLANES:rows, :]
        tail_ref[0, :, cols] = u[rows - SUBLANES:rows, :]

    y_ref[...] = _ffn_body(x_ref[...], gpre_ref[...], gpost_ref[...], up_ref, cw_ref, cb_ref, down_ref, hist, keep)


def _ffn_prompt(x1, fw, b, t, tm):
    n, d = x1.shape
    nt = t // tm
    dff2 = fw["up"].shape[1]
    names = ("g_pre", "g_post", "up", "conv_w", "conv_b", "down")
    rows = pl.BlockSpec((tm, d), lambda i, j: (i * nt + j, 0))
    return pl.pallas_call(
        _ffn_prompt_kernel,
        grid=(b, nt),
        in_specs=[rows] + [_const(fw[k].shape) for k in names],
        out_specs=[rows, pl.BlockSpec((1, SUBLANES, dff2), lambda i, j: (i, 0, 0))],
        out_shape=[jax.ShapeDtypeStruct((n, d), F32), jax.ShapeDtypeStruct((b, SUBLANES, dff2), F32)],
        scratch_shapes=[pltpu.VMEM((SUBLANES, dff2), F32)],
        compiler_params=_params(("arbitrary", "arbitrary")),
    )(x1, *[fw[k] for k in names])


def _ffn_sample_kernel(x_ref, buf_ref, gpre_ref, gpost_ref, up_ref, cw_ref, cb_ref, down_ref, y_ref, nb_ref):
    dff2 = up_ref.shape[1]

    def hist(u, cols):
        return buf_ref[:, dff2 + cols.start:dff2 + cols.stop], buf_ref[:, cols]

    def keep(u, cols):
        nb_ref[:, cols] = buf_ref[:, dff2 + cols.start:dff2 + cols.stop]
        nb_ref[:, dff2 + cols.start:dff2 + cols.stop] = u

    y_ref[...] = _ffn_body(x_ref[...], gpre_ref[...], gpost_ref[...], up_ref, cw_ref, cb_ref, down_ref, hist, keep)


def _ffn_sample(x1, buf, fw):
    names = ("g_pre", "g_post", "up", "conv_w", "conv_b", "down")
    return pl.pallas_call(
        _ffn_sample_kernel,
        grid=(1,),
        in_specs=[_const(x1.shape), _const(buf.shape)] + [_const(fw[k].shape) for k in names],
        out_specs=[_const(x1.shape), _const(buf.shape)],
        out_shape=[jax.ShapeDtypeStruct(x1.shape, F32), jax.ShapeDtypeStruct(buf.shape, F32)],
        compiler_params=_params(("arbitrary",)),
    )(x1, buf, *[fw[k] for k in names])


def _block_diag(blocks):
    n, bi, bj = blocks.shape
    eye = jnp.eye(n, dtype=blocks.dtype)
    return jnp.einsum("nij,nm->nimj", blocks, eye).reshape(n * bi, n * bj)


def _layer_weights(norm_pre_mix, norm_post_mix, norm_pre_ffn, norm_post_ffn, w_in,
                   rwkv_mu, rwkv_w0, rwkv_w_up, rwkv_a0, rwkv_a_up, rwkv_g_up, rwkv_k_k, rwkv_k_a,
                   rwkv_r_k, rwkv_lnx_w, rwkv_lnx_b, rwkv_w_out,
                   lru_conv_w, lru_conv_b, lru_gx_w, lru_gx_b, lru_ga_w, lru_ga_b, lru_lambda, lru_w_out,
                   w_o, ffn_up, ffn_conv_w, ffn_conv_b, ffn_down):
    row = lambda x: x.reshape(1, -1)
    w = rwkv_w0.shape[-1]
    head_of = jnp.arange(w) // HEAD_DIM
    t_idx = jnp.arange(CHUNK)
    lo_w, lo_a = rwkv_w_up.shape[0], rwkv_a_up.shape[0]
    lora2 = jnp.zeros((lo_w + lo_a, 2 * w), F32).at[:lo_w, :w].set(rwkv_w_up).at[lo_w:, w:].set(rwkv_a_up)
    rw = dict(mu=row(rwkv_mu), w0=row(rwkv_w0), lora2=lora2.astype(BF16), a0=row(rwkv_a0),
              g_up=rwkv_g_up.astype(BF16), k_k=row(rwkv_k_k), k_a=row(rwkv_k_a), r_k=row(rwkv_r_k),
              lnx_w=row(rwkv_lnx_w), lnx_b=row(rwkv_lnx_b), w_out=rwkv_w_out.astype(BF16),
              seg=(head_of[:, None] == head_of[None, :]).astype(BF16),
              tri=(t_idx[:, None] >= t_idx[None, :]).astype(BF16))
    lw = dict(conv_w=lru_conv_w, conv_b=row(lru_conv_b),
              gate_w=jnp.concatenate([_block_diag(lru_gx_w), _block_diag(lru_ga_w)], axis=1).astype(BF16),
              gate_b=jnp.concatenate([row(lru_gx_b), row(lru_ga_b)], axis=1), lam=row(lru_lambda))
    fw = dict(g_pre=row(norm_pre_ffn), g_post=row(norm_post_ffn), up=ffn_up.astype(BF16),
              conv_w=ffn_conv_w, conv_b=row(ffn_conv_b), down=ffn_down.astype(BF16))
    return dict(g_in=row(norm_pre_mix), w_in=w_in.astype(BF16), rw=rw, lw=lw, fw=fw,
                lru_w_out=lru_w_out.astype(BF16), w_o=w_o.astype(BF16), g_mix=row(norm_post_mix))


def _splits(wts):
    w = wts["rw"]["w0"].shape[-1]
    c_rwkv = wts["rw"]["mu"].shape[-1]
    c_lru = wts["lw"]["lam"].shape[-1]
    return (c_rwkv, c_lru, wts["w_in"].shape[1] - c_rwkv - c_lru)


def _prompt_layer(x, wts):
    b, t, d = x.shape
    x2 = x.reshape(b * t, d)
    tm = min(t, 256)
    p, xb, gates = _proj(x2, wts["g_in"], wts["w_in"], _splits(wts), tm)
    oa, p_tail, wkv = _rwkv_prompt(p, wts["rw"], b, t)
    a, u, xb_tail = _lru_pre(xb, wts["lw"], b, t, tm)
    hs, h = _lru_scan(a, u, b, tm)
    x1 = _mix(x2, oa, hs, gates, wts["lru_w_out"], wts["w_o"], wts["g_mix"], b, t, tm)
    y, u_tail = _ffn_prompt(x1, wts["fw"], b, t, tm)
    conv_w = wts["lw"]["conv_w"].shape[0]
    ffn_w = wts["fw"]["conv_w"].shape[0]
    state = (p_tail[:, SUBLANES - 1:], wkv, xb_tail[:, SUBLANES - (conv_w - 1):], h,
             u_tail[:, SUBLANES - (ffn_w - 1):])
    return y.reshape(b, t, d), state


def _sample_layer(x, shift, wkv, lru_buf, h0, ffn_buf, wts):
    n, t, d = x.shape
    x2 = x.reshape(n, d)
    p, xb, gates = _proj(x2, wts["g_in"], wts["w_in"], _splits(wts), n)
    oa, wkv_new = _rwkv_sample(p, shift.reshape(n, -1), wkv, wts["rw"], SUBLANES)
    h, lru_new = _lru_sample(xb, lru_buf.reshape(n, -1), h0, wts["lw"])
    x1 = _mix(x2, oa, h, gates, wts["lru_w_out"], wts["w_o"], wts["g_mix"], 1, n, n)
    y, ffn_new = _ffn_sample(x1, ffn_buf.reshape(n, -1), wts["fw"])
    state = (p.reshape(n, 1, -1), wkv_new, lru_new.reshape(lru_buf.shape), h, ffn_new.reshape(ffn_buf.shape))
    return y.reshape(n, t, d), state


def kernel(x_prompt, x_sample, state_rwkv_shift, state_rwkv_wkv, state_lru_conv, state_lru_h, state_ffn_conv,
           norm_pre_mix, norm_post_mix, norm_pre_ffn, norm_post_ffn, w_in,
           rwkv_mu, rwkv_w0, rwkv_w_up, rwkv_a0, rwkv_a_up, rwkv_g_up, rwkv_k_k, rwkv_k_a,
           rwkv_r_k, rwkv_lnx_w, rwkv_lnx_b, rwkv_w_out,
           lru_conv_w, lru_conv_b, lru_gx_w, lru_gx_b, lru_ga_w, lru_ga_b, lru_lambda, lru_w_out,
           w_o, ffn_up, ffn_conv_w, ffn_conv_b, ffn_down):
    params = (norm_pre_mix, norm_post_mix, norm_pre_ffn, norm_post_ffn, w_in,
              rwkv_mu, rwkv_w0, rwkv_w_up, rwkv_a0, rwkv_a_up, rwkv_g_up, rwkv_k_k, rwkv_k_a,
              rwkv_r_k, rwkv_lnx_w, rwkv_lnx_b, rwkv_w_out,
              lru_conv_w, lru_conv_b, lru_gx_w, lru_gx_b, lru_ga_w, lru_ga_b, lru_lambda, lru_w_out,
              w_o, ffn_up, ffn_conv_w, ffn_conv_b, ffn_down)
    depth = w_in.shape[0]
    assert depth == 1 and x_sample.shape[1] == 1
    yp, ys = x_prompt, x_sample
    new_p, new_s = [], []
    for l in range(depth):
        wts = _layer_weights(*[q[l] for q in params])
        yp, st_p = _prompt_layer(yp, wts)
        ys, st_s = _sample_layer(ys, state_rwkv_shift[l], state_rwkv_wkv[l], state_lru_conv[l],
                                 state_lru_h[l], state_ffn_conv[l], wts)
        new_p.append(st_p)
        new_s.append(st_s)
    stk = lambda lst, i: jnp.stack([s[i] for s in lst], axis=0)
    return (yp, ys,
            stk(new_p, 0), stk(new_p, 1), stk(new_p, 2), stk(new_p, 3), stk(new_p, 4),
            stk(new_s, 0), stk(new_s, 1), stk(new_s, 2), stk(new_s, 3), stk(new_s, 4))
```

```python
import functools

import jax
import jax.numpy as jnp
from jax import lax
from jax.experimental import pallas as pl
from jax.experimental.pallas import tpu as pltpu

F32 = jnp.float32
BF16 = jnp.bfloat16

NORM_EPS = 1e-6
LN_X_EPS = 64e-5
LRU_C = 8.0
HEAD_DIM = 64
LANES = 128
SUBLANES = 8
CHUNK = 64
WKV_ROWS = 256
FF_COLS = 256
VMEM_LIMIT = 56 * 1024 * 1024

_NN = (((1,), (0,)), ((), ()))
_NT = (((1,), (1,)), ((), ()))
_TN = (((0,), (0,)), ((), ()))


def _rms(x, g):
    return x * lax.rsqrt(jnp.mean(x * x, axis=-1, keepdims=True) + NORM_EPS) * g


def _softplus(x):
    return jnp.maximum(x, 0.0) + jnp.log1p(jnp.exp(-jnp.abs(x)))


def _sigmoid(x):
    return 1.0 / (1.0 + jnp.exp(-x))


def _expm1(x):
    u = jnp.exp(x)
    um1 = u - 1.0
    edge = (um1 == 0.0) | (um1 == -1.0)
    r = um1 * x / jnp.where(edge, 1.0, jnp.log(u))
    return jnp.where(um1 == 0.0, x, jnp.where(um1 == -1.0, -1.0, r))


def _gelu_tanh(x):
    return 0.5 * x * (1.0 + jnp.tanh(0.7978845608028654 * (x + 0.044715 * (x * x * x))))


def _mm(a, b):
    return jnp.dot(a.astype(BF16), b.astype(BF16), preferred_element_type=F32)


def _split2(x):
    hi = x.astype(BF16)
    lo = (x - hi.astype(F32)).astype(BF16)
    return hi, lo


def _mmd(a, b, dims=_NN):
    return lax.dot_general(a.astype(BF16), b.astype(BF16), dims, preferred_element_type=F32)


def _segsum(x, seg):
    hi, lo = _split2(x)
    return (jnp.dot(hi, seg, preferred_element_type=F32) + jnp.dot(lo, seg, preferred_element_type=F32))


def _cumsum_rows(tri, x):
    h1 = x.astype(BF16)
    r1 = x - h1.astype(F32)
    h2 = r1.astype(BF16)
    h3 = (r1 - h2.astype(F32)).astype(BF16)
    d = lambda y: jnp.dot(tri, y, preferred_element_type=F32)
    return d(h1) + (d(h2) + d(h3))


def _shift_rows(x, tail, j):
    xs = pltpu.roll(x, j, axis=0)
    ts = pltpu.roll(tail, j, axis=0)
    row = lax.broadcasted_iota(jnp.int32, ts.shape, 0)
    head = jnp.where(row < j, ts, xs[0:SUBLANES])
    if x.shape[0] == SUBLANES:
        return head
    return jnp.concatenate([head, xs[SUBLANES:]], axis=0)


def _const(shape):
    n = len(shape)
    return pl.BlockSpec(shape, lambda *_: (0,) * n)


def _params(sem):
    return pltpu.CompilerParams(dimension_semantics=sem, vmem_limit_bytes=VMEM_LIMIT)


def _proj_kernel(x_ref, g_ref, w_ref, p_ref, xb_ref, gt_ref, *, splits):
    xn = _rms(x_ref[...], g_ref[...]).astype(BF16)
    c0, c1, c2 = splits
    p_ref[...] = jnp.dot(xn, w_ref[:, 0:c0], preferred_element_type=F32)
    xb_ref[...] = jnp.dot(xn, w_ref[:, c0:c0 + c1], preferred_element_type=F32)
    gt_ref[...] = jnp.dot(xn, w_ref[:, c0 + c1:c0 + c1 + c2], preferred_element_type=F32)


def _proj(x, g, w_in16, splits, tm):
    n, d = x.shape
    c0, c1, c2 = splits
    return pl.pallas_call(
        functools.partial(_proj_kernel, splits=splits),
        grid=(n // tm,),
        in_specs=[pl.BlockSpec((tm, d), lambda i: (i, 0)), _const((1, d)), _const(w_in16.shape)],
        out_specs=[pl.BlockSpec((tm, c0), lambda i: (i, 0)),
                   pl.BlockSpec((tm, c1), lambda i: (i, 0)),
                   pl.BlockSpec((tm, c2), lambda i: (i, 0))],
        out_shape=[jax.ShapeDtypeStruct((n, c0), F32), jax.ShapeDtypeStruct((n, c1), F32),
                   jax.ShapeDtypeStruct((n, c2), F32)],
        compiler_params=_params(("arbitrary",)),
    )(x, g, w_in16)


def _rwkv_pre(p, prev, mu, w0, lora2, a0, g_up, k_k, k_a, seg):
    w = w0.shape[-1]
    m = p + mu * (prev - p)
    r, k, v = m[:, 0:w], m[:, w:2 * w], m[:, 2 * w:3 * w]
    z = m[:, 3 * w:3 * w + LANES]
    lane = lax.broadcasted_iota(jnp.int32, z.shape, 1)
    lor = _mm(jnp.where(lane < HEAD_DIM, jnp.tanh(z), z), lora2)
    wlog = -_softplus(-(w0 + lor[:, 0:w])) - 0.5
    lw = -jnp.exp(wlog)
    a_sig = _sigmoid(a0 + lor[:, w:2 * w])
    g = _mm(_sigmoid(m[:, 3 * w + LANES:3 * w + 2 * LANES]), g_up)
    kk = k * k_k
    kkn = kk / jnp.maximum(jnp.sqrt(_segsum(kk * kk, seg)), 1e-12)
    k2 = k * (1.0 + (a_sig - 1.0) * k_a)
    return r, lw, k2, v, -kkn, kkn * a_sig, g


def _rwkv_post(y, r, k2, v, g, r_k, lnx_w, lnx_b, w_out, seg):
    inv = 1.0 / HEAD_DIM
    mean = _segsum(y, seg) * inv
    d = y - mean
    var = _segsum(d * d, seg) * inv
    yn = d * lax.rsqrt(var + LN_X_EPS) * lnx_w + lnx_b
    bonus = _segsum(r * k2 * r_k, seg) * v
    return _mm((yn + bonus) * g, w_out)


def _wkv_chunk_summaries(xs, m0, m1, strict, incl, eye2, eyew, bdmask):
    ln = xs[0][0].shape[0]
    m0b = m0 > 0.5
    swap = lambda x: pltpu.roll(x, HEAD_DIM, axis=1)
    bd = lambda x: jnp.concatenate([x * m0, x * m1], axis=0)
    cat = jnp.concatenate
    rts, ats, bts, kts, bhs, khs, vs, wls = zip(*xs)
    gms = [_mmd(cat([at * m0, at * m1, rt * m0, rt * m1], axis=0), cat([bt, kt], axis=0), _NT)
           for rt, at, bt, kt in zip(rts, ats, bts, kts)]
    ga0 = [jnp.where(strict, g[0:ln], 0.0) for g in gms]
    ga1 = [jnp.where(strict, g[ln:2 * ln], 0.0) for g in gms]
    gr0 = [jnp.where(incl, g[2 * ln:3 * ln], 0.0) for g in gms]
    gr1 = [jnp.where(incl, g[3 * ln:4 * ln], 0.0) for g in gms]
    n_ab = [jnp.where(m0b, x, swap(y)) for x, y in zip(ga0, ga1)]
    n_ak = [jnp.where(m0b, swap(x), y) for x, y in zip(ga0, ga1)]
    n_rb = [jnp.where(m0b, x, swap(y)) for x, y in zip(gr0, gr1)]
    n_rk = [jnp.where(m0b, swap(x), y) for x, y in zip(gr0, gr1)]
    tinv = [eye2 + n for n in n_ab]
    npow = n_ab
    for _ in range((ln - 1).bit_length() - 1):
        npow = [_mmd(n, bd(n)) for n in npow]
        tinv = [t + _mmd(t, bd(n)) for t, n in zip(tinv, npow)]
    bdv = [bd(v) for v in vs]
    akv = [_mmd(n, b) for n, b in zip(n_ak, bdv)]
    tt = [_mmd(t, cat([bd(at), bd(ak)], axis=1)) for t, at, ak in zip(tinv, ats, akv)]
    ta = [x[:, 0:LANES] for x in tt]
    tk = [x[:, LANES:2 * LANES] for x in tt]
    ms = [eyew * wl + _mmd(a, bh, _TN) * bdmask for a, bh, wl in zip(ta, bhs, wls)]
    cs = [_mmd(cat([k, v], axis=0), cat([bh, kh], axis=0), _TN) * bdmask
          for k, v, bh, kh in zip(tk, vs, bhs, khs)]
    qy = [_mmd(n, cat([bd(a), bd(k)], axis=1)) for n, a, k in zip(n_rb, ta, tk)]
    qs = [rt + x[:, 0:LANES] for rt, x in zip(rts, qy)]
    y0 = [x[:, LANES:2 * LANES] + _mmd(n, b) for x, n, b in zip(qy, n_rk, bdv)]
    return list(zip(qs, y0, ms, cs))


def _rwkv_prompt_kernel(p_ref, mu_ref, w0_ref, lora_ref, a0_ref, gup_ref, kk_ref, ka_ref, rk_ref,
                        lnw_ref, lnb_ref, wout_ref, seg_ref, tri_ref,
                        oa_ref, tail_ref, wkv_ref, prev_scr, s_scr):
    t = pl.program_id(1)

    @pl.when(t == 0)
    def _init():
        prev_scr[...] = jnp.zeros_like(prev_scr)
        s_scr[...] = jnp.zeros_like(s_scr)

    p = p_ref[...]
    rows = p.shape[0]
    row = lax.broadcasted_iota(jnp.int32, p.shape, 0)
    prev = jnp.where(row == 0, prev_scr[SUBLANES - 1:SUBLANES, :], pltpu.roll(p, 1, axis=0))
    prev_scr[...] = p[rows - SUBLANES:rows, :]
    tail_ref[0] = p[rows - SUBLANES:rows, :]

    seg = seg_ref[...]
    r, lw, k2, v, aa, bb, g = _rwkv_pre(p, prev, mu_ref[...], w0_ref[...], lora_ref[...], a0_ref[...],
                                        gup_ref[...], kk_ref[...], ka_ref[...], seg)
    c = _cumsum_rows(tri_ref[...], lw)

    ln = CHUNK
    ri = lax.broadcasted_iota(jnp.int32, (ln, LANES), 0)
    ci = lax.broadcasted_iota(jnp.int32, (ln, LANES), 1)
    cj = jnp.where(ci < HEAD_DIM, ci, ci - HEAD_DIM)
    m0 = (ci < HEAD_DIM).astype(F32)
    m1 = 1.0 - m0
    strict = cj < ri
    incl = cj <= ri
    eye2 = (cj == ri).astype(F32)
    r2 = lax.broadcasted_iota(jnp.int32, (LANES, LANES), 0)
    c2 = lax.broadcasted_iota(jnp.int32, (LANES, LANES), 1)
    bdmask = ((r2 < HEAD_DIM) == (c2 < HEAD_DIM)).astype(F32)
    eyew = (r2 == c2).astype(F32)

    npairs = s_scr.shape[0]
    insts = []
    for ch in range(rows // ln):
        rs = slice(ch * ln, (ch + 1) * ln)
        c_, lw_ = c[rs], lw[rs]
        cl = c_[ln - 1:ln, :]
        e_n = jnp.exp(-c_)
        e_l = jnp.exp(cl - c_)
        rt = r[rs] * jnp.exp(c_)
        at = aa[rs] * jnp.exp(c_ - lw_)
        bt, kt = bb[rs] * e_n, k2[rs] * e_n
        bh, kh = bb[rs] * e_l, k2[rs] * e_l
        wl = jnp.exp(cl)
        v_ = v[rs]
        for pr in range(npairs):
            sl = slice(pr * LANES, (pr + 1) * LANES)
            insts.append((rt[:, sl], at[:, sl], bt[:, sl], kt[:, sl], bh[:, sl], kh[:, sl], v_[:, sl], wl[:, sl]))
    summaries = _wkv_chunk_summaries(insts, m0, m1, strict, incl, eye2, eyew, bdmask)

    states = [s_scr[pr] for pr in range(npairs)]
    y_rows = []
    for ch in range(rows // ln):
        ys = []
        for pr in range(npairs):
            q, y0, m, cc = summaries[ch * npairs + pr]
            ys.append(_mmd(q, states[pr], _NT) + y0)
            states[pr] = _mmd(states[pr], m) + cc
        y_rows.append(jnp.concatenate(ys, axis=1))
    for pr in range(npairs):
        s_scr[pr] = states[pr]
    y = jnp.concatenate(y_rows, axis=0)
    oa_ref[...] = _rwkv_post(y, r, k2, v, g, rk_ref[...], lnw_ref[...], lnb_ref[...], wout_ref[...], seg)

    @pl.when(t == pl.num_programs(1) - 1)
    def _fin():
        for pr in range(npairs):
            s = s_scr[pr]
            wkv_ref[0, 2 * pr] = s[0:HEAD_DIM, 0:HEAD_DIM]
            wkv_ref[0, 2 * pr + 1] = pltpu.roll(s[HEAD_DIM:LANES, :], HEAD_DIM, axis=1)[:, 0:HEAD_DIM]


def _rwkv_prompt(p, rw, b, t, tc):
    n, cp = p.shape
    w = rw["w0"].shape[-1]
    heads = w // HEAD_DIM
    nt = t // tc
    dm = rw["w_out"].shape[-1]
    tri = jnp.kron(jnp.eye(tc // CHUNK, dtype=BF16), rw["tri"])
    names = ("mu", "w0", "lora2", "a0", "g_up", "k_k", "k_a", "r_k", "lnx_w", "lnx_b", "w_out", "seg")
    return pl.pallas_call(
        _rwkv_prompt_kernel,
        grid=(b, nt),
        in_specs=[pl.BlockSpec((tc, cp), lambda i, j: (i * nt + j, 0))] + [_const(rw[k].shape) for k in names]
                 + [_const(tri.shape)],
        out_specs=[pl.BlockSpec((tc, dm), lambda i, j: (i * nt + j, 0)),
                   pl.BlockSpec((1, SUBLANES, cp), lambda i, j: (i, 0, 0)),
                   pl.BlockSpec((1, heads, HEAD_DIM, HEAD_DIM), lambda i, j: (i, 0, 0, 0))],
        out_shape=[jax.ShapeDtypeStruct((n, dm), F32),
                   jax.ShapeDtypeStruct((b, SUBLANES, cp), F32),
                   jax.ShapeDtypeStruct((b, heads, HEAD_DIM, HEAD_DIM), F32)],
        scratch_shapes=[pltpu.VMEM((SUBLANES, cp), F32), pltpu.VMEM((heads // 2, LANES, LANES), F32)],
        compiler_params=_params(("arbitrary", "arbitrary")),
    )(p, *[rw[k] for k in names], tri)


def _rwkv_sample_pre_kernel(p_ref, prev_ref, mu_ref, w0_ref, lora_ref, a0_ref, gup_ref, kk_ref, ka_ref, seg_ref,
                            r_ref, w_ref, k_ref, v_ref, a_ref, b_ref, g_ref):
    r, lw, k2, v, aa, bb, g = _rwkv_pre(p_ref[...], prev_ref[...], mu_ref[...], w0_ref[...], lora_ref[...],
                                        a0_ref[...], gup_ref[...], kk_ref[...], ka_ref[...], seg_ref[...])
    r_ref[...] = r
    w_ref[...] = jnp.exp(lw)
    k_ref[...] = k2
    v_ref[...] = v
    a_ref[...] = aa
    b_ref[...] = bb
    g_ref[...] = g


def _rwkv_sample_step_kernel(s_ref, r_ref, w_ref, k_ref, v_ref, a_ref, b_ref, y_ref, so_ref):
    s = s_ref[...]
    sa = jnp.sum(s * a_ref[...], axis=-1, keepdims=True)
    s_new = s * w_ref[...] + sa * b_ref[...] + v_ref[...] * k_ref[...]
    so_ref[...] = s_new
    y_ref[...] = jnp.sum(s_new * r_ref[...], axis=-1, keepdims=True)


def _rwkv_sample_post_kernel(y_ref, r_ref, k_ref, v_ref, g_ref, rk_ref, lnw_ref, lnb_ref, wout_ref, seg_ref, oa_ref):
    oa_ref[...] = _rwkv_post(y_ref[...], r_ref[...], k_ref[...], v_ref[...], g_ref[...], rk_ref[...],
                             lnw_ref[...], lnb_ref[...], wout_ref[...], seg_ref[...])


def _rwkv_sample(p, shift, wkv, rw, bt):
    n, cp = p.shape
    w = rw["w0"].shape[-1]
    heads = w // HEAD_DIM
    dm = rw["w_out"].shape[-1]
    pre_names = ("mu", "w0", "lora2", "a0", "g_up", "k_k", "k_a", "seg")
    vec = jax.ShapeDtypeStruct((n, w), F32)
    r, dec, k2, v, aa, bb, g = pl.pallas_call(
        _rwkv_sample_pre_kernel,
        grid=(1,),
        in_specs=[_const((n, cp)), _const((n, cp))] + [_const(rw[k].shape) for k in pre_names],
        out_specs=[_const((n, w))] * 7,
        out_shape=[vec] * 7,
        compiler_params=_params(("arbitrary",)),
    )(p, shift, *[rw[k] for k in pre_names])

    rowv = lambda x: x.reshape(n, heads, 1, HEAD_DIM)
    row_spec = pl.BlockSpec((bt, heads, 1, HEAD_DIM), lambda i: (i, 0, 0, 0))
    col_spec = pl.BlockSpec((bt, heads, HEAD_DIM, 1), lambda i: (i, 0, 0, 0))
    st_spec = pl.BlockSpec((bt, heads, HEAD_DIM, HEAD_DIM), lambda i: (i, 0, 0, 0))
    y, wkv_new = pl.pallas_call(
        _rwkv_sample_step_kernel,
        grid=(n // bt,),
        in_specs=[st_spec, row_spec, row_spec, row_spec, col_spec, row_spec, row_spec],
        out_specs=[col_spec, st_spec],
        out_shape=[jax.ShapeDtypeStruct((n, heads, HEAD_DIM, 1), F32),
                   jax.ShapeDtypeStruct((n, heads, HEAD_DIM, HEAD_DIM), F32)],
        compiler_params=_params(("arbitrary",)),
    )(wkv, rowv(r), rowv(dec), rowv(k2), v.reshape(n, heads, HEAD_DIM, 1), rowv(aa), rowv(bb))

    post_names = ("r_k", "lnx_w", "lnx_b", "w_out", "seg")
    oa = pl.pallas_call(
        _rwkv_sample_post_kernel,
        grid=(1,),
        in_specs=[_const((n, w))] * 5 + [_const(rw[k].shape) for k in post_names],
        out_specs=_const((n, dm)),
        out_shape=jax.ShapeDtypeStruct((n, dm), F32),
        compiler_params=_params(("arbitrary",)),
    )(y.reshape(n, w), r, k2, v, g, *[rw[k] for k in post_names])
    return oa, wkv_new


def _lru_gates(xc, gw, gb, lam):
    c = xc.shape[-1]
    gates = _mm(xc, gw) + gb
    gx = _sigmoid(gates[:, 0:c])
    ga = _sigmoid(gates[:, c:2 * c])
    log_a = -LRU_C * ga * _softplus(-lam)
    a = jnp.exp(log_a)
    u = jnp.sqrt(-_expm1(2.0 * log_a)) * (gx * xc)
    return a, u


def _lru_pre_kernel(x_ref, cw_ref, cb_ref, gw_ref, gb_ref, lam_ref, a_ref, u_ref, tail_ref, tail_scr):
    @pl.when(pl.program_id(1) == 0)
    def _init():
        tail_scr[...] = jnp.zeros_like(tail_scr)

    x = x_ref[...]
    rows = x.shape[0]
    tail = tail_scr[...]
    width = cw_ref.shape[0]
    xc = cb_ref[...] + cw_ref[width - 1:width, :] * x
    for j in range(1, width):
        xc = xc + cw_ref[width - 1 - j:width - j, :] * _shift_rows(x, tail, j)
    tail_scr[...] = x[rows - SUBLANES:rows, :]
    tail_ref[0] = x[rows - SUBLANES:rows, :]
    a, u = _lru_gates(xc, gw_ref[...], gb_ref[...], lam_ref[...])
    a_ref[...] = a
    u_ref[...] = u


def _lru_pre(xb, lw, b, t, tc):
    n, c = xb.shape
    nt = t // tc
    names = ("conv_w", "conv_b", "gate_w", "gate_b", "lam")
    tb = jax.ShapeDtypeStruct((t, b * c), F32)
    tb_spec = pl.BlockSpec((tc, c), lambda i, j: (j, i))
    return pl.pallas_call(
        _lru_pre_kernel,
        grid=(b, nt),
        in_specs=[pl.BlockSpec((tc, c), lambda i, j: (i * nt + j, 0))] + [_const(lw[k].shape) for k in names],
        out_specs=[tb_spec, tb_spec, pl.BlockSpec((1, SUBLANES, c), lambda i, j: (i, 0, 0))],
        out_shape=[tb, tb, jax.ShapeDtypeStruct((b, SUBLANES, c), F32)],
        scratch_shapes=[pltpu.VMEM((SUBLANES, c), F32)],
        compiler_params=_params(("arbitrary", "arbitrary")),
    )(xb, *[lw[k] for k in names])


def _lru_scan_kernel(a_ref, u_ref, hs_ref, h_ref, h_scr):
    @pl.when(pl.program_id(0) == 0)
    def _init():
        h_scr[...] = jnp.zeros_like(h_scr)

    def body(i, h):
        h = a_ref[i] * h + u_ref[i]
        hs_ref[i] = h
        return h

    h = lax.fori_loop(0, a_ref.shape[0], body, h_scr[...], unroll=8)
    h_scr[...] = h
    h_ref[...] = h


def _lru_scan(a, u, b, tt):
    t = a.shape[0]
    c = a.shape[1] // b
    a3, u3 = a.reshape(t, b, c), u.reshape(t, b, c)
    spec = pl.BlockSpec((tt, b, c), lambda i: (i, 0, 0))
    hs, h = pl.pallas_call(
        _lru_scan_kernel,
        grid=(t // tt,),
        in_specs=[spec, spec],
        out_specs=[spec, _const((b, c))],
        out_shape=[jax.ShapeDtypeStruct((t, b, c), F32), jax.ShapeDtypeStruct((b, c), F32)],
        scratch_shapes=[pltpu.VMEM((b, c), F32)],
        compiler_params=_params(("arbitrary",)),
    )(a3, u3)
    return hs.reshape(t, b * c), h


def _lru_sample_kernel(x_ref, buf_ref, h0_ref, cw_ref, cb_ref, gw_ref, gb_ref, lam_ref, h_ref, nb_ref):
    x = x_ref[...]
    c = x.shape[-1]
    width = cw_ref.shape[0]
    xc = cb_ref[...] + cw_ref[width - 1:width, :] * x
    for j in range(width - 1):
        xc = xc + cw_ref[j:j + 1, :] * buf_ref[:, j * c:(j + 1) * c]
    a, u = _lru_gates(xc, gw_ref[...], gb_ref[...], lam_ref[...])
    h_ref[...] = a * h0_ref[...] + u
    nb_ref[:, 0:(width - 2) * c] = buf_ref[:, c:(width - 1) * c]
    nb_ref[:, (width - 2) * c:(width - 1) * c] = x


def _lru_sample(xb, buf, h0, lw):
    n, c = xb.shape
    names = ("conv_w", "conv_b", "gate_w", "gate_b", "lam")
    return pl.pallas_call(
        _lru_sample_kernel,
        grid=(1,),
        in_specs=[_const(xb.shape), _const(buf.shape), _const(h0.shape)] + [_const(lw[k].shape) for k in names],
        out_specs=[_const((n, c)), _const(buf.shape)],
        out_shape=[jax.ShapeDtypeStruct((n, c), F32), jax.ShapeDtypeStruct(buf.shape, F32)],
        compiler_params=_params(("arbitrary",)),
    )(xb, buf, h0, *[lw[k] for k in names])


def _mix_kernel(x_ref, oa_ref, hs_ref, gt_ref, lwo_ref, wo_ref, g_ref, o_ref):
    d = x_ref.shape[-1]
    ob = _mm(hs_ref[...], lwo_ref[...])
    ga = _sigmoid(gt_ref[:, 0:d])
    gb = _sigmoid(gt_ref[:, d:2 * d])
    mix = _mm(ga * oa_ref[...] + gb * ob, wo_ref[...])
    o_ref[...] = x_ref[...] + _rms(mix, g_ref[...])


def _mix(x, oa, hs, gates, lwo16, wo16, g, b, t, tm):
    n, d = x.shape
    c = hs.shape[1] // b
    nt = t // tm
    rows = lambda w: pl.BlockSpec((tm, w), lambda i, j: (i * nt + j, 0))
    return pl.pallas_call(
        _mix_kernel,
        grid=(b, nt),
        in_specs=[rows(d), rows(d), pl.BlockSpec((tm, c), lambda i, j: (j, i)), rows(2 * d),
                  _const(lwo16.shape), _const(wo16.shape), _const(g.shape)],
        out_specs=rows(d),
        out_shape=jax.ShapeDtypeStruct((n, d), F32),
        compiler_params=_params(("arbitrary", "arbitrary")),
    )(x, oa, hs, gates, lwo16, wo16, g)


def _ffn_body(x1, gpre, gpost, up_ref, cw_ref, cb_ref, down_ref, hist, keep):
    dff = down_ref.shape[0]
    hn = _rms(x1, gpre).astype(BF16)
    f = jnp.zeros(x1.shape, F32)
    for c0 in range(0, dff, FF_COLS):
        halves = []
        for off in (c0, dff + c0):
            cols = slice(off, off + FF_COLS)
            u = jnp.dot(hn, up_ref[:, cols], preferred_element_type=F32)
            u1, u2 = hist(u, cols)
            halves.append(cb_ref[:, cols] + cw_ref[2:3, cols] * u + cw_ref[1:2, cols] * u1 + cw_ref[0:1, cols] * u2)
            keep(u, cols)
        act = _gelu_tanh(halves[0]) * halves[1]
        f = f + jnp.dot(act.astype(BF16), down_ref[c0:c0 + FF_COLS, :], preferred_element_type=F32)
    return x1 + _rms(f, gpost)


def _ffn_prompt_kernel(x_ref, gpre_ref, gpost_ref, up_ref, cw_ref, cb_ref, down_ref, y_ref, tail_ref, tail_scr):
    @pl.when(pl.program_id(1) == 0)
    def _init():
        tail_scr[...] = jnp.zeros_like(tail_scr)

    rows = x_ref.shape[0]

    def hist(u, cols):
        tail = tail_scr[:, cols]
        return _shift_rows(u, tail, 1), _shift_rows(u, tail, 2)

    def keep(u, cols):
        tail_scr[:, cols] = u[rows - SUBLANES:rows, :]
        tail_ref[0, :, cols] = u[rows - SUBLANES:rows, :]

    y_ref[...] = _ffn_body(x_ref[...], gpre_ref[...], gpost_ref[...], up_ref, cw_ref, cb_ref, down_ref, hist, keep)


def _ffn_prompt(x1, fw, b, t, tm):
    n, d = x1.shape
    nt = t // tm
    dff2 = fw["up"].shape[1]
    names = ("g_pre", "g_post", "up", "conv_w", "conv_b", "down")
    rows = pl.BlockSpec((tm, d), lambda i, j: (i * nt + j, 0))
    return pl.pallas_call(
        _ffn_prompt_kernel,
        grid=(b, nt),
        in_specs=[rows] + [_const(fw[k].shape) for k in names],
        out_specs=[rows, pl.BlockSpec((1, SUBLANES, dff2), lambda i, j: (i, 0, 0))],
        out_shape=[jax.ShapeDtypeStruct((n, d), F32), jax.ShapeDtypeStruct((b, SUBLANES, dff2), F32)],
        scratch_shapes=[pltpu.VMEM((SUBLANES, dff2), F32)],
        compiler_params=_params(("arbitrary", "arbitrary")),
    )(x1, *[fw[k] for k in names])


def _ffn_sample_kernel(x_ref, buf_ref, gpre_ref, gpost_ref, up_ref, cw_ref, cb_ref, down_ref, y_ref, nb_ref):
    dff2 = up_ref.shape[1]

    def hist(u, cols):
        return buf_ref[:, dff2 + cols.start:dff2 + cols.stop], buf_ref[:, cols]

    def keep(u, cols):
        nb_ref[:, cols] = buf_ref[:, dff2 + cols.start:dff2 + cols.stop]
        nb_ref[:, dff2 + cols.start:dff2 + cols.stop] = u

    y_ref[...] = _ffn_body(x_ref[...], gpre_ref[...], gpost_ref[...], up_ref, cw_ref, cb_ref, down_ref, hist, keep)


def _ffn_sample(x1, buf, fw):
    names = ("g_pre", "g_post", "up", "conv_w", "conv_b", "down")
    return pl.pallas_call(
        _ffn_sample_kernel,
        grid=(1,),
        in_specs=[_const(x1.shape), _const(buf.shape)] + [_const(fw[k].shape) for k in names],
        out_specs=[_const(x1.shape), _const(buf.shape)],
        out_shape=[jax.ShapeDtypeStruct(x1.shape, F32), jax.ShapeDtypeStruct(buf.shape, F32)],
        compiler_params=_params(("arbitrary",)),
    )(x1, buf, *[fw[k] for k in names])


def _block_diag(blocks):
    n, bi, bj = blocks.shape
    eye = jnp.eye(n, dtype=blocks.dtype)
    return jnp.einsum("nij,nm->nimj", blocks, eye).reshape(n * bi, n * bj)


def _layer_weights(norm_pre_mix, norm_post_mix, norm_pre_ffn, norm_post_ffn, w_in,
                   rwkv_mu, rwkv_w0, rwkv_w_up, rwkv_a0, rwkv_a_up, rwkv_g_up, rwkv_k_k, rwkv_k_a,
                   rwkv_r_k, rwkv_lnx_w, rwkv_lnx_b, rwkv_w_out,
                   lru_conv_w, lru_conv_b, lru_gx_w, lru_gx_b, lru_ga_w, lru_ga_b, lru_lambda, lru_w_out,
                   w_o, ffn_up, ffn_conv_w, ffn_conv_b, ffn_down):
    row = lambda x: x.reshape(1, -1)
    w = rwkv_w0.shape[-1]
    head_of = jnp.arange(w) // HEAD_DIM
    t_idx = jnp.arange(CHUNK)
    lo_w, lo_a = rwkv_w_up.shape[0], rwkv_a_up.shape[0]
    lora2 = jnp.zeros((lo_w + lo_a, 2 * w), F32).at[:lo_w, :w].set(rwkv_w_up).at[lo_w:, w:].set(rwkv_a_up)
    rw = dict(mu=row(rwkv_mu), w0=row(rwkv_w0), lora2=lora2.astype(BF16), a0=row(rwkv_a0),
              g_up=rwkv_g_up.astype(BF16), k_k=row(rwkv_k_k), k_a=row(rwkv_k_a), r_k=row(rwkv_r_k),
              lnx_w=row(rwkv_lnx_w), lnx_b=row(rwkv_lnx_b), w_out=rwkv_w_out.astype(BF16),
              seg=(head_of[:, None] == head_of[None, :]).astype(BF16),
              tri=(t_idx[:, None] >= t_idx[None, :]).astype(BF16))
    lw = dict(conv_w=lru_conv_w, conv_b=row(lru_conv_b),
              gate_w=jnp.concatenate([_block_diag(lru_gx_w), _block_diag(lru_ga_w)], axis=1).astype(BF16),
              gate_b=jnp.concatenate([row(lru_gx_b), row(lru_ga_b)], axis=1), lam=row(lru_lambda))
    fw = dict(g_pre=row(norm_pre_ffn), g_post=row(norm_post_ffn), up=ffn_up.astype(BF16),
              conv_w=ffn_conv_w, conv_b=row(ffn_conv_b), down=ffn_down.astype(BF16))
    return dict(g_in=row(norm_pre_mix), w_in=w_in.astype(BF16), rw=rw, lw=lw, fw=fw,
                lru_w_out=lru_w_out.astype(BF16), w_o=w_o.astype(BF16), g_mix=row(norm_post_mix))


def _splits(wts):
    w = wts["rw"]["w0"].shape[-1]
    c_rwkv = wts["rw"]["mu"].shape[-1]
    c_lru = wts["lw"]["lam"].shape[-1]
    return (c_rwkv, c_lru, wts["w_in"].shape[1] - c_rwkv - c_lru)


def _prompt_layer(x, wts):
    b, t, d = x.shape
    x2 = x.reshape(b * t, d)
    tm = min(t, 256)
    p, xb, gates = _proj(x2, wts["g_in"], wts["w_in"], _splits(wts), tm)
    oa, p_tail, wkv = _rwkv_prompt(p, wts["rw"], b, t, min(t, WKV_ROWS))
    a, u, xb_tail = _lru_pre(xb, wts["lw"], b, t, tm)
    hs, h = _lru_scan(a, u, b, tm)
    x1 = _mix(x2, oa, hs, gates, wts["lru_w_out"], wts["w_o"], wts["g_mix"], b, t, tm)
    y, u_tail = _ffn_prompt(x1, wts["fw"], b, t, tm)
    conv_w = wts["lw"]["conv_w"].shape[0]
    ffn_w = wts["fw"]["conv_w"].shape[0]
    state = (p_tail[:, SUBLANES - 1:], wkv, xb_tail[:, SUBLANES - (conv_w - 1):], h,
             u_tail[:, SUBLANES - (ffn_w - 1):])
    return y.reshape(b, t, d), state


def _sample_layer(x, shift, wkv, lru_buf, h0, ffn_buf, wts):
    n, t, d = x.shape
    x2 = x.reshape(n, d)
    p, xb, gates = _proj(x2, wts["g_in"], wts["w_in"], _splits(wts), n)
    oa, wkv_new = _rwkv_sample(p, shift.reshape(n, -1), wkv, wts["rw"], SUBLANES)
    h, lru_new = _lru_sample(xb, lru_buf.reshape(n, -1), h0, wts["lw"])
    x1 = _mix(x2, oa, h, gates, wts["lru_w_out"], wts["w_o"], wts["g_mix"], 1, n, n)
    y, ffn_new = _ffn_sample(x1, ffn_buf.reshape(n, -1), wts["fw"])
    state = (p.reshape(n, 1, -1), wkv_new, lru_new.reshape(lru_buf.shape), h, ffn_new.reshape(ffn_buf.shape))
    return y.reshape(n, t, d), state


def kernel(x_prompt, x_sample, state_rwkv_shift, state_rwkv_wkv, state_lru_conv, state_lru_h, state_ffn_conv,
           norm_pre_mix, norm_post_mix, norm_pre_ffn, norm_post_ffn, w_in,
           rwkv_mu, rwkv_w0, rwkv_w_up, rwkv_a0, rwkv_a_up, rwkv_g_up, rwkv_k_k, rwkv_k_a,
           rwkv_r_k, rwkv_lnx_w, rwkv_lnx_b, rwkv_w_out,
           lru_conv_w, lru_conv_b, lru_gx_w, lru_gx_b, lru_ga_w, lru_ga_b, lru_lambda, lru_w_out,
           w_o, ffn_up, ffn_conv_w, ffn_conv_b, ffn_down):
    params = (norm_pre_mix, norm_post_mix, norm_pre_ffn, norm_post_ffn, w_in,
              rwkv_mu, rwkv_w0, rwkv_w_up, rwkv_a0, rwkv_a_up, rwkv_g_up, rwkv_k_k, rwkv_k_a,
              rwkv_r_k, rwkv_lnx_w, rwkv_lnx_b, rwkv_w_out,
              lru_conv_w, lru_conv_b, lru_gx_w, lru_gx_b, lru_ga_w, lru_ga_b, lru_lambda, lru_w_out,
              w_o, ffn_up, ffn_conv_w, ffn_conv_b, ffn_down)
    depth = w_in.shape[0]
    assert depth == 1 and x_sample.shape[1] == 1
    yp, ys = x_prompt, x_sample
    new_p, new_s = [], []
    for l in range(depth):
        wts = _layer_weights(*[q[l] for q in params])
        yp, st_p = _prompt_layer(yp, wts)
        ys, st_s = _sample_layer(ys, state_rwkv_shift[l], state_rwkv_wkv[l], state_lru_conv[l],
                                 state_lru_h[l], state_ffn_conv[l], wts)
        new_p.append(st_p)
        new_s.append(st_s)
    stk = lambda lst, i: jnp.stack([s[i] for s in lst], axis=0)
    return (yp, ys,
            stk(new_p, 0), stk(new_p, 1), stk(new_p, 2), stk(new_p, 3), stk(new_p, 4),
            stk(new_s, 0), stk(new_s, 1), stk(new_s, 2), stk(new_s, 3), stk(new_s, 4))
```

```python
import functools

import jax
import jax.numpy as jnp
from jax import lax
from jax.experimental import pallas as pl
from jax.experimental.pallas import tpu as pltpu

F32 = jnp.float32
BF16 = jnp.bfloat16

NORM_EPS = 1e-6
LN_X_EPS = 64e-5
LRU_C = 8.0
HEAD_DIM = 64
LANES = 128
SUBLANES = 8
CHUNK = 64
WKV_ROWS = 256
FF_COLS = 256
FFN_ROWS = 256
FFN_AHEAD = 2
VMEM_LIMIT = 56 * 1024 * 1024

_NN = (((1,), (0,)), ((), ()))
_NT = (((1,), (1,)), ((), ()))
_TN = (((0,), (0,)), ((), ()))


def _rms(x, g):
    return x * lax.rsqrt(jnp.mean(x * x, axis=-1, keepdims=True) + NORM_EPS) * g


def _softplus(x):
    return jnp.maximum(x, 0.0) + jnp.log1p(jnp.exp(-jnp.abs(x)))


def _sigmoid(x):
    return 1.0 / (1.0 + jnp.exp(-x))


def _expm1(x):
    u = jnp.exp(x)
    um1 = u - 1.0
    edge = (um1 == 0.0) | (um1 == -1.0)
    r = um1 * x / jnp.where(edge, 1.0, jnp.log(u))
    return jnp.where(um1 == 0.0, x, jnp.where(um1 == -1.0, -1.0, r))


def _gelu_tanh(x):
    return 0.5 * x * (1.0 + jnp.tanh(0.7978845608028654 * (x + 0.044715 * (x * x * x))))


def _mm(a, b):
    return jnp.dot(a.astype(BF16), b.astype(BF16), preferred_element_type=F32)


def _split2(x):
    hi = x.astype(BF16)
    lo = (x - hi.astype(F32)).astype(BF16)
    return hi, lo


def _mmd(a, b, dims=_NN):
    return lax.dot_general(a.astype(BF16), b.astype(BF16), dims, preferred_element_type=F32)


def _segsum(x, seg):
    hi, lo = _split2(x)
    return (jnp.dot(hi, seg, preferred_element_type=F32) + jnp.dot(lo, seg, preferred_element_type=F32))


def _cumsum_rows(tri, x):
    h1 = x.astype(BF16)
    r1 = x - h1.astype(F32)
    h2 = r1.astype(BF16)
    h3 = (r1 - h2.astype(F32)).astype(BF16)
    d = lambda y: jnp.dot(tri, y, preferred_element_type=F32)
    return d(h1) + (d(h2) + d(h3))


def _shift_rows(x, tail, j):
    xs = pltpu.roll(x, j, axis=0)
    ts = pltpu.roll(tail, j, axis=0)
    row = lax.broadcasted_iota(jnp.int32, ts.shape, 0)
    head = jnp.where(row < j, ts, xs[0:SUBLANES])
    if x.shape[0] == SUBLANES:
        return head
    return jnp.concatenate([head, xs[SUBLANES:]], axis=0)


def _const(shape):
    n = len(shape)
    return pl.BlockSpec(shape, lambda *_: (0,) * n)


def _params(sem):
    return pltpu.CompilerParams(dimension_semantics=sem, vmem_limit_bytes=VMEM_LIMIT)


def _proj_kernel(x_ref, g_ref, w_ref, p_ref, xb_ref, gt_ref, *, splits):
    xn = _rms(x_ref[...], g_ref[...]).astype(BF16)
    c0, c1, c2 = splits
    p_ref[...] = jnp.dot(xn, w_ref[:, 0:c0], preferred_element_type=F32)
    xb_ref[...] = jnp.dot(xn, w_ref[:, c0:c0 + c1], preferred_element_type=F32)
    gt_ref[...] = jnp.dot(xn, w_ref[:, c0 + c1:c0 + c1 + c2], preferred_element_type=F32)


def _proj(x, g, w_in16, splits, tm):
    n, d = x.shape
    c0, c1, c2 = splits
    return pl.pallas_call(
        functools.partial(_proj_kernel, splits=splits),
        grid=(n // tm,),
        in_specs=[pl.BlockSpec((tm, d), lambda i: (i, 0)), _const((1, d)), _const(w_in16.shape)],
        out_specs=[pl.BlockSpec((tm, c0), lambda i: (i, 0)),
                   pl.BlockSpec((tm, c1), lambda i: (i, 0)),
                   pl.BlockSpec((tm, c2), lambda i: (i, 0))],
        out_shape=[jax.ShapeDtypeStruct((n, c0), F32), jax.ShapeDtypeStruct((n, c1), F32),
                   jax.ShapeDtypeStruct((n, c2), F32)],
        compiler_params=_params(("arbitrary",)),
    )(x, g, w_in16)


def _rwkv_pre(p, prev, mu, w0, lora2, a0, g_up, k_k, k_a, seg):
    w = w0.shape[-1]
    m = p + mu * (prev - p)
    r, k, v = m[:, 0:w], m[:, w:2 * w], m[:, 2 * w:3 * w]
    z = m[:, 3 * w:3 * w + LANES]
    lane = lax.broadcasted_iota(jnp.int32, z.shape, 1)
    lor = _mm(jnp.where(lane < HEAD_DIM, jnp.tanh(z), z), lora2)
    wlog = -_softplus(-(w0 + lor[:, 0:w])) - 0.5
    lw = -jnp.exp(wlog)
    a_sig = _sigmoid(a0 + lor[:, w:2 * w])
    g = _mm(_sigmoid(m[:, 3 * w + LANES:3 * w + 2 * LANES]), g_up)
    kk = k * k_k
    kkn = kk / jnp.maximum(jnp.sqrt(_segsum(kk * kk, seg)), 1e-12)
    k2 = k * (1.0 + (a_sig - 1.0) * k_a)
    return r, lw, k2, v, -kkn, kkn * a_sig, g


def _rwkv_post(y, r, k2, v, g, r_k, lnx_w, lnx_b, w_out, seg):
    inv = 1.0 / HEAD_DIM
    mean = _segsum(y, seg) * inv
    d = y - mean
    var = _segsum(d * d, seg) * inv
    yn = d * lax.rsqrt(var + LN_X_EPS) * lnx_w + lnx_b
    bonus = _segsum(r * k2 * r_k, seg) * v
    return _mm((yn + bonus) * g, w_out)


def _wkv_chunk_summaries(xs, m0, m1, strict, incl, eye2, eyew, bdmask):
    ln = xs[0][0].shape[0]
    m0b = m0 > 0.5
    swap = lambda x: pltpu.roll(x, HEAD_DIM, axis=1)
    bd = lambda x: jnp.concatenate([x * m0, x * m1], axis=0)
    cat = jnp.concatenate
    rts, ats, bts, kts, bhs, khs, vs, wls = zip(*xs)
    gms = [_mmd(cat([at * m0, at * m1, rt * m0, rt * m1], axis=0), cat([bt, kt], axis=0), _NT)
           for rt, at, bt, kt in zip(rts, ats, bts, kts)]
    ga0 = [jnp.where(strict, g[0:ln], 0.0) for g in gms]
    ga1 = [jnp.where(strict, g[ln:2 * ln], 0.0) for g in gms]
    gr0 = [jnp.where(incl, g[2 * ln:3 * ln], 0.0) for g in gms]
    gr1 = [jnp.where(incl, g[3 * ln:4 * ln], 0.0) for g in gms]
    n_ab = [jnp.where(m0b, x, swap(y)) for x, y in zip(ga0, ga1)]
    n_ak = [jnp.where(m0b, swap(x), y) for x, y in zip(ga0, ga1)]
    n_rb = [jnp.where(m0b, x, swap(y)) for x, y in zip(gr0, gr1)]
    n_rk = [jnp.where(m0b, swap(x), y) for x, y in zip(gr0, gr1)]
    tinv = [eye2 + n for n in n_ab]
    npow = n_ab
    for _ in range((ln - 1).bit_length() - 1):
        npow = [_mmd(n, bd(n)) for n in npow]
        tinv = [t + _mmd(t, bd(n)) for t, n in zip(tinv, npow)]
    bdv = [bd(v) for v in vs]
    akv = [_mmd(n, b) for n, b in zip(n_ak, bdv)]
    tt = [_mmd(t, cat([bd(at), bd(ak)], axis=1)) for t, at, ak in zip(tinv, ats, akv)]
    ta = [x[:, 0:LANES] for x in tt]
    tk = [x[:, LANES:2 * LANES] for x in tt]
    ms = [eyew * wl + _mmd(a, bh, _TN) * bdmask for a, bh, wl in zip(ta, bhs, wls)]
    cs = [_mmd(cat([k, v], axis=0), cat([bh, kh], axis=0), _TN) * bdmask
          for k, v, bh, kh in zip(tk, vs, bhs, khs)]
    qy = [_mmd(n, cat([bd(a), bd(k)], axis=1)) for n, a, k in zip(n_rb, ta, tk)]
    qs = [rt + x[:, 0:LANES] for rt, x in zip(rts, qy)]
    y0 = [x[:, LANES:2 * LANES] + _mmd(n, b) for x, n, b in zip(qy, n_rk, bdv)]
    return list(zip(qs, y0, ms, cs))


def _rwkv_prompt_kernel(p_ref, mu_ref, w0_ref, lora_ref, a0_ref, gup_ref, kk_ref, ka_ref, rk_ref,
                        lnw_ref, lnb_ref, wout_ref, seg_ref, tri_ref,
                        oa_ref, tail_ref, wkv_ref, prev_scr, s_scr):
    t = pl.program_id(1)

    @pl.when(t == 0)
    def _init():
        prev_scr[...] = jnp.zeros_like(prev_scr)
        s_scr[...] = jnp.zeros_like(s_scr)

    p = p_ref[...]
    rows = p.shape[0]
    row = lax.broadcasted_iota(jnp.int32, p.shape, 0)
    prev = jnp.where(row == 0, prev_scr[SUBLANES - 1:SUBLANES, :], pltpu.roll(p, 1, axis=0))
    prev_scr[...] = p[rows - SUBLANES:rows, :]
    tail_ref[0] = p[rows - SUBLANES:rows, :]

    seg = seg_ref[...]
    r, lw, k2, v, aa, bb, g = _rwkv_pre(p, prev, mu_ref[...], w0_ref[...], lora_ref[...], a0_ref[...],
                                        gup_ref[...], kk_ref[...], ka_ref[...], seg)
    c = _cumsum_rows(tri_ref[...], lw)

    ln = CHUNK
    ri = lax.broadcasted_iota(jnp.int32, (ln, LANES), 0)
    ci = lax.broadcasted_iota(jnp.int32, (ln, LANES), 1)
    cj = jnp.where(ci < HEAD_DIM, ci, ci - HEAD_DIM)
    m0 = (ci < HEAD_DIM).astype(F32)
    m1 = 1.0 - m0
    strict = cj < ri
    incl = cj <= ri
    eye2 = (cj == ri).astype(F32)
    r2 = lax.broadcasted_iota(jnp.int32, (LANES, LANES), 0)
    c2 = lax.broadcasted_iota(jnp.int32, (LANES, LANES), 1)
    bdmask = ((r2 < HEAD_DIM) == (c2 < HEAD_DIM)).astype(F32)
    eyew = (r2 == c2).astype(F32)

    npairs = s_scr.shape[0]
    insts = []
    for ch in range(rows // ln):
        rs = slice(ch * ln, (ch + 1) * ln)
        c_, lw_ = c[rs], lw[rs]
        cl = c_[ln - 1:ln, :]
        e_n = jnp.exp(-c_)
        e_l = jnp.exp(cl - c_)
        rt = r[rs] * jnp.exp(c_)
        at = aa[rs] * jnp.exp(c_ - lw_)
        bt, kt = bb[rs] * e_n, k2[rs] * e_n
        bh, kh = bb[rs] * e_l, k2[rs] * e_l
        wl = jnp.exp(cl)
        v_ = v[rs]
        for pr in range(npairs):
            sl = slice(pr * LANES, (pr + 1) * LANES)
            insts.append((rt[:, sl], at[:, sl], bt[:, sl], kt[:, sl], bh[:, sl], kh[:, sl], v_[:, sl], wl[:, sl]))
    summaries = _wkv_chunk_summaries(insts, m0, m1, strict, incl, eye2, eyew, bdmask)

    states = [s_scr[pr] for pr in range(npairs)]
    y_rows = []
    for ch in range(rows // ln):
        ys = []
        for pr in range(npairs):
            q, y0, m, cc = summaries[ch * npairs + pr]
            ys.append(_mmd(q, states[pr], _NT) + y0)
            states[pr] = _mmd(states[pr], m) + cc
        y_rows.append(jnp.concatenate(ys, axis=1))
    for pr in range(npairs):
        s_scr[pr] = states[pr]
    y = jnp.concatenate(y_rows, axis=0)
    oa_ref[...] = _rwkv_post(y, r, k2, v, g, rk_ref[...], lnw_ref[...], lnb_ref[...], wout_ref[...], seg)

    @pl.when(t == pl.num_programs(1) - 1)
    def _fin():
        for pr in range(npairs):
            s = s_scr[pr]
            wkv_ref[0, 2 * pr] = s[0:HEAD_DIM, 0:HEAD_DIM]
            wkv_ref[0, 2 * pr + 1] = pltpu.roll(s[HEAD_DIM:LANES, :], HEAD_DIM, axis=1)[:, 0:HEAD_DIM]


def _rwkv_prompt(p, rw, b, t, tc):
    n, cp = p.shape
    w = rw["w0"].shape[-1]
    heads = w // HEAD_DIM
    nt = t // tc
    dm = rw["w_out"].shape[-1]
    tri = jnp.kron(jnp.eye(tc // CHUNK, dtype=BF16), rw["tri"])
    names = ("mu", "w0", "lora2", "a0", "g_up", "k_k", "k_a", "r_k", "lnx_w", "lnx_b", "w_out", "seg")
    return pl.pallas_call(
        _rwkv_prompt_kernel,
        grid=(b, nt),
        in_specs=[pl.BlockSpec((tc, cp), lambda i, j: (i * nt + j, 0))] + [_const(rw[k].shape) for k in names]
                 + [_const(tri.shape)],
        out_specs=[pl.BlockSpec((tc, dm), lambda i, j: (i * nt + j, 0)),
                   pl.BlockSpec((1, SUBLANES, cp), lambda i, j: (i, 0, 0)),
                   pl.BlockSpec((1, heads, HEAD_DIM, HEAD_DIM), lambda i, j: (i, 0, 0, 0))],
        out_shape=[jax.ShapeDtypeStruct((n, dm), F32),
                   jax.ShapeDtypeStruct((b, SUBLANES, cp), F32),
                   jax.ShapeDtypeStruct((b, heads, HEAD_DIM, HEAD_DIM), F32)],
        scratch_shapes=[pltpu.VMEM((SUBLANES, cp), F32), pltpu.VMEM((heads // 2, LANES, LANES), F32)],
        compiler_params=_params(("arbitrary", "arbitrary")),
    )(p, *[rw[k] for k in names], tri)


def _rwkv_sample_pre_kernel(p_ref, prev_ref, mu_ref, w0_ref, lora_ref, a0_ref, gup_ref, kk_ref, ka_ref, seg_ref,
                            r_ref, w_ref, k_ref, v_ref, a_ref, b_ref, g_ref):
    r, lw, k2, v, aa, bb, g = _rwkv_pre(p_ref[...], prev_ref[...], mu_ref[...], w0_ref[...], lora_ref[...],
                                        a0_ref[...], gup_ref[...], kk_ref[...], ka_ref[...], seg_ref[...])
    r_ref[...] = r
    w_ref[...] = jnp.exp(lw)
    k_ref[...] = k2
    v_ref[...] = v
    a_ref[...] = aa
    b_ref[...] = bb
    g_ref[...] = g


def _rwkv_sample_step_kernel(s_ref, r_ref, w_ref, k_ref, v_ref, a_ref, b_ref, y_ref, so_ref):
    s = s_ref[...]
    sa = jnp.sum(s * a_ref[...], axis=-1, keepdims=True)
    s_new = s * w_ref[...] + sa * b_ref[...] + v_ref[...] * k_ref[...]
    so_ref[...] = s_new
    y_ref[...] = jnp.sum(s_new * r_ref[...], axis=-1, keepdims=True)


def _rwkv_sample_post_kernel(y_ref, r_ref, k_ref, v_ref, g_ref, rk_ref, lnw_ref, lnb_ref, wout_ref, seg_ref, oa_ref):
    oa_ref[...] = _rwkv_post(y_ref[...], r_ref[...], k_ref[...], v_ref[...], g_ref[...], rk_ref[...],
                             lnw_ref[...], lnb_ref[...], wout_ref[...], seg_ref[...])


def _rwkv_sample(p, shift, wkv, rw, bt):
    n, cp = p.shape
    w = rw["w0"].shape[-1]
    heads = w // HEAD_DIM
    dm = rw["w_out"].shape[-1]
    pre_names = ("mu", "w0", "lora2", "a0", "g_up", "k_k", "k_a", "seg")
    vec = jax.ShapeDtypeStruct((n, w), F32)
    r, dec, k2, v, aa, bb, g = pl.pallas_call(
        _rwkv_sample_pre_kernel,
        grid=(1,),
        in_specs=[_const((n, cp)), _const((n, cp))] + [_const(rw[k].shape) for k in pre_names],
        out_specs=[_const((n, w))] * 7,
        out_shape=[vec] * 7,
        compiler_params=_params(("arbitrary",)),
    )(p, shift, *[rw[k] for k in pre_names])

    rowv = lambda x: x.reshape(n, heads, 1, HEAD_DIM)
    row_spec = pl.BlockSpec((bt, heads, 1, HEAD_DIM), lambda i: (i, 0, 0, 0))
    col_spec = pl.BlockSpec((bt, heads, HEAD_DIM, 1), lambda i: (i, 0, 0, 0))
    st_spec = pl.BlockSpec((bt, heads, HEAD_DIM, HEAD_DIM), lambda i: (i, 0, 0, 0))
    y, wkv_new = pl.pallas_call(
        _rwkv_sample_step_kernel,
        grid=(n // bt,),
        in_specs=[st_spec, row_spec, row_spec, row_spec, col_spec, row_spec, row_spec],
        out_specs=[col_spec, st_spec],
        out_shape=[jax.ShapeDtypeStruct((n, heads, HEAD_DIM, 1), F32),
                   jax.ShapeDtypeStruct((n, heads, HEAD_DIM, HEAD_DIM), F32)],
        compiler_params=_params(("arbitrary",)),
    )(wkv, rowv(r), rowv(dec), rowv(k2), v.reshape(n, heads, HEAD_DIM, 1), rowv(aa), rowv(bb))

    post_names = ("r_k", "lnx_w", "lnx_b", "w_out", "seg")
    oa = pl.pallas_call(
        _rwkv_sample_post_kernel,
        grid=(1,),
        in_specs=[_const((n, w))] * 5 + [_const(rw[k].shape) for k in post_names],
        out_specs=_const((n, dm)),
        out_shape=jax.ShapeDtypeStruct((n, dm), F32),
        compiler_params=_params(("arbitrary",)),
    )(y.reshape(n, w), r, k2, v, g, *[rw[k] for k in post_names])
    return oa, wkv_new


def _lru_gates(xc, gw, gb, lam):
    c = xc.shape[-1]
    gates = _mm(xc, gw) + gb
    gx = _sigmoid(gates[:, 0:c])
    ga = _sigmoid(gates[:, c:2 * c])
    log_a = -LRU_C * ga * _softplus(-lam)
    a = jnp.exp(log_a)
    u = jnp.sqrt(-_expm1(2.0 * log_a)) * (gx * xc)
    return a, u


def _scan_rows(a, u, h0):
    rows = a.shape[0]
    row = lax.broadcasted_iota(jnp.int32, a.shape, 0)
    k = 1
    while k < rows:
        keep = row >= k
        u = u + a * jnp.where(keep, pltpu.roll(u, k, axis=0), 0.0)
        a = a * jnp.where(keep, pltpu.roll(a, k, axis=0), 1.0)
        k *= 2
    return u + a * h0


def _lru_prompt(x, tail, h0, cw_ref, cb_ref, gw, gb, lam):
    width = cw_ref.shape[0]
    xc = cb_ref[...] + cw_ref[width - 1:width, :] * x
    for j in range(1, width):
        xc = xc + cw_ref[width - 1 - j:width - j, :] * _shift_rows(x, tail, j)
    a, u = _lru_gates(xc, gw, gb, lam)
    return _scan_rows(a, u, h0)


def _lru_sample_kernel(x_ref, buf_ref, h0_ref, cw_ref, cb_ref, gw_ref, gb_ref, lam_ref, h_ref, nb_ref):
    x = x_ref[...]
    c = x.shape[-1]
    width = cw_ref.shape[0]
    xc = cb_ref[...] + cw_ref[width - 1:width, :] * x
    for j in range(width - 1):
        xc = xc + cw_ref[j:j + 1, :] * buf_ref[:, j * c:(j + 1) * c]
    a, u = _lru_gates(xc, gw_ref[...], gb_ref[...], lam_ref[...])
    h_ref[...] = a * h0_ref[...] + u
    nb_ref[:, 0:(width - 2) * c] = buf_ref[:, c:(width - 1) * c]
    nb_ref[:, (width - 2) * c:(width - 1) * c] = x


def _lru_sample(xb, buf, h0, lw):
    n, c = xb.shape
    names = ("conv_w", "conv_b", "gate_w", "gate_b", "lam")
    return pl.pallas_call(
        _lru_sample_kernel,
        grid=(1,),
        in_specs=[_const(xb.shape), _const(buf.shape), _const(h0.shape)] + [_const(lw[k].shape) for k in names],
        out_specs=[_const((n, c)), _const(buf.shape)],
        out_shape=[jax.ShapeDtypeStruct((n, c), F32), jax.ShapeDtypeStruct(buf.shape, F32)],
        compiler_params=_params(("arbitrary",)),
    )(xb, buf, h0, *[lw[k] for k in names])


def _mix_rows(x, oa, hs, gates, lwo, wo, g):
    d = x.shape[-1]
    ob = _mm(hs, lwo)
    mix = _mm(_sigmoid(gates[:, 0:d]) * oa + _sigmoid(gates[:, d:2 * d]) * ob, wo)
    return x + _rms(mix, g)


def _mix_sample_kernel(x_ref, oa_ref, hs_ref, gt_ref, lwo_ref, wo_ref, g_ref, o_ref):
    o_ref[...] = _mix_rows(x_ref[...], oa_ref[...], hs_ref[...], gt_ref[...], lwo_ref[...], wo_ref[...], g_ref[...])


def _mix_sample(x, oa, hs, gates, lwo16, wo16, g):
    args = (x, oa, hs, gates, lwo16, wo16, g)
    return pl.pallas_call(
        _mix_sample_kernel,
        grid=(1,),
        in_specs=[_const(v.shape) for v in args],
        out_specs=_const(x.shape),
        out_shape=jax.ShapeDtypeStruct(x.shape, F32),
        compiler_params=_params(("arbitrary",)),
    )(*args)


def _mix_prompt_kernel(x_ref, oa_ref, xb_ref, gt_ref, cw_ref, cb_ref, gw_ref, gb_ref, lam_ref, lwo_ref, wo_ref, g_ref,
                       o_ref, xtail_ref, htail_ref, xtail_scr, h_scr):
    @pl.when(pl.program_id(1) == 0)
    def _init():
        xtail_scr[...] = jnp.zeros_like(xtail_scr)
        h_scr[...] = jnp.zeros_like(h_scr)

    xb = xb_ref[...]
    rows = xb.shape[0]
    hs = _lru_prompt(xb, xtail_scr[...], h_scr[SUBLANES - 1:SUBLANES, :], cw_ref, cb_ref,
                     gw_ref[...], gb_ref[...], lam_ref[...])
    xtail_scr[...] = xb[rows - SUBLANES:rows, :]
    xtail_ref[0] = xb[rows - SUBLANES:rows, :]
    h_scr[...] = hs[rows - SUBLANES:rows, :]
    htail_ref[0] = hs[rows - SUBLANES:rows, :]
    o_ref[...] = _mix_rows(x_ref[...], oa_ref[...], hs, gt_ref[...], lwo_ref[...], wo_ref[...], g_ref[...])


def _mix_prompt(x, oa, xb, gates, lw, lwo16, wo16, g, b, t, tm):
    n, d = x.shape
    c = xb.shape[1]
    nt = t // tm
    names = ("conv_w", "conv_b", "gate_w", "gate_b", "lam")
    consts = [lw[k] for k in names] + [lwo16, wo16, g]
    rows = lambda w: pl.BlockSpec((tm, w), lambda i, j: (i * nt + j, 0))
    tail = pl.BlockSpec((1, SUBLANES, c), lambda i, j: (i, 0, 0))
    return pl.pallas_call(
        _mix_prompt_kernel,
        grid=(b, nt),
        in_specs=[rows(d), rows(d), rows(c), rows(2 * d)] + [_const(v.shape) for v in consts],
        out_specs=[rows(d), tail, tail],
        out_shape=[jax.ShapeDtypeStruct((n, d), F32), jax.ShapeDtypeStruct((b, SUBLANES, c), F32),
                   jax.ShapeDtypeStruct((b, SUBLANES, c), F32)],
        scratch_shapes=[pltpu.VMEM((SUBLANES, c), F32), pltpu.VMEM((SUBLANES, c), F32)],
        compiler_params=_params(("arbitrary", "arbitrary")),
    )(x, oa, xb, gates, *consts)


def _ffn_body(x1, gpre, gpost, up_ref, cw_ref, cb_ref, down_ref, hist, keep):
    dff = down_ref.shape[0]
    hn = _rms(x1, gpre).astype(BF16)
    starts = list(range(0, dff, FF_COLS))

    def up(c0):
        return [(slice(off, off + FF_COLS), jnp.dot(hn, up_ref[:, off:off + FF_COLS], preferred_element_type=F32))
                for off in (c0, dff + c0)]

    ups = [up(c0) for c0 in starts[:FFN_AHEAD]]
    f = jnp.zeros(x1.shape, F32)
    for i, c0 in enumerate(starts):
        if i + FFN_AHEAD < len(starts):
            ups.append(up(starts[i + FFN_AHEAD]))
        halves = []
        for cols, u in ups[i]:
            u1, u2 = hist(u, cols)
            halves.append(cb_ref[:, cols] + cw_ref[2:3, cols] * u + cw_ref[1:2, cols] * u1 + cw_ref[0:1, cols] * u2)
            keep(u, cols)
        act = _gelu_tanh(halves[0]) * halves[1]
        f = f + jnp.dot(act.astype(BF16), down_ref[c0:c0 + FF_COLS, :], preferred_element_type=F32)
    return x1 + _rms(f, gpost)


def _ffn_prompt_kernel(x_ref, gpre_ref, gpost_ref, up_ref, cw_ref, cb_ref, down_ref, y_ref, tail_ref, tail_scr):
    @pl.when(pl.program_id(1) == 0)
    def _init():
        tail_scr[...] = jnp.zeros_like(tail_scr)

    rows = x_ref.shape[0]

    def hist(u, cols):
        tail = tail_scr[:, cols]
        return _shift_rows(u, tail, 1), _shift_rows(u, tail, 2)

    def keep(u, cols):
        tail_scr[:, cols] = u[rows - SUBLANES:rows, :]
        tail_ref[0, :, cols] = u[rows - SUBLANES:rows, :]

    y_ref[...] = _ffn_body(x_ref[...], gpre_ref[...], gpost_ref[...], up_ref, cw_ref, cb_ref, down_ref, hist, keep)


def _ffn_prompt(x1, fw, b, t, tm):
    n, d = x1.shape
    nt = t // tm
    dff2 = fw["up"].shape[1]
    names = ("g_pre", "g_post", "up", "conv_w", "conv_b", "down")
    rows = pl.BlockSpec((tm, d), lambda i, j: (i * nt + j, 0))
    return pl.pallas_call(
        _ffn_prompt_kernel,
        grid=(b, nt),
        in_specs=[rows] + [_const(fw[k].shape) for k in names],
        out_specs=[rows, pl.BlockSpec((1, SUBLANES, dff2), lambda i, j: (i, 0, 0))],
        out_shape=[jax.ShapeDtypeStruct((n, d), F32), jax.ShapeDtypeStruct((b, SUBLANES, dff2), F32)],
        scratch_shapes=[pltpu.VMEM((SUBLANES, dff2), F32)],
        compiler_params=_params(("arbitrary", "arbitrary")),
    )(x1, *[fw[k] for k in names])


def _ffn_sample_kernel(x_ref, buf_ref, gpre_ref, gpost_ref, up_ref, cw_ref, cb_ref, down_ref, y_ref, nb_ref):
    dff2 = up_ref.shape[1]

    def hist(u, cols):
        return buf_ref[:, dff2 + cols.start:dff2 + cols.stop], buf_ref[:, cols]

    def keep(u, cols):
        nb_ref[:, cols] = buf_ref[:, dff2 + cols.start:dff2 + cols.stop]
        nb_ref[:, dff2 + cols.start:dff2 + cols.stop] = u

    y_ref[...] = _ffn_body(x_ref[...], gpre_ref[...], gpost_ref[...], up_ref, cw_ref, cb_ref, down_ref, hist, keep)


def _ffn_sample(x1, buf, fw):
    names = ("g_pre", "g_post", "up", "conv_w", "conv_b", "down")
    return pl.pallas_call(
        _ffn_sample_kernel,
        grid=(1,),
        in_specs=[_const(x1.shape), _const(buf.shape)] + [_const(fw[k].shape) for k in names],
        out_specs=[_const(x1.shape), _const(buf.shape)],
        out_shape=[jax.ShapeDtypeStruct(x1.shape, F32), jax.ShapeDtypeStruct(buf.shape, F32)],
        compiler_params=_params(("arbitrary",)),
    )(x1, buf, *[fw[k] for k in names])


def _block_diag(blocks):
    n, bi, bj = blocks.shape
    eye = jnp.eye(n, dtype=blocks.dtype)
    return jnp.einsum("nij,nm->nimj", blocks, eye).reshape(n * bi, n * bj)


def _layer_weights(norm_pre_mix, norm_post_mix, norm_pre_ffn, norm_post_ffn, w_in,
                   rwkv_mu, rwkv_w0, rwkv_w_up, rwkv_a0, rwkv_a_up, rwkv_g_up, rwkv_k_k, rwkv_k_a,
                   rwkv_r_k, rwkv_lnx_w, rwkv_lnx_b, rwkv_w_out,
                   lru_conv_w, lru_conv_b, lru_gx_w, lru_gx_b, lru_ga_w, lru_ga_b, lru_lambda, lru_w_out,
                   w_o, ffn_up, ffn_conv_w, ffn_conv_b, ffn_down):
    row = lambda x: x.reshape(1, -1)
    w = rwkv_w0.shape[-1]
    head_of = jnp.arange(w) // HEAD_DIM
    t_idx = jnp.arange(CHUNK)
    lo_w, lo_a = rwkv_w_up.shape[0], rwkv_a_up.shape[0]
    lora2 = jnp.zeros((lo_w + lo_a, 2 * w), F32).at[:lo_w, :w].set(rwkv_w_up).at[lo_w:, w:].set(rwkv_a_up)
    rw = dict(mu=row(rwkv_mu), w0=row(rwkv_w0), lora2=lora2.astype(BF16), a0=row(rwkv_a0),
              g_up=rwkv_g_up.astype(BF16), k_k=row(rwkv_k_k), k_a=row(rwkv_k_a), r_k=row(rwkv_r_k),
              lnx_w=row(rwkv_lnx_w), lnx_b=row(rwkv_lnx_b), w_out=rwkv_w_out.astype(BF16),
              seg=(head_of[:, None] == head_of[None, :]).astype(BF16),
              tri=(t_idx[:, None] >= t_idx[None, :]).astype(BF16))
    lw = dict(conv_w=lru_conv_w, conv_b=row(lru_conv_b),
              gate_w=jnp.concatenate([_block_diag(lru_gx_w), _block_diag(lru_ga_w)], axis=1).astype(BF16),
              gate_b=jnp.concatenate([row(lru_gx_b), row(lru_ga_b)], axis=1), lam=row(lru_lambda))
    fw = dict(g_pre=row(norm_pre_ffn), g_post=row(norm_post_ffn), up=ffn_up.astype(BF16),
              conv_w=ffn_conv_w, conv_b=row(ffn_conv_b), down=ffn_down.astype(BF16))
    return dict(g_in=row(norm_pre_mix), w_in=w_in.astype(BF16), rw=rw, lw=lw, fw=fw,
                lru_w_out=lru_w_out.astype(BF16), w_o=w_o.astype(BF16), g_mix=row(norm_post_mix))


def _splits(wts):
    w = wts["rw"]["w0"].shape[-1]
    c_rwkv = wts["rw"]["mu"].shape[-1]
    c_lru = wts["lw"]["lam"].shape[-1]
    return (c_rwkv, c_lru, wts["w_in"].shape[1] - c_rwkv - c_lru)


def _prompt_layer(x, wts):
    b, t, d = x.shape
    x2 = x.reshape(b * t, d)
    tm = min(t, 256)
    p, xb, gates = _proj(x2, wts["g_in"], wts["w_in"], _splits(wts), tm)
    oa, p_tail, wkv = _rwkv_prompt(p, wts["rw"], b, t, min(t, WKV_ROWS))
    x1, xb_tail, h_tail = _mix_prompt(x2, oa, xb, gates, wts["lw"], wts["lru_w_out"], wts["w_o"], wts["g_mix"],
                                      b, t, tm)
    y, u_tail = _ffn_prompt(x1, wts["fw"], b, t, min(t, FFN_ROWS))
    conv_w = wts["lw"]["conv_w"].shape[0]
    ffn_w = wts["fw"]["conv_w"].shape[0]
    state = (p_tail[:, SUBLANES - 1:], wkv, xb_tail[:, SUBLANES - (conv_w - 1):], h_tail[:, SUBLANES - 1],
             u_tail[:, SUBLANES - (ffn_w - 1):])
    return y.reshape(b, t, d), state


def _sample_layer(x, shift, wkv, lru_buf, h0, ffn_buf, wts):
    n, t, d = x.shape
    x2 = x.reshape(n, d)
    p, xb, gates = _proj(x2, wts["g_in"], wts["w_in"], _splits(wts), n)
    oa, wkv_new = _rwkv_sample(p, shift.reshape(n, -1), wkv, wts["rw"], SUBLANES)
    h, lru_new = _lru_sample(xb, lru_buf.reshape(n, -1), h0, wts["lw"])
    x1 = _mix_sample(x2, oa, h, gates, wts["lru_w_out"], wts["w_o"], wts["g_mix"])
    y, ffn_new = _ffn_sample(x1, ffn_buf.reshape(n, -1), wts["fw"])
    state = (p.reshape(n, 1, -1), wkv_new, lru_new.reshape(lru_buf.shape), h, ffn_new.reshape(ffn_buf.shape))
    return y.reshape(n, t, d), state


def kernel(x_prompt, x_sample, state_rwkv_shift, state_rwkv_wkv, state_lru_conv, state_lru_h, state_ffn_conv,
           norm_pre_mix, norm_post_mix, norm_pre_ffn, norm_post_ffn, w_in,
           rwkv_mu, rwkv_w0, rwkv_w_up, rwkv_a0, rwkv_a_up, rwkv_g_up, rwkv_k_k, rwkv_k_a,
           rwkv_r_k, rwkv_lnx_w, rwkv_lnx_b, rwkv_w_out,
           lru_conv_w, lru_conv_b, lru_gx_w, lru_gx_b, lru_ga_w, lru_ga_b, lru_lambda, lru_w_out,
           w_o, ffn_up, ffn_conv_w, ffn_conv_b, ffn_down):
    params = (norm_pre_mix, norm_post_mix, norm_pre_ffn, norm_post_ffn, w_in,
              rwkv_mu, rwkv_w0, rwkv_w_up, rwkv_a0, rwkv_a_up, rwkv_g_up, rwkv_k_k, rwkv_k_a,
              rwkv_r_k, rwkv_lnx_w, rwkv_lnx_b, rwkv_w_out,
              lru_conv_w, lru_conv_b, lru_gx_w, lru_gx_b, lru_ga_w, lru_ga_b, lru_lambda, lru_w_out,
              w_o, ffn_up, ffn_conv_w, ffn_conv_b, ffn_down)
    depth = w_in.shape[0]
    assert depth == 1 and x_sample.shape[1] == 1
    yp, ys = x_prompt, x_sample
    new_p, new_s = [], []
    for l in range(depth):
        wts = _layer_weights(*[q[l] for q in params])
        yp, st_p = _prompt_layer(yp, wts)
        ys, st_s = _sample_layer(ys, state_rwkv_shift[l], state_rwkv_wkv[l], state_lru_conv[l],
                                 state_lru_h[l], state_ffn_conv[l], wts)
        new_p.append(st_p)
        new_s.append(st_s)
    stk = lambda lst, i: jnp.stack([s[i] for s in lst], axis=0)
    return (yp, ys,
            stk(new_p, 0), stk(new_p, 1), stk(new_p, 2), stk(new_p, 3), stk(new_p, 4),
            stk(new_s, 0), stk(new_s, 1), stk(new_s, 2), stk(new_s, 3), stk(new_s, 4))
```

```python
import functools

import jax
import jax.numpy as jnp
from jax import lax
from jax.experimental import pallas as pl
from jax.experimental.pallas import tpu as pltpu

F32 = jnp.float32
BF16 = jnp.bfloat16

NORM_EPS = 1e-6
LN_X_EPS = 64e-5
LRU_C = 8.0
HEAD_DIM = 64
LANES = 128
SUBLANES = 8
CHUNK = 64
WKV_ROWS = 256
FF_COLS = 256
FFN_ROWS = 256
FFN_AHEAD = 2
VMEM_LIMIT = 56 * 1024 * 1024

_NN = (((1,), (0,)), ((), ()))
_NT = (((1,), (1,)), ((), ()))
_TN = (((0,), (0,)), ((), ()))


def _rms(x, g):
    return x * lax.rsqrt(jnp.mean(x * x, axis=-1, keepdims=True) + NORM_EPS) * g


def _softplus(x):
    return jnp.maximum(x, 0.0) + jnp.log1p(jnp.exp(-jnp.abs(x)))


def _sigmoid(x):
    return 1.0 / (1.0 + jnp.exp(-x))


def _expm1(x):
    u = jnp.exp(x)
    um1 = u - 1.0
    edge = (um1 == 0.0) | (um1 == -1.0)
    r = um1 * x / jnp.where(edge, 1.0, jnp.log(u))
    return jnp.where(um1 == 0.0, x, jnp.where(um1 == -1.0, -1.0, r))


def _gelu_tanh(x):
    return 0.5 * x * (1.0 + jnp.tanh(0.7978845608028654 * (x + 0.044715 * (x * x * x))))


def _mm(a, b):
    return jnp.dot(a.astype(BF16), b.astype(BF16), preferred_element_type=F32)


def _split2(x):
    hi = x.astype(BF16)
    lo = (x - hi.astype(F32)).astype(BF16)
    return hi, lo


def _mmd(a, b, dims=_NN):
    return lax.dot_general(a.astype(BF16), b.astype(BF16), dims, preferred_element_type=F32)


def _segsum(x, seg):
    hi, lo = _split2(x)
    return (jnp.dot(hi, seg, preferred_element_type=F32) + jnp.dot(lo, seg, preferred_element_type=F32))


def _cumsum_rows(tri, x):
    h1 = x.astype(BF16)
    r1 = x - h1.astype(F32)
    h2 = r1.astype(BF16)
    h3 = (r1 - h2.astype(F32)).astype(BF16)
    d = lambda y: jnp.dot(tri, y, preferred_element_type=F32)
    return d(h1) + (d(h2) + d(h3))


def _shift_rows(x, tail, j):
    xs = pltpu.roll(x, j, axis=0)
    ts = pltpu.roll(tail, j, axis=0)
    row = lax.broadcasted_iota(jnp.int32, ts.shape, 0)
    head = jnp.where(row < j, ts, xs[0:SUBLANES])
    if x.shape[0] == SUBLANES:
        return head
    return jnp.concatenate([head, xs[SUBLANES:]], axis=0)


def _const(shape):
    n = len(shape)
    return pl.BlockSpec(shape, lambda *_: (0,) * n)


def _params(sem):
    return pltpu.CompilerParams(dimension_semantics=sem, vmem_limit_bytes=VMEM_LIMIT)


def _proj_kernel(x_ref, g_ref, w_ref, p_ref, xb_ref, gt_ref, *, splits):
    xn = _rms(x_ref[...], g_ref[...]).astype(BF16)
    c0, c1, c2 = splits
    p_ref[...] = jnp.dot(xn, w_ref[:, 0:c0], preferred_element_type=F32)
    xb_ref[...] = jnp.dot(xn, w_ref[:, c0:c0 + c1], preferred_element_type=F32)
    gt_ref[...] = jnp.dot(xn, w_ref[:, c0 + c1:c0 + c1 + c2], preferred_element_type=F32)


def _proj(x, g, w_in16, splits, tm):
    n, d = x.shape
    c0, c1, c2 = splits
    return pl.pallas_call(
        functools.partial(_proj_kernel, splits=splits),
        grid=(n // tm,),
        in_specs=[pl.BlockSpec((tm, d), lambda i: (i, 0)), _const((1, d)), _const(w_in16.shape)],
        out_specs=[pl.BlockSpec((tm, c0), lambda i: (i, 0)),
                   pl.BlockSpec((tm, c1), lambda i: (i, 0)),
                   pl.BlockSpec((tm, c2), lambda i: (i, 0))],
        out_shape=[jax.ShapeDtypeStruct((n, c0), F32), jax.ShapeDtypeStruct((n, c1), F32),
                   jax.ShapeDtypeStruct((n, c2), F32)],
        compiler_params=_params(("arbitrary",)),
    )(x, g, w_in16)


def _rwkv_pre(p, prev, mu, w0, lora2, a0, g_up, k_k, k_a, seg):
    w = w0.shape[-1]
    m = p + mu * (prev - p)
    r, k, v = m[:, 0:w], m[:, w:2 * w], m[:, 2 * w:3 * w]
    z = m[:, 3 * w:3 * w + LANES]
    lane = lax.broadcasted_iota(jnp.int32, z.shape, 1)
    lor = _mm(jnp.where(lane < HEAD_DIM, jnp.tanh(z), z), lora2)
    wlog = -_softplus(-(w0 + lor[:, 0:w])) - 0.5
    lw = -jnp.exp(wlog)
    a_sig = _sigmoid(a0 + lor[:, w:2 * w])
    g = _mm(_sigmoid(m[:, 3 * w + LANES:3 * w + 2 * LANES]), g_up)
    kk = k * k_k
    kkn = kk / jnp.maximum(jnp.sqrt(_segsum(kk * kk, seg)), 1e-12)
    k2 = k * (1.0 + (a_sig - 1.0) * k_a)
    return r, lw, k2, v, -kkn, kkn * a_sig, g


def _rwkv_post(y, r, k2, v, g, r_k, lnx_w, lnx_b, w_out, seg):
    inv = 1.0 / HEAD_DIM
    mean = _segsum(y, seg) * inv
    d = y - mean
    var = _segsum(d * d, seg) * inv
    yn = d * lax.rsqrt(var + LN_X_EPS) * lnx_w + lnx_b
    bonus = _segsum(r * k2 * r_k, seg) * v
    return _mm((yn + bonus) * g, w_out)


def _wkv_chunk_summaries(xs, m0, m1, strict, incl, eye2, eyew, bdmask):
    ln = xs[0][0].shape[0]
    m0b = m0 > 0.5
    swap = lambda x: pltpu.roll(x, HEAD_DIM, axis=1)
    bd = lambda x: jnp.concatenate([x * m0, x * m1], axis=0)
    cat = jnp.concatenate
    rts, ats, bts, kts, bhs, khs, vs, wls = zip(*xs)
    gms = [_mmd(cat([at * m0, at * m1, rt * m0, rt * m1], axis=0), cat([bt, kt], axis=0), _NT)
           for rt, at, bt, kt in zip(rts, ats, bts, kts)]
    ga0 = [jnp.where(strict, g[0:ln], 0.0) for g in gms]
    ga1 = [jnp.where(strict, g[ln:2 * ln], 0.0) for g in gms]
    gr0 = [jnp.where(incl, g[2 * ln:3 * ln], 0.0) for g in gms]
    gr1 = [jnp.where(incl, g[3 * ln:4 * ln], 0.0) for g in gms]
    n_ab = [jnp.where(m0b, x, swap(y)) for x, y in zip(ga0, ga1)]
    n_ak = [jnp.where(m0b, swap(x), y) for x, y in zip(ga0, ga1)]
    n_rb = [jnp.where(m0b, x, swap(y)) for x, y in zip(gr0, gr1)]
    n_rk = [jnp.where(m0b, swap(x), y) for x, y in zip(gr0, gr1)]
    tinv = [eye2 + n for n in n_ab]
    npow = n_ab
    for _ in range((ln - 1).bit_length() - 1):
        npow = [_mmd(n, bd(n)) for n in npow]
        tinv = [t + _mmd(t, bd(n)) for t, n in zip(tinv, npow)]
    bdv = [bd(v) for v in vs]
    akv = [_mmd(n, b) for n, b in zip(n_ak, bdv)]
    tt = [_mmd(t, cat([bd(at), bd(ak)], axis=1)) for t, at, ak in zip(tinv, ats, akv)]
    ta = [x[:, 0:LANES] for x in tt]
    tk = [x[:, LANES:2 * LANES] for x in tt]
    ms = [eyew * wl + _mmd(a, bh, _TN) * bdmask for a, bh, wl in zip(ta, bhs, wls)]
    cs = [_mmd(cat([k, v], axis=0), cat([bh, kh], axis=0), _TN) * bdmask
          for k, v, bh, kh in zip(tk, vs, bhs, khs)]
    qy = [_mmd(n, cat([bd(a), bd(k)], axis=1)) for n, a, k in zip(n_rb, ta, tk)]
    qs = [rt + x[:, 0:LANES] for rt, x in zip(rts, qy)]
    y0 = [x[:, LANES:2 * LANES] + _mmd(n, b) for x, n, b in zip(qy, n_rk, bdv)]
    return list(zip(qs, y0, ms, cs))


def _rwkv_prompt_kernel(p_ref, mu_ref, w0_ref, lora_ref, a0_ref, gup_ref, kk_ref, ka_ref, rk_ref,
                        lnw_ref, lnb_ref, wout_ref, seg_ref, tri_ref,
                        oa_ref, tail_ref, wkv_ref, prev_scr, s_scr):
    t = pl.program_id(1)

    @pl.when(t == 0)
    def _init():
        prev_scr[...] = jnp.zeros_like(prev_scr)
        s_scr[...] = jnp.zeros_like(s_scr)

    p = p_ref[...]
    rows = p.shape[0]
    row = lax.broadcasted_iota(jnp.int32, p.shape, 0)
    prev = jnp.where(row == 0, prev_scr[SUBLANES - 1:SUBLANES, :], pltpu.roll(p, 1, axis=0))
    prev_scr[...] = p[rows - SUBLANES:rows, :]
    tail_ref[0] = p[rows - SUBLANES:rows, :]

    seg = seg_ref[...]
    r, lw, k2, v, aa, bb, g = _rwkv_pre(p, prev, mu_ref[...], w0_ref[...], lora_ref[...], a0_ref[...],
                                        gup_ref[...], kk_ref[...], ka_ref[...], seg)
    c = _cumsum_rows(tri_ref[...], lw)

    ln = CHUNK
    ri = lax.broadcasted_iota(jnp.int32, (ln, LANES), 0)
    ci = lax.broadcasted_iota(jnp.int32, (ln, LANES), 1)
    cj = jnp.where(ci < HEAD_DIM, ci, ci - HEAD_DIM)
    m0 = (ci < HEAD_DIM).astype(F32)
    m1 = 1.0 - m0
    strict = cj < ri
    incl = cj <= ri
    eye2 = (cj == ri).astype(F32)
    r2 = lax.broadcasted_iota(jnp.int32, (LANES, LANES), 0)
    c2 = lax.broadcasted_iota(jnp.int32, (LANES, LANES), 1)
    bdmask = ((r2 < HEAD_DIM) == (c2 < HEAD_DIM)).astype(F32)
    eyew = (r2 == c2).astype(F32)

    npairs = s_scr.shape[0]
    insts = []
    for ch in range(rows // ln):
        rs = slice(ch * ln, (ch + 1) * ln)
        c_, lw_ = c[rs], lw[rs]
        cl = c_[ln - 1:ln, :]
        e_n = jnp.exp(-c_)
        e_l = jnp.exp(cl - c_)
        rt = r[rs] * jnp.exp(c_)
        at = aa[rs] * jnp.exp(c_ - lw_)
        bt, kt = bb[rs] * e_n, k2[rs] * e_n
        bh, kh = bb[rs] * e_l, k2[rs] * e_l
        wl = jnp.exp(cl)
        v_ = v[rs]
        for pr in range(npairs):
            sl = slice(pr * LANES, (pr + 1) * LANES)
            insts.append((rt[:, sl], at[:, sl], bt[:, sl], kt[:, sl], bh[:, sl], kh[:, sl], v_[:, sl], wl[:, sl]))
    summaries = _wkv_chunk_summaries(insts, m0, m1, strict, incl, eye2, eyew, bdmask)

    states = [s_scr[pr] for pr in range(npairs)]
    y_rows = []
    for ch in range(rows // ln):
        ys = []
        for pr in range(npairs):
            q, y0, m, cc = summaries[ch * npairs + pr]
            ys.append(_mmd(q, states[pr], _NT) + y0)
            states[pr] = _mmd(states[pr], m) + cc
        y_rows.append(jnp.concatenate(ys, axis=1))
    for pr in range(npairs):
        s_scr[pr] = states[pr]
    y = jnp.concatenate(y_rows, axis=0)
    oa_ref[...] = _rwkv_post(y, r, k2, v, g, rk_ref[...], lnw_ref[...], lnb_ref[...], wout_ref[...], seg)

    @pl.when(t == pl.num_programs(1) - 1)
    def _fin():
        for pr in range(npairs):
            s = s_scr[pr]
            wkv_ref[0, 2 * pr] = s[0:HEAD_DIM, 0:HEAD_DIM]
            wkv_ref[0, 2 * pr + 1] = pltpu.roll(s[HEAD_DIM:LANES, :], HEAD_DIM, axis=1)[:, 0:HEAD_DIM]


def _rwkv_prompt(p, rw, b, t, tc):
    n, cp = p.shape
    w = rw["w0"].shape[-1]
    heads = w // HEAD_DIM
    nt = t // tc
    dm = rw["w_out"].shape[-1]
    tri = jnp.kron(jnp.eye(tc // CHUNK, dtype=BF16), rw["tri"])
    names = ("mu", "w0", "lora2", "a0", "g_up", "k_k", "k_a", "r_k", "lnx_w", "lnx_b", "w_out", "seg")
    return pl.pallas_call(
        _rwkv_prompt_kernel,
        grid=(b, nt),
        in_specs=[pl.BlockSpec((tc, cp), lambda i, j: (i * nt + j, 0))] + [_const(rw[k].shape) for k in names]
                 + [_const(tri.shape)],
        out_specs=[pl.BlockSpec((tc, dm), lambda i, j: (i * nt + j, 0)),
                   pl.BlockSpec((1, SUBLANES, cp), lambda i, j: (i, 0, 0)),
                   pl.BlockSpec((1, heads, HEAD_DIM, HEAD_DIM), lambda i, j: (i, 0, 0, 0))],
        out_shape=[jax.ShapeDtypeStruct((n, dm), F32),
                   jax.ShapeDtypeStruct((b, SUBLANES, cp), F32),
                   jax.ShapeDtypeStruct((b, heads, HEAD_DIM, HEAD_DIM), F32)],
        scratch_shapes=[pltpu.VMEM((SUBLANES, cp), F32), pltpu.VMEM((heads // 2, LANES, LANES), F32)],
        compiler_params=_params(("arbitrary", "arbitrary")),
    )(p, *[rw[k] for k in names], tri)


def _rwkv_sample_pre_kernel(p_ref, prev_ref, mu_ref, w0_ref, lora_ref, a0_ref, gup_ref, kk_ref, ka_ref, seg_ref,
                            rt_ref, wt_ref, kt_ref, vt_ref, at_ref, bt_ref, r_ref, k_ref, v_ref, g_ref):
    r, lw, k2, v, aa, bb, g = _rwkv_pre(p_ref[...], prev_ref[...], mu_ref[...], w0_ref[...], lora_ref[...],
                                        a0_ref[...], gup_ref[...], kk_ref[...], ka_ref[...], seg_ref[...])
    rt_ref[...] = r.T
    wt_ref[...] = jnp.exp(lw).T
    kt_ref[...] = k2.T
    vt_ref[...] = v.T
    at_ref[...] = aa.T
    bt_ref[...] = bb.T
    r_ref[...] = r
    k_ref[...] = k2
    v_ref[...] = v
    g_ref[...] = g


def _rwkv_sample_step_kernel(s_ref, r_ref, w_ref, k_ref, v_ref, a_ref, b_ref, y_ref, so_ref):
    r, w, k, a, b = r_ref[...], w_ref[...], k_ref[...], a_ref[...], b_ref[...]

    def body(i, carry):
        base = pl.multiple_of(i * SUBLANES, SUBLANES)
        vrows = v_ref[pl.ds(base, SUBLANES), :]
        ys = []
        for j in range(SUBLANES):
            s = s_ref[0, base + j]
            sa = jnp.sum(s * a, axis=0, keepdims=True)
            s_new = s * w + sa * b + vrows[j:j + 1, :] * k
            so_ref[0, base + j] = s_new
            ys.append(jnp.sum(s_new * r, axis=0, keepdims=True))
        y_ref[pl.ds(base, SUBLANES), :] = jnp.concatenate(ys, axis=0)
        return carry

    lax.fori_loop(0, s_ref.shape[1] // SUBLANES, body, 0)


def _rwkv_sample_post_kernel(yt_ref, r_ref, k_ref, v_ref, g_ref, rk_ref, lnw_ref, lnb_ref, wout_ref, seg_ref, oa_ref):
    oa_ref[...] = _rwkv_post(yt_ref[...].T, r_ref[...], k_ref[...], v_ref[...], g_ref[...], rk_ref[...],
                             lnw_ref[...], lnb_ref[...], wout_ref[...], seg_ref[...])


def _rwkv_sample(p, shift, wkv, rw):
    n, cp = p.shape
    w = rw["w0"].shape[-1]
    heads = w // HEAD_DIM
    dm = rw["w_out"].shape[-1]
    pre_names = ("mu", "w0", "lora2", "a0", "g_up", "k_k", "k_a", "seg")
    vec = jax.ShapeDtypeStruct((n, w), F32)
    vec_t = jax.ShapeDtypeStruct((w, n), F32)
    rt, wt, kt, vt, at, bt, r, k2, v, g = pl.pallas_call(
        _rwkv_sample_pre_kernel,
        grid=(1,),
        in_specs=[_const((n, cp)), _const((n, cp))] + [_const(rw[k].shape) for k in pre_names],
        out_specs=[_const((w, n))] * 6 + [_const((n, w))] * 4,
        out_shape=[vec_t] * 6 + [vec] * 4,
        compiler_params=_params(("arbitrary",)),
    )(p, shift, *[rw[k] for k in pre_names])

    head_spec = pl.BlockSpec((HEAD_DIM, n), lambda h: (h, 0))
    st_spec = pl.BlockSpec((1, HEAD_DIM, HEAD_DIM, n), lambda h: (h, 0, 0, 0))
    yt, wkv_new = pl.pallas_call(
        _rwkv_sample_step_kernel,
        grid=(heads,),
        in_specs=[st_spec] + [head_spec] * 6,
        out_specs=[head_spec, st_spec],
        out_shape=[vec_t, jax.ShapeDtypeStruct(wkv.shape, F32)],
        compiler_params=_params(("arbitrary",)),
    )(wkv, rt, wt, kt, vt, at, bt)

    post_names = ("r_k", "lnx_w", "lnx_b", "w_out", "seg")
    oa = pl.pallas_call(
        _rwkv_sample_post_kernel,
        grid=(1,),
        in_specs=[_const((w, n))] + [_const((n, w))] * 4 + [_const(rw[k].shape) for k in post_names],
        out_specs=_const((n, dm)),
        out_shape=jax.ShapeDtypeStruct((n, dm), F32),
        compiler_params=_params(("arbitrary",)),
    )(yt, r, k2, v, g, *[rw[k] for k in post_names])
    return oa, wkv_new


def _lru_gates(xc, gw, gb, lam):
    c = xc.shape[-1]
    gates = _mm(xc, gw) + gb
    gx = _sigmoid(gates[:, 0:c])
    ga = _sigmoid(gates[:, c:2 * c])
    log_a = -LRU_C * ga * _softplus(-lam)
    a = jnp.exp(log_a)
    u = jnp.sqrt(-_expm1(2.0 * log_a)) * (gx * xc)
    return a, u


def _scan_rows(a, u, h0):
    rows = a.shape[0]
    row = lax.broadcasted_iota(jnp.int32, a.shape, 0)
    k = 1
    while k < rows:
        keep = row >= k
        u = u + a * jnp.where(keep, pltpu.roll(u, k, axis=0), 0.0)
        a = a * jnp.where(keep, pltpu.roll(a, k, axis=0), 1.0)
        k *= 2
    return u + a * h0


def _lru_prompt(x, tail, h0, cw_ref, cb_ref, gw, gb, lam):
    width = cw_ref.shape[0]
    xc = cb_ref[...] + cw_ref[width - 1:width, :] * x
    for j in range(1, width):
        xc = xc + cw_ref[width - 1 - j:width - j, :] * _shift_rows(x, tail, j)
    a, u = _lru_gates(xc, gw, gb, lam)
    return _scan_rows(a, u, h0)


def _lru_sample_kernel(x_ref, *refs):
    *buf_refs, h0_ref, cw_ref, cb_ref, gw_ref, gb_ref, lam_ref, h_ref = refs
    width = cw_ref.shape[0]
    xc = cb_ref[...] + cw_ref[width - 1:width, :] * x_ref[...]
    for j, buf_ref in enumerate(buf_refs):
        xc = xc + cw_ref[j:j + 1, :] * buf_ref[...]
    a, u = _lru_gates(xc, gw_ref[...], gb_ref[...], lam_ref[...])
    h_ref[...] = a * h0_ref[...] + u


def _lru_sample(xb, bufs, h0, lw):
    names = ("conv_w", "conv_b", "gate_w", "gate_b", "lam")
    args = (xb, *bufs, h0, *[lw[k] for k in names])
    return pl.pallas_call(
        _lru_sample_kernel,
        grid=(1,),
        in_specs=[_const(v.shape) for v in args],
        out_specs=_const(xb.shape),
        out_shape=jax.ShapeDtypeStruct(xb.shape, F32),
        compiler_params=_params(("arbitrary",)),
    )(*args)


def _mix_rows(x, oa, hs, gates, lwo, wo, g):
    d = x.shape[-1]
    ob = _mm(hs, lwo)
    mix = _mm(_sigmoid(gates[:, 0:d]) * oa + _sigmoid(gates[:, d:2 * d]) * ob, wo)
    return x + _rms(mix, g)


def _mix_sample_kernel(x_ref, oa_ref, hs_ref, gt_ref, lwo_ref, wo_ref, g_ref, o_ref):
    o_ref[...] = _mix_rows(x_ref[...], oa_ref[...], hs_ref[...], gt_ref[...], lwo_ref[...], wo_ref[...], g_ref[...])


def _mix_sample(x, oa, hs, gates, lwo16, wo16, g):
    args = (x, oa, hs, gates, lwo16, wo16, g)
    return pl.pallas_call(
        _mix_sample_kernel,
        grid=(1,),
        in_specs=[_const(v.shape) for v in args],
        out_specs=_const(x.shape),
        out_shape=jax.ShapeDtypeStruct(x.shape, F32),
        compiler_params=_params(("arbitrary",)),
    )(*args)


def _mix_prompt_kernel(x_ref, oa_ref, xb_ref, gt_ref, cw_ref, cb_ref, gw_ref, gb_ref, lam_ref, lwo_ref, wo_ref, g_ref,
                       o_ref, xtail_ref, htail_ref, xtail_scr, h_scr):
    @pl.when(pl.program_id(1) == 0)
    def _init():
        xtail_scr[...] = jnp.zeros_like(xtail_scr)
        h_scr[...] = jnp.zeros_like(h_scr)

    xb = xb_ref[...]
    rows = xb.shape[0]
    hs = _lru_prompt(xb, xtail_scr[...], h_scr[SUBLANES - 1:SUBLANES, :], cw_ref, cb_ref,
                     gw_ref[...], gb_ref[...], lam_ref[...])
    xtail_scr[...] = xb[rows - SUBLANES:rows, :]
    xtail_ref[0] = xb[rows - SUBLANES:rows, :]
    h_scr[...] = hs[rows - SUBLANES:rows, :]
    htail_ref[0] = hs[rows - SUBLANES:rows, :]
    o_ref[...] = _mix_rows(x_ref[...], oa_ref[...], hs, gt_ref[...], lwo_ref[...], wo_ref[...], g_ref[...])


def _mix_prompt(x, oa, xb, gates, lw, lwo16, wo16, g, b, t, tm):
    n, d = x.shape
    c = xb.shape[1]
    nt = t // tm
    names = ("conv_w", "conv_b", "gate_w", "gate_b", "lam")
    consts = [lw[k] for k in names] + [lwo16, wo16, g]
    rows = lambda w: pl.BlockSpec((tm, w), lambda i, j: (i * nt + j, 0))
    tail = pl.BlockSpec((1, SUBLANES, c), lambda i, j: (i, 0, 0))
    return pl.pallas_call(
        _mix_prompt_kernel,
        grid=(b, nt),
        in_specs=[rows(d), rows(d), rows(c), rows(2 * d)] + [_const(v.shape) for v in consts],
        out_specs=[rows(d), tail, tail],
        out_shape=[jax.ShapeDtypeStruct((n, d), F32), jax.ShapeDtypeStruct((b, SUBLANES, c), F32),
                   jax.ShapeDtypeStruct((b, SUBLANES, c), F32)],
        scratch_shapes=[pltpu.VMEM((SUBLANES, c), F32), pltpu.VMEM((SUBLANES, c), F32)],
        compiler_params=_params(("arbitrary", "arbitrary")),
    )(x, oa, xb, gates, *consts)


def _ffn_body(x1, gpre, gpost, up_ref, cw_ref, cb_ref, down_ref, hist, keep):
    dff = down_ref.shape[0]
    hn = _rms(x1, gpre).astype(BF16)
    starts = list(range(0, dff, FF_COLS))

    def up(c0):
        return [(slice(off, off + FF_COLS), jnp.dot(hn, up_ref[:, off:off + FF_COLS], preferred_element_type=F32))
                for off in (c0, dff + c0)]

    ups = [up(c0) for c0 in starts[:FFN_AHEAD]]
    f = jnp.zeros(x1.shape, F32)
    for i, c0 in enumerate(starts):
        if i + FFN_AHEAD < len(starts):
            ups.append(up(starts[i + FFN_AHEAD]))
        halves = []
        for cols, u in ups[i]:
            u1, u2 = hist(u, cols)
            halves.append(cb_ref[:, cols] + cw_ref[2:3, cols] * u + cw_ref[1:2, cols] * u1 + cw_ref[0:1, cols] * u2)
            keep(u, cols)
        act = _gelu_tanh(halves[0]) * halves[1]
        f = f + jnp.dot(act.astype(BF16), down_ref[c0:c0 + FF_COLS, :], preferred_element_type=F32)
    return x1 + _rms(f, gpost)


def _ffn_prompt_kernel(x_ref, gpre_ref, gpost_ref, up_ref, cw_ref, cb_ref, down_ref, y_ref, tail_ref, tail_scr):
    @pl.when(pl.program_id(1) == 0)
    def _init():
        tail_scr[...] = jnp.zeros_like(tail_scr)

    rows = x_ref.shape[0]

    def hist(u, cols):
        tail = tail_scr[:, cols]
        return _shift_rows(u, tail, 1), _shift_rows(u, tail, 2)

    def keep(u, cols):
        tail_scr[:, cols] = u[rows - SUBLANES:rows, :]
        tail_ref[0, :, cols] = u[rows - SUBLANES:rows, :]

    y_ref[...] = _ffn_body(x_ref[...], gpre_ref[...], gpost_ref[...], up_ref, cw_ref, cb_ref, down_ref, hist, keep)


def _ffn_prompt(x1, fw, b, t, tm):
    n, d = x1.shape
    nt = t // tm
    dff2 = fw["up"].shape[1]
    names = ("g_pre", "g_post", "up", "conv_w", "conv_b", "down")
    rows = pl.BlockSpec((tm, d), lambda i, j: (i * nt + j, 0))
    return pl.pallas_call(
        _ffn_prompt_kernel,
        grid=(b, nt),
        in_specs=[rows] + [_const(fw[k].shape) for k in names],
        out_specs=[rows, pl.BlockSpec((1, SUBLANES, dff2), lambda i, j: (i, 0, 0))],
        out_shape=[jax.ShapeDtypeStruct((n, d), F32), jax.ShapeDtypeStruct((b, SUBLANES, dff2), F32)],
        scratch_shapes=[pltpu.VMEM((SUBLANES, dff2), F32)],
        compiler_params=_params(("arbitrary", "arbitrary")),
    )(x1, *[fw[k] for k in names])


def _ffn_sample_kernel(x_ref, b0_ref, b1_ref, gpre_ref, gpost_ref, up_ref, cw_ref, cb_ref, down_ref, y_ref, u_ref):
    def hist(u, cols):
        return b1_ref[:, cols], b0_ref[:, cols]

    def keep(u, cols):
        u_ref[:, cols] = u

    y_ref[...] = _ffn_body(x_ref[...], gpre_ref[...], gpost_ref[...], up_ref, cw_ref, cb_ref, down_ref, hist, keep)


def _ffn_sample(x1, buf0, buf1, fw):
    names = ("g_pre", "g_post", "up", "conv_w", "conv_b", "down")
    args = (x1, buf0, buf1, *[fw[k] for k in names])
    return pl.pallas_call(
        _ffn_sample_kernel,
        grid=(1,),
        in_specs=[_const(v.shape) for v in args],
        out_specs=[_const(x1.shape), _const(buf0.shape)],
        out_shape=[jax.ShapeDtypeStruct(x1.shape, F32), jax.ShapeDtypeStruct(buf0.shape, F32)],
        compiler_params=_params(("arbitrary",)),
    )(*args)


def _block_diag(blocks):
    n, bi, bj = blocks.shape
    eye = jnp.eye(n, dtype=blocks.dtype)
    return jnp.einsum("nij,nm->nimj", blocks, eye).reshape(n * bi, n * bj)


def _layer_weights(norm_pre_mix, norm_post_mix, norm_pre_ffn, norm_post_ffn, w_in,
                   rwkv_mu, rwkv_w0, rwkv_w_up, rwkv_a0, rwkv_a_up, rwkv_g_up, rwkv_k_k, rwkv_k_a,
                   rwkv_r_k, rwkv_lnx_w, rwkv_lnx_b, rwkv_w_out,
                   lru_conv_w, lru_conv_b, lru_gx_w, lru_gx_b, lru_ga_w, lru_ga_b, lru_lambda, lru_w_out,
                   w_o, ffn_up, ffn_conv_w, ffn_conv_b, ffn_down):
    row = lambda x: x.reshape(1, -1)
    w = rwkv_w0.shape[-1]
    head_of = jnp.arange(w) // HEAD_DIM
    t_idx = jnp.arange(CHUNK)
    lo_w, lo_a = rwkv_w_up.shape[0], rwkv_a_up.shape[0]
    lora2 = jnp.zeros((lo_w + lo_a, 2 * w), F32).at[:lo_w, :w].set(rwkv_w_up).at[lo_w:, w:].set(rwkv_a_up)
    rw = dict(mu=row(rwkv_mu), w0=row(rwkv_w0), lora2=lora2.astype(BF16), a0=row(rwkv_a0),
              g_up=rwkv_g_up.astype(BF16), k_k=row(rwkv_k_k), k_a=row(rwkv_k_a), r_k=row(rwkv_r_k),
              lnx_w=row(rwkv_lnx_w), lnx_b=row(rwkv_lnx_b), w_out=rwkv_w_out.astype(BF16),
              seg=(head_of[:, None] == head_of[None, :]).astype(BF16),
              tri=(t_idx[:, None] >= t_idx[None, :]).astype(BF16))
    lw = dict(conv_w=lru_conv_w, conv_b=row(lru_conv_b),
              gate_w=jnp.concatenate([_block_diag(lru_gx_w), _block_diag(lru_ga_w)], axis=1).astype(BF16),
              gate_b=jnp.concatenate([row(lru_gx_b), row(lru_ga_b)], axis=1), lam=row(lru_lambda))
    fw = dict(g_pre=row(norm_pre_ffn), g_post=row(norm_post_ffn), up=ffn_up.astype(BF16),
              conv_w=ffn_conv_w, conv_b=row(ffn_conv_b), down=ffn_down.astype(BF16))
    return dict(g_in=row(norm_pre_mix), w_in=w_in.astype(BF16), rw=rw, lw=lw, fw=fw,
                lru_w_out=lru_w_out.astype(BF16), w_o=w_o.astype(BF16), g_mix=row(norm_post_mix))


def _splits(wts):
    w = wts["rw"]["w0"].shape[-1]
    c_rwkv = wts["rw"]["mu"].shape[-1]
    c_lru = wts["lw"]["lam"].shape[-1]
    return (c_rwkv, c_lru, wts["w_in"].shape[1] - c_rwkv - c_lru)


def _prompt_layer(x, wts):
    b, t, d = x.shape
    x2 = x.reshape(b * t, d)
    tm = min(t, 256)
    p, xb, gates = _proj(x2, wts["g_in"], wts["w_in"], _splits(wts), tm)
    oa, p_tail, wkv = _rwkv_prompt(p, wts["rw"], b, t, min(t, WKV_ROWS))
    x1, xb_tail, h_tail = _mix_prompt(x2, oa, xb, gates, wts["lw"], wts["lru_w_out"], wts["w_o"], wts["g_mix"],
                                      b, t, tm)
    y, u_tail = _ffn_prompt(x1, wts["fw"], b, t, min(t, FFN_ROWS))
    conv_w = wts["lw"]["conv_w"].shape[0]
    ffn_w = wts["fw"]["conv_w"].shape[0]
    state = (p_tail[:, SUBLANES - 1:], wkv, xb_tail[:, SUBLANES - (conv_w - 1):], h_tail[:, SUBLANES - 1],
             u_tail[:, SUBLANES - (ffn_w - 1):])
    return y.reshape(b, t, d), state


def _sample_layer(x, shift, wkv, lru_buf, h0, ffn_buf, wts):
    n, t, d = x.shape
    x2 = x.reshape(n, d)
    p, xb, gates = _proj(x2, wts["g_in"], wts["w_in"], _splits(wts), n)
    oa, wkv_t = _rwkv_sample(p, shift[:, 0, :], jnp.transpose(wkv, (1, 2, 3, 0)), wts["rw"])
    lru_bufs = [lru_buf[:, j, :] for j in range(lru_buf.shape[1])]
    h = _lru_sample(xb, lru_bufs, h0, wts["lw"])
    x1 = _mix_sample(x2, oa, h, gates, wts["lru_w_out"], wts["w_o"], wts["g_mix"])
    y, u = _ffn_sample(x1, ffn_buf[:, 0, :], ffn_buf[:, 1, :], wts["fw"])
    state = (p.reshape(n, 1, -1), jnp.transpose(wkv_t, (3, 0, 1, 2)), jnp.stack(lru_bufs[1:] + [xb], axis=1), h,
             jnp.stack([ffn_buf[:, 1, :], u], axis=1))
    return y.reshape(n, t, d), state


def kernel(x_prompt, x_sample, state_rwkv_shift, state_rwkv_wkv, state_lru_conv, state_lru_h, state_ffn_conv,
           norm_pre_mix, norm_post_mix, norm_pre_ffn, norm_post_ffn, w_in,
           rwkv_mu, rwkv_w0, rwkv_w_up, rwkv_a0, rwkv_a_up, rwkv_g_up, rwkv_k_k, rwkv_k_a,
           rwkv_r_k, rwkv_lnx_w, rwkv_lnx_b, rwkv_w_out,
           lru_conv_w, lru_conv_b, lru_gx_w, lru_gx_b, lru_ga_w, lru_ga_b, lru_lambda, lru_w_out,
           w_o, ffn_up, ffn_conv_w, ffn_conv_b, ffn_down):
    params = (norm_pre_mix, norm_post_mix, norm_pre_ffn, norm_post_ffn, w_in,
              rwkv_mu, rwkv_w0, rwkv_w_up, rwkv_a0, rwkv_a_up, rwkv_g_up, rwkv_k_k, rwkv_k_a,
              rwkv_r_k, rwkv_lnx_w, rwkv_lnx_b, rwkv_w_out,
              lru_conv_w, lru_conv_b, lru_gx_w, lru_gx_b, lru_ga_w, lru_ga_b, lru_lambda, lru_w_out,
              w_o, ffn_up, ffn_conv_w, ffn_conv_b, ffn_down)
    depth = w_in.shape[0]
    assert depth == 1 and x_sample.shape[1] == 1
    yp, ys = x_prompt, x_sample
    new_p, new_s = [], []
    for l in range(depth):
        wts = _layer_weights(*[q[l] for q in params])
        yp, st_p = _prompt_layer(yp, wts)
        ys, st_s = _sample_layer(ys, state_rwkv_shift[l], state_rwkv_wkv[l], state_lru_conv[l],
                                 state_lru_h[l], state_ffn_conv[l], wts)
        new_p.append(st_p)
        new_s.append(st_s)
    stk = lambda lst, i: jnp.stack([s[i] for s in lst], axis=0)
    return (yp, ys,
            stk(new_p, 0), stk(new_p, 1), stk(new_p, 2), stk(new_p, 3), stk(new_p, 4),
            stk(new_s, 0), stk(new_s, 1), stk(new_s, 2), stk(new_s, 3), stk(new_s, 4))
```

```python
import functools

import jax
import jax.numpy as jnp
from jax import lax
from jax.experimental import pallas as pl
from jax.experimental.pallas import tpu as pltpu

F32 = jnp.float32
BF16 = jnp.bfloat16

NORM_EPS = 1e-6
LN_X_EPS = 64e-5
LRU_C = 8.0
HEAD_DIM = 64
LANES = 128
SUBLANES = 8
CHUNK = 64
WKV_ROWS = 256
FF_COLS = 256
FFN_ROWS = 256
FFN_AHEAD = 2
VMEM_LIMIT = 56 * 1024 * 1024

_NN = (((1,), (0,)), ((), ()))
_NT = (((1,), (1,)), ((), ()))
_TN = (((0,), (0,)), ((), ()))


def _rms(x, g):
    return x * lax.rsqrt(jnp.mean(x * x, axis=-1, keepdims=True) + NORM_EPS) * g


def _softplus(x):
    return jnp.maximum(x, 0.0) + jnp.log1p(jnp.exp(-jnp.abs(x)))


def _sigmoid(x):
    return 1.0 / (1.0 + jnp.exp(-x))


def _expm1(x):
    u = jnp.exp(x)
    um1 = u - 1.0
    edge = (um1 == 0.0) | (um1 == -1.0)
    r = um1 * x / jnp.where(edge, 1.0, jnp.log(u))
    return jnp.where(um1 == 0.0, x, jnp.where(um1 == -1.0, -1.0, r))


def _gelu_tanh(x):
    return 0.5 * x * (1.0 + jnp.tanh(0.7978845608028654 * (x + 0.044715 * (x * x * x))))


def _mm(a, b):
    return jnp.dot(a.astype(BF16), b.astype(BF16), preferred_element_type=F32)


def _split2(x):
    hi = x.astype(BF16)
    lo = (x - hi.astype(F32)).astype(BF16)
    return hi, lo


def _mmd(a, b, dims=_NN):
    return lax.dot_general(a.astype(BF16), b.astype(BF16), dims, preferred_element_type=F32)


def _segsum(x, seg):
    hi, lo = _split2(x)
    return (jnp.dot(hi, seg, preferred_element_type=F32) + jnp.dot(lo, seg, preferred_element_type=F32))


def _cumsum_rows(tri, x):
    h1 = x.astype(BF16)
    r1 = x - h1.astype(F32)
    h2 = r1.astype(BF16)
    h3 = (r1 - h2.astype(F32)).astype(BF16)
    d = lambda y: jnp.dot(tri, y, preferred_element_type=F32)
    return d(h1) + (d(h2) + d(h3))


def _shift_rows(x, tail, j):
    xs = pltpu.roll(x, j, axis=0)
    ts = pltpu.roll(tail, j, axis=0)
    row = lax.broadcasted_iota(jnp.int32, ts.shape, 0)
    head = jnp.where(row < j, ts, xs[0:SUBLANES])
    if x.shape[0] == SUBLANES:
        return head
    return jnp.concatenate([head, xs[SUBLANES:]], axis=0)


def _const(shape):
    n = len(shape)
    return pl.BlockSpec(shape, lambda *_: (0,) * n)


def _params(sem):
    return pltpu.CompilerParams(dimension_semantics=sem, vmem_limit_bytes=VMEM_LIMIT)


def _proj_kernel(x_ref, g_ref, w_ref, p_ref, xb_ref, gt_ref, *, splits):
    xn = _rms(x_ref[...], g_ref[...]).astype(BF16)
    c0, c1, c2 = splits
    p_ref[...] = jnp.dot(xn, w_ref[:, 0:c0], preferred_element_type=F32)
    xb_ref[...] = jnp.dot(xn, w_ref[:, c0:c0 + c1], preferred_element_type=F32)
    gt_ref[...] = jnp.dot(xn, w_ref[:, c0 + c1:c0 + c1 + c2], preferred_element_type=F32)


def _proj(x, g, w_in16, splits, tm):
    n, d = x.shape
    c0, c1, c2 = splits
    return pl.pallas_call(
        functools.partial(_proj_kernel, splits=splits),
        grid=(n // tm,),
        in_specs=[pl.BlockSpec((tm, d), lambda i: (i, 0)), _const((1, d)), _const(w_in16.shape)],
        out_specs=[pl.BlockSpec((tm, c0), lambda i: (i, 0)),
                   pl.BlockSpec((tm, c1), lambda i: (i, 0)),
                   pl.BlockSpec((tm, c2), lambda i: (i, 0))],
        out_shape=[jax.ShapeDtypeStruct((n, c0), F32), jax.ShapeDtypeStruct((n, c1), F32),
                   jax.ShapeDtypeStruct((n, c2), F32)],
        compiler_params=_params(("arbitrary",)),
    )(x, g, w_in16)


def _rwkv_pre(p, prev, mu, w0, lora2, a0, g_up, k_k, k_a, seg):
    w = w0.shape[-1]
    m = p + mu * (prev - p)
    r, k, v = m[:, 0:w], m[:, w:2 * w], m[:, 2 * w:3 * w]
    z = m[:, 3 * w:3 * w + LANES]
    lane = lax.broadcasted_iota(jnp.int32, z.shape, 1)
    lor = _mm(jnp.where(lane < HEAD_DIM, jnp.tanh(z), z), lora2)
    wlog = -_softplus(-(w0 + lor[:, 0:w])) - 0.5
    lw = -jnp.exp(wlog)
    a_sig = _sigmoid(a0 + lor[:, w:2 * w])
    g = _mm(_sigmoid(m[:, 3 * w + LANES:3 * w + 2 * LANES]), g_up)
    kk = k * k_k
    kkn = kk / jnp.maximum(jnp.sqrt(_segsum(kk * kk, seg)), 1e-12)
    k2 = k * (1.0 + (a_sig - 1.0) * k_a)
    return r, lw, k2, v, -kkn, kkn * a_sig, g


def _rwkv_post(y, r, k2, v, g, r_k, lnx_w, lnx_b, w_out, seg):
    inv = 1.0 / HEAD_DIM
    mean = _segsum(y, seg) * inv
    d = y - mean
    var = _segsum(d * d, seg) * inv
    yn = d * lax.rsqrt(var + LN_X_EPS) * lnx_w + lnx_b
    bonus = _segsum(r * k2 * r_k, seg) * v
    return _mm((yn + bonus) * g, w_out)


def _wkv_chunk_summaries(xs, m0, m1, strict, incl, eye2, eyew, bdmask):
    ln = xs[0][0].shape[0]
    m0b = m0 > 0.5
    swap = lambda x: pltpu.roll(x, HEAD_DIM, axis=1)
    h16 = lambda x: x.astype(BF16)
    zero16 = jnp.zeros((), BF16)
    bd = lambda x: jnp.concatenate([jnp.where(m0b, x, zero16), jnp.where(m0b, zero16, x)], axis=0)
    cat = jnp.concatenate
    rts, ats, bts, kts, bhs, khs, vs, wls = zip(*xs)
    at16 = [h16(x) for x in ats]
    v16 = [h16(x) for x in vs]
    bdv = [bd(x) for x in v16]
    gms = [_mmd(cat([bd(a), bd(h16(rt))], axis=0), cat([bt, kt], axis=0), _NT)
           for rt, a, bt, kt in zip(rts, at16, bts, kts)]
    ga0 = [jnp.where(strict, g[0:ln], 0.0) for g in gms]
    ga1 = [jnp.where(strict, g[ln:2 * ln], 0.0) for g in gms]
    gr0 = [jnp.where(incl, g[2 * ln:3 * ln], 0.0) for g in gms]
    gr1 = [jnp.where(incl, g[3 * ln:4 * ln], 0.0) for g in gms]
    n_ab = [jnp.where(m0b, x, swap(y)) for x, y in zip(ga0, ga1)]
    n_ak = [jnp.where(m0b, swap(x), y) for x, y in zip(ga0, ga1)]
    n_rb = [jnp.where(m0b, x, swap(y)) for x, y in zip(gr0, gr1)]
    n_rk = [jnp.where(m0b, swap(x), y) for x, y in zip(gr0, gr1)]
    tinv = [eye2 + n for n in n_ab]
    steps = (ln - 1).bit_length() - 1
    if steps > 0:
        npow = [h16(_mmd(n, bd(n))) for n in (h16(x) for x in n_ab)]
        for _ in range(steps - 1):
            both = [_mmd(cat([n, h16(t)], axis=0), bd(n)) for n, t in zip(npow, tinv)]
            tinv = [t + x[ln:2 * ln] for t, x in zip(tinv, both)]
            npow = [h16(x[0:ln]) for x in both]
        tinv = [t + _mmd(t, bd(n)) for t, n in zip(tinv, npow)]
    kv = [_mmd(cat([nk, nr], axis=0), b) for nk, nr, b in zip(n_ak, n_rk, bdv)]
    akv = [x[0:ln] for x in kv]
    rkv = [x[ln:2 * ln] for x in kv]
    tt = [_mmd(t, cat([bd(a), bd(h16(ak))], axis=1)) for t, a, ak in zip(tinv, at16, akv)]
    ta = [h16(x[:, 0:LANES]) for x in tt]
    tk = [h16(x[:, LANES:2 * LANES]) for x in tt]
    mc = [_mmd(cat([cat([a, k], axis=1), cat([jnp.zeros_like(v), v], axis=1)], axis=0), cat([bh, kh], axis=0), _TN)
          for a, k, v, bh, kh in zip(ta, tk, v16, bhs, khs)]
    ms = [eyew * wl + x[0:LANES] * bdmask for x, wl in zip(mc, wls)]
    cs = [x[LANES:2 * LANES] * bdmask for x in mc]
    qy = [_mmd(n, cat([bd(a), bd(k)], axis=1)) for n, a, k in zip(n_rb, ta, tk)]
    qs = [rt + x[:, 0:LANES] for rt, x in zip(rts, qy)]
    y0 = [x[:, LANES:2 * LANES] + r for x, r in zip(qy, rkv)]
    return list(zip(qs, y0, ms, cs))


def _rwkv_prompt_kernel(p_ref, mu_ref, w0_ref, lora_ref, a0_ref, gup_ref, kk_ref, ka_ref, rk_ref,
                        lnw_ref, lnb_ref, wout_ref, seg_ref, tri_ref,
                        oa_ref, tail_ref, wkv_ref, prev_scr, s_scr):
    t = pl.program_id(1)

    @pl.when(t == 0)
    def _init():
        prev_scr[...] = jnp.zeros_like(prev_scr)
        s_scr[...] = jnp.zeros_like(s_scr)

    p = p_ref[...]
    rows = p.shape[0]
    row = lax.broadcasted_iota(jnp.int32, p.shape, 0)
    prev = jnp.where(row == 0, prev_scr[SUBLANES - 1:SUBLANES, :], pltpu.roll(p, 1, axis=0))
    prev_scr[...] = p[rows - SUBLANES:rows, :]
    tail_ref[0] = p[rows - SUBLANES:rows, :]

    seg = seg_ref[...]
    r, lw, k2, v, aa, bb, g = _rwkv_pre(p, prev, mu_ref[...], w0_ref[...], lora_ref[...], a0_ref[...],
                                        gup_ref[...], kk_ref[...], ka_ref[...], seg)
    c = _cumsum_rows(tri_ref[...], lw)

    ln = CHUNK
    ri = lax.broadcasted_iota(jnp.int32, (ln, LANES), 0)
    ci = lax.broadcasted_iota(jnp.int32, (ln, LANES), 1)
    cj = jnp.where(ci < HEAD_DIM, ci, ci - HEAD_DIM)
    m0 = (ci < HEAD_DIM).astype(F32)
    m1 = 1.0 - m0
    strict = cj < ri
    incl = cj <= ri
    eye2 = (cj == ri).astype(F32)
    r2 = lax.broadcasted_iota(jnp.int32, (LANES, LANES), 0)
    c2 = lax.broadcasted_iota(jnp.int32, (LANES, LANES), 1)
    bdmask = ((r2 < HEAD_DIM) == (c2 < HEAD_DIM)).astype(F32)
    eyew = (r2 == c2).astype(F32)

    npairs = s_scr.shape[0]
    insts = []
    for ch in range(rows // ln):
        rs = slice(ch * ln, (ch + 1) * ln)
        c_, lw_ = c[rs], lw[rs]
        cl = c_[ln - 1:ln, :]
        e_n = jnp.exp(-c_)
        e_l = jnp.exp(cl - c_)
        rt = r[rs] * jnp.exp(c_)
        at = aa[rs] * jnp.exp(c_ - lw_)
        bt, kt = bb[rs] * e_n, k2[rs] * e_n
        bh, kh = bb[rs] * e_l, k2[rs] * e_l
        wl = jnp.exp(cl)
        v_ = v[rs]
        for pr in range(npairs):
            sl = slice(pr * LANES, (pr + 1) * LANES)
            insts.append((rt[:, sl], at[:, sl], bt[:, sl], kt[:, sl], bh[:, sl], kh[:, sl], v_[:, sl], wl[:, sl]))
    summaries = _wkv_chunk_summaries(insts, m0, m1, strict, incl, eye2, eyew, bdmask)

    states = [s_scr[pr] for pr in range(npairs)]
    y_rows = []
    for ch in range(rows // ln):
        ys = []
        for pr in range(npairs):
            q, y0, m, cc = summaries[ch * npairs + pr]
            ys.append(_mmd(q, states[pr], _NT) + y0)
            states[pr] = _mmd(states[pr], m) + cc
        y_rows.append(jnp.concatenate(ys, axis=1))
    for pr in range(npairs):
        s_scr[pr] = states[pr]
    y = jnp.concatenate(y_rows, axis=0)
    oa_ref[...] = _rwkv_post(y, r, k2, v, g, rk_ref[...], lnw_ref[...], lnb_ref[...], wout_ref[...], seg)

    @pl.when(t == pl.num_programs(1) - 1)
    def _fin():
        for pr in range(npairs):
            s = s_scr[pr]
            wkv_ref[0, 2 * pr] = s[0:HEAD_DIM, 0:HEAD_DIM]
            wkv_ref[0, 2 * pr + 1] = pltpu.roll(s[HEAD_DIM:LANES, :], HEAD_DIM, axis=1)[:, 0:HEAD_DIM]


def _rwkv_prompt(p, rw, b, t, tc):
    n, cp = p.shape
    w = rw["w0"].shape[-1]
    heads = w // HEAD_DIM
    nt = t // tc
    dm = rw["w_out"].shape[-1]
    tri = jnp.kron(jnp.eye(tc // CHUNK, dtype=BF16), rw["tri"])
    names = ("mu", "w0", "lora2", "a0", "g_up", "k_k", "k_a", "r_k", "lnx_w", "lnx_b", "w_out", "seg")
    return pl.pallas_call(
        _rwkv_prompt_kernel,
        grid=(b, nt),
        in_specs=[pl.BlockSpec((tc, cp), lambda i, j: (i * nt + j, 0))] + [_const(rw[k].shape) for k in names]
                 + [_const(tri.shape)],
        out_specs=[pl.BlockSpec((tc, dm), lambda i, j: (i * nt + j, 0)),
                   pl.BlockSpec((1, SUBLANES, cp), lambda i, j: (i, 0, 0)),
                   pl.BlockSpec((1, heads, HEAD_DIM, HEAD_DIM), lambda i, j: (i, 0, 0, 0))],
        out_shape=[jax.ShapeDtypeStruct((n, dm), F32),
                   jax.ShapeDtypeStruct((b, SUBLANES, cp), F32),
                   jax.ShapeDtypeStruct((b, heads, HEAD_DIM, HEAD_DIM), F32)],
        scratch_shapes=[pltpu.VMEM((SUBLANES, cp), F32), pltpu.VMEM((heads // 2, LANES, LANES), F32)],
        compiler_params=_params(("arbitrary", "arbitrary")),
    )(p, *[rw[k] for k in names], tri)


def _rwkv_sample_pre_kernel(p_ref, prev_ref, mu_ref, w0_ref, lora_ref, a0_ref, gup_ref, kk_ref, ka_ref, seg_ref,
                            rt_ref, wt_ref, kt_ref, vt_ref, at_ref, bt_ref, r_ref, k_ref, v_ref, g_ref):
    r, lw, k2, v, aa, bb, g = _rwkv_pre(p_ref[...], prev_ref[...], mu_ref[...], w0_ref[...], lora_ref[...],
                                        a0_ref[...], gup_ref[...], kk_ref[...], ka_ref[...], seg_ref[...])
    rt_ref[...] = r.T
    wt_ref[...] = jnp.exp(lw).T
    kt_ref[...] = k2.T
    vt_ref[...] = v.T
    at_ref[...] = aa.T
    bt_ref[...] = bb.T
    r_ref[...] = r
    k_ref[...] = k2
    v_ref[...] = v
    g_ref[...] = g


def _rwkv_sample_step_kernel(s_ref, r_ref, w_ref, k_ref, v_ref, a_ref, b_ref, y_ref, so_ref):
    r, w, k, a, b = r_ref[...], w_ref[...], k_ref[...], a_ref[...], b_ref[...]

    def body(i, carry):
        base = pl.multiple_of(i * SUBLANES, SUBLANES)
        vrows = v_ref[pl.ds(base, SUBLANES), :]
        ys = []
        for j in range(SUBLANES):
            s = s_ref[0, base + j]
            sa = jnp.sum(s * a, axis=0, keepdims=True)
            s_new = s * w + sa * b + vrows[j:j + 1, :] * k
            so_ref[0, base + j] = s_new
            ys.append(jnp.sum(s_new * r, axis=0, keepdims=True))
        y_ref[pl.ds(base, SUBLANES), :] = jnp.concatenate(ys, axis=0)
        return carry

    lax.fori_loop(0, s_ref.shape[1] // SUBLANES, body, 0)


def _rwkv_sample_post_kernel(yt_ref, r_ref, k_ref, v_ref, g_ref, rk_ref, lnw_ref, lnb_ref, wout_ref, seg_ref, oa_ref):
    oa_ref[...] = _rwkv_post(yt_ref[...].T, r_ref[...], k_ref[...], v_ref[...], g_ref[...], rk_ref[...],
                             lnw_ref[...], lnb_ref[...], wout_ref[...], seg_ref[...])


def _rwkv_sample(p, shift, wkv, rw):
    n, cp = p.shape
    w = rw["w0"].shape[-1]
    heads = w // HEAD_DIM
    dm = rw["w_out"].shape[-1]
    pre_names = ("mu", "w0", "lora2", "a0", "g_up", "k_k", "k_a", "seg")
    vec = jax.ShapeDtypeStruct((n, w), F32)
    vec_t = jax.ShapeDtypeStruct((w, n), F32)
    rt, wt, kt, vt, at, bt, r, k2, v, g = pl.pallas_call(
        _rwkv_sample_pre_kernel,
        grid=(1,),
        in_specs=[_const((n, cp)), _const((n, cp))] + [_const(rw[k].shape) for k in pre_names],
        out_specs=[_const((w, n))] * 6 + [_const((n, w))] * 4,
        out_shape=[vec_t] * 6 + [vec] * 4,
        compiler_params=_params(("arbitrary",)),
    )(p, shift, *[rw[k] for k in pre_names])

    head_spec = pl.BlockSpec((HEAD_DIM, n), lambda h: (h, 0))
    st_spec = pl.BlockSpec((1, HEAD_DIM, HEAD_DIM, n), lambda h: (h, 0, 0, 0))
    yt, wkv_new = pl.pallas_call(
        _rwkv_sample_step_kernel,
        grid=(heads,),
        in_specs=[st_spec] + [head_spec] * 6,
        out_specs=[head_spec, st_spec],
        out_shape=[vec_t, jax.ShapeDtypeStruct(wkv.shape, F32)],
        compiler_params=_params(("arbitrary",)),
    )(wkv, rt, wt, kt, vt, at, bt)

    post_names = ("r_k", "lnx_w", "lnx_b", "w_out", "seg")
    oa = pl.pallas_call(
        _rwkv_sample_post_kernel,
        grid=(1,),
        in_specs=[_const((w, n))] + [_const((n, w))] * 4 + [_const(rw[k].shape) for k in post_names],
        out_specs=_const((n, dm)),
        out_shape=jax.ShapeDtypeStruct((n, dm), F32),
        compiler_params=_params(("arbitrary",)),
    )(yt, r, k2, v, g, *[rw[k] for k in post_names])
    return oa, wkv_new


def _lru_gates(xc, gw, gb, lam):
    c = xc.shape[-1]
    gates = _mm(xc, gw) + gb
    gx = _sigmoid(gates[:, 0:c])
    ga = _sigmoid(gates[:, c:2 * c])
    log_a = -LRU_C * ga * _softplus(-lam)
    a = jnp.exp(log_a)
    u = jnp.sqrt(-_expm1(2.0 * log_a)) * (gx * xc)
    return a, u


def _scan_rows(a, u, h0):
    rows = a.shape[0]
    row = lax.broadcasted_iota(jnp.int32, a.shape, 0)
    k = 1
    while k < rows:
        keep = row >= k
        u = u + a * jnp.where(keep, pltpu.roll(u, k, axis=0), 0.0)
        a = a * jnp.where(keep, pltpu.roll(a, k, axis=0), 1.0)
        k *= 2
    return u + a * h0


def _lru_prompt(x, tail, h0, cw_ref, cb_ref, gw, gb, lam):
    width = cw_ref.shape[0]
    xc = cb_ref[...] + cw_ref[width - 1:width, :] * x
    for j in range(1, width):
        xc = xc + cw_ref[width - 1 - j:width - j, :] * _shift_rows(x, tail, j)
    a, u = _lru_gates(xc, gw, gb, lam)
    return _scan_rows(a, u, h0)


def _lru_sample_kernel(x_ref, *refs):
    *buf_refs, h0_ref, cw_ref, cb_ref, gw_ref, gb_ref, lam_ref, h_ref = refs
    width = cw_ref.shape[0]
    xc = cb_ref[...] + cw_ref[width - 1:width, :] * x_ref[...]
    for j, buf_ref in enumerate(buf_refs):
        xc = xc + cw_ref[j:j + 1, :] * buf_ref[...]
    a, u = _lru_gates(xc, gw_ref[...], gb_ref[...], lam_ref[...])
    h_ref[...] = a * h0_ref[...] + u


def _lru_sample(xb, bufs, h0, lw):
    names = ("conv_w", "conv_b", "gate_w", "gate_b", "lam")
    args = (xb, *bufs, h0, *[lw[k] for k in names])
    return pl.pallas_call(
        _lru_sample_kernel,
        grid=(1,),
        in_specs=[_const(v.shape) for v in args],
        out_specs=_const(xb.shape),
        out_shape=jax.ShapeDtypeStruct(xb.shape, F32),
        compiler_params=_params(("arbitrary",)),
    )(*args)


def _mix_rows(x, oa, hs, gates, lwo, wo, g):
    d = x.shape[-1]
    ob = _mm(hs, lwo)
    mix = _mm(_sigmoid(gates[:, 0:d]) * oa + _sigmoid(gates[:, d:2 * d]) * ob, wo)
    return x + _rms(mix, g)


def _mix_sample_kernel(x_ref, oa_ref, hs_ref, gt_ref, lwo_ref, wo_ref, g_ref, o_ref):
    o_ref[...] = _mix_rows(x_ref[...], oa_ref[...], hs_ref[...], gt_ref[...], lwo_ref[...], wo_ref[...], g_ref[...])


def _mix_sample(x, oa, hs, gates, lwo16, wo16, g):
    args = (x, oa, hs, gates, lwo16, wo16, g)
    return pl.pallas_call(
        _mix_sample_kernel,
        grid=(1,),
        in_specs=[_const(v.shape) for v in args],
        out_specs=_const(x.shape),
        out_shape=jax.ShapeDtypeStruct(x.shape, F32),
        compiler_params=_params(("arbitrary",)),
    )(*args)


def _mix_prompt_kernel(x_ref, oa_ref, xb_ref, gt_ref, cw_ref, cb_ref, gw_ref, gb_ref, lam_ref, lwo_ref, wo_ref, g_ref,
                       o_ref, xtail_ref, htail_ref, xtail_scr, h_scr):
    @pl.when(pl.program_id(1) == 0)
    def _init():
        xtail_scr[...] = jnp.zeros_like(xtail_scr)
        h_scr[...] = jnp.zeros_like(h_scr)

    xb = xb_ref[...]
    rows = xb.shape[0]
    hs = _lru_prompt(xb, xtail_scr[...], h_scr[SUBLANES - 1:SUBLANES, :], cw_ref, cb_ref,
                     gw_ref[...], gb_ref[...], lam_ref[...])
    xtail_scr[...] = xb[rows - SUBLANES:rows, :]
    xtail_ref[0] = xb[rows - SUBLANES:rows, :]
    h_scr[...] = hs[rows - SUBLANES:rows, :]
    htail_ref[0] = hs[rows - SUBLANES:rows, :]
    o_ref[...] = _mix_rows(x_ref[...], oa_ref[...], hs, gt_ref[...], lwo_ref[...], wo_ref[...], g_ref[...])


def _mix_prompt(x, oa, xb, gates, lw, lwo16, wo16, g, b, t, tm):
    n, d = x.shape
    c = xb.shape[1]
    nt = t // tm
    names = ("conv_w", "conv_b", "gate_w", "gate_b", "lam")
    consts = [lw[k] for k in names] + [lwo16, wo16, g]
    rows = lambda w: pl.BlockSpec((tm, w), lambda i, j: (i * nt + j, 0))
    tail = pl.BlockSpec((1, SUBLANES, c), lambda i, j: (i, 0, 0))
    return pl.pallas_call(
        _mix_prompt_kernel,
        grid=(b, nt),
        in_specs=[rows(d), rows(d), rows(c), rows(2 * d)] + [_const(v.shape) for v in consts],
        out_specs=[rows(d), tail, tail],
        out_shape=[jax.ShapeDtypeStruct((n, d), F32), jax.ShapeDtypeStruct((b, SUBLANES, c), F32),
                   jax.ShapeDtypeStruct((b, SUBLANES, c), F32)],
        scratch_shapes=[pltpu.VMEM((SUBLANES, c), F32), pltpu.VMEM((SUBLANES, c), F32)],
        compiler_params=_params(("arbitrary", "arbitrary")),
    )(x, oa, xb, gates, *consts)


def _ffn_body(x1, gpre, gpost, up_ref, cw_ref, cb_ref, down_ref, hist, keep):
    dff = down_ref.shape[0]
    hn = _rms(x1, gpre).astype(BF16)
    starts = list(range(0, dff, FF_COLS))

    def up(c0):
        return [(slice(off, off + FF_COLS), jnp.dot(hn, up_ref[:, off:off + FF_COLS], preferred_element_type=F32))
                for off in (c0, dff + c0)]

    ups = [up(c0) for c0 in starts[:FFN_AHEAD]]
    f = jnp.zeros(x1.shape, F32)
    for i, c0 in enumerate(starts):
        if i + FFN_AHEAD < len(starts):
            ups.append(up(starts[i + FFN_AHEAD]))
        halves = []
        for cols, u in ups[i]:
            u1, u2 = hist(u, cols)
            halves.append(cb_ref[:, cols] + cw_ref[2:3, cols] * u + cw_ref[1:2, cols] * u1 + cw_ref[0:1, cols] * u2)
            keep(u, cols)
        act = _gelu_tanh(halves[0]) * halves[1]
        f = f + jnp.dot(act.astype(BF16), down_ref[c0:c0 + FF_COLS, :], preferred_element_type=F32)
    return x1 + _rms(f, gpost)


def _ffn_prompt_kernel(x_ref, gpre_ref, gpost_ref, up_ref, cw_ref, cb_ref, down_ref, y_ref, tail_ref, tail_scr):
    @pl.when(pl.program_id(1) == 0)
    def _init():
        tail_scr[...] = jnp.zeros_like(tail_scr)

    rows = x_ref.shape[0]

    def hist(u, cols):
        tail = tail_scr[:, cols]
        return _shift_rows(u, tail, 1), _shift_rows(u, tail, 2)

    def keep(u, cols):
        tail_scr[:, cols] = u[rows - SUBLANES:rows, :]
        tail_ref[0, :, cols] = u[rows - SUBLANES:rows, :]

    y_ref[...] = _ffn_body(x_ref[...], gpre_ref[...], gpost_ref[...], up_ref, cw_ref, cb_ref, down_ref, hist, keep)


def _ffn_prompt(x1, fw, b, t, tm):
    n, d = x1.shape
    nt = t // tm
    dff2 = fw["up"].shape[1]
    names = ("g_pre", "g_post", "up", "conv_w", "conv_b", "down")
    rows = pl.BlockSpec((tm, d), lambda i, j: (i * nt + j, 0))
    return pl.pallas_call(
        _ffn_prompt_kernel,
        grid=(b, nt),
        in_specs=[rows] + [_const(fw[k].shape) for k in names],
        out_specs=[rows, pl.BlockSpec((1, SUBLANES, dff2), lambda i, j: (i, 0, 0))],
        out_shape=[jax.ShapeDtypeStruct((n, d), F32), jax.ShapeDtypeStruct((b, SUBLANES, dff2), F32)],
        scratch_shapes=[pltpu.VMEM((SUBLANES, dff2), F32)],
        compiler_params=_params(("arbitrary", "arbitrary")),
    )(x1, *[fw[k] for k in names])


def _ffn_sample_kernel(x_ref, b0_ref, b1_ref, gpre_ref, gpost_ref, up_ref, cw_ref, cb_ref, down_ref, y_ref, u_ref):
    def hist(u, cols):
        return b1_ref[:, cols], b0_ref[:, cols]

    def keep(u, cols):
        u_ref[:, cols] = u

    y_ref[...] = _ffn_body(x_ref[...], gpre_ref[...], gpost_ref[...], up_ref, cw_ref, cb_ref, down_ref, hist, keep)


def _ffn_sample(x1, buf0, buf1, fw):
    names = ("g_pre", "g_post", "up", "conv_w", "conv_b", "down")
    args = (x1, buf0, buf1, *[fw[k] for k in names])
    return pl.pallas_call(
        _ffn_sample_kernel,
        grid=(1,),
        in_specs=[_const(v.shape) for v in args],
        out_specs=[_const(x1.shape), _const(buf0.shape)],
        out_shape=[jax.ShapeDtypeStruct(x1.shape, F32), jax.ShapeDtypeStruct(buf0.shape, F32)],
        compiler_params=_params(("arbitrary",)),
    )(*args)


def _block_diag(blocks):
    n, bi, bj = blocks.shape
    eye = jnp.eye(n, dtype=blocks.dtype)
    return jnp.einsum("nij,nm->nimj", blocks, eye).reshape(n * bi, n * bj)


def _layer_weights(norm_pre_mix, norm_post_mix, norm_pre_ffn, norm_post_ffn, w_in,
                   rwkv_mu, rwkv_w0, rwkv_w_up, rwkv_a0, rwkv_a_up, rwkv_g_up, rwkv_k_k, rwkv_k_a,
                   rwkv_r_k, rwkv_lnx_w, rwkv_lnx_b, rwkv_w_out,
                   lru_conv_w, lru_conv_b, lru_gx_w, lru_gx_b, lru_ga_w, lru_ga_b, lru_lambda, lru_w_out,
                   w_o, ffn_up, ffn_conv_w, ffn_conv_b, ffn_down):
    row = lambda x: x.reshape(1, -1)
    w = rwkv_w0.shape[-1]
    head_of = jnp.arange(w) // HEAD_DIM
    t_idx = jnp.arange(CHUNK)
    lo_w, lo_a = rwkv_w_up.shape[0], rwkv_a_up.shape[0]
    lora2 = jnp.zeros((lo_w + lo_a, 2 * w), F32).at[:lo_w, :w].set(rwkv_w_up).at[lo_w:, w:].set(rwkv_a_up)
    rw = dict(mu=row(rwkv_mu), w0=row(rwkv_w0), lora2=lora2.astype(BF16), a0=row(rwkv_a0),
              g_up=rwkv_g_up.astype(BF16), k_k=row(rwkv_k_k), k_a=row(rwkv_k_a), r_k=row(rwkv_r_k),
              lnx_w=row(rwkv_lnx_w), lnx_b=row(rwkv_lnx_b), w_out=rwkv_w_out.astype(BF16),
              seg=(head_of[:, None] == head_of[None, :]).astype(BF16),
              tri=(t_idx[:, None] >= t_idx[None, :]).astype(BF16))
    lw = dict(conv_w=lru_conv_w, conv_b=row(lru_conv_b),
              gate_w=jnp.concatenate([_block_diag(lru_gx_w), _block_diag(lru_ga_w)], axis=1).astype(BF16),
              gate_b=jnp.concatenate([row(lru_gx_b), row(lru_ga_b)], axis=1), lam=row(lru_lambda))
    fw = dict(g_pre=row(norm_pre_ffn), g_post=row(norm_post_ffn), up=ffn_up.astype(BF16),
              conv_w=ffn_conv_w, conv_b=row(ffn_conv_b), down=ffn_down.astype(BF16))
    return dict(g_in=row(norm_pre_mix), w_in=w_in.astype(BF16), rw=rw, lw=lw, fw=fw,
                lru_w_out=lru_w_out.astype(BF16), w_o=w_o.astype(BF16), g_mix=row(norm_post_mix))


def _splits(wts):
    w = wts["rw"]["w0"].shape[-1]
    c_rwkv = wts["rw"]["mu"].shape[-1]
    c_lru = wts["lw"]["lam"].shape[-1]
    return (c_rwkv, c_lru, wts["w_in"].shape[1] - c_rwkv - c_lru)


def _prompt_layer(x, wts):
    b, t, d = x.shape
    x2 = x.reshape(b * t, d)
    tm = min(t, 256)
    p, xb, gates = _proj(x2, wts["g_in"], wts["w_in"], _splits(wts), tm)
    oa, p_tail, wkv = _rwkv_prompt(p, wts["rw"], b, t, min(t, WKV_ROWS))
    x1, xb_tail, h_tail = _mix_prompt(x2, oa, xb, gates, wts["lw"], wts["lru_w_out"], wts["w_o"], wts["g_mix"],
                                      b, t, tm)
    y, u_tail = _ffn_prompt(x1, wts["fw"], b, t, min(t, FFN_ROWS))
    conv_w = wts["lw"]["conv_w"].shape[0]
    ffn_w = wts["fw"]["conv_w"].shape[0]
    state = (p_tail[:, SUBLANES - 1:], wkv, xb_tail[:, SUBLANES - (conv_w - 1):], h_tail[:, SUBLANES - 1],
             u_tail[:, SUBLANES - (ffn_w - 1):])
    return y.reshape(b, t, d), state


def _sample_layer(x, shift, wkv, lru_buf, h0, ffn_buf, wts):
    n, t, d = x.shape
    x2 = x.reshape(n, d)
    p, xb, gates = _proj(x2, wts["g_in"], wts["w_in"], _splits(wts), n)
    oa, wkv_t = _rwkv_sample(p, shift[:, 0, :], jnp.transpose(wkv, (1, 2, 3, 0)), wts["rw"])
    lru_bufs = [lru_buf[:, j, :] for j in range(lru_buf.shape[1])]
    h = _lru_sample(xb, lru_bufs, h0, wts["lw"])
    x1 = _mix_sample(x2, oa, h, gates, wts["lru_w_out"], wts["w_o"], wts["g_mix"])
    y, u = _ffn_sample(x1, ffn_buf[:, 0, :], ffn_buf[:, 1, :], wts["fw"])
    state = (p.reshape(n, 1, -1), jnp.transpose(wkv_t, (3, 0, 1, 2)), jnp.stack(lru_bufs[1:] + [xb], axis=1), h,
             jnp.stack([ffn_buf[:, 1, :], u], axis=1))
    return y.reshape(n, t, d), state


def kernel(x_prompt, x_sample, state_rwkv_shift, state_rwkv_wkv, state_lru_conv, state_lru_h, state_ffn_conv,
           norm_pre_mix, norm_post_mix, norm_pre_ffn, norm_post_ffn, w_in,
           rwkv_mu, rwkv_w0, rwkv_w_up, rwkv_a0, rwkv_a_up, rwkv_g_up, rwkv_k_k, rwkv_k_a,
           rwkv_r_k, rwkv_lnx_w, rwkv_lnx_b, rwkv_w_out,
           lru_conv_w, lru_conv_b, lru_gx_w, lru_gx_b, lru_ga_w, lru_ga_b, lru_lambda, lru_w_out,
           w_o, ffn_up, ffn_conv_w, ffn_conv_b, ffn_down):
    params = (norm_pre_mix, norm_post_mix, norm_pre_ffn, norm_post_ffn, w_in,
              rwkv_mu, rwkv_w0, rwkv_w_up, rwkv_a0, rwkv_a_up, rwkv_g_up, rwkv_k_k, rwkv_k_a,
              rwkv_r_k, rwkv_lnx_w, rwkv_lnx_b, rwkv_w_out,
              lru_conv_w, lru_conv_b, lru_gx_w, lru_gx_b, lru_ga_w, lru_ga_b, lru_lambda, lru_w_out,
              w_o, ffn_up, ffn_conv_w, ffn_conv_b, ffn_down)
    depth = w_in.shape[0]
    assert depth == 1 and x_sample.shape[1] == 1
    yp, ys = x_prompt, x_sample
    new_p, new_s = [], []
    for l in range(depth):
        wts = _layer_weights(*[q[l] for q in params])
        yp, st_p = _prompt_layer(yp, wts)
        ys, st_s = _sample_layer(ys, state_rwkv_shift[l], state_rwkv_wkv[l], state_lru_conv[l],
                                 state_lru_h[l], state_ffn_conv[l], wts)
        new_p.append(st_p)
        new_s.append(st_s)
    stk = lambda lst, i: jnp.stack([s[i] for s in lst], axis=0)
    return (yp, ys,
            stk(new_p, 0), stk(new_p, 1), stk(new_p, 2), stk(new_p, 3), stk(new_p, 4),
            stk(new_s, 0), stk(new_s, 1), stk(new_s, 2), stk(new_s, 3), stk(new_s, 4))
```

```python
import functools

import jax
import jax.numpy as jnp
from jax import lax
from jax.experimental import pallas as pl
from jax.experimental.pallas import tpu as pltpu

F32 = jnp.float32
BF16 = jnp.bfloat16

NORM_EPS = 1e-6
LN_X_EPS = 64e-5
LRU_C = 8.0
HEAD_DIM = 64
LANES = 128
SUBLANES = 8
CHUNK = 64
ROW_TILE = 512
WKV_ROWS = 256
FF_COLS = 256
FFN_ROWS = 256
FFN_AHEAD = 2
VMEM_LIMIT = 56 * 1024 * 1024

_NN = (((1,), (0,)), ((), ()))
_NT = (((1,), (1,)), ((), ()))
_TN = (((0,), (0,)), ((), ()))


def _rms(x, g):
    return x * lax.rsqrt(jnp.mean(x * x, axis=-1, keepdims=True) + NORM_EPS) * g


def _softplus(x):
    return jnp.maximum(x, 0.0) + jnp.log1p(jnp.exp(-jnp.abs(x)))


def _sigmoid(x):
    return 1.0 / (1.0 + jnp.exp(-x))


def _gelu_tanh(x):
    return 0.5 * x * (1.0 + jnp.tanh(0.7978845608028654 * (x + 0.044715 * (x * x * x))))


def _mm(a, b):
    return jnp.dot(a.astype(BF16), b.astype(BF16), preferred_element_type=F32)


def _split2(x):
    hi = x.astype(BF16)
    lo = (x - hi.astype(F32)).astype(BF16)
    return hi, lo


def _mmd(a, b, dims=_NN):
    return lax.dot_general(a.astype(BF16), b.astype(BF16), dims, preferred_element_type=F32)


def _segsum(x, seg):
    hi, lo = _split2(x)
    return (jnp.dot(hi, seg, preferred_element_type=F32) + jnp.dot(lo, seg, preferred_element_type=F32))


def _cumsum_rows(tri, x):
    h1 = x.astype(BF16)
    r1 = x - h1.astype(F32)
    h2 = r1.astype(BF16)
    h3 = (r1 - h2.astype(F32)).astype(BF16)
    d = lambda y: jnp.dot(tri, y, preferred_element_type=F32)
    return d(h1) + (d(h2) + d(h3))


def _shift_rows(x, tail, j):
    xs = pltpu.roll(x, j, axis=0)
    ts = pltpu.roll(tail, j, axis=0)
    row = lax.broadcasted_iota(jnp.int32, ts.shape, 0)
    head = jnp.where(row < j, ts, xs[0:SUBLANES])
    if x.shape[0] == SUBLANES:
        return head
    return jnp.concatenate([head, xs[SUBLANES:]], axis=0)


def _const(shape):
    n = len(shape)
    return pl.BlockSpec(shape, lambda *_: (0,) * n)


def _params(sem):
    return pltpu.CompilerParams(dimension_semantics=sem, vmem_limit_bytes=VMEM_LIMIT)


def _proj_kernel(x_ref, g_ref, w_ref, p_ref, xb_ref, gt_ref, *, splits):
    xn = _rms(x_ref[...], g_ref[...]).astype(BF16)
    c0, c1, c2 = splits
    p_ref[...] = jnp.dot(xn, w_ref[:, 0:c0], preferred_element_type=F32)
    xb_ref[...] = jnp.dot(xn, w_ref[:, c0:c0 + c1], preferred_element_type=F32)
    gt_ref[...] = jnp.dot(xn, w_ref[:, c0 + c1:c0 + c1 + c2], preferred_element_type=F32)


def _proj(x, g, w_in16, splits, tm):
    n, d = x.shape
    c0, c1, c2 = splits
    return pl.pallas_call(
        functools.partial(_proj_kernel, splits=splits),
        grid=(n // tm,),
        in_specs=[pl.BlockSpec((tm, d), lambda i: (i, 0)), _const((1, d)), _const(w_in16.shape)],
        out_specs=[pl.BlockSpec((tm, c0), lambda i: (i, 0)),
                   pl.BlockSpec((tm, c1), lambda i: (i, 0)),
                   pl.BlockSpec((tm, c2), lambda i: (i, 0))],
        out_shape=[jax.ShapeDtypeStruct((n, c0), F32), jax.ShapeDtypeStruct((n, c1), F32),
                   jax.ShapeDtypeStruct((n, c2), F32)],
        compiler_params=_params(("arbitrary",)),
    )(x, g, w_in16)


def _rwkv_pre(p, prev, mu, w0, lora2, a0, g_up, k_k, k_a, seg):
    w = w0.shape[-1]
    m = p + mu * (prev - p)
    r, k, v = m[:, 0:w], m[:, w:2 * w], m[:, 2 * w:3 * w]
    z = m[:, 3 * w:3 * w + LANES]
    lane = lax.broadcasted_iota(jnp.int32, z.shape, 1)
    lor = _mm(jnp.where(lane < HEAD_DIM, jnp.tanh(z), z), lora2)
    wlog = -_softplus(-(w0 + lor[:, 0:w])) - 0.5
    lw = -jnp.exp(wlog)
    a_sig = _sigmoid(a0 + lor[:, w:2 * w])
    g = _mm(_sigmoid(m[:, 3 * w + LANES:3 * w + 2 * LANES]), g_up)
    kk = k * k_k
    kkn = kk / jnp.maximum(jnp.sqrt(_segsum(kk * kk, seg)), 1e-12)
    k2 = k * (1.0 + (a_sig - 1.0) * k_a)
    return r, lw, k2, v, -kkn, kkn * a_sig, g


def _rwkv_post(y, r, k2, v, g, r_k, lnx_w, lnx_b, w_out, seg):
    inv = 1.0 / HEAD_DIM
    mean = _segsum(y, seg) * inv
    d = y - mean
    var = _segsum(d * d, seg) * inv
    yn = d * lax.rsqrt(var + LN_X_EPS) * lnx_w + lnx_b
    bonus = _segsum(r * k2 * r_k, seg) * v
    return _mm((yn + bonus) * g, w_out)


def _wkv_chunk_summaries(xs, m0, m1, strict, incl, eye2, eyew, bdmask):
    ln = xs[0][0].shape[0]
    m0b = m0 > 0.5
    swap = lambda x: pltpu.roll(x, HEAD_DIM, axis=1)
    h16 = lambda x: x.astype(BF16)
    zero16 = jnp.zeros((), BF16)
    bd = lambda x: jnp.concatenate([jnp.where(m0b, x, zero16), jnp.where(m0b, zero16, x)], axis=0)
    cat = jnp.concatenate
    rts, ats, bts, kts, bhs, khs, vs, wls = zip(*xs)
    at16 = [h16(x) for x in ats]
    v16 = [h16(x) for x in vs]
    bdv = [bd(x) for x in v16]
    gms = [_mmd(cat([bd(a), bd(h16(rt))], axis=0), cat([bt, kt], axis=0), _NT)
           for rt, a, bt, kt in zip(rts, at16, bts, kts)]
    ga0 = [jnp.where(strict, g[0:ln], 0.0) for g in gms]
    ga1 = [jnp.where(strict, g[ln:2 * ln], 0.0) for g in gms]
    gr0 = [jnp.where(incl, g[2 * ln:3 * ln], 0.0) for g in gms]
    gr1 = [jnp.where(incl, g[3 * ln:4 * ln], 0.0) for g in gms]
    n_ab = [jnp.where(m0b, x, swap(y)) for x, y in zip(ga0, ga1)]
    n_ak = [jnp.where(m0b, swap(x), y) for x, y in zip(ga0, ga1)]
    n_rb = [jnp.where(m0b, x, swap(y)) for x, y in zip(gr0, gr1)]
    n_rk = [jnp.where(m0b, swap(x), y) for x, y in zip(gr0, gr1)]
    tinv = [eye2 + n for n in n_ab]
    steps = (ln - 1).bit_length() - 1
    if steps > 0:
        npow = [h16(_mmd(n, bd(n))) for n in (h16(x) for x in n_ab)]
        for _ in range(steps - 1):
            both = [_mmd(cat([n, h16(t)], axis=0), bd(n)) for n, t in zip(npow, tinv)]
            tinv = [t + x[ln:2 * ln] for t, x in zip(tinv, both)]
            npow = [h16(x[0:ln]) for x in both]
        tinv = [t + _mmd(t, bd(n)) for t, n in zip(tinv, npow)]
    kv = [_mmd(cat([nk, nr], axis=0), b) for nk, nr, b in zip(n_ak, n_rk, bdv)]
    akv = [x[0:ln] for x in kv]
    rkv = [x[ln:2 * ln] for x in kv]
    tt = [_mmd(t, cat([bd(a), bd(h16(ak))], axis=1)) for t, a, ak in zip(tinv, at16, akv)]
    ta = [h16(x[:, 0:LANES]) for x in tt]
    tk = [h16(x[:, LANES:2 * LANES]) for x in tt]
    mc = [_mmd(cat([cat([a, k], axis=1), cat([jnp.zeros_like(v), v], axis=1)], axis=0), cat([bh, kh], axis=0), _TN)
          for a, k, v, bh, kh in zip(ta, tk, v16, bhs, khs)]
    ms = [eyew * wl + x[0:LANES] * bdmask for x, wl in zip(mc, wls)]
    cs = [x[LANES:2 * LANES] * bdmask for x in mc]
    qy = [_mmd(n, cat([bd(a), bd(k)], axis=1)) for n, a, k in zip(n_rb, ta, tk)]
    qs = [rt + x[:, 0:LANES] for rt, x in zip(rts, qy)]
    y0 = [x[:, LANES:2 * LANES] + r for x, r in zip(qy, rkv)]
    return list(zip(qs, y0, ms, cs))


def _rwkv_prompt_kernel(p_ref, mu_ref, w0_ref, lora_ref, a0_ref, gup_ref, kk_ref, ka_ref, rk_ref,
                        lnw_ref, lnb_ref, wout_ref, seg_ref, tri_ref,
                        oa_ref, tail_ref, wkv_ref, prev_scr, s_scr):
    t = pl.program_id(1)

    @pl.when(t == 0)
    def _init():
        prev_scr[...] = jnp.zeros_like(prev_scr)
        s_scr[...] = jnp.zeros_like(s_scr)

    p = p_ref[...]
    rows = p.shape[0]
    row = lax.broadcasted_iota(jnp.int32, p.shape, 0)
    prev = jnp.where(row == 0, prev_scr[SUBLANES - 1:SUBLANES, :], pltpu.roll(p, 1, axis=0))
    prev_scr[...] = p[rows - SUBLANES:rows, :]
    tail_ref[0] = p[rows - SUBLANES:rows, :]

    seg = seg_ref[...]
    r, lw, k2, v, aa, bb, g = _rwkv_pre(p, prev, mu_ref[...], w0_ref[...], lora_ref[...], a0_ref[...],
                                        gup_ref[...], kk_ref[...], ka_ref[...], seg)
    c = _cumsum_rows(tri_ref[...], lw)

    ln = CHUNK
    ri = lax.broadcasted_iota(jnp.int32, (ln, LANES), 0)
    ci = lax.broadcasted_iota(jnp.int32, (ln, LANES), 1)
    cj = jnp.where(ci < HEAD_DIM, ci, ci - HEAD_DIM)
    m0 = (ci < HEAD_DIM).astype(F32)
    m1 = 1.0 - m0
    strict = cj < ri
    incl = cj <= ri
    eye2 = (cj == ri).astype(F32)
    r2 = lax.broadcasted_iota(jnp.int32, (LANES, LANES), 0)
    c2 = lax.broadcasted_iota(jnp.int32, (LANES, LANES), 1)
    bdmask = ((r2 < HEAD_DIM) == (c2 < HEAD_DIM)).astype(F32)
    eyew = (r2 == c2).astype(F32)

    npairs = s_scr.shape[0]
    insts = []
    for ch in range(rows // ln):
        rs = slice(ch * ln, (ch + 1) * ln)
        c_, lw_ = c[rs], lw[rs]
        cl = c_[ln - 1:ln, :]
        e_n = jnp.exp(-c_)
        e_l = jnp.exp(cl - c_)
        rt = r[rs] * jnp.exp(c_)
        at = aa[rs] * jnp.exp(c_ - lw_)
        bt, kt = bb[rs] * e_n, k2[rs] * e_n
        bh, kh = bb[rs] * e_l, k2[rs] * e_l
        wl = jnp.exp(cl)
        v_ = v[rs]
        for pr in range(npairs):
            sl = slice(pr * LANES, (pr + 1) * LANES)
            insts.append((rt[:, sl], at[:, sl], bt[:, sl], kt[:, sl], bh[:, sl], kh[:, sl], v_[:, sl], wl[:, sl]))
    summaries = _wkv_chunk_summaries(insts, m0, m1, strict, incl, eye2, eyew, bdmask)

    states = [s_scr[pr] for pr in range(npairs)]
    y_rows = []
    for ch in range(rows // ln):
        ys = []
        for pr in range(npairs):
            q, y0, m, cc = summaries[ch * npairs + pr]
            ys.append(_mmd(q, states[pr], _NT) + y0)
            states[pr] = _mmd(states[pr], m) + cc
        y_rows.append(jnp.concatenate(ys, axis=1))
    for pr in range(npairs):
        s_scr[pr] = states[pr]
    y = jnp.concatenate(y_rows, axis=0)
    oa_ref[...] = _rwkv_post(y, r, k2, v, g, rk_ref[...], lnw_ref[...], lnb_ref[...], wout_ref[...], seg)

    @pl.when(t == pl.num_programs(1) - 1)
    def _fin():
        for pr in range(npairs):
            s = s_scr[pr]
            wkv_ref[0, 2 * pr] = s[0:HEAD_DIM, 0:HEAD_DIM]
            wkv_ref[0, 2 * pr + 1] = pltpu.roll(s[HEAD_DIM:LANES, :], HEAD_DIM, axis=1)[:, 0:HEAD_DIM]


def _rwkv_prompt(p, rw, b, t, tc):
    n, cp = p.shape
    w = rw["w0"].shape[-1]
    heads = w // HEAD_DIM
    nt = t // tc
    dm = rw["w_out"].shape[-1]
    tri = jnp.kron(jnp.eye(tc // CHUNK, dtype=BF16), rw["tri"])
    names = ("mu", "w0", "lora2", "a0", "g_up", "k_k", "k_a", "r_k", "lnx_w", "lnx_b", "w_out", "seg")
    return pl.pallas_call(
        _rwkv_prompt_kernel,
        grid=(b, nt),
        in_specs=[pl.BlockSpec((tc, cp), lambda i, j: (i * nt + j, 0))] + [_const(rw[k].shape) for k in names]
                 + [_const(tri.shape)],
        out_specs=[pl.BlockSpec((tc, dm), lambda i, j: (i * nt + j, 0)),
                   pl.BlockSpec((1, SUBLANES, cp), lambda i, j: (i, 0, 0)),
                   pl.BlockSpec((1, heads, HEAD_DIM, HEAD_DIM), lambda i, j: (i, 0, 0, 0))],
        out_shape=[jax.ShapeDtypeStruct((n, dm), F32),
                   jax.ShapeDtypeStruct((b, SUBLANES, cp), F32),
                   jax.ShapeDtypeStruct((b, heads, HEAD_DIM, HEAD_DIM), F32)],
        scratch_shapes=[pltpu.VMEM((SUBLANES, cp), F32), pltpu.VMEM((heads // 2, LANES, LANES), F32)],
        compiler_params=_params(("arbitrary", "arbitrary")),
    )(p, *[rw[k] for k in names], tri)


def _rwkv_sample_pre_kernel(p_ref, prev_ref, mu_ref, w0_ref, lora_ref, a0_ref, gup_ref, kk_ref, ka_ref, seg_ref,
                            rt_ref, wt_ref, kt_ref, vt_ref, at_ref, bt_ref, r_ref, k_ref, v_ref, g_ref):
    r, lw, k2, v, aa, bb, g = _rwkv_pre(p_ref[...], prev_ref[...], mu_ref[...], w0_ref[...], lora_ref[...],
                                        a0_ref[...], gup_ref[...], kk_ref[...], ka_ref[...], seg_ref[...])
    rt_ref[...] = r.T
    wt_ref[...] = jnp.exp(lw).T
    kt_ref[...] = k2.T
    vt_ref[...] = v.T
    at_ref[...] = aa.T
    bt_ref[...] = bb.T
    r_ref[...] = r
    k_ref[...] = k2
    v_ref[...] = v
    g_ref[...] = g


def _rwkv_sample_step_kernel(s_ref, r_ref, w_ref, k_ref, v_ref, a_ref, b_ref, y_ref, so_ref):
    r, w, k, a, b = r_ref[...], w_ref[...], k_ref[...], a_ref[...], b_ref[...]

    def body(i, carry):
        base = pl.multiple_of(i * SUBLANES, SUBLANES)
        vrows = v_ref[pl.ds(base, SUBLANES), :]
        ys = []
        for j in range(SUBLANES):
            s = s_ref[0, base + j]
            sa = jnp.sum(s * a, axis=0, keepdims=True)
            s_new = s * w + sa * b + vrows[j:j + 1, :] * k
            so_ref[0, base + j] = s_new
            ys.append(jnp.sum(s_new * r, axis=0, keepdims=True))
        y_ref[pl.ds(base, SUBLANES), :] = jnp.concatenate(ys, axis=0)
        return carry

    lax.fori_loop(0, s_ref.shape[1] // SUBLANES, body, 0)


def _rwkv_sample_post_kernel(yt_ref, r_ref, k_ref, v_ref, g_ref, rk_ref, lnw_ref, lnb_ref, wout_ref, seg_ref, oa_ref):
    oa_ref[...] = _rwkv_post(yt_ref[...].T, r_ref[...], k_ref[...], v_ref[...], g_ref[...], rk_ref[...],
                             lnw_ref[...], lnb_ref[...], wout_ref[...], seg_ref[...])


def _rwkv_sample(p, shift, wkv, rw):
    n, cp = p.shape
    w = rw["w0"].shape[-1]
    heads = w // HEAD_DIM
    dm = rw["w_out"].shape[-1]
    pre_names = ("mu", "w0", "lora2", "a0", "g_up", "k_k", "k_a", "seg")
    vec = jax.ShapeDtypeStruct((n, w), F32)
    vec_t = jax.ShapeDtypeStruct((w, n), F32)
    rt, wt, kt, vt, at, bt, r, k2, v, g = pl.pallas_call(
        _rwkv_sample_pre_kernel,
        grid=(1,),
        in_specs=[_const((n, cp)), _const((n, cp))] + [_const(rw[k].shape) for k in pre_names],
        out_specs=[_const((w, n))] * 6 + [_const((n, w))] * 4,
        out_shape=[vec_t] * 6 + [vec] * 4,
        compiler_params=_params(("arbitrary",)),
    )(p, shift, *[rw[k] for k in pre_names])

    head_spec = pl.BlockSpec((HEAD_DIM, n), lambda h: (h, 0))
    st_spec = pl.BlockSpec((1, HEAD_DIM, HEAD_DIM, n), lambda h: (h, 0, 0, 0))
    yt, wkv_new = pl.pallas_call(
        _rwkv_sample_step_kernel,
        grid=(heads,),
        in_specs=[st_spec] + [head_spec] * 6,
        out_specs=[head_spec, st_spec],
        out_shape=[vec_t, jax.ShapeDtypeStruct(wkv.shape, F32)],
        compiler_params=_params(("arbitrary",)),
    )(wkv, rt, wt, kt, vt, at, bt)

    post_names = ("r_k", "lnx_w", "lnx_b", "w_out", "seg")
    oa = pl.pallas_call(
        _rwkv_sample_post_kernel,
        grid=(1,),
        in_specs=[_const((w, n))] + [_const((n, w))] * 4 + [_const(rw[k].shape) for k in post_names],
        out_specs=_const((n, dm)),
        out_shape=jax.ShapeDtypeStruct((n, dm), F32),
        compiler_params=_params(("arbitrary",)),
    )(yt, r, k2, v, g, *[rw[k] for k in post_names])
    return oa, wkv_new


def _lru_gates(xc, gw, gb, lam):
    c = xc.shape[-1]
    gates = _mm(xc, gw) + gb
    gx = _sigmoid(gates[:, 0:c])
    ga = _sigmoid(gates[:, c:2 * c])
    log_a = -LRU_C * ga * _softplus(-lam)
    a = jnp.exp(log_a)
    u = jnp.sqrt(1.0 - a * a) * (gx * xc)
    return a, u


def _scan_rows(a, u, h0):
    rows, c = a.shape
    groups = rows // SUBLANES
    a3 = a.reshape(groups, SUBLANES, c)
    u3 = u.reshape(groups, SUBLANES, c)
    sub = lax.broadcasted_iota(jnp.int32, a3.shape, 1)
    k = 1
    while k < SUBLANES:
        keep = sub >= k
        u3 = u3 + a3 * jnp.where(keep, pltpu.roll(u3, k, axis=1), 0.0)
        a3 = a3 * jnp.where(keep, pltpu.roll(a3, k, axis=1), 1.0)
        k *= 2
    h = h0
    out = []
    for i in range(groups):
        hi = u3[i] + a3[i] * h
        out.append(hi)
        h = hi[SUBLANES - 1:SUBLANES, :]
    return jnp.concatenate(out, axis=0)


def _lru_prompt(x, tail, h0, cw_ref, cb_ref, gw, gb, lam):
    width = cw_ref.shape[0]
    xc = cb_ref[...] + cw_ref[width - 1:width, :] * x
    for j in range(1, width):
        xc = xc + cw_ref[width - 1 - j:width - j, :] * _shift_rows(x, tail, j)
    a, u = _lru_gates(xc, gw, gb, lam)
    return _scan_rows(a, u, h0)


def _lru_sample_kernel(x_ref, *refs):
    *buf_refs, h0_ref, cw_ref, cb_ref, gw_ref, gb_ref, lam_ref, h_ref = refs
    width = cw_ref.shape[0]
    xc = cb_ref[...] + cw_ref[width - 1:width, :] * x_ref[...]
    for j, buf_ref in enumerate(buf_refs):
        xc = xc + cw_ref[j:j + 1, :] * buf_ref[...]
    a, u = _lru_gates(xc, gw_ref[...], gb_ref[...], lam_ref[...])
    h_ref[...] = a * h0_ref[...] + u


def _lru_sample(xb, bufs, h0, lw):
    names = ("conv_w", "conv_b", "gate_w", "gate_b", "lam")
    args = (xb, *bufs, h0, *[lw[k] for k in names])
    return pl.pallas_call(
        _lru_sample_kernel,
        grid=(1,),
        in_specs=[_const(v.shape) for v in args],
        out_specs=_const(xb.shape),
        out_shape=jax.ShapeDtypeStruct(xb.shape, F32),
        compiler_params=_params(("arbitrary",)),
    )(*args)


def _mix_rows(x, oa, hs, gates, lwo, wo, g):
    d = x.shape[-1]
    ob = _mm(hs, lwo)
    mix = _mm(_sigmoid(gates[:, 0:d]) * oa + _sigmoid(gates[:, d:2 * d]) * ob, wo)
    return x + _rms(mix, g)


def _mix_sample_kernel(x_ref, oa_ref, hs_ref, gt_ref, lwo_ref, wo_ref, g_ref, o_ref):
    o_ref[...] = _mix_rows(x_ref[...], oa_ref[...], hs_ref[...], gt_ref[...], lwo_ref[...], wo_ref[...], g_ref[...])


def _mix_sample(x, oa, hs, gates, lwo16, wo16, g):
    args = (x, oa, hs, gates, lwo16, wo16, g)
    return pl.pallas_call(
        _mix_sample_kernel,
        grid=(1,),
        in_specs=[_const(v.shape) for v in args],
        out_specs=_const(x.shape),
        out_shape=jax.ShapeDtypeStruct(x.shape, F32),
        compiler_params=_params(("arbitrary",)),
    )(*args)


def _mix_prompt_kernel(x_ref, oa_ref, xb_ref, gt_ref, cw_ref, cb_ref, gw_ref, gb_ref, lam_ref, lwo_ref, wo_ref, g_ref,
                       o_ref, xtail_ref, htail_ref, xtail_scr, h_scr):
    @pl.when(pl.program_id(1) == 0)
    def _init():
        xtail_scr[...] = jnp.zeros_like(xtail_scr)
        h_scr[...] = jnp.zeros_like(h_scr)

    xb = xb_ref[...]
    rows = xb.shape[0]
    hs = _lru_prompt(xb, xtail_scr[...], h_scr[SUBLANES - 1:SUBLANES, :], cw_ref, cb_ref,
                     gw_ref[...], gb_ref[...], lam_ref[...])
    xtail_scr[...] = xb[rows - SUBLANES:rows, :]
    xtail_ref[0] = xb[rows - SUBLANES:rows, :]
    h_scr[...] = hs[rows - SUBLANES:rows, :]
    htail_ref[0] = hs[rows - SUBLANES:rows, :]
    o_ref[...] = _mix_rows(x_ref[...], oa_ref[...], hs, gt_ref[...], lwo_ref[...], wo_ref[...], g_ref[...])


def _mix_prompt(x, oa, xb, gates, lw, lwo16, wo16, g, b, t, tm):
    n, d = x.shape
    c = xb.shape[1]
    nt = t // tm
    names = ("conv_w", "conv_b", "gate_w", "gate_b", "lam")
    consts = [lw[k] for k in names] + [lwo16, wo16, g]
    rows = lambda w: pl.BlockSpec((tm, w), lambda i, j: (i * nt + j, 0))
    tail = pl.BlockSpec((1, SUBLANES, c), lambda i, j: (i, 0, 0))
    return pl.pallas_call(
        _mix_prompt_kernel,
        grid=(b, nt),
        in_specs=[rows(d), rows(d), rows(c), rows(2 * d)] + [_const(v.shape) for v in consts],
        out_specs=[rows(d), tail, tail],
        out_shape=[jax.ShapeDtypeStruct((n, d), F32), jax.ShapeDtypeStruct((b, SUBLANES, c), F32),
                   jax.ShapeDtypeStruct((b, SUBLANES, c), F32)],
        scratch_shapes=[pltpu.VMEM((SUBLANES, c), F32), pltpu.VMEM((SUBLANES, c), F32)],
        compiler_params=_params(("arbitrary", "arbitrary")),
    )(x, oa, xb, gates, *consts)


def _ffn_body(x1, gpre, gpost, up_ref, cw_ref, cb_ref, down_ref, hist, keep):
    dff = down_ref.shape[0]
    hn = _rms(x1, gpre).astype(BF16)
    starts = list(range(0, dff, FF_COLS))

    def up(c0):
        return [(slice(off, off + FF_COLS), jnp.dot(hn, up_ref[:, off:off + FF_COLS], preferred_element_type=F32))
                for off in (c0, dff + c0)]

    ups = [up(c0) for c0 in starts[:FFN_AHEAD]]
    f = jnp.zeros(x1.shape, F32)
    for i, c0 in enumerate(starts):
        if i + FFN_AHEAD < len(starts):
            ups.append(up(starts[i + FFN_AHEAD]))
        halves = []
        for cols, u in ups[i]:
            u1, u2 = hist(u, cols)
            halves.append(cb_ref[:, cols] + cw_ref[2:3, cols] * u + cw_ref[1:2, cols] * u1 + cw_ref[0:1, cols] * u2)
            keep(u, cols)
        act = _gelu_tanh(halves[0]) * halves[1]
        f = f + jnp.dot(act.astype(BF16), down_ref[c0:c0 + FF_COLS, :], preferred_element_type=F32)
    return x1 + _rms(f, gpost)


def _ffn_prompt_kernel(x_ref, gpre_ref, gpost_ref, up_ref, cw_ref, cb_ref, down_ref, y_ref, tail_ref, tail_scr):
    @pl.when(pl.program_id(1) == 0)
    def _init():
        tail_scr[...] = jnp.zeros_like(tail_scr)

    rows = x_ref.shape[0]

    def hist(u, cols):
        tail = tail_scr[:, cols]
        return _shift_rows(u, tail, 1), _shift_rows(u, tail, 2)

    def keep(u, cols):
        tail_scr[:, cols] = u[rows - SUBLANES:rows, :]
        tail_ref[0, :, cols] = u[rows - SUBLANES:rows, :]

    y_ref[...] = _ffn_body(x_ref[...], gpre_ref[...], gpost_ref[...], up_ref, cw_ref, cb_ref, down_ref, hist, keep)


def _ffn_prompt(x1, fw, b, t, tm):
    n, d = x1.shape
    nt = t // tm
    dff2 = fw["up"].shape[1]
    names = ("g_pre", "g_post", "up", "conv_w", "conv_b", "down")
    rows = pl.BlockSpec((tm, d), lambda i, j: (i * nt + j, 0))
    return pl.pallas_call(
        _ffn_prompt_kernel,
        grid=(b, nt),
        in_specs=[rows] + [_const(fw[k].shape) for k in names],
        out_specs=[rows, pl.BlockSpec((1, SUBLANES, dff2), lambda i, j: (i, 0, 0))],
        out_shape=[jax.ShapeDtypeStruct((n, d), F32), jax.ShapeDtypeStruct((b, SUBLANES, dff2), F32)],
        scratch_shapes=[pltpu.VMEM((SUBLANES, dff2), F32)],
        compiler_params=_params(("arbitrary", "arbitrary")),
    )(x1, *[fw[k] for k in names])


def _ffn_sample_kernel(x_ref, b0_ref, b1_ref, gpre_ref, gpost_ref, up_ref, cw_ref, cb_ref, down_ref, y_ref, u_ref):
    def hist(u, cols):
        return b1_ref[:, cols], b0_ref[:, cols]

    def keep(u, cols):
        u_ref[:, cols] = u

    y_ref[...] = _ffn_body(x_ref[...], gpre_ref[...], gpost_ref[...], up_ref, cw_ref, cb_ref, down_ref, hist, keep)


def _ffn_sample(x1, buf0, buf1, fw):
    names = ("g_pre", "g_post", "up", "conv_w", "conv_b", "down")
    args = (x1, buf0, buf1, *[fw[k] for k in names])
    return pl.pallas_call(
        _ffn_sample_kernel,
        grid=(1,),
        in_specs=[_const(v.shape) for v in args],
        out_specs=[_const(x1.shape), _const(buf0.shape)],
        out_shape=[jax.ShapeDtypeStruct(x1.shape, F32), jax.ShapeDtypeStruct(buf0.shape, F32)],
        compiler_params=_params(("arbitrary",)),
    )(*args)


def _block_diag(blocks):
    n, bi, bj = blocks.shape
    eye = jnp.eye(n, dtype=blocks.dtype)
    return jnp.einsum("nij,nm->nimj", blocks, eye).reshape(n * bi, n * bj)


def _layer_weights(norm_pre_mix, norm_post_mix, norm_pre_ffn, norm_post_ffn, w_in,
                   rwkv_mu, rwkv_w0, rwkv_w_up, rwkv_a0, rwkv_a_up, rwkv_g_up, rwkv_k_k, rwkv_k_a,
                   rwkv_r_k, rwkv_lnx_w, rwkv_lnx_b, rwkv_w_out,
                   lru_conv_w, lru_conv_b, lru_gx_w, lru_gx_b, lru_ga_w, lru_ga_b, lru_lambda, lru_w_out,
                   w_o, ffn_up, ffn_conv_w, ffn_conv_b, ffn_down):
    row = lambda x: x.reshape(1, -1)
    w = rwkv_w0.shape[-1]
    head_of = jnp.arange(w) // HEAD_DIM
    t_idx = jnp.arange(CHUNK)
    lo_w, lo_a = rwkv_w_up.shape[0], rwkv_a_up.shape[0]
    lora2 = jnp.zeros((lo_w + lo_a, 2 * w), F32).at[:lo_w, :w].set(rwkv_w_up).at[lo_w:, w:].set(rwkv_a_up)
    rw = dict(mu=row(rwkv_mu), w0=row(rwkv_w0), lora2=lora2.astype(BF16), a0=row(rwkv_a0),
              g_up=rwkv_g_up.astype(BF16), k_k=row(rwkv_k_k), k_a=row(rwkv_k_a), r_k=row(rwkv_r_k),
              lnx_w=row(rwkv_lnx_w), lnx_b=row(rwkv_lnx_b), w_out=rwkv_w_out.astype(BF16),
              seg=(head_of[:, None] == head_of[None, :]).astype(BF16),
              tri=(t_idx[:, None] >= t_idx[None, :]).astype(BF16))
    lw = dict(conv_w=lru_conv_w, conv_b=row(lru_conv_b),
              gate_w=jnp.concatenate([_block_diag(lru_gx_w), _block_diag(lru_ga_w)], axis=1).astype(BF16),
              gate_b=jnp.concatenate([row(lru_gx_b), row(lru_ga_b)], axis=1), lam=row(lru_lambda))
    fw = dict(g_pre=row(norm_pre_ffn), g_post=row(norm_post_ffn), up=ffn_up.astype(BF16),
              conv_w=ffn_conv_w, conv_b=row(ffn_conv_b), down=ffn_down.astype(BF16))
    return dict(g_in=row(norm_pre_mix), w_in=w_in.astype(BF16), rw=rw, lw=lw, fw=fw,
                lru_w_out=lru_w_out.astype(BF16), w_o=w_o.astype(BF16), g_mix=row(norm_post_mix))


def _splits(wts):
    w = wts["rw"]["w0"].shape[-1]
    c_rwkv = wts["rw"]["mu"].shape[-1]
    c_lru = wts["lw"]["lam"].shape[-1]
    return (c_rwkv, c_lru, wts["w_in"].shape[1] - c_rwkv - c_lru)


def _prompt_layer(x, wts):
    b, t, d = x.shape
    x2 = x.reshape(b * t, d)
    tm = min(t, ROW_TILE)
    p, xb, gates = _proj(x2, wts["g_in"], wts["w_in"], _splits(wts), tm)
    oa, p_tail, wkv = _rwkv_prompt(p, wts["rw"], b, t, min(t, WKV_ROWS))
    x1, xb_tail, h_tail = _mix_prompt(x2, oa, xb, gates, wts["lw"], wts["lru_w_out"], wts["w_o"], wts["g_mix"],
                                      b, t, tm)
    y, u_tail = _ffn_prompt(x1, wts["fw"], b, t, min(t, FFN_ROWS))
    conv_w = wts["lw"]["conv_w"].shape[0]
    ffn_w = wts["fw"]["conv_w"].shape[0]
    state = (p_tail[:, SUBLANES - 1:], wkv, xb_tail[:, SUBLANES - (conv_w - 1):], h_tail[:, SUBLANES - 1],
             u_tail[:, SUBLANES - (ffn_w - 1):])
    return y.reshape(b, t, d), state


def _sample_layer(x, shift, wkv, lru_buf, h0, ffn_buf, wts):
    n, t, d = x.shape
    x2 = x.reshape(n, d)
    p, xb, gates = _proj(x2, wts["g_in"], wts["w_in"], _splits(wts), n)
    oa, wkv_t = _rwkv_sample(p, shift[:, 0, :], jnp.transpose(wkv, (1, 2, 3, 0)), wts["rw"])
    lru_bufs = [lru_buf[:, j, :] for j in range(lru_buf.shape[1])]
    h = _lru_sample(xb, lru_bufs, h0, wts["lw"])
    x1 = _mix_sample(x2, oa, h, gates, wts["lru_w_out"], wts["w_o"], wts["g_mix"])
    y, u = _ffn_sample(x1, ffn_buf[:, 0, :], ffn_buf[:, 1, :], wts["fw"])
    state = (p.reshape(n, 1, -1), jnp.transpose(wkv_t, (3, 0, 1, 2)), jnp.stack(lru_bufs[1:] + [xb], axis=1), h,
             jnp.stack([ffn_buf[:, 1, :], u], axis=1))
    return y.reshape(n, t, d), state


def kernel(x_prompt, x_sample, state_rwkv_shift, state_rwkv_wkv, state_lru_conv, state_lru_h, state_ffn_conv,
           norm_pre_mix, norm_post_mix, norm_pre_ffn, norm_post_ffn, w_in,
           rwkv_mu, rwkv_w0, rwkv_w_up, rwkv_a0, rwkv_a_up, rwkv_g_up, rwkv_k_k, rwkv_k_a,
           rwkv_r_k, rwkv_lnx_w, rwkv_lnx_b, rwkv_w_out,
           lru_conv_w, lru_conv_b, lru_gx_w, lru_gx_b, lru_ga_w, lru_ga_b, lru_lambda, lru_w_out,
           w_o, ffn_up, ffn_conv_w, ffn_conv_b, ffn_down):
    params = (norm_pre_mix, norm_post_mix, norm_pre_ffn, norm_post_ffn, w_in,
              rwkv_mu, rwkv_w0, rwkv_w_up, rwkv_a0, rwkv_a_up, rwkv_g_up, rwkv_k_k, rwkv_k_a,
              rwkv_r_k, rwkv_lnx_w, rwkv_lnx_b, rwkv_w_out,
              lru_conv_w, lru_conv_b, lru_gx_w, lru_gx_b, lru_ga_w, lru_ga_b, lru_lambda, lru_w_out,
              w_o, ffn_up, ffn_conv_w, ffn_conv_b, ffn_down)
    depth = w_in.shape[0]
    assert depth == 1 and x_sample.shape[1] == 1
    yp, ys = x_prompt, x_sample
    new_p, new_s = [], []
    for l in range(depth):
        wts = _layer_weights(*[q[l] for q in params])
        yp, st_p = _prompt_layer(yp, wts)
        ys, st_s = _sample_layer(ys, state_rwkv_shift[l], state_rwkv_wkv[l], state_lru_conv[l],
                                 state_lru_h[l], state_ffn_conv[l], wts)
        new_p.append(st_p)
        new_s.append(st_s)
    stk = lambda lst, i: jnp.stack([s[i] for s in lst], axis=0)
    return (yp, ys,
            stk(new_p, 0), stk(new_p, 1), stk(new_p, 2), stk(new_p, 3), stk(new_p, 4),
            stk(new_s, 0), stk(new_s, 1), stk(new_s, 2), stk(new_s, 3), stk(new_s, 4))
```

```python
import functools
import math

import jax
import jax.numpy as jnp
from jax import lax
from jax.experimental import pallas as pl
from jax.experimental.pallas import tpu as pltpu

F32 = jnp.float32
BF16 = jnp.bfloat16

NORM_EPS = 1e-6
LN_X_EPS = 64e-5
LRU_C = 8.0
DECAY_SCALE = math.exp(-0.5)
HEAD_DIM = 64
LANES = 128
SUBLANES = 8
CHUNK = 64
ROW_TILE = 512
WKV_ROWS = 256
FF_COLS = 256
FFN_ROWS = 256
FFN_AHEAD = 3
VMEM_LIMIT = 56 * 1024 * 1024

_NN = (((1,), (0,)), ((), ()))
_NT = (((1,), (1,)), ((), ()))
_TN = (((0,), (0,)), ((), ()))


def _rms(x, g):
    return x * lax.rsqrt(jnp.mean(x * x, axis=-1, keepdims=True) + NORM_EPS) * g


def _softplus(x):
    return jnp.maximum(x, 0.0) + jnp.log1p(jnp.exp(-jnp.abs(x)))


def _sigmoid(x):
    return 1.0 / (1.0 + jnp.exp(-x))


def _gelu_tanh(x):
    return 0.5 * x * (1.0 + jnp.tanh(0.7978845608028654 * (x + 0.044715 * (x * x * x))))


def _mm(a, b):
    return jnp.dot(a.astype(BF16), b.astype(BF16), preferred_element_type=F32)


def _mmd(a, b, dims=_NN):
    return lax.dot_general(a.astype(BF16), b.astype(BF16), dims, preferred_element_type=F32)


def _segsum(x, seg):
    return jnp.dot(x.astype(BF16), seg, preferred_element_type=F32)


def _cumsum_rows(tri, x):
    h1 = x.astype(BF16)
    r1 = x - h1.astype(F32)
    h2 = r1.astype(BF16)
    h3 = (r1 - h2.astype(F32)).astype(BF16)
    d = lambda y: jnp.dot(tri, y, preferred_element_type=F32)
    return d(h1) + (d(h2) + d(h3))


def _shift_rows(x, tail, j):
    xs = pltpu.roll(x, j, axis=0)
    ts = pltpu.roll(tail, j, axis=0)
    row = lax.broadcasted_iota(jnp.int32, ts.shape, 0)
    head = jnp.where(row < j, ts, xs[0:SUBLANES])
    if x.shape[0] == SUBLANES:
        return head
    return jnp.concatenate([head, xs[SUBLANES:]], axis=0)


def _const(shape):
    n = len(shape)
    return pl.BlockSpec(shape, lambda *_: (0,) * n)


def _params(sem):
    return pltpu.CompilerParams(dimension_semantics=sem, vmem_limit_bytes=VMEM_LIMIT)


def _proj_kernel(x_ref, g_ref, w_ref, p_ref, xb_ref, gt_ref, *, splits):
    xn = _rms(x_ref[...], g_ref[...]).astype(BF16)
    c0, c1, c2 = splits
    p_ref[...] = jnp.dot(xn, w_ref[:, 0:c0], preferred_element_type=F32)
    xb_ref[...] = jnp.dot(xn, w_ref[:, c0:c0 + c1], preferred_element_type=F32)
    gt_ref[...] = jnp.dot(xn, w_ref[:, c0 + c1:c0 + c1 + c2], preferred_element_type=F32)


def _proj(x, g, w_in16, splits, tm):
    n, d = x.shape
    c0, c1, c2 = splits
    return pl.pallas_call(
        functools.partial(_proj_kernel, splits=splits),
        grid=(n // tm,),
        in_specs=[pl.BlockSpec((tm, d), lambda i: (i, 0)), _const((1, d)), _const(w_in16.shape)],
        out_specs=[pl.BlockSpec((tm, c0), lambda i: (i, 0)),
                   pl.BlockSpec((tm, c1), lambda i: (i, 0)),
                   pl.BlockSpec((tm, c2), lambda i: (i, 0))],
        out_shape=[jax.ShapeDtypeStruct((n, c0), F32), jax.ShapeDtypeStruct((n, c1), F32),
                   jax.ShapeDtypeStruct((n, c2), F32)],
        compiler_params=_params(("arbitrary",)),
    )(x, g, w_in16)


def _rwkv_pre(p, prev, mu, w0, lora2, a0, g_up, k_k, k_a, seg):
    w = w0.shape[-1]
    m = p + mu * (prev - p)
    r, k, v = m[:, 0:w], m[:, w:2 * w], m[:, 2 * w:3 * w]
    z = m[:, 3 * w:3 * w + LANES]
    lane = lax.broadcasted_iota(jnp.int32, z.shape, 1)
    lor = _mm(jnp.where(lane < HEAD_DIM, jnp.tanh(z), z), lora2)
    lw = -DECAY_SCALE * _sigmoid(w0 + lor[:, 0:w])
    a_sig = _sigmoid(a0 + lor[:, w:2 * w])
    g = _mm(_sigmoid(m[:, 3 * w + LANES:3 * w + 2 * LANES]), g_up)
    kk = k * k_k
    kkn = kk * lax.rsqrt(jnp.maximum(_segsum(kk * kk, seg), 1e-24))
    k2 = k * (1.0 + (a_sig - 1.0) * k_a)
    return r, lw, k2, v, -kkn, kkn * a_sig, g


def _rwkv_post(y, r, k2, v, g, r_k, lnx_w, lnx_b, w_out, seg):
    inv = 1.0 / HEAD_DIM
    mean = _segsum(y, seg) * inv
    d = y - mean
    var = _segsum(d * d, seg) * inv
    yn = d * lax.rsqrt(var + LN_X_EPS) * lnx_w + lnx_b
    bonus = _segsum(r * k2 * r_k, seg) * v
    return _mm((yn + bonus) * g, w_out)


def _wkv_chunk_summaries(xs, m0, m1, strict, incl, eye2, eyew, bdmask):
    ln = xs[0][0].shape[0]
    m0b = m0 > 0.5
    swap = lambda x: pltpu.roll(x, HEAD_DIM, axis=1)
    h16 = lambda x: x.astype(BF16)
    zero16 = jnp.zeros((), BF16)
    bd = lambda x: jnp.concatenate([jnp.where(m0b, x, zero16), jnp.where(m0b, zero16, x)], axis=0)
    cat = jnp.concatenate
    rts, ats, bts, kts, bhs, khs, vs, wls = zip(*xs)
    at16 = [h16(x) for x in ats]
    v16 = [h16(x) for x in vs]
    bdv = [bd(x) for x in v16]
    gms = [_mmd(cat([bd(a), bd(h16(rt))], axis=0), cat([bt, kt], axis=0), _NT)
           for rt, a, bt, kt in zip(rts, at16, bts, kts)]
    ga0 = [jnp.where(strict, g[0:ln], 0.0) for g in gms]
    ga1 = [jnp.where(strict, g[ln:2 * ln], 0.0) for g in gms]
    gr0 = [jnp.where(incl, g[2 * ln:3 * ln], 0.0) for g in gms]
    gr1 = [jnp.where(incl, g[3 * ln:4 * ln], 0.0) for g in gms]
    n_ab = [jnp.where(m0b, x, swap(y)) for x, y in zip(ga0, ga1)]
    n_ak = [jnp.where(m0b, swap(x), y) for x, y in zip(ga0, ga1)]
    n_rb = [jnp.where(m0b, x, swap(y)) for x, y in zip(gr0, gr1)]
    n_rk = [jnp.where(m0b, swap(x), y) for x, y in zip(gr0, gr1)]
    tinv = [eye2 + n for n in n_ab]
    steps = (ln - 1).bit_length() - 1
    if steps > 0:
        npow = [h16(_mmd(n, bd(n))) for n in (h16(x) for x in n_ab)]
        for _ in range(steps - 1):
            both = [_mmd(cat([n, h16(t)], axis=0), bd(n)) for n, t in zip(npow, tinv)]
            tinv = [t + x[ln:2 * ln] for t, x in zip(tinv, both)]
            npow = [h16(x[0:ln]) for x in both]
        tinv = [t + _mmd(t, bd(n)) for t, n in zip(tinv, npow)]
    kv = [_mmd(cat([nk, nr], axis=0), b) for nk, nr, b in zip(n_ak, n_rk, bdv)]
    akv = [x[0:ln] for x in kv]
    rkv = [x[ln:2 * ln] for x in kv]
    tt = [_mmd(t, cat([bd(a), bd(h16(ak))], axis=1)) for t, a, ak in zip(tinv, at16, akv)]
    ta = [h16(x[:, 0:LANES]) for x in tt]
    tk = [h16(x[:, LANES:2 * LANES]) for x in tt]
    mc = [_mmd(cat([cat([a, k], axis=1), cat([jnp.zeros_like(v), v], axis=1)], axis=0), cat([bh, kh], axis=0), _TN)
          for a, k, v, bh, kh in zip(ta, tk, v16, bhs, khs)]
    ms = [eyew * wl + x[0:LANES] * bdmask for x, wl in zip(mc, wls)]
    cs = [x[LANES:2 * LANES] * bdmask for x in mc]
    qy = [_mmd(n, cat([bd(a), bd(k)], axis=1)) for n, a, k in zip(n_rb, ta, tk)]
    qs = [rt + x[:, 0:LANES] for rt, x in zip(rts, qy)]
    y0 = [x[:, LANES:2 * LANES] + r for x, r in zip(qy, rkv)]
    return list(zip(qs, y0, ms, cs))


def _rwkv_prompt_kernel(p_ref, mu_ref, w0_ref, lora_ref, a0_ref, gup_ref, kk_ref, ka_ref, rk_ref,
                        lnw_ref, lnb_ref, wout_ref, seg_ref, tri_ref,
                        oa_ref, tail_ref, wkv_ref, prev_scr, s_scr):
    t = pl.program_id(1)

    @pl.when(t == 0)
    def _init():
        prev_scr[...] = jnp.zeros_like(prev_scr)
        s_scr[...] = jnp.zeros_like(s_scr)

    p = p_ref[...]
    rows = p.shape[0]
    row = lax.broadcasted_iota(jnp.int32, p.shape, 0)
    prev = jnp.where(row == 0, prev_scr[SUBLANES - 1:SUBLANES, :], pltpu.roll(p, 1, axis=0))
    prev_scr[...] = p[rows - SUBLANES:rows, :]
    tail_ref[0] = p[rows - SUBLANES:rows, :]

    seg = seg_ref[...]
    r, lw, k2, v, aa, bb, g = _rwkv_pre(p, prev, mu_ref[...], w0_ref[...], lora_ref[...], a0_ref[...],
                                        gup_ref[...], kk_ref[...], ka_ref[...], seg)
    c = _cumsum_rows(tri_ref[...], lw)

    ln = CHUNK
    ri = lax.broadcasted_iota(jnp.int32, (ln, LANES), 0)
    ci = lax.broadcasted_iota(jnp.int32, (ln, LANES), 1)
    cj = jnp.where(ci < HEAD_DIM, ci, ci - HEAD_DIM)
    m0 = (ci < HEAD_DIM).astype(F32)
    m1 = 1.0 - m0
    strict = cj < ri
    incl = cj <= ri
    eye2 = (cj == ri).astype(F32)
    r2 = lax.broadcasted_iota(jnp.int32, (LANES, LANES), 0)
    c2 = lax.broadcasted_iota(jnp.int32, (LANES, LANES), 1)
    bdmask = ((r2 < HEAD_DIM) == (c2 < HEAD_DIM)).astype(F32)
    eyew = (r2 == c2).astype(F32)

    npairs = s_scr.shape[0]
    insts = []
    for ch in range(rows // ln):
        rs = slice(ch * ln, (ch + 1) * ln)
        c_, lw_ = c[rs], lw[rs]
        cl = c_[ln - 1:ln, :]
        e_n = jnp.exp(-c_)
        e_l = jnp.exp(cl - c_)
        rt = r[rs] * jnp.exp(c_)
        at = aa[rs] * jnp.exp(c_ - lw_)
        bt, kt = bb[rs] * e_n, k2[rs] * e_n
        bh, kh = bb[rs] * e_l, k2[rs] * e_l
        wl = jnp.exp(cl)
        v_ = v[rs]
        for pr in range(npairs):
            sl = slice(pr * LANES, (pr + 1) * LANES)
            insts.append((rt[:, sl], at[:, sl], bt[:, sl], kt[:, sl], bh[:, sl], kh[:, sl], v_[:, sl], wl[:, sl]))
    summaries = _wkv_chunk_summaries(insts, m0, m1, strict, incl, eye2, eyew, bdmask)

    states = [s_scr[pr] for pr in range(npairs)]
    y_rows = []
    for ch in range(rows // ln):
        ys = []
        for pr in range(npairs):
            q, y0, m, cc = summaries[ch * npairs + pr]
            ys.append(_mmd(q, states[pr], _NT) + y0)
            states[pr] = _mmd(states[pr], m) + cc
        y_rows.append(jnp.concatenate(ys, axis=1))
    for pr in range(npairs):
        s_scr[pr] = states[pr]
    y = jnp.concatenate(y_rows, axis=0)
    oa_ref[...] = _rwkv_post(y, r, k2, v, g, rk_ref[...], lnw_ref[...], lnb_ref[...], wout_ref[...], seg)

    @pl.when(t == pl.num_programs(1) - 1)
    def _fin():
        for pr in range(npairs):
            s = s_scr[pr]
            wkv_ref[0, 2 * pr] = s[0:HEAD_DIM, 0:HEAD_DIM]
            wkv_ref[0, 2 * pr + 1] = pltpu.roll(s[HEAD_DIM:LANES, :], HEAD_DIM, axis=1)[:, 0:HEAD_DIM]


def _rwkv_prompt(p, rw, b, t, tc):
    n, cp = p.shape
    w = rw["w0"].shape[-1]
    heads = w // HEAD_DIM
    nt = t // tc
    dm = rw["w_out"].shape[-1]
    tri = jnp.kron(jnp.eye(tc // CHUNK, dtype=BF16), rw["tri"])
    names = ("mu", "w0", "lora2", "a0", "g_up", "k_k", "k_a", "r_k", "lnx_w", "lnx_b", "w_out", "seg")
    return pl.pallas_call(
        _rwkv_prompt_kernel,
        grid=(b, nt),
        in_specs=[pl.BlockSpec((tc, cp), lambda i, j: (i * nt + j, 0))] + [_const(rw[k].shape) for k in names]
                 + [_const(tri.shape)],
        out_specs=[pl.BlockSpec((tc, dm), lambda i, j: (i * nt + j, 0)),
                   pl.BlockSpec((1, SUBLANES, cp), lambda i, j: (i, 0, 0)),
                   pl.BlockSpec((1, heads, HEAD_DIM, HEAD_DIM), lambda i, j: (i, 0, 0, 0))],
        out_shape=[jax.ShapeDtypeStruct((n, dm), F32),
                   jax.ShapeDtypeStruct((b, SUBLANES, cp), F32),
                   jax.ShapeDtypeStruct((b, heads, HEAD_DIM, HEAD_DIM), F32)],
        scratch_shapes=[pltpu.VMEM((SUBLANES, cp), F32), pltpu.VMEM((heads // 2, LANES, LANES), F32)],
        compiler_params=_params(("arbitrary", "arbitrary")),
    )(p, *[rw[k] for k in names], tri)


def _rwkv_sample_pre_kernel(p_ref, prev_ref, mu_ref, w0_ref, lora_ref, a0_ref, gup_ref, kk_ref, ka_ref, seg_ref,
                            rt_ref, wt_ref, kt_ref, vt_ref, at_ref, bt_ref, r_ref, k_ref, v_ref, g_ref):
    r, lw, k2, v, aa, bb, g = _rwkv_pre(p_ref[...], prev_ref[...], mu_ref[...], w0_ref[...], lora_ref[...],
                                        a0_ref[...], gup_ref[...], kk_ref[...], ka_ref[...], seg_ref[...])
    rt_ref[...] = r.T
    wt_ref[...] = jnp.exp(lw).T
    kt_ref[...] = k2.T
    vt_ref[...] = v.T
    at_ref[...] = aa.T
    bt_ref[...] = bb.T
    r_ref[...] = r
    k_ref[...] = k2
    v_ref[...] = v
    g_ref[...] = g


def _rwkv_sample_step_kernel(s_ref, r_ref, w_ref, k_ref, v_ref, a_ref, b_ref, y_ref, so_ref):
    r, w, k, a, b = r_ref[...], w_ref[...], k_ref[...], a_ref[...], b_ref[...]

    def body(i, carry):
        base = pl.multiple_of(i * SUBLANES, SUBLANES)
        vrows = v_ref[pl.ds(base, SUBLANES), :]
        ys = []
        for j in range(SUBLANES):
            s = s_ref[0, base + j]
            sa = jnp.sum(s * a, axis=0, keepdims=True)
            s_new = s * w + sa * b + vrows[j:j + 1, :] * k
            so_ref[0, base + j] = s_new
            ys.append(jnp.sum(s_new * r, axis=0, keepdims=True))
        y_ref[pl.ds(base, SUBLANES), :] = jnp.concatenate(ys, axis=0)
        return carry

    lax.fori_loop(0, s_ref.shape[1] // SUBLANES, body, 0)


def _rwkv_sample_post_kernel(yt_ref, r_ref, k_ref, v_ref, g_ref, rk_ref, lnw_ref, lnb_ref, wout_ref, seg_ref, oa_ref):
    oa_ref[...] = _rwkv_post(yt_ref[...].T, r_ref[...], k_ref[...], v_ref[...], g_ref[...], rk_ref[...],
                             lnw_ref[...], lnb_ref[...], wout_ref[...], seg_ref[...])


def _rwkv_sample(p, shift, wkv, rw):
    n, cp = p.shape
    w = rw["w0"].shape[-1]
    heads = w // HEAD_DIM
    dm = rw["w_out"].shape[-1]
    pre_names = ("mu", "w0", "lora2", "a0", "g_up", "k_k", "k_a", "seg")
    vec = jax.ShapeDtypeStruct((n, w), F32)
    vec_t = jax.ShapeDtypeStruct((w, n), F32)
    rt, wt, kt, vt, at, bt, r, k2, v, g = pl.pallas_call(
        _rwkv_sample_pre_kernel,
        grid=(1,),
        in_specs=[_const((n, cp)), _const((n, cp))] + [_const(rw[k].shape) for k in pre_names],
        out_specs=[_const((w, n))] * 6 + [_const((n, w))] * 4,
        out_shape=[vec_t] * 6 + [vec] * 4,
        compiler_params=_params(("arbitrary",)),
    )(p, shift, *[rw[k] for k in pre_names])

    head_spec = pl.BlockSpec((HEAD_DIM, n), lambda h: (h, 0))
    st_spec = pl.BlockSpec((1, HEAD_DIM, HEAD_DIM, n), lambda h: (h, 0, 0, 0))
    yt, wkv_new = pl.pallas_call(
        _rwkv_sample_step_kernel,
        grid=(heads,),
        in_specs=[st_spec] + [head_spec] * 6,
        out_specs=[head_spec, st_spec],
        out_shape=[vec_t, jax.ShapeDtypeStruct(wkv.shape, F32)],
        compiler_params=_params(("arbitrary",)),
    )(wkv, rt, wt, kt, vt, at, bt)

    post_names = ("r_k", "lnx_w", "lnx_b", "w_out", "seg")
    oa = pl.pallas_call(
        _rwkv_sample_post_kernel,
        grid=(1,),
        in_specs=[_const((w, n))] + [_const((n, w))] * 4 + [_const(rw[k].shape) for k in post_names],
        out_specs=_const((n, dm)),
        out_shape=jax.ShapeDtypeStruct((n, dm), F32),
        compiler_params=_params(("arbitrary",)),
    )(yt, r, k2, v, g, *[rw[k] for k in post_names])
    return oa, wkv_new


def _lru_gates(xc, gw, gb, lam):
    c = xc.shape[-1]
    gates = _mm(xc, gw) + gb
    gx = _sigmoid(gates[:, 0:c])
    ga = _sigmoid(gates[:, c:2 * c])
    log_a = -LRU_C * ga * _softplus(-lam)
    a = jnp.exp(log_a)
    u = jnp.sqrt(1.0 - a * a) * (gx * xc)
    return a, u


def _scan_rows(a, u, h0):
    rows, c = a.shape
    groups = rows // SUBLANES
    a3 = a.reshape(groups, SUBLANES, c)
    u3 = u.reshape(groups, SUBLANES, c)
    sub = lax.broadcasted_iota(jnp.int32, a3.shape, 1)
    k = 1
    while k < SUBLANES:
        keep = sub >= k
        u3 = u3 + a3 * jnp.where(keep, pltpu.roll(u3, k, axis=1), 0.0)
        a3 = a3 * jnp.where(keep, pltpu.roll(a3, k, axis=1), 1.0)
        k *= 2
    h = h0
    out = []
    for i in range(groups):
        hi = u3[i] + a3[i] * h
        out.append(hi)
        h = hi[SUBLANES - 1:SUBLANES, :]
    return jnp.concatenate(out, axis=0)


def _lru_prompt(x, tail, h0, cw_ref, cb_ref, gw, gb, lam):
    width = cw_ref.shape[0]
    xc = cb_ref[...] + cw_ref[width - 1:width, :] * x
    for j in range(1, width):
        xc = xc + cw_ref[width - 1 - j:width - j, :] * _shift_rows(x, tail, j)
    a, u = _lru_gates(xc, gw, gb, lam)
    return _scan_rows(a, u, h0)


def _lru_sample_kernel(x_ref, *refs):
    *buf_refs, h0_ref, cw_ref, cb_ref, gw_ref, gb_ref, lam_ref, h_ref = refs
    width = cw_ref.shape[0]
    xc = cb_ref[...] + cw_ref[width - 1:width, :] * x_ref[...]
    for j, buf_ref in enumerate(buf_refs):
        xc = xc + cw_ref[j:j + 1, :] * buf_ref[...]
    a, u = _lru_gates(xc, gw_ref[...], gb_ref[...], lam_ref[...])
    h_ref[...] = a * h0_ref[...] + u


def _lru_sample(xb, bufs, h0, lw):
    names = ("conv_w", "conv_b", "gate_w", "gate_b", "lam")
    args = (xb, *bufs, h0, *[lw[k] for k in names])
    return pl.pallas_call(
        _lru_sample_kernel,
        grid=(1,),
        in_specs=[_const(v.shape) for v in args],
        out_specs=_const(xb.shape),
        out_shape=jax.ShapeDtypeStruct(xb.shape, F32),
        compiler_params=_params(("arbitrary",)),
    )(*args)


def _mix_rows(x, oa, hs, gates, lwo, wo, g):
    d = x.shape[-1]
    ob = _mm(hs, lwo)
    mix = _mm(_sigmoid(gates[:, 0:d]) * oa + _sigmoid(gates[:, d:2 * d]) * ob, wo)
    return x + _rms(mix, g)


def _mix_sample_kernel(x_ref, oa_ref, hs_ref, gt_ref, lwo_ref, wo_ref, g_ref, o_ref):
    o_ref[...] = _mix_rows(x_ref[...], oa_ref[...], hs_ref[...], gt_ref[...], lwo_ref[...], wo_ref[...], g_ref[...])


def _mix_sample(x, oa, hs, gates, lwo16, wo16, g):
    args = (x, oa, hs, gates, lwo16, wo16, g)
    return pl.pallas_call(
        _mix_sample_kernel,
        grid=(1,),
        in_specs=[_const(v.shape) for v in args],
        out_specs=_const(x.shape),
        out_shape=jax.ShapeDtypeStruct(x.shape, F32),
        compiler_params=_params(("arbitrary",)),
    )(*args)


def _mix_prompt_kernel(x_ref, oa_ref, xb_ref, gt_ref, cw_ref, cb_ref, gw_ref, gb_ref, lam_ref, lwo_ref, wo_ref, g_ref,
                       o_ref, xtail_ref, htail_ref, xtail_scr, h_scr):
    @pl.when(pl.program_id(1) == 0)
    def _init():
        xtail_scr[...] = jnp.zeros_like(xtail_scr)
        h_scr[...] = jnp.zeros_like(h_scr)

    xb = xb_ref[...]
    rows = xb.shape[0]
    hs = _lru_prompt(xb, xtail_scr[...], h_scr[SUBLANES - 1:SUBLANES, :], cw_ref, cb_ref,
                     gw_ref[...], gb_ref[...], lam_ref[...])
    xtail_scr[...] = xb[rows - SUBLANES:rows, :]
    xtail_ref[0] = xb[rows - SUBLANES:rows, :]
    h_scr[...] = hs[rows - SUBLANES:rows, :]
    htail_ref[0] = hs[rows - SUBLANES:rows, :]
    o_ref[...] = _mix_rows(x_ref[...], oa_ref[...], hs, gt_ref[...], lwo_ref[...], wo_ref[...], g_ref[...])


def _mix_prompt(x, oa, xb, gates, lw, lwo16, wo16, g, b, t, tm):
    n, d = x.shape
    c = xb.shape[1]
    nt = t // tm
    names = ("conv_w", "conv_b", "gate_w", "gate_b", "lam")
    consts = [lw[k] for k in names] + [lwo16, wo16, g]
    rows = lambda w: pl.BlockSpec((tm, w), lambda i, j: (i * nt + j, 0))
    tail = pl.BlockSpec((1, SUBLANES, c), lambda i, j: (i, 0, 0))
    return pl.pallas_call(
        _mix_prompt_kernel,
        grid=(b, nt),
        in_specs=[rows(d), rows(d), rows(c), rows(2 * d)] + [_const(v.shape) for v in consts],
        out_specs=[rows(d), tail, tail],
        out_shape=[jax.ShapeDtypeStruct((n, d), F32), jax.ShapeDtypeStruct((b, SUBLANES, c), F32),
                   jax.ShapeDtypeStruct((b, SUBLANES, c), F32)],
        scratch_shapes=[pltpu.VMEM((SUBLANES, c), F32), pltpu.VMEM((SUBLANES, c), F32)],
        compiler_params=_params(("arbitrary", "arbitrary")),
    )(x, oa, xb, gates, *consts)


def _ffn_body(x1, gpre, gpost, up_ref, cw_ref, cb_ref, down_ref, hist, keep):
    dff = down_ref.shape[0]
    hn = _rms(x1, gpre).astype(BF16)
    starts = list(range(0, dff, FF_COLS))

    def up(c0):
        return [(slice(off, off + FF_COLS), jnp.dot(hn, up_ref[:, off:off + FF_COLS], preferred_element_type=F32))
                for off in (c0, dff + c0)]

    ups = [up(c0) for c0 in starts[:FFN_AHEAD]]
    f = jnp.zeros(x1.shape, F32)
    for i, c0 in enumerate(starts):
        if i + FFN_AHEAD < len(starts):
            ups.append(up(starts[i + FFN_AHEAD]))
        halves = []
        for cols, u in ups[i]:
            u1, u2 = hist(u, cols)
            halves.append(cb_ref[:, cols] + cw_ref[2:3, cols] * u + cw_ref[1:2, cols] * u1 + cw_ref[0:1, cols] * u2)
            keep(u, cols)
        act = _gelu_tanh(halves[0]) * halves[1]
        f = f + jnp.dot(act.astype(BF16), down_ref[c0:c0 + FF_COLS, :], preferred_element_type=F32)
    return x1 + _rms(f, gpost)


def _ffn_prompt_kernel(x_ref, gpre_ref, gpost_ref, up_ref, cw_ref, cb_ref, down_ref, y_ref, tail_ref, tail_scr):
    @pl.when(pl.program_id(1) == 0)
    def _init():
        tail_scr[...] = jnp.zeros_like(tail_scr)

    rows = x_ref.shape[0]

    def hist(u, cols):
        tail = tail_scr[:, cols]
        return _shift_rows(u, tail, 1), _shift_rows(u, tail, 2)

    def keep(u, cols):
        tail_scr[:, cols] = u[rows - SUBLANES:rows, :]
        tail_ref[0, :, cols] = u[rows - SUBLANES:rows, :]

    y_ref[...] = _ffn_body(x_ref[...], gpre_ref[...], gpost_ref[...], up_ref, cw_ref, cb_ref, down_ref, hist, keep)


def _ffn_prompt(x1, fw, b, t, tm):
    n, d = x1.shape
    nt = t // tm
    dff2 = fw["up"].shape[1]
    names = ("g_pre", "g_post", "up", "conv_w", "conv_b", "down")
    rows = pl.BlockSpec((tm, d), lambda i, j: (i * nt + j, 0))
    return pl.pallas_call(
        _ffn_prompt_kernel,
        grid=(b, nt),
        in_specs=[rows] + [_const(fw[k].shape) for k in names],
        out_specs=[rows, pl.BlockSpec((1, SUBLANES, dff2), lambda i, j: (i, 0, 0))],
        out_shape=[jax.ShapeDtypeStruct((n, d), F32), jax.ShapeDtypeStruct((b, SUBLANES, dff2), F32)],
        scratch_shapes=[pltpu.VMEM((SUBLANES, dff2), F32)],
        compiler_params=_params(("arbitrary", "arbitrary")),
    )(x1, *[fw[k] for k in names])


def _ffn_sample_kernel(x_ref, b0_ref, b1_ref, gpre_ref, gpost_ref, up_ref, cw_ref, cb_ref, down_ref, y_ref, u_ref):
    def hist(u, cols):
        return b1_ref[:, cols], b0_ref[:, cols]

    def keep(u, cols):
        u_ref[:, cols] = u

    y_ref[...] = _ffn_body(x_ref[...], gpre_ref[...], gpost_ref[...], up_ref, cw_ref, cb_ref, down_ref, hist, keep)


def _ffn_sample(x1, buf0, buf1, fw):
    names = ("g_pre", "g_post", "up", "conv_w", "conv_b", "down")
    args = (x1, buf0, buf1, *[fw[k] for k in names])
    return pl.pallas_call(
        _ffn_sample_kernel,
        grid=(1,),
        in_specs=[_const(v.shape) for v in args],
        out_specs=[_const(x1.shape), _const(buf0.shape)],
        out_shape=[jax.ShapeDtypeStruct(x1.shape, F32), jax.ShapeDtypeStruct(buf0.shape, F32)],
        compiler_params=_params(("arbitrary",)),
    )(*args)


def _block_diag(blocks):
    n, bi, bj = blocks.shape
    eye = jnp.eye(n, dtype=blocks.dtype)
    return jnp.einsum("nij,nm->nimj", blocks, eye).reshape(n * bi, n * bj)


def _layer_weights(norm_pre_mix, norm_post_mix, norm_pre_ffn, norm_post_ffn, w_in,
                   rwkv_mu, rwkv_w0, rwkv_w_up, rwkv_a0, rwkv_a_up, rwkv_g_up, rwkv_k_k, rwkv_k_a,
                   rwkv_r_k, rwkv_lnx_w, rwkv_lnx_b, rwkv_w_out,
                   lru_conv_w, lru_conv_b, lru_gx_w, lru_gx_b, lru_ga_w, lru_ga_b, lru_lambda, lru_w_out,
                   w_o, ffn_up, ffn_conv_w, ffn_conv_b, ffn_down):
    row = lambda x: x.reshape(1, -1)
    w = rwkv_w0.shape[-1]
    head_of = jnp.arange(w) // HEAD_DIM
    t_idx = jnp.arange(CHUNK)
    lo_w, lo_a = rwkv_w_up.shape[0], rwkv_a_up.shape[0]
    lora2 = jnp.zeros((lo_w + lo_a, 2 * w), F32).at[:lo_w, :w].set(rwkv_w_up).at[lo_w:, w:].set(rwkv_a_up)
    rw = dict(mu=row(rwkv_mu), w0=row(rwkv_w0), lora2=lora2.astype(BF16), a0=row(rwkv_a0),
              g_up=rwkv_g_up.astype(BF16), k_k=row(rwkv_k_k), k_a=row(rwkv_k_a), r_k=row(rwkv_r_k),
              lnx_w=row(rwkv_lnx_w), lnx_b=row(rwkv_lnx_b), w_out=rwkv_w_out.astype(BF16),
              seg=(head_of[:, None] == head_of[None, :]).astype(BF16),
              tri=(t_idx[:, None] >= t_idx[None, :]).astype(BF16))
    lw = dict(conv_w=lru_conv_w, conv_b=row(lru_conv_b),
              gate_w=jnp.concatenate([_block_diag(lru_gx_w), _block_diag(lru_ga_w)], axis=1).astype(BF16),
              gate_b=jnp.concatenate([row(lru_gx_b), row(lru_ga_b)], axis=1), lam=row(lru_lambda))
    fw = dict(g_pre=row(norm_pre_ffn), g_post=row(norm_post_ffn), up=ffn_up.astype(BF16),
              conv_w=ffn_conv_w, conv_b=row(ffn_conv_b), down=ffn_down.astype(BF16))
    return dict(g_in=row(norm_pre_mix), w_in=w_in.astype(BF16), rw=rw, lw=lw, fw=fw,
                lru_w_out=lru_w_out.astype(BF16), w_o=w_o.astype(BF16), g_mix=row(norm_post_mix))


def _splits(wts):
    w = wts["rw"]["w0"].shape[-1]
    c_rwkv = wts["rw"]["mu"].shape[-1]
    c_lru = wts["lw"]["lam"].shape[-1]
    return (c_rwkv, c_lru, wts["w_in"].shape[1] - c_rwkv - c_lru)


def _prompt_layer(x, wts):
    b, t, d = x.shape
    x2 = x.reshape(b * t, d)
    tm = min(t, ROW_TILE)
    p, xb, gates = _proj(x2, wts["g_in"], wts["w_in"], _splits(wts), tm)
    oa, p_tail, wkv = _rwkv_prompt(p, wts["rw"], b, t, min(t, WKV_ROWS))
    x1, xb_tail, h_tail = _mix_prompt(x2, oa, xb, gates, wts["lw"], wts["lru_w_out"], wts["w_o"], wts["g_mix"],
                                      b, t, tm)
    y, u_tail = _ffn_prompt(x1, wts["fw"], b, t, min(t, FFN_ROWS))
    conv_w = wts["lw"]["conv_w"].shape[0]
    ffn_w = wts["fw"]["conv_w"].shape[0]
    state = (p_tail[:, SUBLANES - 1:], wkv, xb_tail[:, SUBLANES - (conv_w - 1):], h_tail[:, SUBLANES - 1],
             u_tail[:, SUBLANES - (ffn_w - 1):])
    return y.reshape(b, t, d), state


def _sample_layer(x, shift, wkv, lru_buf, h0, ffn_buf, wts):
    n, t, d = x.shape
    x2 = x.reshape(n, d)
    p, xb, gates = _proj(x2, wts["g_in"], wts["w_in"], _splits(wts), n)
    oa, wkv_t = _rwkv_sample(p, shift[:, 0, :], jnp.transpose(wkv, (1, 2, 3, 0)), wts["rw"])
    lru_bufs = [lru_buf[:, j, :] for j in range(lru_buf.shape[1])]
    h = _lru_sample(xb, lru_bufs, h0, wts["lw"])
    x1 = _mix_sample(x2, oa, h, gates, wts["lru_w_out"], wts["w_o"], wts["g_mix"])
    y, u = _ffn_sample(x1, ffn_buf[:, 0, :], ffn_buf[:, 1, :], wts["fw"])
    state = (p.reshape(n, 1, -1), jnp.transpose(wkv_t, (3, 0, 1, 2)), jnp.stack(lru_bufs[1:] + [xb], axis=1), h,
             jnp.stack([ffn_buf[:, 1, :], u], axis=1))
    return y.reshape(n, t, d), state


def kernel(x_prompt, x_sample, state_rwkv_shift, state_rwkv_wkv, state_lru_conv, state_lru_h, state_ffn_conv,
           norm_pre_mix, norm_post_mix, norm_pre_ffn, norm_post_ffn, w_in,
           rwkv_mu, rwkv_w0, rwkv_w_up, rwkv_a0, rwkv_a_up, rwkv_g_up, rwkv_k_k, rwkv_k_a,
           rwkv_r_k, rwkv_lnx_w, rwkv_lnx_b, rwkv_w_out,
           lru_conv_w, lru_conv_b, lru_gx_w, lru_gx_b, lru_ga_w, lru_ga_b, lru_lambda, lru_w_out,
           w_o, ffn_up, ffn_conv_w, ffn_conv_b, ffn_down):
    params = (norm_pre_mix, norm_post_mix, norm_pre_ffn, norm_post_ffn, w_in,
              rwkv_mu, rwkv_w0, rwkv_w_up, rwkv_a0, rwkv_a_up, rwkv_g_up, rwkv_k_k, rwkv_k_a,
              rwkv_r_k, rwkv_lnx_w, rwkv_lnx_b, rwkv_w_out,
              lru_conv_w, lru_conv_b, lru_gx_w, lru_gx_b, lru_ga_w, lru_ga_b, lru_lambda, lru_w_out,
              w_o, ffn_up, ffn_conv_w, ffn_conv_b, ffn_down)
    depth = w_in.shape[0]
    assert depth == 1 and x_sample.shape[1] == 1
    yp, ys = x_prompt, x_sample
    new_p, new_s = [], []
    for l in range(depth):
        wts = _layer_weights(*[q[l] for q in params])
        yp, st_p = _prompt_layer(yp, wts)
        ys, st_s = _sample_layer(ys, state_rwkv_shift[l], state_rwkv_wkv[l], state_lru_conv[l],
                                 state_lru_h[l], state_ffn_conv[l], wts)
        new_p.append(st_p)
        new_s.append(st_s)
    stk = lambda lst, i: jnp.stack([s[i] for s in lst], axis=0)
    return (yp, ys,
            stk(new_p, 0), stk(new_p, 1), stk(new_p, 2), stk(new_p, 3), stk(new_p, 4),
            stk(new_s, 0), stk(new_s, 1), stk(new_s, 2), stk(new_s, 3), stk(new_s, 4))
```

```python
import functools
import math

import jax
import jax.numpy as jnp
from jax import lax
from jax.experimental import pallas as pl
from jax.experimental.pallas import tpu as pltpu

F32 = jnp.float32
BF16 = jnp.bfloat16

NORM_EPS = 1e-6
LN_X_EPS = 64e-5
LRU_C = 8.0
DECAY_SCALE = math.exp(-0.5)
HEAD_DIM = 64
LANES = 128
SUBLANES = 8
CHUNK = 64
ROW_TILE = 512
WKV_ROWS = 512
FF_COLS = 256
FFN_ROWS = 256
FFN_AHEAD = 3
VMEM_LIMIT = 56 * 1024 * 1024

_NN = (((1,), (0,)), ((), ()))
_NT = (((1,), (1,)), ((), ()))
_TN = (((0,), (0,)), ((), ()))


def _rms(x, g):
    return x * lax.rsqrt(jnp.mean(x * x, axis=-1, keepdims=True) + NORM_EPS) * g


def _softplus(x):
    return jnp.maximum(x, 0.0) + jnp.log1p(jnp.exp(-jnp.abs(x)))


def _sigmoid(x):
    return 0.5 * jnp.tanh(0.5 * x) + 0.5


def _gelu_tanh(x):
    c = 0.7978845608028654
    hx = 0.5 * x
    return hx + hx * jnp.tanh(x * (c + (c * 0.044715) * (x * x)))


def _mm(a, b):
    return jnp.dot(a.astype(BF16), b.astype(BF16), preferred_element_type=F32)


def _mmd(a, b, dims=_NN):
    return lax.dot_general(a.astype(BF16), b.astype(BF16), dims, preferred_element_type=F32)


def _segsum(x, seg):
    return jnp.dot(x.astype(BF16), seg, preferred_element_type=F32)


def _cumsum_rows(tri, x):
    h1 = x.astype(BF16)
    r1 = x - h1.astype(F32)
    h2 = r1.astype(BF16)
    h3 = (r1 - h2.astype(F32)).astype(BF16)
    d = lambda y: jnp.dot(tri, y, preferred_element_type=F32)
    return d(h1) + (d(h2) + d(h3))


def _shift_rows(x, tail, j):
    xs = pltpu.roll(x, j, axis=0)
    ts = pltpu.roll(tail, j, axis=0)
    row = lax.broadcasted_iota(jnp.int32, ts.shape, 0)
    head = jnp.where(row < j, ts, xs[0:SUBLANES])
    if x.shape[0] == SUBLANES:
        return head
    return jnp.concatenate([head, xs[SUBLANES:]], axis=0)


def _const(shape):
    n = len(shape)
    return pl.BlockSpec(shape, lambda *_: (0,) * n, pipeline_mode=pl.Buffered(1))


def _params(sem):
    return pltpu.CompilerParams(dimension_semantics=sem, vmem_limit_bytes=VMEM_LIMIT)


def _proj_kernel(x_ref, g_ref, w_ref, p_ref, xb_ref, gt_ref, *, splits):
    xn = _rms(x_ref[...], g_ref[...]).astype(BF16)
    c0, c1, c2 = splits
    p_ref[...] = jnp.dot(xn, w_ref[:, 0:c0], preferred_element_type=F32)
    xb_ref[...] = jnp.dot(xn, w_ref[:, c0:c0 + c1], preferred_element_type=F32)
    gt_ref[...] = jnp.dot(xn, w_ref[:, c0 + c1:c0 + c1 + c2], preferred_element_type=F32)


def _proj(x, g, w_in16, splits, tm):
    n, d = x.shape
    c0, c1, c2 = splits
    return pl.pallas_call(
        functools.partial(_proj_kernel, splits=splits),
        grid=(n // tm,),
        in_specs=[pl.BlockSpec((tm, d), lambda i: (i, 0)), _const((1, d)), _const(w_in16.shape)],
        out_specs=[pl.BlockSpec((tm, c0), lambda i: (i, 0)),
                   pl.BlockSpec((tm, c1), lambda i: (i, 0)),
                   pl.BlockSpec((tm, c2), lambda i: (i, 0))],
        out_shape=[jax.ShapeDtypeStruct((n, c0), F32), jax.ShapeDtypeStruct((n, c1), F32),
                   jax.ShapeDtypeStruct((n, c2), F32)],
        compiler_params=_params(("arbitrary",)),
    )(x, g, w_in16)


def _rwkv_pre(p, prev, mu, w0, lora2, a0, g_up, k_k, k_a, seg):
    w = w0.shape[-1]
    m = p + mu * (prev - p)
    r, k, v = m[:, 0:w], m[:, w:2 * w], m[:, 2 * w:3 * w]
    z = m[:, 3 * w:3 * w + LANES]
    lane = lax.broadcasted_iota(jnp.int32, z.shape, 1)
    lor = _mm(jnp.where(lane < HEAD_DIM, jnp.tanh(z), z), lora2)
    lw = -DECAY_SCALE * _sigmoid(w0 + lor[:, 0:w])
    a_sig = _sigmoid(a0 + lor[:, w:2 * w])
    g = _mm(_sigmoid(m[:, 3 * w + LANES:3 * w + 2 * LANES]), g_up)
    kk = k * k_k
    kkn = kk * lax.rsqrt(jnp.maximum(_segsum(kk * kk, seg), 1e-24))
    k2 = k * (1.0 + (a_sig - 1.0) * k_a)
    return r, lw, k2, v, -kkn, kkn * a_sig, g


def _rwkv_post(y, r, k2, v, g, r_k, lnx_w, lnx_b, w_out, seg):
    inv = 1.0 / HEAD_DIM
    mean = _segsum(y, seg) * inv
    d = y - mean
    var = _segsum(d * d, seg) * inv
    yn = d * lax.rsqrt(var + LN_X_EPS) * lnx_w + lnx_b
    bonus = _segsum(r * k2 * r_k, seg) * v
    return _mm((yn + bonus) * g, w_out)


def _wkv_chunk_summaries(xs, m0, m1, strict, incl, eye2, eyew, bdmask):
    ln = xs[0][0].shape[0]
    m0b = m0 > 0.5
    swap = lambda x: pltpu.roll(x, HEAD_DIM, axis=1)
    h16 = lambda x: x.astype(BF16)
    zero16 = jnp.zeros((), BF16)
    bd = lambda x: jnp.concatenate([jnp.where(m0b, x, zero16), jnp.where(m0b, zero16, x)], axis=0)
    cat = jnp.concatenate
    rts, ats, bts, kts, bhs, khs, vs, wls = zip(*xs)
    at16 = [h16(x) for x in ats]
    v16 = [h16(x) for x in vs]
    bdv = [bd(x) for x in v16]
    gms = [_mmd(cat([bd(a), bd(h16(rt))], axis=0), cat([bt, kt], axis=0), _NT)
           for rt, a, bt, kt in zip(rts, at16, bts, kts)]
    ga0 = [jnp.where(strict, g[0:ln], 0.0) for g in gms]
    ga1 = [jnp.where(strict, g[ln:2 * ln], 0.0) for g in gms]
    gr0 = [jnp.where(incl, g[2 * ln:3 * ln], 0.0) for g in gms]
    gr1 = [jnp.where(incl, g[3 * ln:4 * ln], 0.0) for g in gms]
    n_ab = [jnp.where(m0b, x, swap(y)) for x, y in zip(ga0, ga1)]
    n_ak = [jnp.where(m0b, swap(x), y) for x, y in zip(ga0, ga1)]
    n_rb = [jnp.where(m0b, x, swap(y)) for x, y in zip(gr0, gr1)]
    n_rk = [jnp.where(m0b, swap(x), y) for x, y in zip(gr0, gr1)]
    tinv = [eye2 + n for n in n_ab]
    steps = (ln - 1).bit_length() - 1
    if steps > 0:
        npow = [h16(_mmd(n, bd(n))) for n in (h16(x) for x in n_ab)]
        for _ in range(steps - 1):
            both = [_mmd(cat([n, h16(t)], axis=0), bd(n)) for n, t in zip(npow, tinv)]
            tinv = [t + x[ln:2 * ln] for t, x in zip(tinv, both)]
            npow = [h16(x[0:ln]) for x in both]
        tinv = [t + _mmd(t, bd(n)) for t, n in zip(tinv, npow)]
    kv = [_mmd(cat([nk, nr], axis=0), b) for nk, nr, b in zip(n_ak, n_rk, bdv)]
    akv = [x[0:ln] for x in kv]
    rkv = [x[ln:2 * ln] for x in kv]
    tt = [_mmd(t, cat([bd(a), bd(h16(ak))], axis=1)) for t, a, ak in zip(tinv, at16, akv)]
    ta = [h16(x[:, 0:LANES]) for x in tt]
    tk = [h16(x[:, LANES:2 * LANES]) for x in tt]
    mc = [_mmd(cat([cat([a, k], axis=1), cat([jnp.zeros_like(v), v], axis=1)], axis=0), cat([bh, kh], axis=0), _TN)
          for a, k, v, bh, kh in zip(ta, tk, v16, bhs, khs)]
    ms = [eyew * wl + x[0:LANES] * bdmask for x, wl in zip(mc, wls)]
    cs = [x[LANES:2 * LANES] * bdmask for x in mc]
    qy = [_mmd(n, cat([bd(a), bd(k)], axis=1)) for n, a, k in zip(n_rb, ta, tk)]
    qs = [rt + x[:, 0:LANES] for rt, x in zip(rts, qy)]
    y0 = [x[:, LANES:2 * LANES] + r for x, r in zip(qy, rkv)]
    return list(zip(qs, y0, ms, cs))


def _rwkv_prompt_kernel(p_ref, mu_ref, w0_ref, lora_ref, a0_ref, gup_ref, kk_ref, ka_ref, rk_ref,
                        lnw_ref, lnb_ref, wout_ref, seg_ref, tri_ref,
                        oa_ref, tail_ref, wkv_ref, prev_scr, s_scr):
    t = pl.program_id(1)

    @pl.when(t == 0)
    def _init():
        prev_scr[...] = jnp.zeros_like(prev_scr)
        s_scr[...] = jnp.zeros_like(s_scr)

    p = p_ref[...]
    rows = p.shape[0]
    row = lax.broadcasted_iota(jnp.int32, p.shape, 0)
    prev = jnp.where(row == 0, prev_scr[SUBLANES - 1:SUBLANES, :], pltpu.roll(p, 1, axis=0))
    prev_scr[...] = p[rows - SUBLANES:rows, :]
    tail_ref[0] = p[rows - SUBLANES:rows, :]

    seg = seg_ref[...]
    r, lw, k2, v, aa, bb, g = _rwkv_pre(p, prev, mu_ref[...], w0_ref[...], lora_ref[...], a0_ref[...],
                                        gup_ref[...], kk_ref[...], ka_ref[...], seg)
    c = _cumsum_rows(tri_ref[...], lw)

    ln = CHUNK
    ri = lax.broadcasted_iota(jnp.int32, (ln, LANES), 0)
    ci = lax.broadcasted_iota(jnp.int32, (ln, LANES), 1)
    cj = jnp.where(ci < HEAD_DIM, ci, ci - HEAD_DIM)
    m0 = (ci < HEAD_DIM).astype(F32)
    m1 = 1.0 - m0
    strict = cj < ri
    incl = cj <= ri
    eye2 = (cj == ri).astype(F32)
    r2 = lax.broadcasted_iota(jnp.int32, (LANES, LANES), 0)
    c2 = lax.broadcasted_iota(jnp.int32, (LANES, LANES), 1)
    bdmask = ((r2 < HEAD_DIM) == (c2 < HEAD_DIM)).astype(F32)
    eyew = (r2 == c2).astype(F32)

    npairs = s_scr.shape[0]
    insts = []
    for ch in range(rows // ln):
        rs = slice(ch * ln, (ch + 1) * ln)
        c_, lw_ = c[rs], lw[rs]
        cl = c_[ln - 1:ln, :]
        e_n = jnp.exp(-c_)
        e_l = jnp.exp(cl - c_)
        rt = r[rs] * jnp.exp(c_)
        at = aa[rs] * jnp.exp(c_ - lw_)
        bt, kt = bb[rs] * e_n, k2[rs] * e_n
        bh, kh = bb[rs] * e_l, k2[rs] * e_l
        wl = jnp.exp(cl)
        v_ = v[rs]
        for pr in range(npairs):
            sl = slice(pr * LANES, (pr + 1) * LANES)
            insts.append((rt[:, sl], at[:, sl], bt[:, sl], kt[:, sl], bh[:, sl], kh[:, sl], v_[:, sl], wl[:, sl]))
    summaries = _wkv_chunk_summaries(insts, m0, m1, strict, incl, eye2, eyew, bdmask)

    states = [s_scr[pr] for pr in range(npairs)]
    y_rows = []
    for ch in range(rows // ln):
        ys = []
        for pr in range(npairs):
            q, y0, m, cc = summaries[ch * npairs + pr]
            ys.append(_mmd(q, states[pr], _NT) + y0)
            states[pr] = _mmd(states[pr], m) + cc
        y_rows.append(jnp.concatenate(ys, axis=1))
    for pr in range(npairs):
        s_scr[pr] = states[pr]
    y = jnp.concatenate(y_rows, axis=0)
    oa_ref[...] = _rwkv_post(y, r, k2, v, g, rk_ref[...], lnw_ref[...], lnb_ref[...], wout_ref[...], seg)

    @pl.when(t == pl.num_programs(1) - 1)
    def _fin():
        for pr in range(npairs):
            s = s_scr[pr]
            wkv_ref[0, 2 * pr] = s[0:HEAD_DIM, 0:HEAD_DIM]
            wkv_ref[0, 2 * pr + 1] = pltpu.roll(s[HEAD_DIM:LANES, :], HEAD_DIM, axis=1)[:, 0:HEAD_DIM]


def _rwkv_prompt(p, rw, b, t, tc):
    n, cp = p.shape
    w = rw["w0"].shape[-1]
    heads = w // HEAD_DIM
    nt = t // tc
    dm = rw["w_out"].shape[-1]
    tri = jnp.kron(jnp.eye(tc // CHUNK, dtype=BF16), rw["tri"])
    names = ("mu", "w0", "lora2", "a0", "g_up", "k_k", "k_a", "r_k", "lnx_w", "lnx_b", "w_out", "seg")
    return pl.pallas_call(
        _rwkv_prompt_kernel,
        grid=(b, nt),
        in_specs=[pl.BlockSpec((tc, cp), lambda i, j: (i * nt + j, 0))] + [_const(rw[k].shape) for k in names]
                 + [_const(tri.shape)],
        out_specs=[pl.BlockSpec((tc, dm), lambda i, j: (i * nt + j, 0)),
                   pl.BlockSpec((1, SUBLANES, cp), lambda i, j: (i, 0, 0)),
                   pl.BlockSpec((1, heads, HEAD_DIM, HEAD_DIM), lambda i, j: (i, 0, 0, 0))],
        out_shape=[jax.ShapeDtypeStruct((n, dm), F32),
                   jax.ShapeDtypeStruct((b, SUBLANES, cp), F32),
                   jax.ShapeDtypeStruct((b, heads, HEAD_DIM, HEAD_DIM), F32)],
        scratch_shapes=[pltpu.VMEM((SUBLANES, cp), F32), pltpu.VMEM((heads // 2, LANES, LANES), F32)],
        compiler_params=_params(("arbitrary", "arbitrary")),
    )(p, *[rw[k] for k in names], tri)


def _rwkv_sample_pre_kernel(p_ref, prev_ref, mu_ref, w0_ref, lora_ref, a0_ref, gup_ref, kk_ref, ka_ref, seg_ref,
                            rt_ref, wt_ref, kt_ref, vt_ref, at_ref, bt_ref, r_ref, k_ref, v_ref, g_ref):
    r, lw, k2, v, aa, bb, g = _rwkv_pre(p_ref[...], prev_ref[...], mu_ref[...], w0_ref[...], lora_ref[...],
                                        a0_ref[...], gup_ref[...], kk_ref[...], ka_ref[...], seg_ref[...])
    rt_ref[...] = r.T
    wt_ref[...] = jnp.exp(lw).T
    kt_ref[...] = k2.T
    vt_ref[...] = v.T
    at_ref[...] = aa.T
    bt_ref[...] = bb.T
    r_ref[...] = r
    k_ref[...] = k2
    v_ref[...] = v
    g_ref[...] = g


def _rwkv_sample_step_kernel(s_ref, r_ref, w_ref, k_ref, v_ref, a_ref, b_ref, y_ref, so_ref):
    r, w, k, a, b = r_ref[...], w_ref[...], k_ref[...], a_ref[...], b_ref[...]

    def body(i, carry):
        base = pl.multiple_of(i * SUBLANES, SUBLANES)
        vrows = v_ref[pl.ds(base, SUBLANES), :]
        ys = []
        for j in range(SUBLANES):
            s = s_ref[0, base + j]
            sa = jnp.sum(s * a, axis=0, keepdims=True)
            s_new = s * w + sa * b + vrows[j:j + 1, :] * k
            so_ref[0, base + j] = s_new
            ys.append(jnp.sum(s_new * r, axis=0, keepdims=True))
        y_ref[pl.ds(base, SUBLANES), :] = jnp.concatenate(ys, axis=0)
        return carry

    lax.fori_loop(0, s_ref.shape[1] // SUBLANES, body, 0)


def _rwkv_sample_post_kernel(yt_ref, r_ref, k_ref, v_ref, g_ref, rk_ref, lnw_ref, lnb_ref, wout_ref, seg_ref, oa_ref):
    oa_ref[...] = _rwkv_post(yt_ref[...].T, r_ref[...], k_ref[...], v_ref[...], g_ref[...], rk_ref[...],
                             lnw_ref[...], lnb_ref[...], wout_ref[...], seg_ref[...])


def _rwkv_sample(p, shift, wkv, rw):
    n, cp = p.shape
    w = rw["w0"].shape[-1]
    heads = w // HEAD_DIM
    dm = rw["w_out"].shape[-1]
    pre_names = ("mu", "w0", "lora2", "a0", "g_up", "k_k", "k_a", "seg")
    vec = jax.ShapeDtypeStruct((n, w), F32)
    vec_t = jax.ShapeDtypeStruct((w, n), F32)
    rt, wt, kt, vt, at, bt, r, k2, v, g = pl.pallas_call(
        _rwkv_sample_pre_kernel,
        grid=(1,),
        in_specs=[_const((n, cp)), _const((n, cp))] + [_const(rw[k].shape) for k in pre_names],
        out_specs=[_const((w, n))] * 6 + [_const((n, w))] * 4,
        out_shape=[vec_t] * 6 + [vec] * 4,
        compiler_params=_params(("arbitrary",)),
    )(p, shift, *[rw[k] for k in pre_names])

    head_spec = pl.BlockSpec((HEAD_DIM, n), lambda h: (h, 0))
    st_spec = pl.BlockSpec((1, HEAD_DIM, HEAD_DIM, n), lambda h: (h, 0, 0, 0))
    yt, wkv_new = pl.pallas_call(
        _rwkv_sample_step_kernel,
        grid=(heads,),
        in_specs=[st_spec] + [head_spec] * 6,
        out_specs=[head_spec, st_spec],
        out_shape=[vec_t, jax.ShapeDtypeStruct(wkv.shape, F32)],
        compiler_params=_params(("arbitrary",)),
    )(wkv, rt, wt, kt, vt, at, bt)

    post_names = ("r_k", "lnx_w", "lnx_b", "w_out", "seg")
    oa = pl.pallas_call(
        _rwkv_sample_post_kernel,
        grid=(1,),
        in_specs=[_const((w, n))] + [_const((n, w))] * 4 + [_const(rw[k].shape) for k in post_names],
        out_specs=_const((n, dm)),
        out_shape=jax.ShapeDtypeStruct((n, dm), F32),
        compiler_params=_params(("arbitrary",)),
    )(yt, r, k2, v, g, *[rw[k] for k in post_names])
    return oa, wkv_new


def _lru_gates(xc, gw, gb, lam):
    c = xc.shape[-1]
    gates = _mm(xc, gw) + gb
    gx = _sigmoid(gates[:, 0:c])
    ga = _sigmoid(gates[:, c:2 * c])
    log_a = -LRU_C * ga * _softplus(-lam)
    a = jnp.exp(log_a)
    u = jnp.sqrt(1.0 - a * a) * (gx * xc)
    return a, u


def _scan_rows(a, u, h0):
    rows, c = a.shape
    groups = rows // SUBLANES
    a3 = a.reshape(groups, SUBLANES, c)
    u3 = u.reshape(groups, SUBLANES, c)
    sub = lax.broadcasted_iota(jnp.int32, a3.shape, 1)
    k = 1
    while k < SUBLANES:
        keep = sub >= k
        u3 = u3 + a3 * jnp.where(keep, pltpu.roll(u3, k, axis=1), 0.0)
        a3 = a3 * jnp.where(keep, pltpu.roll(a3, k, axis=1), 1.0)
        k *= 2
    h = h0
    out = []
    for i in range(groups):
        hi = u3[i] + a3[i] * h
        out.append(hi)
        h = hi[SUBLANES - 1:SUBLANES, :]
    return jnp.concatenate(out, axis=0)


def _lru_prompt(x, tail, h0, cw_ref, cb_ref, gw, gb, lam):
    width = cw_ref.shape[0]
    xc = cb_ref[...] + cw_ref[width - 1:width, :] * x
    for j in range(1, width):
        xc = xc + cw_ref[width - 1 - j:width - j, :] * _shift_rows(x, tail, j)
    a, u = _lru_gates(xc, gw, gb, lam)
    return _scan_rows(a, u, h0)


def _lru_sample_kernel(x_ref, *refs):
    *buf_refs, h0_ref, cw_ref, cb_ref, gw_ref, gb_ref, lam_ref, h_ref = refs
    width = cw_ref.shape[0]
    xc = cb_ref[...] + cw_ref[width - 1:width, :] * x_ref[...]
    for j, buf_ref in enumerate(buf_refs):
        xc = xc + cw_ref[j:j + 1, :] * buf_ref[...]
    a, u = _lru_gates(xc, gw_ref[...], gb_ref[...], lam_ref[...])
    h_ref[...] = a * h0_ref[...] + u


def _lru_sample(xb, bufs, h0, lw):
    names = ("conv_w", "conv_b", "gate_w", "gate_b", "lam")
    args = (xb, *bufs, h0, *[lw[k] for k in names])
    return pl.pallas_call(
        _lru_sample_kernel,
        grid=(1,),
        in_specs=[_const(v.shape) for v in args],
        out_specs=_const(xb.shape),
        out_shape=jax.ShapeDtypeStruct(xb.shape, F32),
        compiler_params=_params(("arbitrary",)),
    )(*args)


def _mix_rows(x, oa, hs, gates, lwo, wo, g):
    d = x.shape[-1]
    ob = _mm(hs, lwo)
    mix = _mm(_sigmoid(gates[:, 0:d]) * oa + _sigmoid(gates[:, d:2 * d]) * ob, wo)
    return x + _rms(mix, g)


def _mix_sample_kernel(x_ref, oa_ref, hs_ref, gt_ref, lwo_ref, wo_ref, g_ref, o_ref):
    o_ref[...] = _mix_rows(x_ref[...], oa_ref[...], hs_ref[...], gt_ref[...], lwo_ref[...], wo_ref[...], g_ref[...])


def _mix_sample(x, oa, hs, gates, lwo16, wo16, g):
    args = (x, oa, hs, gates, lwo16, wo16, g)
    return pl.pallas_call(
        _mix_sample_kernel,
        grid=(1,),
        in_specs=[_const(v.shape) for v in args],
        out_specs=_const(x.shape),
        out_shape=jax.ShapeDtypeStruct(x.shape, F32),
        compiler_params=_params(("arbitrary",)),
    )(*args)


def _mix_prompt_kernel(x_ref, oa_ref, xb_ref, gt_ref, cw_ref, cb_ref, gw_ref, gb_ref, lam_ref, lwo_ref, wo_ref, g_ref,
                       o_ref, xtail_ref, htail_ref, xtail_scr, h_scr):
    @pl.when(pl.program_id(1) == 0)
    def _init():
        xtail_scr[...] = jnp.zeros_like(xtail_scr)
        h_scr[...] = jnp.zeros_like(h_scr)

    xb = xb_ref[...]
    rows = xb.shape[0]
    hs = _lru_prompt(xb, xtail_scr[...], h_scr[SUBLANES - 1:SUBLANES, :], cw_ref, cb_ref,
                     gw_ref[...], gb_ref[...], lam_ref[...])
    xtail_scr[...] = xb[rows - SUBLANES:rows, :]
    xtail_ref[0] = xb[rows - SUBLANES:rows, :]
    h_scr[...] = hs[rows - SUBLANES:rows, :]
    htail_ref[0] = hs[rows - SUBLANES:rows, :]
    o_ref[...] = _mix_rows(x_ref[...], oa_ref[...], hs, gt_ref[...], lwo_ref[...], wo_ref[...], g_ref[...])


def _mix_prompt(x, oa, xb, gates, lw, lwo16, wo16, g, b, t, tm):
    n, d = x.shape
    c = xb.shape[1]
    nt = t // tm
    names = ("conv_w", "conv_b", "gate_w", "gate_b", "lam")
    consts = [lw[k] for k in names] + [lwo16, wo16, g]
    rows = lambda w: pl.BlockSpec((tm, w), lambda i, j: (i * nt + j, 0))
    tail = pl.BlockSpec((1, SUBLANES, c), lambda i, j: (i, 0, 0))
    return pl.pallas_call(
        _mix_prompt_kernel,
        grid=(b, nt),
        in_specs=[rows(d), rows(d), rows(c), rows(2 * d)] + [_const(v.shape) for v in consts],
        out_specs=[rows(d), tail, tail],
        out_shape=[jax.ShapeDtypeStruct((n, d), F32), jax.ShapeDtypeStruct((b, SUBLANES, c), F32),
                   jax.ShapeDtypeStruct((b, SUBLANES, c), F32)],
        scratch_shapes=[pltpu.VMEM((SUBLANES, c), F32), pltpu.VMEM((SUBLANES, c), F32)],
        compiler_params=_params(("arbitrary", "arbitrary")),
    )(x, oa, xb, gates, *consts)


def _ffn_body(x1, gpre, gpost, up_ref, cw_ref, cb_ref, down_ref, hist, keep):
    dff = down_ref.shape[0]
    hn = _rms(x1, gpre).astype(BF16)
    starts = list(range(0, dff, FF_COLS))

    def up(c0):
        return [(slice(off, off + FF_COLS), jnp.dot(hn, up_ref[:, off:off + FF_COLS], preferred_element_type=F32))
                for off in (c0, dff + c0)]

    ups = [up(c0) for c0 in starts[:FFN_AHEAD]]
    f = jnp.zeros(x1.shape, F32)
    for i, c0 in enumerate(starts):
        if i + FFN_AHEAD < len(starts):
            ups.append(up(starts[i + FFN_AHEAD]))
        halves = []
        for cols, u in ups[i]:
            u1, u2 = hist(u, cols)
            halves.append(cb_ref[:, cols] + cw_ref[2:3, cols] * u + cw_ref[1:2, cols] * u1 + cw_ref[0:1, cols] * u2)
            keep(u, cols)
        act = _gelu_tanh(halves[0]) * halves[1]
        f = f + jnp.dot(act.astype(BF16), down_ref[c0:c0 + FF_COLS, :], preferred_element_type=F32)
    return x1 + _rms(f, gpost)


def _ffn_prompt_kernel(x_ref, gpre_ref, gpost_ref, up_ref, cw_ref, cb_ref, down_ref, y_ref, tail_ref, tail_scr):
    @pl.when(pl.program_id(1) == 0)
    def _init():
        tail_scr[...] = jnp.zeros_like(tail_scr)

    rows = x_ref.shape[0]

    def hist(u, cols):
        tail = tail_scr[:, cols]
        return _shift_rows(u, tail, 1), _shift_rows(u, tail, 2)

    def keep(u, cols):
        tail_scr[:, cols] = u[rows - SUBLANES:rows, :]
        tail_ref[0, :, cols] = u[rows - SUBLANES:rows, :]

    y_ref[...] = _ffn_body(x_ref[...], gpre_ref[...], gpost_ref[...], up_ref, cw_ref, cb_ref, down_ref, hist, keep)


def _ffn_prompt(x1, fw, b, t, tm):
    n, d = x1.shape
    nt = t // tm
    dff2 = fw["up"].shape[1]
    names = ("g_pre", "g_post", "up", "conv_w", "conv_b", "down")
    rows = pl.BlockSpec((tm, d), lambda i, j: (i * nt + j, 0))
    return pl.pallas_call(
        _ffn_prompt_kernel,
        grid=(b, nt),
        in_specs=[rows] + [_const(fw[k].shape) for k in names],
        out_specs=[rows, pl.BlockSpec((1, SUBLANES, dff2), lambda i, j: (i, 0, 0))],
        out_shape=[jax.ShapeDtypeStruct((n, d), F32), jax.ShapeDtypeStruct((b, SUBLANES, dff2), F32)],
        scratch_shapes=[pltpu.VMEM((SUBLANES, dff2), F32)],
        compiler_params=_params(("arbitrary", "arbitrary")),
    )(x1, *[fw[k] for k in names])


def _ffn_sample_kernel(x_ref, b0_ref, b1_ref, gpre_ref, gpost_ref, up_ref, cw_ref, cb_ref, down_ref, y_ref, u_ref):
    def hist(u, cols):
        return b1_ref[:, cols], b0_ref[:, cols]

    def keep(u, cols):
        u_ref[:, cols] = u

    y_ref[...] = _ffn_body(x_ref[...], gpre_ref[...], gpost_ref[...], up_ref, cw_ref, cb_ref, down_ref, hist, keep)


def _ffn_sample(x1, buf0, buf1, fw):
    names = ("g_pre", "g_post", "up", "conv_w", "conv_b", "down")
    args = (x1, buf0, buf1, *[fw[k] for k in names])
    return pl.pallas_call(
        _ffn_sample_kernel,
        grid=(1,),
        in_specs=[_const(v.shape) for v in args],
        out_specs=[_const(x1.shape), _const(buf0.shape)],
        out_shape=[jax.ShapeDtypeStruct(x1.shape, F32), jax.ShapeDtypeStruct(buf0.shape, F32)],
        compiler_params=_params(("arbitrary",)),
    )(*args)


def _block_diag(blocks):
    n, bi, bj = blocks.shape
    eye = jnp.eye(n, dtype=blocks.dtype)
    return jnp.einsum("nij,nm->nimj", blocks, eye).reshape(n * bi, n * bj)


def _layer_weights(norm_pre_mix, norm_post_mix, norm_pre_ffn, norm_post_ffn, w_in,
                   rwkv_mu, rwkv_w0, rwkv_w_up, rwkv_a0, rwkv_a_up, rwkv_g_up, rwkv_k_k, rwkv_k_a,
                   rwkv_r_k, rwkv_lnx_w, rwkv_lnx_b, rwkv_w_out,
                   lru_conv_w, lru_conv_b, lru_gx_w, lru_gx_b, lru_ga_w, lru_ga_b, lru_lambda, lru_w_out,
                   w_o, ffn_up, ffn_conv_w, ffn_conv_b, ffn_down):
    row = lambda x: x.reshape(1, -1)
    w = rwkv_w0.shape[-1]
    head_of = jnp.arange(w) // HEAD_DIM
    t_idx = jnp.arange(CHUNK)
    lo_w, lo_a = rwkv_w_up.shape[0], rwkv_a_up.shape[0]
    lora2 = jnp.zeros((lo_w + lo_a, 2 * w), F32).at[:lo_w, :w].set(rwkv_w_up).at[lo_w:, w:].set(rwkv_a_up)
    rw = dict(mu=row(rwkv_mu), w0=row(rwkv_w0), lora2=lora2.astype(BF16), a0=row(rwkv_a0),
              g_up=rwkv_g_up.astype(BF16), k_k=row(rwkv_k_k), k_a=row(rwkv_k_a), r_k=row(rwkv_r_k),
              lnx_w=row(rwkv_lnx_w), lnx_b=row(rwkv_lnx_b), w_out=rwkv_w_out.astype(BF16),
              seg=(head_of[:, None] == head_of[None, :]).astype(BF16),
              tri=(t_idx[:, None] >= t_idx[None, :]).astype(BF16))
    lw = dict(conv_w=lru_conv_w, conv_b=row(lru_conv_b),
              gate_w=jnp.concatenate([_block_diag(lru_gx_w), _block_diag(lru_ga_w)], axis=1).astype(BF16),
              gate_b=jnp.concatenate([row(lru_gx_b), row(lru_ga_b)], axis=1), lam=row(lru_lambda))
    fw = dict(g_pre=row(norm_pre_ffn), g_post=row(norm_post_ffn), up=ffn_up.astype(BF16),
              conv_w=ffn_conv_w, conv_b=row(ffn_conv_b), down=ffn_down.astype(BF16))
    return dict(g_in=row(norm_pre_mix), w_in=w_in.astype(BF16), rw=rw, lw=lw, fw=fw,
                lru_w_out=lru_w_out.astype(BF16), w_o=w_o.astype(BF16), g_mix=row(norm_post_mix))


def _splits(wts):
    w = wts["rw"]["w0"].shape[-1]
    c_rwkv = wts["rw"]["mu"].shape[-1]
    c_lru = wts["lw"]["lam"].shape[-1]
    return (c_rwkv, c_lru, wts["w_in"].shape[1] - c_rwkv - c_lru)


def _prompt_layer(x, wts):
    b, t, d = x.shape
    x2 = x.reshape(b * t, d)
    tm = min(t, ROW_TILE)
    p, xb, gates = _proj(x2, wts["g_in"], wts["w_in"], _splits(wts), tm)
    oa, p_tail, wkv = _rwkv_prompt(p, wts["rw"], b, t, min(t, WKV_ROWS))
    x1, xb_tail, h_tail = _mix_prompt(x2, oa, xb, gates, wts["lw"], wts["lru_w_out"], wts["w_o"], wts["g_mix"],
                                      b, t, tm)
    y, u_tail = _ffn_prompt(x1, wts["fw"], b, t, min(t, FFN_ROWS))
    conv_w = wts["lw"]["conv_w"].shape[0]
    ffn_w = wts["fw"]["conv_w"].shape[0]
    state = (p_tail[:, SUBLANES - 1:], wkv, xb_tail[:, SUBLANES - (conv_w - 1):], h_tail[:, SUBLANES - 1],
             u_tail[:, SUBLANES - (ffn_w - 1):])
    return y.reshape(b, t, d), state


def _sample_layer(x, shift, wkv, lru_buf, h0, ffn_buf, wts):
    n, t, d = x.shape
    x2 = x.reshape(n, d)
    p, xb, gates = _proj(x2, wts["g_in"], wts["w_in"], _splits(wts), n)
    oa, wkv_t = _rwkv_sample(p, shift[:, 0, :], jnp.transpose(wkv, (1, 2, 3, 0)), wts["rw"])
    lru_bufs = [lru_buf[:, j, :] for j in range(lru_buf.shape[1])]
    h = _lru_sample(xb, lru_bufs, h0, wts["lw"])
    x1 = _mix_sample(x2, oa, h, gates, wts["lru_w_out"], wts["w_o"], wts["g_mix"])
    y, u = _ffn_sample(x1, ffn_buf[:, 0, :], ffn_buf[:, 1, :], wts["fw"])
    state = (p.reshape(n, 1, -1), jnp.transpose(wkv_t, (3, 0, 1, 2)), jnp.stack(lru_bufs[1:] + [xb], axis=1), h,
             jnp.stack([ffn_buf[:, 1, :], u], axis=1))
    return y.reshape(n, t, d), state


def kernel(x_prompt, x_sample, state_rwkv_shift, state_rwkv_wkv, state_lru_conv, state_lru_h, state_ffn_conv,
           norm_pre_mix, norm_post_mix, norm_pre_ffn, norm_post_ffn, w_in,
           rwkv_mu, rwkv_w0, rwkv_w_up, rwkv_a0, rwkv_a_up, rwkv_g_up, rwkv_k_k, rwkv_k_a,
           rwkv_r_k, rwkv_lnx_w, rwkv_lnx_b, rwkv_w_out,
           lru_conv_w, lru_conv_b, lru_gx_w, lru_gx_b, lru_ga_w, lru_ga_b, lru_lambda, lru_w_out,
           w_o, ffn_up, ffn_conv_w, ffn_conv_b, ffn_down):
    params = (norm_pre_mix, norm_post_mix, norm_pre_ffn, norm_post_ffn, w_in,
              rwkv_mu, rwkv_w0, rwkv_w_up, rwkv_a0, rwkv_a_up, rwkv_g_up, rwkv_k_k, rwkv_k_a,
              rwkv_r_k, rwkv_lnx_w, rwkv_lnx_b, rwkv_w_out,
              lru_conv_w, lru_conv_b, lru_gx_w, lru_gx_b, lru_ga_w, lru_ga_b, lru_lambda, lru_w_out,
              w_o, ffn_up, ffn_conv_w, ffn_conv_b, ffn_down)
    depth = w_in.shape[0]
    assert depth == 1 and x_sample.shape[1] == 1
    yp, ys = x_prompt, x_sample
    new_p, new_s = [], []
    for l in range(depth):
        wts = _layer_weights(*[q[l] for q in params])
        yp, st_p = _prompt_layer(yp, wts)
        ys, st_s = _sample_layer(ys, state_rwkv_shift[l], state_rwkv_wkv[l], state_lru_conv[l],
                                 state_lru_h[l], state_ffn_conv[l], wts)
        new_p.append(st_p)
        new_s.append(st_s)
    stk = lambda lst, i: jnp.stack([s[i] for s in lst], axis=0)
    return (yp, ys,
            stk(new_p, 0), stk(new_p, 1), stk(new_p, 2), stk(new_p, 3), stk(new_p, 4),
            stk(new_s, 0), stk(new_s, 1), stk(new_s, 2), stk(new_s, 3), stk(new_s, 4))
```

```python
import functools
import math

import jax
import jax.numpy as jnp
from jax import lax
from jax.experimental import pallas as pl
from jax.experimental.pallas import tpu as pltpu

F32 = jnp.float32
BF16 = jnp.bfloat16

NORM_EPS = 1e-6
LN_X_EPS = 64e-5
LRU_C = 8.0
DECAY_SCALE = math.exp(-0.5)
HEAD_DIM = 64
LANES = 128
SUBLANES = 8
CHUNK = 64
ROW_TILE = 512
WKV_ROWS = 512
FF_COLS = 256
FFN_ROWS = 256
FFN_AHEAD = 3
VMEM_LIMIT = 56 * 1024 * 1024

_NN = (((1,), (0,)), ((), ()))
_NT = (((1,), (1,)), ((), ()))
_TN = (((0,), (0,)), ((), ()))


def _rms(x, g):
    return x * lax.rsqrt(jnp.mean(x * x, axis=-1, keepdims=True) + NORM_EPS) * g


def _softplus(x):
    return jnp.maximum(x, 0.0) + jnp.log1p(jnp.exp(-jnp.abs(x)))


def _sigmoid(x):
    return 0.5 * jnp.tanh(0.5 * x) + 0.5


def _gelu_tanh(x):
    c = 0.7978845608028654
    hx = 0.5 * x
    return hx + hx * jnp.tanh(x * (c + (c * 0.044715) * (x * x)))


def _mm(a, b):
    return jnp.dot(a.astype(BF16), b.astype(BF16), preferred_element_type=F32)


def _mmd(a, b, dims=_NN):
    return lax.dot_general(a.astype(BF16), b.astype(BF16), dims, preferred_element_type=F32)


def _segsum(x, seg):
    return jnp.dot(x.astype(BF16), seg, preferred_element_type=F32)


def _cumsum_rows(tri, x):
    h1 = x.astype(BF16)
    r1 = x - h1.astype(F32)
    h2 = r1.astype(BF16)
    h3 = (r1 - h2.astype(F32)).astype(BF16)
    d = lambda y: jnp.dot(tri, y, preferred_element_type=F32)
    return d(h1) + (d(h2) + d(h3))


def _shift_rows(x, tail, j):
    xs = pltpu.roll(x, j, axis=0)
    ts = pltpu.roll(tail, j, axis=0)
    row = lax.broadcasted_iota(jnp.int32, ts.shape, 0)
    head = jnp.where(row < j, ts, xs[0:SUBLANES])
    if x.shape[0] == SUBLANES:
        return head
    return jnp.concatenate([head, xs[SUBLANES:]], axis=0)


def _const(shape):
    n = len(shape)
    return pl.BlockSpec(shape, lambda *_: (0,) * n, pipeline_mode=pl.Buffered(1))


def _params(sem):
    return pltpu.CompilerParams(dimension_semantics=sem, vmem_limit_bytes=VMEM_LIMIT)


def _proj_kernel(x_ref, g_ref, w_ref, p_ref, xb_ref, gt_ref, *, splits):
    xn = _rms(x_ref[...], g_ref[...]).astype(BF16)
    c0, c1, c2 = splits
    p_ref[...] = jnp.dot(xn, w_ref[:, 0:c0], preferred_element_type=F32)
    xb_ref[...] = jnp.dot(xn, w_ref[:, c0:c0 + c1], preferred_element_type=F32)
    gt_ref[...] = jnp.dot(xn, w_ref[:, c0 + c1:c0 + c1 + c2], preferred_element_type=F32)


def _proj(x, g, w_in16, splits, tm):
    n, d = x.shape
    c0, c1, c2 = splits
    return pl.pallas_call(
        functools.partial(_proj_kernel, splits=splits),
        grid=(n // tm,),
        in_specs=[pl.BlockSpec((tm, d), lambda i: (i, 0)), _const((1, d)), _const(w_in16.shape)],
        out_specs=[pl.BlockSpec((tm, c0), lambda i: (i, 0)),
                   pl.BlockSpec((tm, c1), lambda i: (i, 0)),
                   pl.BlockSpec((tm, c2), lambda i: (i, 0))],
        out_shape=[jax.ShapeDtypeStruct((n, c0), F32), jax.ShapeDtypeStruct((n, c1), F32),
                   jax.ShapeDtypeStruct((n, c2), F32)],
        compiler_params=_params(("arbitrary",)),
    )(x, g, w_in16)


def _rwkv_pre(p, prev, mu, w0, lora2, a0, g_up, k_k, k_a, seg):
    w = w0.shape[-1]
    m = p + mu * (prev - p)
    r, k, v = m[:, 0:w], m[:, w:2 * w], m[:, 2 * w:3 * w]
    z = m[:, 3 * w:3 * w + LANES]
    lane = lax.broadcasted_iota(jnp.int32, z.shape, 1)
    lor = _mm(jnp.where(lane < HEAD_DIM, jnp.tanh(z), z), lora2)
    lw = -DECAY_SCALE * _sigmoid(w0 + lor[:, 0:w])
    a_sig = _sigmoid(a0 + lor[:, w:2 * w])
    g = _mm(_sigmoid(m[:, 3 * w + LANES:3 * w + 2 * LANES]), g_up)
    kk = k * k_k
    kkn = kk * lax.rsqrt(jnp.maximum(_segsum(kk * kk, seg), 1e-24))
    k2 = k * (1.0 + (a_sig - 1.0) * k_a)
    return r, lw, k2, v, -kkn, kkn * a_sig, g


def _rwkv_post(y, r, k2, v, g, r_k, lnx_w, lnx_b, w_out, seg):
    inv = 1.0 / HEAD_DIM
    mean = _segsum(y, seg) * inv
    d = y - mean
    var = _segsum(d * d, seg) * inv
    yn = d * lax.rsqrt(var + LN_X_EPS) * lnx_w + lnx_b
    bonus = _segsum(r * k2 * r_k, seg) * v
    return _mm((yn + bonus) * g, w_out)


def _wkv_chunk_summaries(xs, m0, m1, strict, incl, eye2, eyew, bdmask):
    ln = xs[0][0].shape[0]
    m0b = m0 > 0.5
    swap = lambda x: pltpu.roll(x, HEAD_DIM, axis=1)
    h16 = lambda x: x.astype(BF16)
    zero16 = jnp.zeros((), BF16)
    bd = lambda x: jnp.concatenate([jnp.where(m0b, x, zero16), jnp.where(m0b, zero16, x)], axis=0)
    cat = jnp.concatenate
    rts, ats, bts, kts, bhs, khs, vs, wls = zip(*xs)
    at16 = [h16(x) for x in ats]
    v16 = [h16(x) for x in vs]
    bdv = [bd(x) for x in v16]
    gms = [_mmd(cat([bd(a), bd(h16(rt))], axis=0), cat([bt, kt], axis=0), _NT)
           for rt, a, bt, kt in zip(rts, at16, bts, kts)]
    ga0 = [jnp.where(strict, g[0:ln], 0.0) for g in gms]
    ga1 = [jnp.where(strict, g[ln:2 * ln], 0.0) for g in gms]
    gr0 = [jnp.where(incl, g[2 * ln:3 * ln], 0.0) for g in gms]
    gr1 = [jnp.where(incl, g[3 * ln:4 * ln], 0.0) for g in gms]
    n_ab = [jnp.where(m0b, x, swap(y)) for x, y in zip(ga0, ga1)]
    n_ak = [jnp.where(m0b, swap(x), y) for x, y in zip(ga0, ga1)]
    n_rb = [jnp.where(m0b, x, swap(y)) for x, y in zip(gr0, gr1)]
    n_rk = [jnp.where(m0b, swap(x), y) for x, y in zip(gr0, gr1)]
    tinv = [eye2 + n for n in n_ab]
    steps = (ln - 1).bit_length() - 1
    if steps > 0:
        npow = [h16(_mmd(n, bd(n))) for n in (h16(x) for x in n_ab)]
        for _ in range(steps - 1):
            both = [_mmd(cat([n, h16(t)], axis=0), bd(n)) for n, t in zip(npow, tinv)]
            tinv = [t + x[ln:2 * ln] for t, x in zip(tinv, both)]
            npow = [h16(x[0:ln]) for x in both]
        tinv = [t + _mmd(t, bd(n)) for t, n in zip(tinv, npow)]
    kv = [_mmd(cat([nk, nr], axis=0), b) for nk, nr, b in zip(n_ak, n_rk, bdv)]
    akv = [x[0:ln] for x in kv]
    rkv = [x[ln:2 * ln] for x in kv]
    tt = [_mmd(t, cat([bd(a), bd(h16(ak))], axis=1)) for t, a, ak in zip(tinv, at16, akv)]
    ta = [h16(x[:, 0:LANES]) for x in tt]
    tk = [h16(x[:, LANES:2 * LANES]) for x in tt]
    mc = [_mmd(cat([cat([a, k], axis=1), cat([jnp.zeros_like(v), v], axis=1)], axis=0), cat([bh, kh], axis=0), _TN)
          for a, k, v, bh, kh in zip(ta, tk, v16, bhs, khs)]
    ms = [eyew * wl + x[0:LANES] * bdmask for x, wl in zip(mc, wls)]
    cs = [x[LANES:2 * LANES] * bdmask for x in mc]
    qy = [_mmd(n, cat([bd(a), bd(k)], axis=1)) for n, a, k in zip(n_rb, ta, tk)]
    qs = [rt + x[:, 0:LANES] for rt, x in zip(rts, qy)]
    y0 = [x[:, LANES:2 * LANES] + r for x, r in zip(qy, rkv)]
    return list(zip(qs, y0, ms, cs))


def _rwkv_prompt_kernel(p_ref, mu_ref, w0_ref, lora_ref, a0_ref, gup_ref, kk_ref, ka_ref, rk_ref,
                        lnw_ref, lnb_ref, wout_ref, seg_ref, tri_ref,
                        oa_ref, tail_ref, wkv_ref, prev_scr, s_scr):
    t = pl.program_id(1)

    @pl.when(t == 0)
    def _init():
        prev_scr[...] = jnp.zeros_like(prev_scr)
        s_scr[...] = jnp.zeros_like(s_scr)

    p = p_ref[...]
    rows = p.shape[0]
    row = lax.broadcasted_iota(jnp.int32, p.shape, 0)
    prev = jnp.where(row == 0, prev_scr[SUBLANES - 1:SUBLANES, :], pltpu.roll(p, 1, axis=0))
    prev_scr[...] = p[rows - SUBLANES:rows, :]
    tail_ref[0] = p[rows - SUBLANES:rows, :]

    seg = seg_ref[...]
    r, lw, k2, v, aa, bb, g = _rwkv_pre(p, prev, mu_ref[...], w0_ref[...], lora_ref[...], a0_ref[...],
                                        gup_ref[...], kk_ref[...], ka_ref[...], seg)
    c = _cumsum_rows(tri_ref[...], lw)

    ln = CHUNK
    ri = lax.broadcasted_iota(jnp.int32, (ln, LANES), 0)
    ci = lax.broadcasted_iota(jnp.int32, (ln, LANES), 1)
    cj = jnp.where(ci < HEAD_DIM, ci, ci - HEAD_DIM)
    m0 = (ci < HEAD_DIM).astype(F32)
    m1 = 1.0 - m0
    strict = cj < ri
    incl = cj <= ri
    eye2 = (cj == ri).astype(F32)
    r2 = lax.broadcasted_iota(jnp.int32, (LANES, LANES), 0)
    c2 = lax.broadcasted_iota(jnp.int32, (LANES, LANES), 1)
    bdmask = ((r2 < HEAD_DIM) == (c2 < HEAD_DIM)).astype(F32)
    eyew = (r2 == c2).astype(F32)

    npairs = s_scr.shape[0]
    insts = []
    for ch in range(rows // ln):
        rs = slice(ch * ln, (ch + 1) * ln)
        c_, lw_ = c[rs], lw[rs]
        cl = c_[ln - 1:ln, :]
        e_n = jnp.exp(-c_)
        e_l = jnp.exp(cl - c_)
        rt = r[rs] * jnp.exp(c_)
        at = aa[rs] * jnp.exp(c_ - lw_)
        bt, kt = bb[rs] * e_n, k2[rs] * e_n
        bh, kh = bb[rs] * e_l, k2[rs] * e_l
        wl = jnp.exp(cl)
        v_ = v[rs]
        for pr in range(npairs):
            sl = slice(pr * LANES, (pr + 1) * LANES)
            insts.append((rt[:, sl], at[:, sl], bt[:, sl], kt[:, sl], bh[:, sl], kh[:, sl], v_[:, sl], wl[:, sl]))
    summaries = _wkv_chunk_summaries(insts, m0, m1, strict, incl, eye2, eyew, bdmask)

    states = [s_scr[pr] for pr in range(npairs)]
    y_rows = []
    for ch in range(rows // ln):
        ys = []
        for pr in range(npairs):
            q, y0, m, cc = summaries[ch * npairs + pr]
            ys.append(_mmd(q, states[pr], _NT) + y0)
            states[pr] = _mmd(states[pr], m) + cc
        y_rows.append(jnp.concatenate(ys, axis=1))
    for pr in range(npairs):
        s_scr[pr] = states[pr]
    y = jnp.concatenate(y_rows, axis=0)
    oa_ref[...] = _rwkv_post(y, r, k2, v, g, rk_ref[...], lnw_ref[...], lnb_ref[...], wout_ref[...], seg)

    @pl.when(t == pl.num_programs(1) - 1)
    def _fin():
        for pr in range(npairs):
            s = s_scr[pr]
            wkv_ref[0, 2 * pr] = s[0:HEAD_DIM, 0:HEAD_DIM]
            wkv_ref[0, 2 * pr + 1] = pltpu.roll(s[HEAD_DIM:LANES, :], HEAD_DIM, axis=1)[:, 0:HEAD_DIM]


def _rwkv_prompt(p, rw, b, t, tc):
    n, cp = p.shape
    w = rw["w0"].shape[-1]
    heads = w // HEAD_DIM
    nt = t // tc
    dm = rw["w_out"].shape[-1]
    tri = jnp.kron(jnp.eye(tc // CHUNK, dtype=BF16), rw["tri"])
    names = ("mu", "w0", "lora2", "a0", "g_up", "k_k", "k_a", "r_k", "lnx_w", "lnx_b", "w_out", "seg")
    return pl.pallas_call(
        _rwkv_prompt_kernel,
        grid=(b, nt),
        in_specs=[pl.BlockSpec((tc, cp), lambda i, j: (i * nt + j, 0))] + [_const(rw[k].shape) for k in names]
                 + [_const(tri.shape)],
        out_specs=[pl.BlockSpec((tc, dm), lambda i, j: (i * nt + j, 0)),
                   pl.BlockSpec((1, SUBLANES, cp), lambda i, j: (i, 0, 0)),
                   pl.BlockSpec((1, heads, HEAD_DIM, HEAD_DIM), lambda i, j: (i, 0, 0, 0))],
        out_shape=[jax.ShapeDtypeStruct((n, dm), F32),
                   jax.ShapeDtypeStruct((b, SUBLANES, cp), F32),
                   jax.ShapeDtypeStruct((b, heads, HEAD_DIM, HEAD_DIM), F32)],
        scratch_shapes=[pltpu.VMEM((SUBLANES, cp), F32), pltpu.VMEM((heads // 2, LANES, LANES), F32)],
        compiler_params=_params(("arbitrary", "arbitrary")),
    )(p, *[rw[k] for k in names], tri)


def _rwkv_sample_pre_kernel(p_ref, prev_ref, mu_ref, w0_ref, lora_ref, a0_ref, gup_ref, kk_ref, ka_ref, seg_ref,
                            rt_ref, wt_ref, kt_ref, vt_ref, at_ref, bt_ref, r_ref, k_ref, v_ref, g_ref):
    r, lw, k2, v, aa, bb, g = _rwkv_pre(p_ref[...], prev_ref[...], mu_ref[...], w0_ref[...], lora_ref[...],
                                        a0_ref[...], gup_ref[...], kk_ref[...], ka_ref[...], seg_ref[...])
    rt_ref[...] = r.T
    wt_ref[...] = jnp.exp(lw).T
    kt_ref[...] = k2.T
    vt_ref[...] = v.T
    at_ref[...] = aa.T
    bt_ref[...] = bb.T
    r_ref[...] = r
    k_ref[...] = k2
    v_ref[...] = v
    g_ref[...] = g


def _rwkv_sample_step_kernel(s_ref, r_ref, w_ref, k_ref, v_ref, a_ref, b_ref, y_ref, so_ref):
    r, w, k, a, b = r_ref[...], w_ref[...], k_ref[...], a_ref[...], b_ref[...]

    def body(i, carry):
        base = pl.multiple_of(i * SUBLANES, SUBLANES)
        vrows = v_ref[pl.ds(base, SUBLANES), :]
        ys = []
        for j in range(SUBLANES):
            s = s_ref[0, base + j]
            sa = jnp.sum(s * a, axis=0, keepdims=True)
            s_new = s * w + sa * b + vrows[j:j + 1, :] * k
            so_ref[0, base + j] = s_new
            ys.append(jnp.sum(s_new * r, axis=0, keepdims=True))
        y_ref[pl.ds(base, SUBLANES), :] = jnp.concatenate(ys, axis=0)
        return carry

    lax.fori_loop(0, s_ref.shape[1] // SUBLANES, body, 0)


def _rwkv_sample_post_kernel(yt_ref, r_ref, k_ref, v_ref, g_ref, rk_ref, lnw_ref, lnb_ref, wout_ref, seg_ref, oa_ref):
    oa_ref[...] = _rwkv_post(yt_ref[...].T, r_ref[...], k_ref[...], v_ref[...], g_ref[...], rk_ref[...],
                             lnw_ref[...], lnb_ref[...], wout_ref[...], seg_ref[...])


def _rwkv_sample(p, shift, wkv, rw):
    n, cp = p.shape
    w = rw["w0"].shape[-1]
    heads = w // HEAD_DIM
    dm = rw["w_out"].shape[-1]
    pre_names = ("mu", "w0", "lora2", "a0", "g_up", "k_k", "k_a", "seg")
    vec = jax.ShapeDtypeStruct((n, w), F32)
    vec_t = jax.ShapeDtypeStruct((w, n), F32)
    rt, wt, kt, vt, at, bt, r, k2, v, g = pl.pallas_call(
        _rwkv_sample_pre_kernel,
        grid=(1,),
        in_specs=[_const((n, cp)), _const((n, cp))] + [_const(rw[k].shape) for k in pre_names],
        out_specs=[_const((w, n))] * 6 + [_const((n, w))] * 4,
        out_shape=[vec_t] * 6 + [vec] * 4,
        compiler_params=_params(("arbitrary",)),
    )(p, shift, *[rw[k] for k in pre_names])

    head_spec = pl.BlockSpec((HEAD_DIM, n), lambda h: (h, 0))
    st_spec = pl.BlockSpec((1, HEAD_DIM, HEAD_DIM, n), lambda h: (h, 0, 0, 0))
    yt, wkv_new = pl.pallas_call(
        _rwkv_sample_step_kernel,
        grid=(heads,),
        in_specs=[st_spec] + [head_spec] * 6,
        out_specs=[head_spec, st_spec],
        out_shape=[vec_t, jax.ShapeDtypeStruct(wkv.shape, F32)],
        compiler_params=_params(("arbitrary",)),
    )(wkv, rt, wt, kt, vt, at, bt)

    post_names = ("r_k", "lnx_w", "lnx_b", "w_out", "seg")
    oa = pl.pallas_call(
        _rwkv_sample_post_kernel,
        grid=(1,),
        in_specs=[_const((w, n))] + [_const((n, w))] * 4 + [_const(rw[k].shape) for k in post_names],
        out_specs=_const((n, dm)),
        out_shape=jax.ShapeDtypeStruct((n, dm), F32),
        compiler_params=_params(("arbitrary",)),
    )(yt, r, k2, v, g, *[rw[k] for k in post_names])
    return oa, wkv_new


def _lru_gates(xc, gw, gb, lam):
    c = xc.shape[-1]
    gates = _mm(xc, gw) + gb
    gx = _sigmoid(gates[:, 0:c])
    ga = _sigmoid(gates[:, c:2 * c])
    log_a = -LRU_C * ga * _softplus(-lam)
    a = jnp.exp(log_a)
    u = jnp.sqrt(1.0 - a * a) * (gx * xc)
    return a, u


def _scan_rows(a, u, h0):
    rows, c = a.shape
    groups = rows // SUBLANES
    a3 = a.reshape(groups, SUBLANES, c)
    u3 = u.reshape(groups, SUBLANES, c)
    sub = lax.broadcasted_iota(jnp.int32, a3.shape, 1)
    k = 1
    while k < SUBLANES:
        keep = sub >= k
        u3 = u3 + a3 * jnp.where(keep, pltpu.roll(u3, k, axis=1), 0.0)
        a3 = a3 * jnp.where(keep, pltpu.roll(a3, k, axis=1), 1.0)
        k *= 2
    h = h0
    out = []
    for i in range(groups):
        hi = u3[i] + a3[i] * h
        out.append(hi)
        h = hi[SUBLANES - 1:SUBLANES, :]
    return jnp.concatenate(out, axis=0)


def _lru_prompt(x, tail, h0, cw_ref, cb_ref, gw, gb, lam):
    width = cw_ref.shape[0]
    xc = cb_ref[...] + cw_ref[width - 1:width, :] * x
    for j in range(1, width):
        xc = xc + cw_ref[width - 1 - j:width - j, :] * _shift_rows(x, tail, j)
    a, u = _lru_gates(xc, gw, gb, lam)
    return _scan_rows(a, u, h0)


def _lru_sample_kernel(x_ref, *refs):
    *buf_refs, h0_ref, cw_ref, cb_ref, gw_ref, gb_ref, lam_ref, h_ref = refs
    width = cw_ref.shape[0]
    xc = cb_ref[...] + cw_ref[width - 1:width, :] * x_ref[...]
    for j, buf_ref in enumerate(buf_refs):
        xc = xc + cw_ref[j:j + 1, :] * buf_ref[...]
    a, u = _lru_gates(xc, gw_ref[...], gb_ref[...], lam_ref[...])
    h_ref[...] = a * h0_ref[...] + u


def _lru_sample(xb, bufs, h0, lw):
    names = ("conv_w", "conv_b", "gate_w", "gate_b", "lam")
    args = (xb, *bufs, h0, *[lw[k] for k in names])
    return pl.pallas_call(
        _lru_sample_kernel,
        grid=(1,),
        in_specs=[_const(v.shape) for v in args],
        out_specs=_const(xb.shape),
        out_shape=jax.ShapeDtypeStruct(xb.shape, F32),
        compiler_params=_params(("arbitrary",)),
    )(*args)


def _mix_rows(x, oa, hs, gates, lwo, wo, g):
    d = x.shape[-1]
    ob = _mm(hs, lwo)
    mix = _mm(_sigmoid(gates[:, 0:d]) * oa + _sigmoid(gates[:, d:2 * d]) * ob, wo)
    return x + _rms(mix, g)


def _mix_sample_kernel(x_ref, oa_ref, hs_ref, gt_ref, lwo_ref, wo_ref, g_ref, o_ref):
    o_ref[...] = _mix_rows(x_ref[...], oa_ref[...], hs_ref[...], gt_ref[...], lwo_ref[...], wo_ref[...], g_ref[...])


def _mix_sample(x, oa, hs, gates, lwo16, wo16, g):
    args = (x, oa, hs, gates, lwo16, wo16, g)
    return pl.pallas_call(
        _mix_sample_kernel,
        grid=(1,),
        in_specs=[_const(v.shape) for v in args],
        out_specs=_const(x.shape),
        out_shape=jax.ShapeDtypeStruct(x.shape, F32),
        compiler_params=_params(("arbitrary",)),
    )(*args)


def _mix_prompt_kernel(x_ref, oa_ref, xb_ref, gt_ref, cw_ref, cb_ref, gw_ref, gb_ref, lam_ref, lwo_ref, wo_ref, g_ref,
                       o_ref, xtail_ref, htail_ref, xtail_scr, h_scr):
    @pl.when(pl.program_id(1) == 0)
    def _init():
        xtail_scr[...] = jnp.zeros_like(xtail_scr)
        h_scr[...] = jnp.zeros_like(h_scr)

    xb = xb_ref[...]
    rows = xb.shape[0]
    hs = _lru_prompt(xb, xtail_scr[...], h_scr[SUBLANES - 1:SUBLANES, :], cw_ref, cb_ref,
                     gw_ref[...], gb_ref[...], lam_ref[...])
    xtail_scr[...] = xb[rows - SUBLANES:rows, :]
    xtail_ref[0] = xb[rows - SUBLANES:rows, :]
    h_scr[...] = hs[rows - SUBLANES:rows, :]
    htail_ref[0] = hs[rows - SUBLANES:rows, :]
    o_ref[...] = _mix_rows(x_ref[...], oa_ref[...], hs, gt_ref[...], lwo_ref[...], wo_ref[...], g_ref[...])


def _mix_prompt(x, oa, xb, gates, lw, lwo16, wo16, g, b, t, tm):
    n, d = x.shape
    c = xb.shape[1]
    nt = t // tm
    names = ("conv_w", "conv_b", "gate_w", "gate_b", "lam")
    consts = [lw[k] for k in names] + [lwo16, wo16, g]
    rows = lambda w: pl.BlockSpec((tm, w), lambda i, j: (i * nt + j, 0))
    tail = pl.BlockSpec((1, SUBLANES, c), lambda i, j: (i, 0, 0))
    return pl.pallas_call(
        _mix_prompt_kernel,
        grid=(b, nt),
        in_specs=[rows(d), rows(d), rows(c), rows(2 * d)] + [_const(v.shape) for v in consts],
        out_specs=[rows(d), tail, tail],
        out_shape=[jax.ShapeDtypeStruct((n, d), F32), jax.ShapeDtypeStruct((b, SUBLANES, c), F32),
                   jax.ShapeDtypeStruct((b, SUBLANES, c), F32)],
        scratch_shapes=[pltpu.VMEM((SUBLANES, c), F32), pltpu.VMEM((SUBLANES, c), F32)],
        compiler_params=_params(("arbitrary", "arbitrary")),
    )(x, oa, xb, gates, *consts)


def _ffn_body(x1, gpre, gpost, up_ref, cw_ref, cb_ref, down_ref, hist, keep):
    dff = down_ref.shape[0]
    hn = _rms(x1, gpre).astype(BF16)
    starts = list(range(0, dff, FF_COLS))

    def up(c0):
        return [(slice(off, off + FF_COLS), jnp.dot(hn, up_ref[:, off:off + FF_COLS], preferred_element_type=F32))
                for off in (c0, dff + c0)]

    ups = [up(c0) for c0 in starts[:FFN_AHEAD]]
    f = jnp.zeros(x1.shape, F32)
    for i, c0 in enumerate(starts):
        if i + FFN_AHEAD < len(starts):
            ups.append(up(starts[i + FFN_AHEAD]))
        halves = []
        for cols, u in ups[i]:
            u1, u2 = hist(u, cols)
            halves.append(cb_ref[:, cols] + cw_ref[2:3, cols] * u + cw_ref[1:2, cols] * u1 + cw_ref[0:1, cols] * u2)
            keep(u, cols)
        act = _gelu_tanh(halves[0]) * halves[1]
        f = f + jnp.dot(act.astype(BF16), down_ref[c0:c0 + FF_COLS, :], preferred_element_type=F32)
    return x1 + _rms(f, gpost)


def _ffn_prompt_kernel(x_ref, gpre_ref, gpost_ref, up_ref, cw_ref, cb_ref, down_ref, y_ref, tail_ref, tail_scr):
    @pl.when(pl.program_id(1) == 0)
    def _init():
        tail_scr[...] = jnp.zeros_like(tail_scr)

    rows = x_ref.shape[0]

    def hist(u, cols):
        tail = tail_scr[:, cols]
        return _shift_rows(u, tail, 1), _shift_rows(u, tail, 2)

    def keep(u, cols):
        tail_scr[:, cols] = u[rows - SUBLANES:rows, :]
        tail_ref[0, :, cols] = u[rows - SUBLANES:rows, :]

    y_ref[...] = _ffn_body(x_ref[...], gpre_ref[...], gpost_ref[...], up_ref, cw_ref, cb_ref, down_ref, hist, keep)


def _ffn_prompt(x1, fw, b, t, tm):
    n, d = x1.shape
    nt = t // tm
    dff2 = fw["up"].shape[1]
    names = ("g_pre", "g_post", "up", "conv_w", "conv_b", "down")
    rows = pl.BlockSpec((tm, d), lambda i, j: (i * nt + j, 0))
    return pl.pallas_call(
        _ffn_prompt_kernel,
        grid=(b, nt),
        in_specs=[rows] + [_const(fw[k].shape) for k in names],
        out_specs=[rows, pl.BlockSpec((1, SUBLANES, dff2), lambda i, j: (i, 0, 0))],
        out_shape=[jax.ShapeDtypeStruct((n, d), F32), jax.ShapeDtypeStruct((b, SUBLANES, dff2), F32)],
        scratch_shapes=[pltpu.VMEM((SUBLANES, dff2), F32)],
        compiler_params=_params(("arbitrary", "arbitrary")),
    )(x1, *[fw[k] for k in names])


def _ffn_sample_kernel(x_ref, buf_ref, gpre_ref, gpost_ref, up_ref, cw_ref, cb_ref, down_ref, y_ref, nb_ref):
    def hist(u, cols):
        return buf_ref[:, 1, cols], buf_ref[:, 0, cols]

    def keep(u, cols):
        nb_ref[:, 0, cols] = buf_ref[:, 1, cols]
        nb_ref[:, 1, cols] = u

    y_ref[...] = _ffn_body(x_ref[...], gpre_ref[...], gpost_ref[...], up_ref, cw_ref, cb_ref, down_ref, hist, keep)


def _ffn_sample(x1, buf, fw):
    names = ("g_pre", "g_post", "up", "conv_w", "conv_b", "down")
    args = (x1, buf, *[fw[k] for k in names])
    return pl.pallas_call(
        _ffn_sample_kernel,
        grid=(1,),
        in_specs=[_const(v.shape) for v in args],
        out_specs=[_const(x1.shape), _const(buf.shape)],
        out_shape=[jax.ShapeDtypeStruct(x1.shape, F32), jax.ShapeDtypeStruct(buf.shape, F32)],
        compiler_params=_params(("arbitrary",)),
    )(*args)


def _block_diag(blocks):
    n, bi, bj = blocks.shape
    eye = jnp.eye(n, dtype=blocks.dtype)
    return jnp.einsum("nij,nm->nimj", blocks, eye).reshape(n * bi, n * bj)


def _layer_weights(norm_pre_mix, norm_post_mix, norm_pre_ffn, norm_post_ffn, w_in,
                   rwkv_mu, rwkv_w0, rwkv_w_up, rwkv_a0, rwkv_a_up, rwkv_g_up, rwkv_k_k, rwkv_k_a,
                   rwkv_r_k, rwkv_lnx_w, rwkv_lnx_b, rwkv_w_out,
                   lru_conv_w, lru_conv_b, lru_gx_w, lru_gx_b, lru_ga_w, lru_ga_b, lru_lambda, lru_w_out,
                   w_o, ffn_up, ffn_conv_w, ffn_conv_b, ffn_down):
    row = lambda x: x.reshape(1, -1)
    w = rwkv_w0.shape[-1]
    head_of = jnp.arange(w) // HEAD_DIM
    t_idx = jnp.arange(CHUNK)
    lo_w, lo_a = rwkv_w_up.shape[0], rwkv_a_up.shape[0]
    lora2 = jnp.zeros((lo_w + lo_a, 2 * w), F32).at[:lo_w, :w].set(rwkv_w_up).at[lo_w:, w:].set(rwkv_a_up)
    rw = dict(mu=row(rwkv_mu), w0=row(rwkv_w0), lora2=lora2.astype(BF16), a0=row(rwkv_a0),
              g_up=rwkv_g_up.astype(BF16), k_k=row(rwkv_k_k), k_a=row(rwkv_k_a), r_k=row(rwkv_r_k),
              lnx_w=row(rwkv_lnx_w), lnx_b=row(rwkv_lnx_b), w_out=rwkv_w_out.astype(BF16),
              seg=(head_of[:, None] == head_of[None, :]).astype(BF16),
              tri=(t_idx[:, None] >= t_idx[None, :]).astype(BF16))
    lw = dict(conv_w=lru_conv_w, conv_b=row(lru_conv_b),
              gate_w=jnp.concatenate([_block_diag(lru_gx_w), _block_diag(lru_ga_w)], axis=1).astype(BF16),
              gate_b=jnp.concatenate([row(lru_gx_b), row(lru_ga_b)], axis=1), lam=row(lru_lambda))
    fw = dict(g_pre=row(norm_pre_ffn), g_post=row(norm_post_ffn), up=ffn_up.astype(BF16),
              conv_w=ffn_conv_w, conv_b=row(ffn_conv_b), down=ffn_down.astype(BF16))
    return dict(g_in=row(norm_pre_mix), w_in=w_in.astype(BF16), rw=rw, lw=lw, fw=fw,
                lru_w_out=lru_w_out.astype(BF16), w_o=w_o.astype(BF16), g_mix=row(norm_post_mix))


def _splits(wts):
    w = wts["rw"]["w0"].shape[-1]
    c_rwkv = wts["rw"]["mu"].shape[-1]
    c_lru = wts["lw"]["lam"].shape[-1]
    return (c_rwkv, c_lru, wts["w_in"].shape[1] - c_rwkv - c_lru)


def _prompt_layer(x, wts):
    b, t, d = x.shape
    x2 = x.reshape(b * t, d)
    tm = min(t, ROW_TILE)
    p, xb, gates = _proj(x2, wts["g_in"], wts["w_in"], _splits(wts), tm)
    oa, p_tail, wkv = _rwkv_prompt(p, wts["rw"], b, t, min(t, WKV_ROWS))
    x1, xb_tail, h_tail = _mix_prompt(x2, oa, xb, gates, wts["lw"], wts["lru_w_out"], wts["w_o"], wts["g_mix"],
                                      b, t, tm)
    y, u_tail = _ffn_prompt(x1, wts["fw"], b, t, min(t, FFN_ROWS))
    conv_w = wts["lw"]["conv_w"].shape[0]
    ffn_w = wts["fw"]["conv_w"].shape[0]
    state = (p_tail[:, SUBLANES - 1:], wkv, xb_tail[:, SUBLANES - (conv_w - 1):], h_tail[:, SUBLANES - 1],
             u_tail[:, SUBLANES - (ffn_w - 1):])
    return y.reshape(b, t, d), state


def _sample_layer(x, shift, wkv, lru_buf, h0, ffn_buf, wts):
    n, t, d = x.shape
    x2 = x.reshape(n, d)
    p, xb, gates = _proj(x2, wts["g_in"], wts["w_in"], _splits(wts), n)
    oa, wkv_t = _rwkv_sample(p, shift[:, 0, :], jnp.transpose(wkv, (1, 2, 3, 0)), wts["rw"])
    lru_bufs = [lru_buf[:, j, :] for j in range(lru_buf.shape[1])]
    h = _lru_sample(xb, lru_bufs, h0, wts["lw"])
    x1 = _mix_sample(x2, oa, h, gates, wts["lru_w_out"], wts["w_o"], wts["g_mix"])
    y, ffn_new = _ffn_sample(x1, ffn_buf, wts["fw"])
    state = (p.reshape(n, 1, -1), jnp.transpose(wkv_t, (3, 0, 1, 2)), jnp.stack(lru_bufs[1:] + [xb], axis=1), h,
             ffn_new)
    return y.reshape(n, t, d), state


def kernel(x_prompt, x_sample, state_rwkv_shift, state_rwkv_wkv, state_lru_conv, state_lru_h, state_ffn_conv,
           norm_pre_mix, norm_post_mix, norm_pre_ffn, norm_post_ffn, w_in,
           rwkv_mu, rwkv_w0, rwkv_w_up, rwkv_a0, rwkv_a_up, rwkv_g_up, rwkv_k_k, rwkv_k_a,
           rwkv_r_k, rwkv_lnx_w, rwkv_lnx_b, rwkv_w_out,
           lru_conv_w, lru_conv_b, lru_gx_w, lru_gx_b, lru_ga_w, lru_ga_b, lru_lambda, lru_w_out,
           w_o, ffn_up, ffn_conv_w, ffn_conv_b, ffn_down):
    params = (norm_pre_mix, norm_post_mix, norm_pre_ffn, norm_post_ffn, w_in,
              rwkv_mu, rwkv_w0, rwkv_w_up, rwkv_a0, rwkv_a_up, rwkv_g_up, rwkv_k_k, rwkv_k_a,
              rwkv_r_k, rwkv_lnx_w, rwkv_lnx_b, rwkv_w_out,
              lru_conv_w, lru_conv_b, lru_gx_w, lru_gx_b, lru_ga_w, lru_ga_b, lru_lambda, lru_w_out,
              w_o, ffn_up, ffn_conv_w, ffn_conv_b, ffn_down)
    depth = w_in.shape[0]
    assert depth == 1 and x_sample.shape[1] == 1
    yp, ys = x_prompt, x_sample
    new_p, new_s = [], []
    for l in range(depth):
        wts = _layer_weights(*[q[l] for q in params])
        yp, st_p = _prompt_layer(yp, wts)
        ys, st_s = _sample_layer(ys, state_rwkv_shift[l], state_rwkv_wkv[l], state_lru_conv[l],
                                 state_lru_h[l], state_ffn_conv[l], wts)
        new_p.append(st_p)
        new_s.append(st_s)
    stk = lambda lst, i: jnp.stack([s[i] for s in lst], axis=0)
    return (yp, ys,
            stk(new_p, 0), stk(new_p, 1), stk(new_p, 2), stk(new_p, 3), stk(new_p, 4),
            stk(new_s, 0), stk(new_s, 1), stk(new_s, 2), stk(new_s, 3), stk(new_s, 4))
```

```python
import functools
import math

import jax
import jax.numpy as jnp
from jax import lax
from jax.experimental import pallas as pl
from jax.experimental.pallas import tpu as pltpu

F32 = jnp.float32
BF16 = jnp.bfloat16

NORM_EPS = 1e-6
LN_X_EPS = 64e-5
LRU_C = 8.0
DECAY_SCALE = math.exp(-0.5)
HEAD_DIM = 64
LANES = 128
SUBLANES = 8
CHUNK = 64
ROW_TILE = 512
WKV_ROWS = 512
GATE_COLS = 512
SHIFT_COLS = 256
FF_COLS = 256
FFN_ROWS = 256
FFN_AHEAD = 3
VMEM_LIMIT = 56 * 1024 * 1024

_NN = (((1,), (0,)), ((), ()))
_NT = (((1,), (1,)), ((), ()))
_TN = (((0,), (0,)), ((), ()))


def _rms(x, g):
    return x * lax.rsqrt(jnp.mean(x * x, axis=-1, keepdims=True) + NORM_EPS) * g


def _softplus(x):
    return jnp.maximum(x, 0.0) + jnp.log1p(jnp.exp(-jnp.abs(x)))


def _sigmoid(x):
    return 0.5 * jnp.tanh(0.5 * x) + 0.5


def _gelu_tanh(x):
    c = 0.7978845608028654
    hx = 0.5 * x
    return hx + hx * jnp.tanh(x * (c + (c * 0.044715) * (x * x)))


def _mm(a, b):
    return jnp.dot(a.astype(BF16), b.astype(BF16), preferred_element_type=F32)


def _mmd(a, b, dims=_NN):
    return lax.dot_general(a.astype(BF16), b.astype(BF16), dims, preferred_element_type=F32)


def _segsum(x, seg):
    return jnp.dot(x.astype(BF16), seg, preferred_element_type=F32)


def _cumsum_rows(tri, x):
    h1 = x.astype(BF16)
    r1 = x - h1.astype(F32)
    h2 = r1.astype(BF16)
    h3 = (r1 - h2.astype(F32)).astype(BF16)
    d = lambda y: jnp.dot(tri, y, preferred_element_type=F32)
    return d(h1) + (d(h2) + d(h3))


def _shift_rows(x, tail, j):
    xs = pltpu.roll(x, j, axis=0)
    ts = pltpu.roll(tail, j, axis=0)
    row = lax.broadcasted_iota(jnp.int32, ts.shape, 0)
    head = jnp.where(row < j, ts, xs[0:SUBLANES])
    if x.shape[0] == SUBLANES:
        return head
    return jnp.concatenate([head, xs[SUBLANES:]], axis=0)


def _const(shape):
    n = len(shape)
    return pl.BlockSpec(shape, lambda *_: (0,) * n, pipeline_mode=pl.Buffered(1))


def _params(sem):
    return pltpu.CompilerParams(dimension_semantics=sem, vmem_limit_bytes=VMEM_LIMIT)


def _proj_kernel(x_ref, g_ref, w_ref, p_ref, xb_ref, gt_ref, *, splits):
    xn = _rms(x_ref[...], g_ref[...]).astype(BF16)
    c0, c1, c2 = splits
    p_ref[...] = jnp.dot(xn, w_ref[:, 0:c0], preferred_element_type=F32)
    xb_ref[...] = jnp.dot(xn, w_ref[:, c0:c0 + c1], preferred_element_type=F32)
    gt_ref[...] = jnp.dot(xn, w_ref[:, c0 + c1:c0 + c1 + c2], preferred_element_type=F32)


def _proj_prompt_kernel(x_ref, g_ref, w_ref, mu_ref, m_ref, xb_ref, gt_ref, tail_ref, prev_scr, *, splits):
    @pl.when(pl.program_id(1) == 0)
    def _init():
        prev_scr[...] = jnp.zeros_like(prev_scr)

    xn = _rms(x_ref[...], g_ref[...]).astype(BF16)
    c0, c1, c2 = splits
    g0 = c0 + c1
    p = jnp.dot(xn, w_ref[:, 0:c0], preferred_element_type=F32)
    xb_ref[...] = jnp.dot(xn, w_ref[:, c0:g0], preferred_element_type=F32)
    rows = p.shape[0]
    row = lax.broadcasted_iota(jnp.int32, (rows, SHIFT_COLS), 0)
    shift_chunks = list(range(0, c0, SHIFT_COLS))
    gate_chunks = list(range(0, c2, GATE_COLS))
    per_gate = -(-len(shift_chunks) // len(gate_chunks))
    for i, off in enumerate(gate_chunks):
        gt_ref[:, off:off + GATE_COLS] = jnp.dot(xn, w_ref[:, g0 + off:g0 + off + GATE_COLS],
                                                 preferred_element_type=F32)
        for s0 in shift_chunks[i * per_gate:(i + 1) * per_gate]:
            cs = slice(s0, s0 + SHIFT_COLS)
            pc = p[:, cs]
            prev = jnp.where(row == 0, prev_scr[SUBLANES - 1:SUBLANES, cs], pltpu.roll(pc, 1, axis=0))
            m_ref[:, cs] = pc + mu_ref[:, cs] * (prev - pc)
    prev_scr[...] = p[rows - SUBLANES:rows, :]
    tail_ref[0] = p[rows - SUBLANES:rows, :]


def _proj_prompt(x, g, w_in16, mu, splits, b, t, tm):
    n, d = x.shape
    c0, c1, c2 = splits
    nt = t // tm
    rows = lambda c: pl.BlockSpec((tm, c), lambda i, j: (i * nt + j, 0))
    return pl.pallas_call(
        functools.partial(_proj_prompt_kernel, splits=splits),
        grid=(b, nt),
        in_specs=[rows(d), _const((1, d)), _const(w_in16.shape), _const(mu.shape)],
        out_specs=[rows(c0), rows(c1), rows(c2), pl.BlockSpec((1, SUBLANES, c0), lambda i, j: (i, 0, 0))],
        out_shape=[jax.ShapeDtypeStruct((n, c0), F32), jax.ShapeDtypeStruct((n, c1), F32),
                   jax.ShapeDtypeStruct((n, c2), F32), jax.ShapeDtypeStruct((b, SUBLANES, c0), F32)],
        scratch_shapes=[pltpu.VMEM((SUBLANES, c0), F32)],
        compiler_params=_params(("arbitrary", "arbitrary")),
    )(x, g, w_in16, mu)


def _proj(x, g, w_in16, splits, tm):
    n, d = x.shape
    c0, c1, c2 = splits
    return pl.pallas_call(
        functools.partial(_proj_kernel, splits=splits),
        grid=(n // tm,),
        in_specs=[pl.BlockSpec((tm, d), lambda i: (i, 0)), _const((1, d)), _const(w_in16.shape)],
        out_specs=[pl.BlockSpec((tm, c0), lambda i: (i, 0)),
                   pl.BlockSpec((tm, c1), lambda i: (i, 0)),
                   pl.BlockSpec((tm, c2), lambda i: (i, 0))],
        out_shape=[jax.ShapeDtypeStruct((n, c0), F32), jax.ShapeDtypeStruct((n, c1), F32),
                   jax.ShapeDtypeStruct((n, c2), F32)],
        compiler_params=_params(("arbitrary",)),
    )(x, g, w_in16)


def _rwkv_pre(m, w0, lora2, a0, g_up, k_k, k_a, seg):
    w = w0.shape[-1]
    r, k, v = m[:, 0:w], m[:, w:2 * w], m[:, 2 * w:3 * w]
    z = m[:, 3 * w:3 * w + LANES]
    lane = lax.broadcasted_iota(jnp.int32, z.shape, 1)
    lor = _mm(jnp.where(lane < HEAD_DIM, jnp.tanh(z), z), lora2)
    lw = -DECAY_SCALE * _sigmoid(w0 + lor[:, 0:w])
    a_sig = _sigmoid(a0 + lor[:, w:2 * w])
    g = _mm(_sigmoid(m[:, 3 * w + LANES:3 * w + 2 * LANES]), g_up)
    kk = k * k_k
    kkn = kk * lax.rsqrt(jnp.maximum(_segsum(kk * kk, seg), 1e-24))
    k2 = k * (1.0 + (a_sig - 1.0) * k_a)
    return r, lw, k2, v, -kkn, kkn * a_sig, g


def _rwkv_post(y, r, k2, v, g, r_k, lnx_w, lnx_b, w_out, seg):
    inv = 1.0 / HEAD_DIM
    mean = _segsum(y, seg) * inv
    d = y - mean
    var = _segsum(d * d, seg) * inv
    yn = d * lax.rsqrt(var + LN_X_EPS) * lnx_w + lnx_b
    bonus = _segsum(r * k2 * r_k, seg) * v
    return _mm((yn + bonus) * g, w_out)


def _wkv_chunk_summaries(xs, m0b, strict, incl, eye2, eyew, bdmask):
    ln = xs[0][0].shape[0]
    swap = lambda x: pltpu.roll(x, HEAD_DIM, axis=1)
    h16 = lambda x: x.astype(BF16)
    zero16 = jnp.zeros((), BF16)
    bd = lambda x: jnp.concatenate([jnp.where(m0b, x, zero16), jnp.where(m0b, zero16, x)], axis=0)
    cat = jnp.concatenate
    rts, ats, bts, kts, bhs, khs, vs, wls = zip(*xs)
    at16 = [h16(x) for x in ats]
    v16 = [h16(x) for x in vs]
    bdv = [bd(x) for x in v16]
    gms = [_mmd(cat([bd(a), bd(h16(rt))], axis=0), cat([bt, kt], axis=0), _NT)
           for rt, a, bt, kt in zip(rts, at16, bts, kts)]
    ga0 = [jnp.where(strict, g[0:ln], 0.0) for g in gms]
    ga1 = [jnp.where(strict, g[ln:2 * ln], 0.0) for g in gms]
    gr0 = [jnp.where(incl, g[2 * ln:3 * ln], 0.0) for g in gms]
    gr1 = [jnp.where(incl, g[3 * ln:4 * ln], 0.0) for g in gms]
    n_ab = [jnp.where(m0b, x, swap(y)) for x, y in zip(ga0, ga1)]
    n_ak = [jnp.where(m0b, swap(x), y) for x, y in zip(ga0, ga1)]
    n_rb = [jnp.where(m0b, x, swap(y)) for x, y in zip(gr0, gr1)]
    n_rk = [jnp.where(m0b, swap(x), y) for x, y in zip(gr0, gr1)]
    tinv = [eye2 + n for n in n_ab]
    steps = (ln - 1).bit_length() - 1
    if steps > 0:
        npow = [h16(_mmd(n, bd(n))) for n in (h16(x) for x in n_ab)]
        for _ in range(steps - 1):
            both = [_mmd(cat([n, h16(t)], axis=0), bd(n)) for n, t in zip(npow, tinv)]
            tinv = [t + x[ln:2 * ln] for t, x in zip(tinv, both)]
            npow = [h16(x[0:ln]) for x in both]
        tinv = [t + _mmd(t, bd(n)) for t, n in zip(tinv, npow)]
    kv = [_mmd(cat([nk, nr], axis=0), b) for nk, nr, b in zip(n_ak, n_rk, bdv)]
    akv = [x[0:ln] for x in kv]
    rkv = [x[ln:2 * ln] for x in kv]
    tt = [_mmd(t, cat([bd(a), bd(h16(ak))], axis=1)) for t, a, ak in zip(tinv, at16, akv)]
    ta = [h16(x[:, 0:LANES]) for x in tt]
    tk = [h16(x[:, LANES:2 * LANES]) for x in tt]
    mc = [_mmd(cat([cat([a, k], axis=1), cat([jnp.zeros_like(v), v], axis=1)], axis=0), cat([bh, kh], axis=0), _TN)
          for a, k, v, bh, kh in zip(ta, tk, v16, bhs, khs)]
    ms = [eyew * wl + x[0:LANES] * bdmask for x, wl in zip(mc, wls)]
    cs = [x[LANES:2 * LANES] * bdmask for x in mc]
    qy = [_mmd(n, cat([bd(a), bd(k)], axis=1)) for n, a, k in zip(n_rb, ta, tk)]
    qs = [rt + x[:, 0:LANES] for rt, x in zip(rts, qy)]
    y0 = [x[:, LANES:2 * LANES] + r for x, r in zip(qy, rkv)]
    return list(zip(qs, y0, ms, cs))


def _rwkv_prompt_kernel(m_ref, w0_ref, lora_ref, a0_ref, gup_ref, kk_ref, ka_ref, rk_ref,
                        lnw_ref, lnb_ref, wout_ref, seg_ref, tri_ref, oa_ref, wkv_ref, s_scr):
    t = pl.program_id(1)

    @pl.when(t == 0)
    def _init():
        s_scr[...] = jnp.zeros_like(s_scr)

    rows = m_ref.shape[0]
    seg = seg_ref[...]
    r, lw, k2, v, aa, bb, g = _rwkv_pre(m_ref[...], w0_ref[...], lora_ref[...], a0_ref[...],
                                        gup_ref[...], kk_ref[...], ka_ref[...], seg)
    c = _cumsum_rows(tri_ref[...], lw)

    ln = CHUNK
    ri = lax.broadcasted_iota(jnp.int32, (ln, LANES), 0)
    ci = lax.broadcasted_iota(jnp.int32, (ln, LANES), 1)
    cj = jnp.where(ci < HEAD_DIM, ci, ci - HEAD_DIM)
    m0b = ci < HEAD_DIM
    strict = cj < ri
    incl = cj <= ri
    eye2 = (cj == ri).astype(F32)
    r2 = lax.broadcasted_iota(jnp.int32, (LANES, LANES), 0)
    c2 = lax.broadcasted_iota(jnp.int32, (LANES, LANES), 1)
    bdmask = ((r2 < HEAD_DIM) == (c2 < HEAD_DIM)).astype(F32)
    eyew = (r2 == c2).astype(F32)

    npairs = s_scr.shape[0]
    insts = []
    for ch in range(rows // ln):
        rs = slice(ch * ln, (ch + 1) * ln)
        c_, lw_ = c[rs], lw[rs]
        cl = c_[ln - 1:ln, :]
        e_n = jnp.exp(-c_)
        e_l = jnp.exp(cl - c_)
        rt = r[rs] * jnp.exp(c_)
        at = aa[rs] * jnp.exp(c_ - lw_)
        bt, kt = bb[rs] * e_n, k2[rs] * e_n
        bh, kh = bb[rs] * e_l, k2[rs] * e_l
        wl = jnp.exp(cl)
        v_ = v[rs]
        for pr in range(npairs):
            sl = slice(pr * LANES, (pr + 1) * LANES)
            insts.append((rt[:, sl], at[:, sl], bt[:, sl], kt[:, sl], bh[:, sl], kh[:, sl], v_[:, sl], wl[:, sl]))
    summaries = _wkv_chunk_summaries(insts, m0b, strict, incl, eye2, eyew, bdmask)

    states = [s_scr[pr] for pr in range(npairs)]
    y_rows = []
    for ch in range(rows // ln):
        ys = []
        for pr in range(npairs):
            q, y0, m, cc = summaries[ch * npairs + pr]
            ys.append(_mmd(q, states[pr], _NT) + y0)
            states[pr] = _mmd(states[pr], m) + cc
        y_rows.append(jnp.concatenate(ys, axis=1))
    for pr in range(npairs):
        s_scr[pr] = states[pr]
    y = jnp.concatenate(y_rows, axis=0)
    oa_ref[...] = _rwkv_post(y, r, k2, v, g, rk_ref[...], lnw_ref[...], lnb_ref[...], wout_ref[...], seg)

    @pl.when(t == pl.num_programs(1) - 1)
    def _fin():
        for pr in range(npairs):
            s = s_scr[pr]
            wkv_ref[0, 2 * pr] = s[0:HEAD_DIM, 0:HEAD_DIM]
            wkv_ref[0, 2 * pr + 1] = pltpu.roll(s[HEAD_DIM:LANES, :], HEAD_DIM, axis=1)[:, 0:HEAD_DIM]


def _rwkv_prompt(m, rw, b, t, tc):
    n, cp = m.shape
    w = rw["w0"].shape[-1]
    heads = w // HEAD_DIM
    nt = t // tc
    dm = rw["w_out"].shape[-1]
    tri = jnp.kron(jnp.eye(tc // CHUNK, dtype=BF16), rw["tri"])
    names = ("w0", "lora2", "a0", "g_up", "k_k", "k_a", "r_k", "lnx_w", "lnx_b", "w_out", "seg")
    return pl.pallas_call(
        _rwkv_prompt_kernel,
        grid=(b, nt),
        in_specs=[pl.BlockSpec((tc, cp), lambda i, j: (i * nt + j, 0))] + [_const(rw[k].shape) for k in names]
                 + [_const(tri.shape)],
        out_specs=[pl.BlockSpec((tc, dm), lambda i, j: (i * nt + j, 0)),
                   pl.BlockSpec((1, heads, HEAD_DIM, HEAD_DIM), lambda i, j: (i, 0, 0, 0))],
        out_shape=[jax.ShapeDtypeStruct((n, dm), F32),
                   jax.ShapeDtypeStruct((b, heads, HEAD_DIM, HEAD_DIM), F32)],
        scratch_shapes=[pltpu.VMEM((heads // 2, LANES, LANES), F32)],
        compiler_params=_params(("arbitrary", "arbitrary")),
    )(m, *[rw[k] for k in names], tri)


def _rwkv_sample_pre_kernel(p_ref, prev_ref, mu_ref, w0_ref, lora_ref, a0_ref, gup_ref, kk_ref, ka_ref, seg_ref,
                            rt_ref, wt_ref, kt_ref, vt_ref, at_ref, bt_ref, r_ref, k_ref, v_ref, g_ref):
    p = p_ref[...]
    r, lw, k2, v, aa, bb, g = _rwkv_pre(p + mu_ref[...] * (prev_ref[...] - p), w0_ref[...], lora_ref[...],
                                        a0_ref[...], gup_ref[...], kk_ref[...], ka_ref[...], seg_ref[...])
    rt_ref[...] = r.T
    wt_ref[...] = jnp.exp(lw).T
    kt_ref[...] = k2.T
    vt_ref[...] = v.T
    at_ref[...] = aa.T
    bt_ref[...] = bb.T
    r_ref[...] = r
    k_ref[...] = k2
    v_ref[...] = v
    g_ref[...] = g


def _rwkv_sample_step_kernel(s_ref, r_ref, w_ref, k_ref, v_ref, a_ref, b_ref, y_ref, so_ref):
    r, w, k, a, b = r_ref[...], w_ref[...], k_ref[...], a_ref[...], b_ref[...]

    def body(i, carry):
        base = pl.multiple_of(i * SUBLANES, SUBLANES)
        vrows = v_ref[pl.ds(base, SUBLANES), :]
        ys = []
        for j in range(SUBLANES):
            s = s_ref[0, base + j]
            sa = jnp.sum(s * a, axis=0, keepdims=True)
            s_new = s * w + sa * b + vrows[j:j + 1, :] * k
            so_ref[0, base + j] = s_new
            ys.append(jnp.sum(s_new * r, axis=0, keepdims=True))
        y_ref[pl.ds(base, SUBLANES), :] = jnp.concatenate(ys, axis=0)
        return carry

    lax.fori_loop(0, s_ref.shape[1] // SUBLANES, body, 0)


def _rwkv_sample_post_kernel(yt_ref, r_ref, k_ref, v_ref, g_ref, rk_ref, lnw_ref, lnb_ref, wout_ref, seg_ref, oa_ref):
    oa_ref[...] = _rwkv_post(yt_ref[...].T, r_ref[...], k_ref[...], v_ref[...], g_ref[...], rk_ref[...],
                             lnw_ref[...], lnb_ref[...], wout_ref[...], seg_ref[...])


def _rwkv_sample(p, shift, wkv, rw):
    n, cp = p.shape
    w = rw["w0"].shape[-1]
    heads = w // HEAD_DIM
    dm = rw["w_out"].shape[-1]
    pre_names = ("mu", "w0", "lora2", "a0", "g_up", "k_k", "k_a", "seg")
    vec = jax.ShapeDtypeStruct((n, w), F32)
    vec_t = jax.ShapeDtypeStruct((w, n), F32)
    rt, wt, kt, vt, at, bt, r, k2, v, g = pl.pallas_call(
        _rwkv_sample_pre_kernel,
        grid=(1,),
        in_specs=[_const((n, cp)), _const((n, cp))] + [_const(rw[k].shape) for k in pre_names],
        out_specs=[_const((w, n))] * 6 + [_const((n, w))] * 4,
        out_shape=[vec_t] * 6 + [vec] * 4,
        compiler_params=_params(("arbitrary",)),
    )(p, shift, *[rw[k] for k in pre_names])

    head_spec = pl.BlockSpec((HEAD_DIM, n), lambda h: (h, 0))
    st_spec = pl.BlockSpec((1, HEAD_DIM, HEAD_DIM, n), lambda h: (h, 0, 0, 0))
    yt, wkv_new = pl.pallas_call(
        _rwkv_sample_step_kernel,
        grid=(heads,),
        in_specs=[st_spec] + [head_spec] * 6,
        out_specs=[head_spec, st_spec],
        out_shape=[vec_t, jax.ShapeDtypeStruct(wkv.shape, F32)],
        compiler_params=_params(("arbitrary",)),
    )(wkv, rt, wt, kt, vt, at, bt)

    post_names = ("r_k", "lnx_w", "lnx_b", "w_out", "seg")
    oa = pl.pallas_call(
        _rwkv_sample_post_kernel,
        grid=(1,),
        in_specs=[_const((w, n))] + [_const((n, w))] * 4 + [_const(rw[k].shape) for k in post_names],
        out_specs=_const((n, dm)),
        out_shape=jax.ShapeDtypeStruct((n, dm), F32),
        compiler_params=_params(("arbitrary",)),
    )(yt, r, k2, v, g, *[rw[k] for k in post_names])
    return oa, wkv_new


def _lru_gates(xc, gw, gb, lam):
    c = xc.shape[-1]
    gates = _mm(xc, gw) + gb
    gx = _sigmoid(gates[:, 0:c])
    ga = _sigmoid(gates[:, c:2 * c])
    log_a = -LRU_C * ga * _softplus(-lam)
    a = jnp.exp(log_a)
    u = jnp.sqrt(1.0 - a * a) * (gx * xc)
    return a, u


def _scan_rows(a, u, h0):
    rows, c = a.shape
    groups = rows // SUBLANES
    a3 = a.reshape(groups, SUBLANES, c)
    u3 = u.reshape(groups, SUBLANES, c)
    sub = lax.broadcasted_iota(jnp.int32, a3.shape, 1)
    k = 1
    while k < SUBLANES:
        keep = sub >= k
        u3 = u3 + a3 * jnp.where(keep, pltpu.roll(u3, k, axis=1), 0.0)
        a3 = a3 * jnp.where(keep, pltpu.roll(a3, k, axis=1), 1.0)
        k *= 2
    h = h0
    out = []
    for i in range(groups):
        hi = u3[i] + a3[i] * h
        out.append(hi)
        h = hi[SUBLANES - 1:SUBLANES, :]
    return jnp.concatenate(out, axis=0)


def _lru_prompt(x, tail, h0, cw_ref, cb_ref, gw, gb, lam):
    width = cw_ref.shape[0]
    xc = cb_ref[...] + cw_ref[width - 1:width, :] * x
    for j in range(1, width):
        xc = xc + cw_ref[width - 1 - j:width - j, :] * _shift_rows(x, tail, j)
    a, u = _lru_gates(xc, gw, gb, lam)
    return _scan_rows(a, u, h0)


def _lru_sample_kernel(x_ref, *refs):
    *buf_refs, h0_ref, cw_ref, cb_ref, gw_ref, gb_ref, lam_ref, h_ref = refs
    width = cw_ref.shape[0]
    xc = cb_ref[...] + cw_ref[width - 1:width, :] * x_ref[...]
    for j, buf_ref in enumerate(buf_refs):
        xc = xc + cw_ref[j:j + 1, :] * buf_ref[...]
    a, u = _lru_gates(xc, gw_ref[...], gb_ref[...], lam_ref[...])
    h_ref[...] = a * h0_ref[...] + u


def _lru_sample(xb, bufs, h0, lw):
    names = ("conv_w", "conv_b", "gate_w", "gate_b", "lam")
    args = (xb, *bufs, h0, *[lw[k] for k in names])
    return pl.pallas_call(
        _lru_sample_kernel,
        grid=(1,),
        in_specs=[_const(v.shape) for v in args],
        out_specs=_const(xb.shape),
        out_shape=jax.ShapeDtypeStruct(xb.shape, F32),
        compiler_params=_params(("arbitrary",)),
    )(*args)


def _mix_rows(x, oa, hs, gates, lwo, wo, g):
    d = x.shape[-1]
    ob = _mm(hs, lwo)
    mix = _mm(_sigmoid(gates[:, 0:d]) * oa + _sigmoid(gates[:, d:2 * d]) * ob, wo)
    return x + _rms(mix, g)


def _mix_sample_kernel(x_ref, oa_ref, hs_ref, gt_ref, lwo_ref, wo_ref, g_ref, o_ref):
    o_ref[...] = _mix_rows(x_ref[...], oa_ref[...], hs_ref[...], gt_ref[...], lwo_ref[...], wo_ref[...], g_ref[...])


def _mix_sample(x, oa, hs, gates, lwo16, wo16, g):
    args = (x, oa, hs, gates, lwo16, wo16, g)
    return pl.pallas_call(
        _mix_sample_kernel,
        grid=(1,),
        in_specs=[_const(v.shape) for v in args],
        out_specs=_const(x.shape),
        out_shape=jax.ShapeDtypeStruct(x.shape, F32),
        compiler_params=_params(("arbitrary",)),
    )(*args)


def _mix_prompt_kernel(x_ref, oa_ref, xb_ref, gt_ref, cw_ref, cb_ref, gw_ref, gb_ref, lam_ref, lwo_ref, wo_ref, g_ref,
                       o_ref, xtail_ref, htail_ref, xtail_scr, h_scr):
    @pl.when(pl.program_id(1) == 0)
    def _init():
        xtail_scr[...] = jnp.zeros_like(xtail_scr)
        h_scr[...] = jnp.zeros_like(h_scr)

    xb = xb_ref[...]
    rows = xb.shape[0]
    hs = _lru_prompt(xb, xtail_scr[...], h_scr[SUBLANES - 1:SUBLANES, :], cw_ref, cb_ref,
                     gw_ref[...], gb_ref[...], lam_ref[...])
    xtail_scr[...] = xb[rows - SUBLANES:rows, :]
    xtail_ref[0] = xb[rows - SUBLANES:rows, :]
    h_scr[...] = hs[rows - SUBLANES:rows, :]
    htail_ref[0] = hs[rows - SUBLANES:rows, :]
    o_ref[...] = _mix_rows(x_ref[...], oa_ref[...], hs, gt_ref[...], lwo_ref[...], wo_ref[...], g_ref[...])


def _mix_prompt(x, oa, xb, gates, lw, lwo16, wo16, g, b, t, tm):
    n, d = x.shape
    c = xb.shape[1]
    nt = t // tm
    names = ("conv_w", "conv_b", "gate_w", "gate_b", "lam")
    consts = [lw[k] for k in names] + [lwo16, wo16, g]
    rows = lambda w: pl.BlockSpec((tm, w), lambda i, j: (i * nt + j, 0))
    tail = pl.BlockSpec((1, SUBLANES, c), lambda i, j: (i, 0, 0))
    return pl.pallas_call(
        _mix_prompt_kernel,
        grid=(b, nt),
        in_specs=[rows(d), rows(d), rows(c), rows(2 * d)] + [_const(v.shape) for v in consts],
        out_specs=[rows(d), tail, tail],
        out_shape=[jax.ShapeDtypeStruct((n, d), F32), jax.ShapeDtypeStruct((b, SUBLANES, c), F32),
                   jax.ShapeDtypeStruct((b, SUBLANES, c), F32)],
        scratch_shapes=[pltpu.VMEM((SUBLANES, c), F32), pltpu.VMEM((SUBLANES, c), F32)],
        compiler_params=_params(("arbitrary", "arbitrary")),
    )(x, oa, xb, gates, *consts)


def _ffn_body(x1, gpre, gpost, up_ref, cw_ref, cb_ref, down_ref, hist, keep):
    dff = down_ref.shape[0]
    hn = _rms(x1, gpre).astype(BF16)
    starts = list(range(0, dff, FF_COLS))

    def up(c0):
        return [(slice(off, off + FF_COLS), jnp.dot(hn, up_ref[:, off:off + FF_COLS], preferred_element_type=F32))
                for off in (c0, dff + c0)]

    ups = [up(c0) for c0 in starts[:FFN_AHEAD]]
    f = jnp.zeros(x1.shape, F32)
    for i, c0 in enumerate(starts):
        if i + FFN_AHEAD < len(starts):
            ups.append(up(starts[i + FFN_AHEAD]))
        halves = []
        for cols, u in ups[i]:
            u1, u2 = hist(u, cols)
            halves.append(cb_ref[:, cols] + cw_ref[2:3, cols] * u + cw_ref[1:2, cols] * u1 + cw_ref[0:1, cols] * u2)
            keep(u, cols)
        act = _gelu_tanh(halves[0]) * halves[1]
        f = f + jnp.dot(act.astype(BF16), down_ref[c0:c0 + FF_COLS, :], preferred_element_type=F32)
    return x1 + _rms(f, gpost)


def _ffn_prompt_kernel(x_ref, gpre_ref, gpost_ref, up_ref, cw_ref, cb_ref, down_ref, y_ref, tail_ref, tail_scr):
    @pl.when(pl.program_id(1) == 0)
    def _init():
        tail_scr[...] = jnp.zeros_like(tail_scr)

    rows = x_ref.shape[0]

    def hist(u, cols):
        tail = tail_scr[:, cols]
        return _shift_rows(u, tail, 1), _shift_rows(u, tail, 2)

    def keep(u, cols):
        tail_scr[:, cols] = u[rows - SUBLANES:rows, :]
        tail_ref[0, :, cols] = u[rows - SUBLANES:rows, :]

    y_ref[...] = _ffn_body(x_ref[...], gpre_ref[...], gpost_ref[...], up_ref, cw_ref, cb_ref, down_ref, hist, keep)


def _ffn_prompt(x1, fw, b, t, tm):
    n, d = x1.shape
    nt = t // tm
    dff2 = fw["up"].shape[1]
    names = ("g_pre", "g_post", "up", "conv_w", "conv_b", "down")
    rows = pl.BlockSpec((tm, d), lambda i, j: (i * nt + j, 0))
    return pl.pallas_call(
        _ffn_prompt_kernel,
        grid=(b, nt),
        in_specs=[rows] + [_const(fw[k].shape) for k in names],
        out_specs=[rows, pl.BlockSpec((1, SUBLANES, dff2), lambda i, j: (i, 0, 0))],
        out_shape=[jax.ShapeDtypeStruct((n, d), F32), jax.ShapeDtypeStruct((b, SUBLANES, dff2), F32)],
        scratch_shapes=[pltpu.VMEM((SUBLANES, dff2), F32)],
        compiler_params=_params(("arbitrary", "arbitrary")),
    )(x1, *[fw[k] for k in names])


def _ffn_sample_kernel(x_ref, buf_ref, gpre_ref, gpost_ref, up_ref, cw_ref, cb_ref, down_ref, y_ref, nb_ref):
    def hist(u, cols):
        return buf_ref[:, 1, cols], buf_ref[:, 0, cols]

    def keep(u, cols):
        nb_ref[:, 0, cols] = buf_ref[:, 1, cols]
        nb_ref[:, 1, cols] = u

    y_ref[...] = _ffn_body(x_ref[...], gpre_ref[...], gpost_ref[...], up_ref, cw_ref, cb_ref, down_ref, hist, keep)


def _ffn_sample(x1, buf, fw):
    names = ("g_pre", "g_post", "up", "conv_w", "conv_b", "down")
    args = (x1, buf, *[fw[k] for k in names])
    return pl.pallas_call(
        _ffn_sample_kernel,
        grid=(1,),
        in_specs=[_const(v.shape) for v in args],
        out_specs=[_const(x1.shape), _const(buf.shape)],
        out_shape=[jax.ShapeDtypeStruct(x1.shape, F32), jax.ShapeDtypeStruct(buf.shape, F32)],
        compiler_params=_params(("arbitrary",)),
    )(*args)


def _block_diag(blocks):
    n, bi, bj = blocks.shape
    eye = jnp.eye(n, dtype=blocks.dtype)
    return jnp.einsum("nij,nm->nimj", blocks, eye).reshape(n * bi, n * bj)


def _layer_weights(norm_pre_mix, norm_post_mix, norm_pre_ffn, norm_post_ffn, w_in,
                   rwkv_mu, rwkv_w0, rwkv_w_up, rwkv_a0, rwkv_a_up, rwkv_g_up, rwkv_k_k, rwkv_k_a,
                   rwkv_r_k, rwkv_lnx_w, rwkv_lnx_b, rwkv_w_out,
                   lru_conv_w, lru_conv_b, lru_gx_w, lru_gx_b, lru_ga_w, lru_ga_b, lru_lambda, lru_w_out,
                   w_o, ffn_up, ffn_conv_w, ffn_conv_b, ffn_down):
    row = lambda x: x.reshape(1, -1)
    w = rwkv_w0.shape[-1]
    head_of = jnp.arange(w) // HEAD_DIM
    t_idx = jnp.arange(CHUNK)
    lo_w, lo_a = rwkv_w_up.shape[0], rwkv_a_up.shape[0]
    lora2 = jnp.zeros((lo_w + lo_a, 2 * w), F32).at[:lo_w, :w].set(rwkv_w_up).at[lo_w:, w:].set(rwkv_a_up)
    rw = dict(mu=row(rwkv_mu), w0=row(rwkv_w0), lora2=lora2.astype(BF16), a0=row(rwkv_a0),
              g_up=rwkv_g_up.astype(BF16), k_k=row(rwkv_k_k), k_a=row(rwkv_k_a), r_k=row(rwkv_r_k),
              lnx_w=row(rwkv_lnx_w), lnx_b=row(rwkv_lnx_b), w_out=rwkv_w_out.astype(BF16),
              seg=(head_of[:, None] == head_of[None, :]).astype(BF16),
              tri=(t_idx[:, None] >= t_idx[None, :]).astype(BF16))
    lw = dict(conv_w=lru_conv_w, conv_b=row(lru_conv_b),
              gate_w=jnp.concatenate([_block_diag(lru_gx_w), _block_diag(lru_ga_w)], axis=1).astype(BF16),
              gate_b=jnp.concatenate([row(lru_gx_b), row(lru_ga_b)], axis=1), lam=row(lru_lambda))
    fw = dict(g_pre=row(norm_pre_ffn), g_post=row(norm_post_ffn), up=ffn_up.astype(BF16),
              conv_w=ffn_conv_w, conv_b=row(ffn_conv_b), down=ffn_down.astype(BF16))
    return dict(g_in=row(norm_pre_mix), w_in=w_in.astype(BF16), rw=rw, lw=lw, fw=fw,
                lru_w_out=lru_w_out.astype(BF16), w_o=w_o.astype(BF16), g_mix=row(norm_post_mix))


def _splits(wts):
    w = wts["rw"]["w0"].shape[-1]
    c_rwkv = wts["rw"]["mu"].shape[-1]
    c_lru = wts["lw"]["lam"].shape[-1]
    return (c_rwkv, c_lru, wts["w_in"].shape[1] - c_rwkv - c_lru)


def _prompt_layer(x, wts):
    b, t, d = x.shape
    x2 = x.reshape(b * t, d)
    tm = min(t, ROW_TILE)
    m, xb, gates, p_tail = _proj_prompt(x2, wts["g_in"], wts["w_in"], wts["rw"]["mu"], _splits(wts), b, t, tm)
    oa, wkv = _rwkv_prompt(m, wts["rw"], b, t, min(t, WKV_ROWS))
    x1, xb_tail, h_tail = _mix_prompt(x2, oa, xb, gates, wts["lw"], wts["lru_w_out"], wts["w_o"], wts["g_mix"],
                                      b, t, tm)
    y, u_tail = _ffn_prompt(x1, wts["fw"], b, t, min(t, FFN_ROWS))
    conv_w = wts["lw"]["conv_w"].shape[0]
    ffn_w = wts["fw"]["conv_w"].shape[0]
    state = (p_tail[:, SUBLANES - 1:], wkv, xb_tail[:, SUBLANES - (conv_w - 1):], h_tail[:, SUBLANES - 1],
             u_tail[:, SUBLANES - (ffn_w - 1):])
    return y.reshape(b, t, d), state


def _sample_layer(x, shift, wkv, lru_buf, h0, ffn_buf, wts):
    n, t, d = x.shape
    x2 = x.reshape(n, d)
    p, xb, gates = _proj(x2, wts["g_in"], wts["w_in"], _splits(wts), n)
    oa, wkv_t = _rwkv_sample(p, shift[:, 0, :], jnp.transpose(wkv, (1, 2, 3, 0)), wts["rw"])
    lru_bufs = [lru_buf[:, j, :] for j in range(lru_buf.shape[1])]
    h = _lru_sample(xb, lru_bufs, h0, wts["lw"])
    x1 = _mix_sample(x2, oa, h, gates, wts["lru_w_out"], wts["w_o"], wts["g_mix"])
    y, ffn_new = _ffn_sample(x1, ffn_buf, wts["fw"])
    state = (p.reshape(n, 1, -1), jnp.transpose(wkv_t, (3, 0, 1, 2)), jnp.stack(lru_bufs[1:] + [xb], axis=1), h,
             ffn_new)
    return y.reshape(n, t, d), state


def kernel(x_prompt, x_sample, state_rwkv_shift, state_rwkv_wkv, state_lru_conv, state_lru_h, state_ffn_conv,
           norm_pre_mix, norm_post_mix, norm_pre_ffn, norm_post_ffn, w_in,
           rwkv_mu, rwkv_w0, rwkv_w_up, rwkv_a0, rwkv_a_up, rwkv_g_up, rwkv_k_k, rwkv_k_a,
           rwkv_r_k, rwkv_lnx_w, rwkv_lnx_b, rwkv_w_out,
           lru_conv_w, lru_conv_b, lru_gx_w, lru_gx_b, lru_ga_w, lru_ga_b, lru_lambda, lru_w_out,
           w_o, ffn_up, ffn_conv_w, ffn_conv_b, ffn_down):
    params = (norm_pre_mix, norm_post_mix, norm_pre_ffn, norm_post_ffn, w_in,
              rwkv_mu, rwkv_w0, rwkv_w_up, rwkv_a0, rwkv_a_up, rwkv_g_up, rwkv_k_k, rwkv_k_a,
              rwkv_r_k, rwkv_lnx_w, rwkv_lnx_b, rwkv_w_out,
              lru_conv_w, lru_conv_b, lru_gx_w, lru_gx_b, lru_ga_w, lru_ga_b, lru_lambda, lru_w_out,
              w_o, ffn_up, ffn_conv_w, ffn_conv_b, ffn_down)
    depth = w_in.shape[0]
    assert depth == 1 and x_sample.shape[1] == 1
    yp, ys = x_prompt, x_sample
    new_p, new_s = [], []
    for l in range(depth):
        wts = _layer_weights(*[q[l] for q in params])
        yp, st_p = _prompt_layer(yp, wts)
        ys, st_s = _sample_layer(ys, state_rwkv_shift[l], state_rwkv_wkv[l], state_lru_conv[l],
                                 state_lru_h[l], state_ffn_conv[l], wts)
        new_p.append(st_p)
        new_s.append(st_s)
    stk = lambda lst, i: jnp.stack([s[i] for s in lst], axis=0)
    return (yp, ys,
            stk(new_p, 0), stk(new_p, 1), stk(new_p, 2), stk(new_p, 3), stk(new_p, 4),
            stk(new_s, 0), stk(new_s, 1), stk(new_s, 2), stk(new_s, 3), stk(new_s, 4))
```

```python
import functools
import math

import jax
import jax.numpy as jnp
from jax import lax
from jax.experimental import pallas as pl
from jax.experimental.pallas import tpu as pltpu

F32 = jnp.float32
BF16 = jnp.bfloat16

NORM_EPS = 1e-6
LN_X_EPS = 64e-5
LRU_C = 8.0
DECAY_SCALE = math.exp(-0.5)
HEAD_DIM = 64
LANES = 128
SUBLANES = 8
CHUNK = 64
ROW_TILE = 512
WKV_ROWS = 512
GATE_COLS = 512
SHIFT_COLS = 256
FF_COLS = 256
FFN_ROWS = 256
FFN_AHEAD = 3
VMEM_LIMIT = 56 * 1024 * 1024

_NN = (((1,), (0,)), ((), ()))
_NT = (((1,), (1,)), ((), ()))
_TN = (((0,), (0,)), ((), ()))


def _rms(x, g):
    return x * lax.rsqrt(jnp.mean(x * x, axis=-1, keepdims=True) + NORM_EPS) * g


def _softplus(x):
    return jnp.maximum(x, 0.0) + jnp.log1p(jnp.exp(-jnp.abs(x)))


def _sigmoid(x):
    return 0.5 * jnp.tanh(0.5 * x) + 0.5


def _gelu_tanh(x):
    c = 0.7978845608028654
    hx = 0.5 * x
    return hx + hx * jnp.tanh(x * (c + (c * 0.044715) * (x * x)))


def _mm(a, b):
    return jnp.dot(a.astype(BF16), b.astype(BF16), preferred_element_type=F32)


def _mmd(a, b, dims=_NN):
    return lax.dot_general(a.astype(BF16), b.astype(BF16), dims, preferred_element_type=F32)


def _segsum(x, seg):
    return jnp.dot(x.astype(BF16), seg, preferred_element_type=F32)


def _cumsum_rows(tri, x):
    hi = x.astype(BF16)
    lo = (x - hi.astype(F32)).astype(BF16)
    d = lambda y: jnp.dot(tri, y, preferred_element_type=F32)
    return d(hi) + d(lo)


def _shift_rows(x, tail, j):
    xs = pltpu.roll(x, j, axis=0)
    ts = pltpu.roll(tail, j, axis=0)
    row = lax.broadcasted_iota(jnp.int32, ts.shape, 0)
    head = jnp.where(row < j, ts, xs[0:SUBLANES])
    if x.shape[0] == SUBLANES:
        return head
    return jnp.concatenate([head, xs[SUBLANES:]], axis=0)


def _const(shape):
    n = len(shape)
    return pl.BlockSpec(shape, lambda *_: (0,) * n, pipeline_mode=pl.Buffered(1))


def _params(sem):
    return pltpu.CompilerParams(dimension_semantics=sem, vmem_limit_bytes=VMEM_LIMIT)


def _proj_kernel(x_ref, g_ref, w_ref, p_ref, xb_ref, gt_ref, *, splits):
    xn = _rms(x_ref[...], g_ref[...]).astype(BF16)
    c0, c1, c2 = splits
    p_ref[...] = jnp.dot(xn, w_ref[:, 0:c0], preferred_element_type=F32)
    xb_ref[...] = jnp.dot(xn, w_ref[:, c0:c0 + c1], preferred_element_type=F32)
    gt_ref[...] = jnp.dot(xn, w_ref[:, c0 + c1:c0 + c1 + c2], preferred_element_type=F32)


def _proj_prompt_kernel(x_ref, g_ref, w_ref, mu_ref, m_ref, xb_ref, gt_ref, tail_ref, prev_scr, *, splits):
    @pl.when(pl.program_id(1) == 0)
    def _init():
        prev_scr[...] = jnp.zeros_like(prev_scr)

    xn = _rms(x_ref[...], g_ref[...]).astype(BF16)
    c0, c1, c2 = splits
    g0 = c0 + c1
    p = jnp.dot(xn, w_ref[:, 0:c0], preferred_element_type=F32)
    xb_ref[...] = jnp.dot(xn, w_ref[:, c0:g0], preferred_element_type=F32)
    rows = p.shape[0]
    row = lax.broadcasted_iota(jnp.int32, (rows, SHIFT_COLS), 0)
    shift_chunks = list(range(0, c0, SHIFT_COLS))
    gate_chunks = list(range(0, c2, GATE_COLS))
    per_gate = -(-len(shift_chunks) // len(gate_chunks))
    for i, off in enumerate(gate_chunks):
        gt_ref[:, off:off + GATE_COLS] = jnp.dot(xn, w_ref[:, g0 + off:g0 + off + GATE_COLS],
                                                 preferred_element_type=F32)
        for s0 in shift_chunks[i * per_gate:(i + 1) * per_gate]:
            cs = slice(s0, s0 + SHIFT_COLS)
            pc = p[:, cs]
            prev = jnp.where(row == 0, prev_scr[SUBLANES - 1:SUBLANES, cs], pltpu.roll(pc, 1, axis=0))
            m_ref[:, cs] = pc + mu_ref[:, cs] * (prev - pc)
    prev_scr[...] = p[rows - SUBLANES:rows, :]
    tail_ref[0] = p[rows - SUBLANES:rows, :]


def _proj_prompt(x, g, w_in16, mu, splits, b, t, tm):
    n, d = x.shape
    c0, c1, c2 = splits
    nt = t // tm
    rows = lambda c: pl.BlockSpec((tm, c), lambda i, j: (i * nt + j, 0))
    return pl.pallas_call(
        functools.partial(_proj_prompt_kernel, splits=splits),
        grid=(b, nt),
        in_specs=[rows(d), _const((1, d)), _const(w_in16.shape), _const(mu.shape)],
        out_specs=[rows(c0), rows(c1), rows(c2), pl.BlockSpec((1, SUBLANES, c0), lambda i, j: (i, 0, 0))],
        out_shape=[jax.ShapeDtypeStruct((n, c0), F32), jax.ShapeDtypeStruct((n, c1), F32),
                   jax.ShapeDtypeStruct((n, c2), F32), jax.ShapeDtypeStruct((b, SUBLANES, c0), F32)],
        scratch_shapes=[pltpu.VMEM((SUBLANES, c0), F32)],
        compiler_params=_params(("arbitrary", "arbitrary")),
    )(x, g, w_in16, mu)


def _proj(x, g, w_in16, splits, tm):
    n, d = x.shape
    c0, c1, c2 = splits
    return pl.pallas_call(
        functools.partial(_proj_kernel, splits=splits),
        grid=(n // tm,),
        in_specs=[pl.BlockSpec((tm, d), lambda i: (i, 0)), _const((1, d)), _const(w_in16.shape)],
        out_specs=[pl.BlockSpec((tm, c0), lambda i: (i, 0)),
                   pl.BlockSpec((tm, c1), lambda i: (i, 0)),
                   pl.BlockSpec((tm, c2), lambda i: (i, 0))],
        out_shape=[jax.ShapeDtypeStruct((n, c0), F32), jax.ShapeDtypeStruct((n, c1), F32),
                   jax.ShapeDtypeStruct((n, c2), F32)],
        compiler_params=_params(("arbitrary",)),
    )(x, g, w_in16)


def _rwkv_pre(m, w0, lora2, a0, g_up, k_k, k_a, seg):
    w = w0.shape[-1]
    r, k, v = m[:, 0:w], m[:, w:2 * w], m[:, 2 * w:3 * w]
    z = m[:, 3 * w:3 * w + LANES]
    lane = lax.broadcasted_iota(jnp.int32, z.shape, 1)
    lor = _mm(jnp.where(lane < HEAD_DIM, jnp.tanh(z), z), lora2)
    lw = -DECAY_SCALE * _sigmoid(w0 + lor[:, 0:w])
    a_sig = _sigmoid(a0 + lor[:, w:2 * w])
    g = _mm(_sigmoid(m[:, 3 * w + LANES:3 * w + 2 * LANES]), g_up)
    kk = k * k_k
    kkn = kk * lax.rsqrt(jnp.maximum(_segsum(kk * kk, seg), 1e-24))
    k2 = k * (1.0 + (a_sig - 1.0) * k_a)
    return r, lw, k2, v, -kkn, kkn * a_sig, g


def _rwkv_post(y, r, k2, v, g, r_k, lnx_w, lnx_b, w_out, seg):
    inv = 1.0 / HEAD_DIM
    mean = _segsum(y, seg) * inv
    d = y - mean
    var = _segsum(d * d, seg) * inv
    yn = d * lax.rsqrt(var + LN_X_EPS) * lnx_w + lnx_b
    bonus = _segsum(r * k2 * r_k, seg) * v
    return _mm((yn + bonus) * g, w_out)


def _wkv_chunk_summaries(xs, m0b, strict, incl, eye2, eyew):
    ln = xs[0][0].shape[0]
    swap = lambda x: pltpu.roll(x, HEAD_DIM, axis=1)
    h16 = lambda x: x.astype(BF16)
    zero16 = jnp.zeros((), BF16)
    bd = lambda x: jnp.concatenate([jnp.where(m0b, x, zero16), jnp.where(m0b, zero16, x)], axis=0)
    cat = jnp.concatenate
    rts, ats, bts, kts, bhs, khs, vs, wls = zip(*xs)
    at16 = [h16(x) for x in ats]
    v16 = [h16(x) for x in vs]
    bdv = [bd(x) for x in v16]
    gms = [_mmd(cat([bd(a), bd(h16(rt))], axis=0), cat([bt, kt], axis=0), _NT)
           for rt, a, bt, kt in zip(rts, at16, bts, kts)]
    ga0 = [jnp.where(strict, g[0:ln], 0.0) for g in gms]
    ga1 = [jnp.where(strict, g[ln:2 * ln], 0.0) for g in gms]
    gr0 = [jnp.where(incl, g[2 * ln:3 * ln], 0.0) for g in gms]
    gr1 = [jnp.where(incl, g[3 * ln:4 * ln], 0.0) for g in gms]
    n_ab = [jnp.where(m0b, x, swap(y)) for x, y in zip(ga0, ga1)]
    n_ak = [jnp.where(m0b, swap(x), y) for x, y in zip(ga0, ga1)]
    n_rb = [jnp.where(m0b, x, swap(y)) for x, y in zip(gr0, gr1)]
    n_rk = [jnp.where(m0b, swap(x), y) for x, y in zip(gr0, gr1)]
    tinv = [eye2 + n for n in n_ab]
    steps = (ln - 1).bit_length() - 1
    if steps > 0:
        npow = [h16(_mmd(n, bd(n))) for n in (h16(x) for x in n_ab)]
        for _ in range(steps - 1):
            both = [_mmd(cat([n, h16(t)], axis=0), bd(n)) for n, t in zip(npow, tinv)]
            tinv = [t + x[ln:2 * ln] for t, x in zip(tinv, both)]
            npow = [h16(x[0:ln]) for x in both]
        tinv = [t + _mmd(t, bd(n)) for t, n in zip(tinv, npow)]
    kv = [_mmd(cat([nk, nr], axis=0), b) for nk, nr, b in zip(n_ak, n_rk, bdv)]
    akv = [x[0:ln] for x in kv]
    rkv = [x[ln:2 * ln] for x in kv]
    tt = [_mmd(t, cat([bd(a), bd(h16(ak))], axis=1)) for t, a, ak in zip(tinv, at16, akv)]
    ta = [h16(x[:, 0:LANES]) for x in tt]
    tk = [h16(x[:, LANES:2 * LANES]) for x in tt]
    mc = [_mmd(cat([cat([a, k], axis=1), cat([jnp.zeros_like(v), v], axis=1)], axis=0), cat([bh, kh], axis=0), _TN)
          for a, k, v, bh, kh in zip(ta, tk, v16, bhs, khs)]
    ms = [eyew * wl + x[0:LANES] for x, wl in zip(mc, wls)]
    cs = [x[LANES:2 * LANES] for x in mc]
    qy = [_mmd(n, cat([bd(a), bd(k)], axis=1)) for n, a, k in zip(n_rb, ta, tk)]
    qs = [rt + x[:, 0:LANES] for rt, x in zip(rts, qy)]
    y0 = [x[:, LANES:2 * LANES] + r for x, r in zip(qy, rkv)]
    return list(zip(qs, y0, ms, cs))


def _rwkv_prompt_kernel(m_ref, w0_ref, lora_ref, a0_ref, gup_ref, kk_ref, ka_ref, rk_ref,
                        lnw_ref, lnb_ref, wout_ref, seg_ref, tri_ref, oa_ref, wkv_ref, s_scr):
    t = pl.program_id(1)

    @pl.when(t == 0)
    def _init():
        s_scr[...] = jnp.zeros_like(s_scr)

    rows = m_ref.shape[0]
    seg = seg_ref[...]
    r, lw, k2, v, aa, bb, g = _rwkv_pre(m_ref[...], w0_ref[...], lora_ref[...], a0_ref[...],
                                        gup_ref[...], kk_ref[...], ka_ref[...], seg)
    c = _cumsum_rows(tri_ref[...], lw)

    ln = CHUNK
    ri = lax.broadcasted_iota(jnp.int32, (ln, LANES), 0)
    ci = lax.broadcasted_iota(jnp.int32, (ln, LANES), 1)
    cj = jnp.where(ci < HEAD_DIM, ci, ci - HEAD_DIM)
    m0b = ci < HEAD_DIM
    strict = cj < ri
    incl = cj <= ri
    eye2 = (cj == ri).astype(F32)
    r2 = lax.broadcasted_iota(jnp.int32, (LANES, LANES), 0)
    c2 = lax.broadcasted_iota(jnp.int32, (LANES, LANES), 1)
    bdmask = ((r2 < HEAD_DIM) == (c2 < HEAD_DIM)).astype(F32)
    eyew = (r2 == c2).astype(F32)

    npairs = s_scr.shape[0]
    insts = []
    for ch in range(rows // ln):
        rs = slice(ch * ln, (ch + 1) * ln)
        c_, lw_ = c[rs], lw[rs]
        cl = c_[ln - 1:ln, :]
        e_n = jnp.exp(-c_)
        e_l = jnp.exp(cl - c_)
        rt = r[rs] * jnp.exp(c_)
        at = aa[rs] * jnp.exp(c_ - lw_)
        bt, kt = bb[rs] * e_n, k2[rs] * e_n
        bh, kh = bb[rs] * e_l, k2[rs] * e_l
        wl = jnp.exp(cl)
        v_ = v[rs]
        for pr in range(npairs):
            sl = slice(pr * LANES, (pr + 1) * LANES)
            insts.append((rt[:, sl], at[:, sl], bt[:, sl], kt[:, sl], bh[:, sl], kh[:, sl], v_[:, sl], wl[:, sl]))
    summaries = _wkv_chunk_summaries(insts, m0b, strict, incl, eye2, eyew)

    states = [s_scr[pr] for pr in range(npairs)]
    y_rows = []
    for ch in range(rows // ln):
        ys = []
        for pr in range(npairs):
            q, y0, m, cc = summaries[ch * npairs + pr]
            ys.append(_mmd(q, states[pr], _NT) + y0)
            states[pr] = (_mmd(states[pr], m) + cc) * bdmask
        y_rows.append(jnp.concatenate(ys, axis=1))
    for pr in range(npairs):
        s_scr[pr] = states[pr]
    y = jnp.concatenate(y_rows, axis=0)
    oa_ref[...] = _rwkv_post(y, r, k2, v, g, rk_ref[...], lnw_ref[...], lnb_ref[...], wout_ref[...], seg)

    @pl.when(t == pl.num_programs(1) - 1)
    def _fin():
        for pr in range(npairs):
            s = s_scr[pr]
            wkv_ref[0, 2 * pr] = s[0:HEAD_DIM, 0:HEAD_DIM]
            wkv_ref[0, 2 * pr + 1] = pltpu.roll(s[HEAD_DIM:LANES, :], HEAD_DIM, axis=1)[:, 0:HEAD_DIM]


def _rwkv_prompt(m, rw, b, t, tc):
    n, cp = m.shape
    w = rw["w0"].shape[-1]
    heads = w // HEAD_DIM
    nt = t // tc
    dm = rw["w_out"].shape[-1]
    tri = jnp.kron(jnp.eye(tc // CHUNK, dtype=BF16), rw["tri"])
    names = ("w0", "lora2", "a0", "g_up", "k_k", "k_a", "r_k", "lnx_w", "lnx_b", "w_out", "seg")
    return pl.pallas_call(
        _rwkv_prompt_kernel,
        grid=(b, nt),
        in_specs=[pl.BlockSpec((tc, cp), lambda i, j: (i * nt + j, 0))] + [_const(rw[k].shape) for k in names]
                 + [_const(tri.shape)],
        out_specs=[pl.BlockSpec((tc, dm), lambda i, j: (i * nt + j, 0)),
                   pl.BlockSpec((1, heads, HEAD_DIM, HEAD_DIM), lambda i, j: (i, 0, 0, 0))],
        out_shape=[jax.ShapeDtypeStruct((n, dm), F32),
                   jax.ShapeDtypeStruct((b, heads, HEAD_DIM, HEAD_DIM), F32)],
        scratch_shapes=[pltpu.VMEM((heads // 2, LANES, LANES), F32)],
        compiler_params=_params(("arbitrary", "arbitrary")),
    )(m, *[rw[k] for k in names], tri)


def _rwkv_sample_pre_kernel(p_ref, prev_ref, mu_ref, w0_ref, lora_ref, a0_ref, gup_ref, kk_ref, ka_ref, seg_ref,
                            rt_ref, wt_ref, kt_ref, vt_ref, at_ref, bt_ref, r_ref, k_ref, v_ref, g_ref):
    p = p_ref[...]
    r, lw, k2, v, aa, bb, g = _rwkv_pre(p + mu_ref[...] * (prev_ref[...] - p), w0_ref[...], lora_ref[...],
                                        a0_ref[...], gup_ref[...], kk_ref[...], ka_ref[...], seg_ref[...])
    rt_ref[...] = r.T
    wt_ref[...] = jnp.exp(lw).T
    kt_ref[...] = k2.T
    vt_ref[...] = v.T
    at_ref[...] = aa.T
    bt_ref[...] = bb.T
    r_ref[...] = r
    k_ref[...] = k2
    v_ref[...] = v
    g_ref[...] = g


def _rwkv_sample_step_kernel(s_ref, r_ref, w_ref, k_ref, v_ref, a_ref, b_ref, y_ref, so_ref):
    r, w, k, a, b = r_ref[...], w_ref[...], k_ref[...], a_ref[...], b_ref[...]

    def body(i, carry):
        base = pl.multiple_of(i * SUBLANES, SUBLANES)
        vrows = v_ref[pl.ds(base, SUBLANES), :]
        ys = []
        for j in range(SUBLANES):
            s = s_ref[0, base + j]
            sa = jnp.sum(s * a, axis=0, keepdims=True)
            s_new = s * w + sa * b + vrows[j:j + 1, :] * k
            so_ref[0, base + j] = s_new
            ys.append(jnp.sum(s_new * r, axis=0, keepdims=True))
        y_ref[pl.ds(base, SUBLANES), :] = jnp.concatenate(ys, axis=0)
        return carry

    lax.fori_loop(0, s_ref.shape[1] // SUBLANES, body, 0)


def _rwkv_sample_post_kernel(yt_ref, r_ref, k_ref, v_ref, g_ref, rk_ref, lnw_ref, lnb_ref, wout_ref, seg_ref, oa_ref):
    oa_ref[...] = _rwkv_post(yt_ref[...].T, r_ref[...], k_ref[...], v_ref[...], g_ref[...], rk_ref[...],
                             lnw_ref[...], lnb_ref[...], wout_ref[...], seg_ref[...])


def _rwkv_sample(p, shift, wkv, rw):
    n, cp = p.shape
    w = rw["w0"].shape[-1]
    heads = w // HEAD_DIM
    dm = rw["w_out"].shape[-1]
    pre_names = ("mu", "w0", "lora2", "a0", "g_up", "k_k", "k_a", "seg")
    vec = jax.ShapeDtypeStruct((n, w), F32)
    vec_t = jax.ShapeDtypeStruct((w, n), F32)
    rt, wt, kt, vt, at, bt, r, k2, v, g = pl.pallas_call(
        _rwkv_sample_pre_kernel,
        grid=(1,),
        in_specs=[_const((n, cp)), _const((n, cp))] + [_const(rw[k].shape) for k in pre_names],
        out_specs=[_const((w, n))] * 6 + [_const((n, w))] * 4,
        out_shape=[vec_t] * 6 + [vec] * 4,
        compiler_params=_params(("arbitrary",)),
    )(p, shift, *[rw[k] for k in pre_names])

    head_spec = pl.BlockSpec((HEAD_DIM, n), lambda h: (h, 0))
    st_spec = pl.BlockSpec((1, HEAD_DIM, HEAD_DIM, n), lambda h: (h, 0, 0, 0))
    yt, wkv_new = pl.pallas_call(
        _rwkv_sample_step_kernel,
        grid=(heads,),
        in_specs=[st_spec] + [head_spec] * 6,
        out_specs=[head_spec, st_spec],
        out_shape=[vec_t, jax.ShapeDtypeStruct(wkv.shape, F32)],
        compiler_params=_params(("arbitrary",)),
    )(wkv, rt, wt, kt, vt, at, bt)

    post_names = ("r_k", "lnx_w", "lnx_b", "w_out", "seg")
    oa = pl.pallas_call(
        _rwkv_sample_post_kernel,
        grid=(1,),
        in_specs=[_const((w, n))] + [_const((n, w))] * 4 + [_const(rw[k].shape) for k in post_names],
        out_specs=_const((n, dm)),
        out_shape=jax.ShapeDtypeStruct((n, dm), F32),
        compiler_params=_params(("arbitrary",)),
    )(yt, r, k2, v, g, *[rw[k] for k in post_names])
    return oa, wkv_new


def _lru_gates(xc, gw, gb, lam):
    c = xc.shape[-1]
    gates = _mm(xc, gw) + gb
    gx = _sigmoid(gates[:, 0:c])
    ga = _sigmoid(gates[:, c:2 * c])
    log_a = -LRU_C * ga * _softplus(-lam)
    a = jnp.exp(log_a)
    u = jnp.sqrt(1.0 - a * a) * (gx * xc)
    return a, u


def _scan_rows(a, u, h0):
    rows, c = a.shape
    groups = rows // SUBLANES
    a3 = a.reshape(groups, SUBLANES, c)
    u3 = u.reshape(groups, SUBLANES, c)
    sub = lax.broadcasted_iota(jnp.int32, a3.shape, 1)
    k = 1
    while k < SUBLANES:
        keep = sub >= k
        u3 = u3 + a3 * jnp.where(keep, pltpu.roll(u3, k, axis=1), 0.0)
        a3 = a3 * jnp.where(keep, pltpu.roll(a3, k, axis=1), 1.0)
        k *= 2
    h = h0
    out = []
    for i in range(groups):
        hi = u3[i] + a3[i] * h
        out.append(hi)
        h = hi[SUBLANES - 1:SUBLANES, :]
    return jnp.concatenate(out, axis=0)


def _lru_prompt(x, tail, h0, cw_ref, cb_ref, gw, gb, lam):
    width = cw_ref.shape[0]
    xc = cb_ref[...] + cw_ref[width - 1:width, :] * x
    for j in range(1, width):
        xc = xc + cw_ref[width - 1 - j:width - j, :] * _shift_rows(x, tail, j)
    a, u = _lru_gates(xc, gw, gb, lam)
    return _scan_rows(a, u, h0)


def _lru_sample_kernel(x_ref, *refs):
    *buf_refs, h0_ref, cw_ref, cb_ref, gw_ref, gb_ref, lam_ref, h_ref = refs
    width = cw_ref.shape[0]
    xc = cb_ref[...] + cw_ref[width - 1:width, :] * x_ref[...]
    for j, buf_ref in enumerate(buf_refs):
        xc = xc + cw_ref[j:j + 1, :] * buf_ref[...]
    a, u = _lru_gates(xc, gw_ref[...], gb_ref[...], lam_ref[...])
    h_ref[...] = a * h0_ref[...] + u


def _lru_sample(xb, bufs, h0, lw):
    names = ("conv_w", "conv_b", "gate_w", "gate_b", "lam")
    args = (xb, *bufs, h0, *[lw[k] for k in names])
    return pl.pallas_call(
        _lru_sample_kernel,
        grid=(1,),
        in_specs=[_const(v.shape) for v in args],
        out_specs=_const(xb.shape),
        out_shape=jax.ShapeDtypeStruct(xb.shape, F32),
        compiler_params=_params(("arbitrary",)),
    )(*args)


def _mix_rows(x, oa, hs, gates, lwo, wo, g):
    d = x.shape[-1]
    ob = _mm(hs, lwo)
    mix = _mm(_sigmoid(gates[:, 0:d]) * oa + _sigmoid(gates[:, d:2 * d]) * ob, wo)
    return x + _rms(mix, g)


def _mix_sample_kernel(x_ref, oa_ref, hs_ref, gt_ref, lwo_ref, wo_ref, g_ref, o_ref):
    o_ref[...] = _mix_rows(x_ref[...], oa_ref[...], hs_ref[...], gt_ref[...], lwo_ref[...], wo_ref[...], g_ref[...])


def _mix_sample(x, oa, hs, gates, lwo16, wo16, g):
    args = (x, oa, hs, gates, lwo16, wo16, g)
    return pl.pallas_call(
        _mix_sample_kernel,
        grid=(1,),
        in_specs=[_const(v.shape) for v in args],
        out_specs=_const(x.shape),
        out_shape=jax.ShapeDtypeStruct(x.shape, F32),
        compiler_params=_params(("arbitrary",)),
    )(*args)


def _mix_prompt_kernel(x_ref, oa_ref, xb_ref, gt_ref, cw_ref, cb_ref, gw_ref, gb_ref, lam_ref, lwo_ref, wo_ref, g_ref,
                       o_ref, xtail_ref, htail_ref, xtail_scr, h_scr):
    @pl.when(pl.program_id(1) == 0)
    def _init():
        xtail_scr[...] = jnp.zeros_like(xtail_scr)
        h_scr[...] = jnp.zeros_like(h_scr)

    xb = xb_ref[...]
    rows = xb.shape[0]
    hs = _lru_prompt(xb, xtail_scr[...], h_scr[SUBLANES - 1:SUBLANES, :], cw_ref, cb_ref,
                     gw_ref[...], gb_ref[...], lam_ref[...])
    xtail_scr[...] = xb[rows - SUBLANES:rows, :]
    xtail_ref[0] = xb[rows - SUBLANES:rows, :]
    h_scr[...] = hs[rows - SUBLANES:rows, :]
    htail_ref[0] = hs[rows - SUBLANES:rows, :]
    o_ref[...] = _mix_rows(x_ref[...], oa_ref[...], hs, gt_ref[...], lwo_ref[...], wo_ref[...], g_ref[...])


def _mix_prompt(x, oa, xb, gates, lw, lwo16, wo16, g, b, t, tm):
    n, d = x.shape
    c = xb.shape[1]
    nt = t // tm
    names = ("conv_w", "conv_b", "gate_w", "gate_b", "lam")
    consts = [lw[k] for k in names] + [lwo16, wo16, g]
    rows = lambda w: pl.BlockSpec((tm, w), lambda i, j: (i * nt + j, 0))
    tail = pl.BlockSpec((1, SUBLANES, c), lambda i, j: (i, 0, 0))
    return pl.pallas_call(
        _mix_prompt_kernel,
        grid=(b, nt),
        in_specs=[rows(d), rows(d), rows(c), rows(2 * d)] + [_const(v.shape) for v in consts],
        out_specs=[rows(d), tail, tail],
        out_shape=[jax.ShapeDtypeStruct((n, d), F32), jax.ShapeDtypeStruct((b, SUBLANES, c), F32),
                   jax.ShapeDtypeStruct((b, SUBLANES, c), F32)],
        scratch_shapes=[pltpu.VMEM((SUBLANES, c), F32), pltpu.VMEM((SUBLANES, c), F32)],
        compiler_params=_params(("arbitrary", "arbitrary")),
    )(x, oa, xb, gates, *consts)


def _ffn_body(x1, gpre, gpost, up_ref, cw_ref, cb_ref, down_ref, hist, keep):
    dff = down_ref.shape[0]
    hn = _rms(x1, gpre).astype(BF16)
    starts = list(range(0, dff, FF_COLS))

    def up(c0):
        return [(slice(off, off + FF_COLS), jnp.dot(hn, up_ref[:, off:off + FF_COLS], preferred_element_type=F32))
                for off in (c0, dff + c0)]

    ups = [up(c0) for c0 in starts[:FFN_AHEAD]]
    f = jnp.zeros(x1.shape, F32)
    for i, c0 in enumerate(starts):
        if i + FFN_AHEAD < len(starts):
            ups.append(up(starts[i + FFN_AHEAD]))
        halves = []
        for cols, u in ups[i]:
            u1, u2 = hist(u, cols)
            halves.append(cb_ref[:, cols] + cw_ref[2:3, cols] * u + cw_ref[1:2, cols] * u1 + cw_ref[0:1, cols] * u2)
            keep(u, cols)
        act = _gelu_tanh(halves[0]) * halves[1]
        f = f + jnp.dot(act.astype(BF16), down_ref[c0:c0 + FF_COLS, :], preferred_element_type=F32)
    return x1 + _rms(f, gpost)


def _ffn_prompt_kernel(x_ref, gpre_ref, gpost_ref, up_ref, cw_ref, cb_ref, down_ref, y_ref, tail_ref, tail_scr):
    @pl.when(pl.program_id(1) == 0)
    def _init():
        tail_scr[...] = jnp.zeros_like(tail_scr)

    rows = x_ref.shape[0]

    def hist(u, cols):
        tail = tail_scr[:, cols]
        return _shift_rows(u, tail, 1), _shift_rows(u, tail, 2)

    def keep(u, cols):
        tail_scr[:, cols] = u[rows - SUBLANES:rows, :]
        tail_ref[0, :, cols] = u[rows - SUBLANES:rows, :]

    y_ref[...] = _ffn_body(x_ref[...], gpre_ref[...], gpost_ref[...], up_ref, cw_ref, cb_ref, down_ref, hist, keep)


def _ffn_prompt(x1, fw, b, t, tm):
    n, d = x1.shape
    nt = t // tm
    dff2 = fw["up"].shape[1]
    names = ("g_pre", "g_post", "up", "conv_w", "conv_b", "down")
    rows = pl.BlockSpec((tm, d), lambda i, j: (i * nt + j, 0))
    return pl.pallas_call(
        _ffn_prompt_kernel,
        grid=(b, nt),
        in_specs=[rows] + [_const(fw[k].shape) for k in names],
        out_specs=[rows, pl.BlockSpec((1, SUBLANES, dff2), lambda i, j: (i, 0, 0))],
        out_shape=[jax.ShapeDtypeStruct((n, d), F32), jax.ShapeDtypeStruct((b, SUBLANES, dff2), F32)],
        scratch_shapes=[pltpu.VMEM((SUBLANES, dff2), F32)],
        compiler_params=_params(("arbitrary", "arbitrary")),
    )(x1, *[fw[k] for k in names])


def _ffn_sample_kernel(x_ref, buf_ref, gpre_ref, gpost_ref, up_ref, cw_ref, cb_ref, down_ref, y_ref, nb_ref):
    def hist(u, cols):
        return buf_ref[:, 1, cols], buf_ref[:, 0, cols]

    def keep(u, cols):
        nb_ref[:, 0, cols] = buf_ref[:, 1, cols]
        nb_ref[:, 1, cols] = u

    y_ref[...] = _ffn_body(x_ref[...], gpre_ref[...], gpost_ref[...], up_ref, cw_ref, cb_ref, down_ref, hist, keep)


def _ffn_sample(x1, buf, fw):
    names = ("g_pre", "g_post", "up", "conv_w", "conv_b", "down")
    args = (x1, buf, *[fw[k] for k in names])
    return pl.pallas_call(
        _ffn_sample_kernel,
        grid=(1,),
        in_specs=[_const(v.shape) for v in args],
        out_specs=[_const(x1.shape), _const(buf.shape)],
        out_shape=[jax.ShapeDtypeStruct(x1.shape, F32), jax.ShapeDtypeStruct(buf.shape, F32)],
        compiler_params=_params(("arbitrary",)),
    )(*args)


def _block_diag(blocks):
    n, bi, bj = blocks.shape
    eye = jnp.eye(n, dtype=blocks.dtype)
    return jnp.einsum("nij,nm->nimj", blocks, eye).reshape(n * bi, n * bj)


def _layer_weights(norm_pre_mix, norm_post_mix, norm_pre_ffn, norm_post_ffn, w_in,
                   rwkv_mu, rwkv_w0, rwkv_w_up, rwkv_a0, rwkv_a_up, rwkv_g_up, rwkv_k_k, rwkv_k_a,
                   rwkv_r_k, rwkv_lnx_w, rwkv_lnx_b, rwkv_w_out,
                   lru_conv_w, lru_conv_b, lru_gx_w, lru_gx_b, lru_ga_w, lru_ga_b, lru_lambda, lru_w_out,
                   w_o, ffn_up, ffn_conv_w, ffn_conv_b, ffn_down):
    row = lambda x: x.reshape(1, -1)
    w = rwkv_w0.shape[-1]
    head_of = jnp.arange(w) // HEAD_DIM
    t_idx = jnp.arange(CHUNK)
    lo_w, lo_a = rwkv_w_up.shape[0], rwkv_a_up.shape[0]
    lora2 = jnp.zeros((lo_w + lo_a, 2 * w), F32).at[:lo_w, :w].set(rwkv_w_up).at[lo_w:, w:].set(rwkv_a_up)
    rw = dict(mu=row(rwkv_mu), w0=row(rwkv_w0), lora2=lora2.astype(BF16), a0=row(rwkv_a0),
              g_up=rwkv_g_up.astype(BF16), k_k=row(rwkv_k_k), k_a=row(rwkv_k_a), r_k=row(rwkv_r_k),
              lnx_w=row(rwkv_lnx_w), lnx_b=row(rwkv_lnx_b), w_out=rwkv_w_out.astype(BF16),
              seg=(head_of[:, None] == head_of[None, :]).astype(BF16),
              tri=(t_idx[:, None] >= t_idx[None, :]).astype(BF16))
    lw = dict(conv_w=lru_conv_w, conv_b=row(lru_conv_b),
              gate_w=jnp.concatenate([_block_diag(lru_gx_w), _block_diag(lru_ga_w)], axis=1).astype(BF16),
              gate_b=jnp.concatenate([row(lru_gx_b), row(lru_ga_b)], axis=1), lam=row(lru_lambda))
    fw = dict(g_pre=row(norm_pre_ffn), g_post=row(norm_post_ffn), up=ffn_up.astype(BF16),
              conv_w=ffn_conv_w, conv_b=row(ffn_conv_b), down=ffn_down.astype(BF16))
    return dict(g_in=row(norm_pre_mix), w_in=w_in.astype(BF16), rw=rw, lw=lw, fw=fw,
                lru_w_out=lru_w_out.astype(BF16), w_o=w_o.astype(BF16), g_mix=row(norm_post_mix))


def _splits(wts):
    w = wts["rw"]["w0"].shape[-1]
    c_rwkv = wts["rw"]["mu"].shape[-1]
    c_lru = wts["lw"]["lam"].shape[-1]
    return (c_rwkv, c_lru, wts["w_in"].shape[1] - c_rwkv - c_lru)


def _prompt_layer(x, wts):
    b, t, d = x.shape
    x2 = x.reshape(b * t, d)
    tm = min(t, ROW_TILE)
    m, xb, gates, p_tail = _proj_prompt(x2, wts["g_in"], wts["w_in"], wts["rw"]["mu"], _splits(wts), b, t, tm)
    oa, wkv = _rwkv_prompt(m, wts["rw"], b, t, min(t, WKV_ROWS))
    x1, xb_tail, h_tail = _mix_prompt(x2, oa, xb, gates, wts["lw"], wts["lru_w_out"], wts["w_o"], wts["g_mix"],
                                      b, t, tm)
    y, u_tail = _ffn_prompt(x1, wts["fw"], b, t, min(t, FFN_ROWS))
    conv_w = wts["lw"]["conv_w"].shape[0]
    ffn_w = wts["fw"]["conv_w"].shape[0]
    state = (p_tail[:, SUBLANES - 1:], wkv, xb_tail[:, SUBLANES - (conv_w - 1):], h_tail[:, SUBLANES - 1],
             u_tail[:, SUBLANES - (ffn_w - 1):])
    return y.reshape(b, t, d), state


def _sample_layer(x, shift, wkv, lru_buf, h0, ffn_buf, wts):
    n, t, d = x.shape
    x2 = x.reshape(n, d)
    p, xb, gates = _proj(x2, wts["g_in"], wts["w_in"], _splits(wts), n)
    oa, wkv_t = _rwkv_sample(p, shift[:, 0, :], jnp.transpose(wkv, (1, 2, 3, 0)), wts["rw"])
    lru_bufs = [lru_buf[:, j, :] for j in range(lru_buf.shape[1])]
    h = _lru_sample(xb, lru_bufs, h0, wts["lw"])
    x1 = _mix_sample(x2, oa, h, gates, wts["lru_w_out"], wts["w_o"], wts["g_mix"])
    y, ffn_new = _ffn_sample(x1, ffn_buf, wts["fw"])
    state = (p.reshape(n, 1, -1), jnp.transpose(wkv_t, (3, 0, 1, 2)), jnp.stack(lru_bufs[1:] + [xb], axis=1), h,
             ffn_new)
    return y.reshape(n, t, d), state


def kernel(x_prompt, x_sample, state_rwkv_shift, state_rwkv_wkv, state_lru_conv, state_lru_h, state_ffn_conv,
           norm_pre_mix, norm_post_mix, norm_pre_ffn, norm_post_ffn, w_in,
           rwkv_mu, rwkv_w0, rwkv_w_up, rwkv_a0, rwkv_a_up, rwkv_g_up, rwkv_k_k, rwkv_k_a,
           rwkv_r_k, rwkv_lnx_w, rwkv_lnx_b, rwkv_w_out,
           lru_conv_w, lru_conv_b, lru_gx_w, lru_gx_b, lru_ga_w, lru_ga_b, lru_lambda, lru_w_out,
           w_o, ffn_up, ffn_conv_w, ffn_conv_b, ffn_down):
    params = (norm_pre_mix, norm_post_mix, norm_pre_ffn, norm_post_ffn, w_in,
              rwkv_mu, rwkv_w0, rwkv_w_up, rwkv_a0, rwkv_a_up, rwkv_g_up, rwkv_k_k, rwkv_k_a,
              rwkv_r_k, rwkv_lnx_w, rwkv_lnx_b, rwkv_w_out,
              lru_conv_w, lru_conv_b, lru_gx_w, lru_gx_b, lru_ga_w, lru_ga_b, lru_lambda, lru_w_out,
              w_o, ffn_up, ffn_conv_w, ffn_conv_b, ffn_down)
    depth = w_in.shape[0]
    assert depth == 1 and x_sample.shape[1] == 1
    yp, ys = x_prompt, x_sample
    new_p, new_s = [], []
    for l in range(depth):
        wts = _layer_weights(*[q[l] for q in params])
        yp, st_p = _prompt_layer(yp, wts)
        ys, st_s = _sample_layer(ys, state_rwkv_shift[l], state_rwkv_wkv[l], state_lru_conv[l],
                                 state_lru_h[l], state_ffn_conv[l], wts)
        new_p.append(st_p)
        new_s.append(st_s)
    stk = lambda lst, i: jnp.stack([s[i] for s in lst], axis=0)
    return (yp, ys,
            stk(new_p, 0), stk(new_p, 1), stk(new_p, 2), stk(new_p, 3), stk(new_p, 4),
            stk(new_s, 0), stk(new_s, 1), stk(new_s, 2), stk(new_s, 3), stk(new_s, 4))
```

```python
import functools
import math

import jax
import jax.numpy as jnp
from jax import lax
from jax.experimental import pallas as pl
from jax.experimental.pallas import tpu as pltpu

F32 = jnp.float32
BF16 = jnp.bfloat16

NORM_EPS = 1e-6
LN_X_EPS = 64e-5
LRU_C = 8.0
DECAY_SCALE = math.exp(-0.5)
HEAD_DIM = 64
LANES = 128
SUBLANES = 8
CHUNK = 64
ROW_TILE = 1024
WKV_ROWS = 512
GATE_COLS = 512
SHIFT_COLS = 256
FF_COLS = 256
FFN_ROWS = 256
FFN_AHEAD = 3
VMEM_LIMIT = 60 * 1024 * 1024

_NN = (((1,), (0,)), ((), ()))
_NT = (((1,), (1,)), ((), ()))
_TN = (((0,), (0,)), ((), ()))


def _rms(x, g):
    return x * lax.rsqrt(jnp.mean(x * x, axis=-1, keepdims=True) + NORM_EPS) * g


def _softplus(x):
    return jnp.maximum(x, 0.0) + jnp.log1p(jnp.exp(-jnp.abs(x)))


def _sigmoid(x):
    return 0.5 * jnp.tanh(0.5 * x) + 0.5


def _gelu_tanh(x):
    c = 0.7978845608028654
    hx = 0.5 * x
    return hx + hx * jnp.tanh(x * (c + (c * 0.044715) * (x * x)))


def _mm(a, b):
    return jnp.dot(a.astype(BF16), b.astype(BF16), preferred_element_type=F32)


def _mmd(a, b, dims=_NN):
    return lax.dot_general(a.astype(BF16), b.astype(BF16), dims, preferred_element_type=F32)


def _segsum(x, seg):
    return jnp.dot(x.astype(BF16), seg, preferred_element_type=F32)


def _cumsum_rows(tri, x):
    hi = x.astype(BF16)
    lo = (x - hi.astype(F32)).astype(BF16)
    d = lambda y: jnp.dot(tri, y, preferred_element_type=F32)
    return d(hi) + d(lo)


def _shift_rows(x, tail, j):
    xs = pltpu.roll(x, j, axis=0)
    ts = pltpu.roll(tail, j, axis=0)
    row = lax.broadcasted_iota(jnp.int32, ts.shape, 0)
    head = jnp.where(row < j, ts, xs[0:SUBLANES])
    if x.shape[0] == SUBLANES:
        return head
    return jnp.concatenate([head, xs[SUBLANES:]], axis=0)


def _const(shape):
    n = len(shape)
    return pl.BlockSpec(shape, lambda *_: (0,) * n, pipeline_mode=pl.Buffered(1))


def _params(sem):
    return pltpu.CompilerParams(dimension_semantics=sem, vmem_limit_bytes=VMEM_LIMIT)


def _proj_kernel(x_ref, g_ref, w_ref, p_ref, xb_ref, gt_ref, *, splits):
    xn = _rms(x_ref[...], g_ref[...]).astype(BF16)
    c0, c1, c2 = splits
    p_ref[...] = jnp.dot(xn, w_ref[:, 0:c0], preferred_element_type=F32)
    xb_ref[...] = jnp.dot(xn, w_ref[:, c0:c0 + c1], preferred_element_type=F32)
    gt_ref[...] = jnp.dot(xn, w_ref[:, c0 + c1:c0 + c1 + c2], preferred_element_type=F32)


def _proj_prompt_kernel(x_ref, g_ref, w_ref, mu_ref, m_ref, xb_ref, gt_ref, tail_ref, prev_scr, *, splits):
    @pl.when(pl.program_id(1) == 0)
    def _init():
        prev_scr[...] = jnp.zeros_like(prev_scr)

    xn = _rms(x_ref[...], g_ref[...]).astype(BF16)
    c0, c1, c2 = splits
    g0 = c0 + c1
    p = jnp.dot(xn, w_ref[:, 0:c0], preferred_element_type=F32)
    xb_ref[...] = jnp.dot(xn, w_ref[:, c0:g0], preferred_element_type=F32)
    rows = p.shape[0]
    row = lax.broadcasted_iota(jnp.int32, (rows, SHIFT_COLS), 0)
    shift_chunks = list(range(0, c0, SHIFT_COLS))
    gate_chunks = list(range(0, c2, GATE_COLS))
    per_gate = -(-len(shift_chunks) // len(gate_chunks))
    for i, off in enumerate(gate_chunks):
        gt_ref[:, off:off + GATE_COLS] = jnp.dot(xn, w_ref[:, g0 + off:g0 + off + GATE_COLS],
                                                 preferred_element_type=F32)
        for s0 in shift_chunks[i * per_gate:(i + 1) * per_gate]:
            cs = slice(s0, s0 + SHIFT_COLS)
            pc = p[:, cs]
            prev = jnp.where(row == 0, prev_scr[SUBLANES - 1:SUBLANES, cs], pltpu.roll(pc, 1, axis=0))
            m_ref[:, cs] = pc + mu_ref[:, cs] * (prev - pc)
    prev_scr[...] = p[rows - SUBLANES:rows, :]
    tail_ref[0] = p[rows - SUBLANES:rows, :]


def _proj_prompt(x, g, w_in16, mu, splits, b, t, tm):
    n, d = x.shape
    c0, c1, c2 = splits
    nt = t // tm
    rows = lambda c: pl.BlockSpec((tm, c), lambda i, j: (i * nt + j, 0))
    return pl.pallas_call(
        functools.partial(_proj_prompt_kernel, splits=splits),
        grid=(b, nt),
        in_specs=[rows(d), _const((1, d)), _const(w_in16.shape), _const(mu.shape)],
        out_specs=[rows(c0), rows(c1), rows(c2), pl.BlockSpec((1, SUBLANES, c0), lambda i, j: (i, 0, 0))],
        out_shape=[jax.ShapeDtypeStruct((n, c0), F32), jax.ShapeDtypeStruct((n, c1), F32),
                   jax.ShapeDtypeStruct((n, c2), F32), jax.ShapeDtypeStruct((b, SUBLANES, c0), F32)],
        scratch_shapes=[pltpu.VMEM((SUBLANES, c0), F32)],
        compiler_params=_params(("arbitrary", "arbitrary")),
    )(x, g, w_in16, mu)


def _proj(x, g, w_in16, splits, tm):
    n, d = x.shape
    c0, c1, c2 = splits
    return pl.pallas_call(
        functools.partial(_proj_kernel, splits=splits),
        grid=(n // tm,),
        in_specs=[pl.BlockSpec((tm, d), lambda i: (i, 0)), _const((1, d)), _const(w_in16.shape)],
        out_specs=[pl.BlockSpec((tm, c0), lambda i: (i, 0)),
                   pl.BlockSpec((tm, c1), lambda i: (i, 0)),
                   pl.BlockSpec((tm, c2), lambda i: (i, 0))],
        out_shape=[jax.ShapeDtypeStruct((n, c0), F32), jax.ShapeDtypeStruct((n, c1), F32),
                   jax.ShapeDtypeStruct((n, c2), F32)],
        compiler_params=_params(("arbitrary",)),
    )(x, g, w_in16)


def _rwkv_pre(m, w0, lora2, a0, g_up, k_k, k_a, seg):
    w = w0.shape[-1]
    r, k, v = m[:, 0:w], m[:, w:2 * w], m[:, 2 * w:3 * w]
    z = m[:, 3 * w:3 * w + LANES]
    lane = lax.broadcasted_iota(jnp.int32, z.shape, 1)
    lor = _mm(jnp.where(lane < HEAD_DIM, jnp.tanh(z), z), lora2)
    lw = -DECAY_SCALE * _sigmoid(w0 + lor[:, 0:w])
    a_sig = _sigmoid(a0 + lor[:, w:2 * w])
    g = _mm(_sigmoid(m[:, 3 * w + LANES:3 * w + 2 * LANES]), g_up)
    kk = k * k_k
    kkn = kk * lax.rsqrt(jnp.maximum(_segsum(kk * kk, seg), 1e-24))
    k2 = k * (1.0 + (a_sig - 1.0) * k_a)
    return r, lw, k2, v, -kkn, kkn * a_sig, g


def _rwkv_post(y, r, k2, v, g, r_k, lnx_w, lnx_b, w_out, seg):
    inv = 1.0 / HEAD_DIM
    mean = _segsum(y, seg) * inv
    d = y - mean
    var = _segsum(d * d, seg) * inv
    yn = d * lax.rsqrt(var + LN_X_EPS) * lnx_w + lnx_b
    bonus = _segsum(r * k2 * r_k, seg) * v
    return _mm((yn + bonus) * g, w_out)


def _wkv_chunk_summaries(xs, m0b, strict, incl, eye2, eyew):
    ln = xs[0][0].shape[0]
    swap = lambda x: pltpu.roll(x, HEAD_DIM, axis=1)
    h16 = lambda x: x.astype(BF16)
    zero16 = jnp.zeros((), BF16)
    bd = lambda x: jnp.concatenate([jnp.where(m0b, x, zero16), jnp.where(m0b, zero16, x)], axis=0)
    cat = jnp.concatenate
    rts, ats, bts, kts, bhs, khs, vs, wls = zip(*xs)
    at16 = [h16(x) for x in ats]
    v16 = [h16(x) for x in vs]
    bdv = [bd(x) for x in v16]
    gms = [_mmd(cat([bd(a), bd(h16(rt))], axis=0), cat([bt, kt], axis=0), _NT)
           for rt, a, bt, kt in zip(rts, at16, bts, kts)]
    ga0 = [jnp.where(strict, g[0:ln], 0.0) for g in gms]
    ga1 = [jnp.where(strict, g[ln:2 * ln], 0.0) for g in gms]
    gr0 = [jnp.where(incl, g[2 * ln:3 * ln], 0.0) for g in gms]
    gr1 = [jnp.where(incl, g[3 * ln:4 * ln], 0.0) for g in gms]
    n_ab = [jnp.where(m0b, x, swap(y)) for x, y in zip(ga0, ga1)]
    n_ak = [jnp.where(m0b, swap(x), y) for x, y in zip(ga0, ga1)]
    n_rb = [jnp.where(m0b, x, swap(y)) for x, y in zip(gr0, gr1)]
    n_rk = [jnp.where(m0b, swap(x), y) for x, y in zip(gr0, gr1)]
    tinv = [eye2 + n for n in n_ab]
    steps = (ln - 1).bit_length() - 1
    if steps > 0:
        npow = [h16(_mmd(n, bd(n))) for n in (h16(x) for x in n_ab)]
        for _ in range(steps - 1):
            both = [_mmd(cat([n, h16(t)], axis=0), bd(n)) for n, t in zip(npow, tinv)]
            tinv = [t + x[ln:2 * ln] for t, x in zip(tinv, both)]
            npow = [h16(x[0:ln]) for x in both]
        tinv = [t + _mmd(t, bd(n)) for t, n in zip(tinv, npow)]
    kv = [_mmd(cat([nk, nr], axis=0), b) for nk, nr, b in zip(n_ak, n_rk, bdv)]
    akv = [x[0:ln] for x in kv]
    rkv = [x[ln:2 * ln] for x in kv]
    tt = [_mmd(t, cat([bd(a), bd(h16(ak))], axis=1)) for t, a, ak in zip(tinv, at16, akv)]
    ta = [h16(x[:, 0:LANES]) for x in tt]
    tk = [h16(x[:, LANES:2 * LANES]) for x in tt]
    mc = [_mmd(cat([cat([a, k], axis=1), cat([jnp.zeros_like(v), v], axis=1)], axis=0), cat([bh, kh], axis=0), _TN)
          for a, k, v, bh, kh in zip(ta, tk, v16, bhs, khs)]
    ms = [eyew * wl + x[0:LANES] for x, wl in zip(mc, wls)]
    cs = [x[LANES:2 * LANES] for x in mc]
    qy = [_mmd(n, cat([bd(a), bd(k)], axis=1)) for n, a, k in zip(n_rb, ta, tk)]
    qs = [rt + x[:, 0:LANES] for rt, x in zip(rts, qy)]
    y0 = [x[:, LANES:2 * LANES] + r for x, r in zip(qy, rkv)]
    return list(zip(qs, y0, ms, cs))


def _rwkv_prompt_kernel(m_ref, w0_ref, lora_ref, a0_ref, gup_ref, kk_ref, ka_ref, rk_ref,
                        lnw_ref, lnb_ref, wout_ref, seg_ref, tri_ref, oa_ref, wkv_ref, s_scr):
    t = pl.program_id(1)

    @pl.when(t == 0)
    def _init():
        s_scr[...] = jnp.zeros_like(s_scr)

    rows = m_ref.shape[0]
    seg = seg_ref[...]
    r, lw, k2, v, aa, bb, g = _rwkv_pre(m_ref[...], w0_ref[...], lora_ref[...], a0_ref[...],
                                        gup_ref[...], kk_ref[...], ka_ref[...], seg)
    c = _cumsum_rows(tri_ref[...], lw)

    ln = CHUNK
    ri = lax.broadcasted_iota(jnp.int32, (ln, LANES), 0)
    ci = lax.broadcasted_iota(jnp.int32, (ln, LANES), 1)
    cj = jnp.where(ci < HEAD_DIM, ci, ci - HEAD_DIM)
    m0b = ci < HEAD_DIM
    strict = cj < ri
    incl = cj <= ri
    eye2 = (cj == ri).astype(F32)
    r2 = lax.broadcasted_iota(jnp.int32, (LANES, LANES), 0)
    c2 = lax.broadcasted_iota(jnp.int32, (LANES, LANES), 1)
    bdmask = ((r2 < HEAD_DIM) == (c2 < HEAD_DIM)).astype(F32)
    eyew = (r2 == c2).astype(F32)

    npairs = s_scr.shape[0]
    insts = []
    for ch in range(rows // ln):
        rs = slice(ch * ln, (ch + 1) * ln)
        c_, lw_ = c[rs], lw[rs]
        cl = c_[ln - 1:ln, :]
        e_n = jnp.exp(-c_)
        e_l = jnp.exp(cl - c_)
        rt = r[rs] * jnp.exp(c_)
        at = aa[rs] * jnp.exp(c_ - lw_)
        bt, kt = bb[rs] * e_n, k2[rs] * e_n
        bh, kh = bb[rs] * e_l, k2[rs] * e_l
        wl = jnp.exp(cl)
        v_ = v[rs]
        for pr in range(npairs):
            sl = slice(pr * LANES, (pr + 1) * LANES)
            insts.append((rt[:, sl], at[:, sl], bt[:, sl], kt[:, sl], bh[:, sl], kh[:, sl], v_[:, sl], wl[:, sl]))
    summaries = _wkv_chunk_summaries(insts, m0b, strict, incl, eye2, eyew)

    states = [s_scr[pr] for pr in range(npairs)]
    y_rows = []
    for ch in range(rows // ln):
        ys = []
        for pr in range(npairs):
            q, y0, m, cc = summaries[ch * npairs + pr]
            ys.append(_mmd(q, states[pr], _NT) + y0)
            states[pr] = (_mmd(states[pr], m) + cc) * bdmask
        y_rows.append(jnp.concatenate(ys, axis=1))
    for pr in range(npairs):
        s_scr[pr] = states[pr]
    y = jnp.concatenate(y_rows, axis=0)
    oa_ref[...] = _rwkv_post(y, r, k2, v, g, rk_ref[...], lnw_ref[...], lnb_ref[...], wout_ref[...], seg)

    @pl.when(t == pl.num_programs(1) - 1)
    def _fin():
        for pr in range(npairs):
            s = s_scr[pr]
            wkv_ref[0, 2 * pr] = s[0:HEAD_DIM, 0:HEAD_DIM]
            wkv_ref[0, 2 * pr + 1] = pltpu.roll(s[HEAD_DIM:LANES, :], HEAD_DIM, axis=1)[:, 0:HEAD_DIM]


def _rwkv_prompt(m, rw, b, t, tc):
    n, cp = m.shape
    w = rw["w0"].shape[-1]
    heads = w // HEAD_DIM
    nt = t // tc
    dm = rw["w_out"].shape[-1]
    tri = jnp.kron(jnp.eye(tc // CHUNK, dtype=BF16), rw["tri"])
    names = ("w0", "lora2", "a0", "g_up", "k_k", "k_a", "r_k", "lnx_w", "lnx_b", "w_out", "seg")
    return pl.pallas_call(
        _rwkv_prompt_kernel,
        grid=(b, nt),
        in_specs=[pl.BlockSpec((tc, cp), lambda i, j: (i * nt + j, 0))] + [_const(rw[k].shape) for k in names]
                 + [_const(tri.shape)],
        out_specs=[pl.BlockSpec((tc, dm), lambda i, j: (i * nt + j, 0)),
                   pl.BlockSpec((1, heads, HEAD_DIM, HEAD_DIM), lambda i, j: (i, 0, 0, 0))],
        out_shape=[jax.ShapeDtypeStruct((n, dm), F32),
                   jax.ShapeDtypeStruct((b, heads, HEAD_DIM, HEAD_DIM), F32)],
        scratch_shapes=[pltpu.VMEM((heads // 2, LANES, LANES), F32)],
        compiler_params=_params(("arbitrary", "arbitrary")),
    )(m, *[rw[k] for k in names], tri)


def _rwkv_sample_pre_kernel(p_ref, prev_ref, mu_ref, w0_ref, lora_ref, a0_ref, gup_ref, kk_ref, ka_ref, seg_ref,
                            rt_ref, wt_ref, kt_ref, vt_ref, at_ref, bt_ref, r_ref, k_ref, v_ref, g_ref):
    p = p_ref[...]
    r, lw, k2, v, aa, bb, g = _rwkv_pre(p + mu_ref[...] * (prev_ref[...] - p), w0_ref[...], lora_ref[...],
                                        a0_ref[...], gup_ref[...], kk_ref[...], ka_ref[...], seg_ref[...])
    rt_ref[...] = r.T
    wt_ref[...] = jnp.exp(lw).T
    kt_ref[...] = k2.T
    vt_ref[...] = v.T
    at_ref[...] = aa.T
    bt_ref[...] = bb.T
    r_ref[...] = r
    k_ref[...] = k2
    v_ref[...] = v
    g_ref[...] = g


def _rwkv_sample_step_kernel(s_ref, r_ref, w_ref, k_ref, v_ref, a_ref, b_ref, y_ref, so_ref):
    r, w, k, a, b = r_ref[...], w_ref[...], k_ref[...], a_ref[...], b_ref[...]

    def body(i, carry):
        base = pl.multiple_of(i * SUBLANES, SUBLANES)
        vrows = v_ref[pl.ds(base, SUBLANES), :]
        ys = []
        for j in range(SUBLANES):
            s = s_ref[0, base + j]
            sa = jnp.sum(s * a, axis=0, keepdims=True)
            s_new = s * w + sa * b + vrows[j:j + 1, :] * k
            so_ref[0, base + j] = s_new
            ys.append(jnp.sum(s_new * r, axis=0, keepdims=True))
        y_ref[pl.ds(base, SUBLANES), :] = jnp.concatenate(ys, axis=0)
        return carry

    lax.fori_loop(0, s_ref.shape[1] // SUBLANES, body, 0)


def _rwkv_sample_post_kernel(yt_ref, r_ref, k_ref, v_ref, g_ref, rk_ref, lnw_ref, lnb_ref, wout_ref, seg_ref, oa_ref):
    oa_ref[...] = _rwkv_post(yt_ref[...].T, r_ref[...], k_ref[...], v_ref[...], g_ref[...], rk_ref[...],
                             lnw_ref[...], lnb_ref[...], wout_ref[...], seg_ref[...])


def _rwkv_sample(p, shift, wkv, rw):
    n, cp = p.shape
    w = rw["w0"].shape[-1]
    heads = w // HEAD_DIM
    dm = rw["w_out"].shape[-1]
    pre_names = ("mu", "w0", "lora2", "a0", "g_up", "k_k", "k_a", "seg")
    vec = jax.ShapeDtypeStruct((n, w), F32)
    vec_t = jax.ShapeDtypeStruct((w, n), F32)
    rt, wt, kt, vt, at, bt, r, k2, v, g = pl.pallas_call(
        _rwkv_sample_pre_kernel,
        grid=(1,),
        in_specs=[_const((n, cp)), _const((n, cp))] + [_const(rw[k].shape) for k in pre_names],
        out_specs=[_const((w, n))] * 6 + [_const((n, w))] * 4,
        out_shape=[vec_t] * 6 + [vec] * 4,
        compiler_params=_params(("arbitrary",)),
    )(p, shift, *[rw[k] for k in pre_names])

    head_spec = pl.BlockSpec((HEAD_DIM, n), lambda h: (h, 0))
    st_spec = pl.BlockSpec((1, HEAD_DIM, HEAD_DIM, n), lambda h: (h, 0, 0, 0))
    yt, wkv_new = pl.pallas_call(
        _rwkv_sample_step_kernel,
        grid=(heads,),
        in_specs=[st_spec] + [head_spec] * 6,
        out_specs=[head_spec, st_spec],
        out_shape=[vec_t, jax.ShapeDtypeStruct(wkv.shape, F32)],
        compiler_params=_params(("arbitrary",)),
    )(wkv, rt, wt, kt, vt, at, bt)

    post_names = ("r_k", "lnx_w", "lnx_b", "w_out", "seg")
    oa = pl.pallas_call(
        _rwkv_sample_post_kernel,
        grid=(1,),
        in_specs=[_const((w, n))] + [_const((n, w))] * 4 + [_const(rw[k].shape) for k in post_names],
        out_specs=_const((n, dm)),
        out_shape=jax.ShapeDtypeStruct((n, dm), F32),
        compiler_params=_params(("arbitrary",)),
    )(yt, r, k2, v, g, *[rw[k] for k in post_names])
    return oa, wkv_new


def _lru_gates(xc, gw, gb, lam):
    c = xc.shape[-1]
    gates = _mm(xc, gw) + gb
    gx = _sigmoid(gates[:, 0:c])
    ga = _sigmoid(gates[:, c:2 * c])
    log_a = -LRU_C * ga * _softplus(-lam)
    a = jnp.exp(log_a)
    u = jnp.sqrt(1.0 - a * a) * (gx * xc)
    return a, u


def _scan_rows(a, u, h0):
    rows, c = a.shape
    groups = rows // SUBLANES
    a3 = a.reshape(groups, SUBLANES, c)
    u3 = u.reshape(groups, SUBLANES, c)
    sub = lax.broadcasted_iota(jnp.int32, a3.shape, 1)
    k = 1
    while k < SUBLANES:
        keep = sub >= k
        u3 = u3 + a3 * jnp.where(keep, pltpu.roll(u3, k, axis=1), 0.0)
        a3 = a3 * jnp.where(keep, pltpu.roll(a3, k, axis=1), 1.0)
        k *= 2
    h = h0
    out = []
    for i in range(groups):
        hi = u3[i] + a3[i] * h
        out.append(hi)
        h = hi[SUBLANES - 1:SUBLANES, :]
    return jnp.concatenate(out, axis=0)


def _lru_prompt(x, tail, h0, cw_ref, cb_ref, gw, gb, lam):
    width = cw_ref.shape[0]
    xc = cb_ref[...] + cw_ref[width - 1:width, :] * x
    for j in range(1, width):
        xc = xc + cw_ref[width - 1 - j:width - j, :] * _shift_rows(x, tail, j)
    a, u = _lru_gates(xc, gw, gb, lam)
    return _scan_rows(a, u, h0)


def _lru_sample_kernel(x_ref, *refs):
    *buf_refs, h0_ref, cw_ref, cb_ref, gw_ref, gb_ref, lam_ref, h_ref = refs
    width = cw_ref.shape[0]
    xc = cb_ref[...] + cw_ref[width - 1:width, :] * x_ref[...]
    for j, buf_ref in enumerate(buf_refs):
        xc = xc + cw_ref[j:j + 1, :] * buf_ref[...]
    a, u = _lru_gates(xc, gw_ref[...], gb_ref[...], lam_ref[...])
    h_ref[...] = a * h0_ref[...] + u


def _lru_sample(xb, bufs, h0, lw):
    names = ("conv_w", "conv_b", "gate_w", "gate_b", "lam")
    args = (xb, *bufs, h0, *[lw[k] for k in names])
    return pl.pallas_call(
        _lru_sample_kernel,
        grid=(1,),
        in_specs=[_const(v.shape) for v in args],
        out_specs=_const(xb.shape),
        out_shape=jax.ShapeDtypeStruct(xb.shape, F32),
        compiler_params=_params(("arbitrary",)),
    )(*args)


def _mix_rows(x, oa, hs, gates, lwo, wo, g):
    d = x.shape[-1]
    ob = _mm(hs, lwo)
    mix = _mm(_sigmoid(gates[:, 0:d]) * oa + _sigmoid(gates[:, d:2 * d]) * ob, wo)
    return x + _rms(mix, g)


def _mix_sample_kernel(x_ref, oa_ref, hs_ref, gt_ref, lwo_ref, wo_ref, g_ref, o_ref):
    o_ref[...] = _mix_rows(x_ref[...], oa_ref[...], hs_ref[...], gt_ref[...], lwo_ref[...], wo_ref[...], g_ref[...])


def _mix_sample(x, oa, hs, gates, lwo16, wo16, g):
    args = (x, oa, hs, gates, lwo16, wo16, g)
    return pl.pallas_call(
        _mix_sample_kernel,
        grid=(1,),
        in_specs=[_const(v.shape) for v in args],
        out_specs=_const(x.shape),
        out_shape=jax.ShapeDtypeStruct(x.shape, F32),
        compiler_params=_params(("arbitrary",)),
    )(*args)


def _mix_prompt_kernel(x_ref, oa_ref, xb_ref, gt_ref, cw_ref, cb_ref, gw_ref, gb_ref, lam_ref, lwo_ref, wo_ref, g_ref,
                       o_ref, xtail_ref, htail_ref, xtail_scr, h_scr):
    @pl.when(pl.program_id(1) == 0)
    def _init():
        xtail_scr[...] = jnp.zeros_like(xtail_scr)
        h_scr[...] = jnp.zeros_like(h_scr)

    xb = xb_ref[...]
    rows = xb.shape[0]
    hs = _lru_prompt(xb, xtail_scr[...], h_scr[SUBLANES - 1:SUBLANES, :], cw_ref, cb_ref,
                     gw_ref[...], gb_ref[...], lam_ref[...])
    xtail_scr[...] = xb[rows - SUBLANES:rows, :]
    xtail_ref[0] = xb[rows - SUBLANES:rows, :]
    h_scr[...] = hs[rows - SUBLANES:rows, :]
    htail_ref[0] = hs[rows - SUBLANES:rows, :]
    o_ref[...] = _mix_rows(x_ref[...], oa_ref[...], hs, gt_ref[...], lwo_ref[...], wo_ref[...], g_ref[...])


def _mix_prompt(x, oa, xb, gates, lw, lwo16, wo16, g, b, t, tm):
    n, d = x.shape
    c = xb.shape[1]
    nt = t // tm
    names = ("conv_w", "conv_b", "gate_w", "gate_b", "lam")
    consts = [lw[k] for k in names] + [lwo16, wo16, g]
    rows = lambda w: pl.BlockSpec((tm, w), lambda i, j: (i * nt + j, 0))
    tail = pl.BlockSpec((1, SUBLANES, c), lambda i, j: (i, 0, 0))
    return pl.pallas_call(
        _mix_prompt_kernel,
        grid=(b, nt),
        in_specs=[rows(d), rows(d), rows(c), rows(2 * d)] + [_const(v.shape) for v in consts],
        out_specs=[rows(d), tail, tail],
        out_shape=[jax.ShapeDtypeStruct((n, d), F32), jax.ShapeDtypeStruct((b, SUBLANES, c), F32),
                   jax.ShapeDtypeStruct((b, SUBLANES, c), F32)],
        scratch_shapes=[pltpu.VMEM((SUBLANES, c), F32), pltpu.VMEM((SUBLANES, c), F32)],
        compiler_params=_params(("arbitrary", "arbitrary")),
    )(x, oa, xb, gates, *consts)


def _ffn_body(x1, gpre, gpost, up_ref, cw_ref, cb_ref, down_ref, hist, keep):
    dff = down_ref.shape[0]
    hn = _rms(x1, gpre).astype(BF16)
    starts = list(range(0, dff, FF_COLS))

    def up(c0):
        return [(slice(off, off + FF_COLS), jnp.dot(hn, up_ref[:, off:off + FF_COLS], preferred_element_type=F32))
                for off in (c0, dff + c0)]

    ups = [up(c0) for c0 in starts[:FFN_AHEAD]]
    f = jnp.zeros(x1.shape, F32)
    for i, c0 in enumerate(starts):
        if i + FFN_AHEAD < len(starts):
            ups.append(up(starts[i + FFN_AHEAD]))
        halves = []
        for cols, u in ups[i]:
            u1, u2 = hist(u, cols)
            halves.append(cb_ref[:, cols] + cw_ref[2:3, cols] * u + cw_ref[1:2, cols] * u1 + cw_ref[0:1, cols] * u2)
            keep(u, cols)
        act = _gelu_tanh(halves[0]) * halves[1]
        f = f + jnp.dot(act.astype(BF16), down_ref[c0:c0 + FF_COLS, :], preferred_element_type=F32)
    return x1 + _rms(f, gpost)


def _ffn_prompt_kernel(x_ref, gpre_ref, gpost_ref, up_ref, cw_ref, cb_ref, down_ref, y_ref, tail_ref, tail_scr):
    @pl.when(pl.program_id(1) == 0)
    def _init():
        tail_scr[...] = jnp.zeros_like(tail_scr)

    rows = x_ref.shape[0]

    def hist(u, cols):
        tail = tail_scr[:, cols]
        return _shift_rows(u, tail, 1), _shift_rows(u, tail, 2)

    def keep(u, cols):
        tail_scr[:, cols] = u[rows - SUBLANES:rows, :]
        tail_ref[0, :, cols] = u[rows - SUBLANES:rows, :]

    y_ref[...] = _ffn_body(x_ref[...], gpre_ref[...], gpost_ref[...], up_ref, cw_ref, cb_ref, down_ref, hist, keep)


def _ffn_prompt(x1, fw, b, t, tm):
    n, d = x1.shape
    nt = t // tm
    dff2 = fw["up"].shape[1]
    names = ("g_pre", "g_post", "up", "conv_w", "conv_b", "down")
    rows = pl.BlockSpec((tm, d), lambda i, j: (i * nt + j, 0))
    return pl.pallas_call(
        _ffn_prompt_kernel,
        grid=(b, nt),
        in_specs=[rows] + [_const(fw[k].shape) for k in names],
        out_specs=[rows, pl.BlockSpec((1, SUBLANES, dff2), lambda i, j: (i, 0, 0))],
        out_shape=[jax.ShapeDtypeStruct((n, d), F32), jax.ShapeDtypeStruct((b, SUBLANES, dff2), F32)],
        scratch_shapes=[pltpu.VMEM((SUBLANES, dff2), F32)],
        compiler_params=_params(("arbitrary", "arbitrary")),
    )(x1, *[fw[k] for k in names])


def _ffn_sample_kernel(x_ref, buf_ref, gpre_ref, gpost_ref, up_ref, cw_ref, cb_ref, down_ref, y_ref, nb_ref):
    def hist(u, cols):
        return buf_ref[:, 1, cols], buf_ref[:, 0, cols]

    def keep(u, cols):
        nb_ref[:, 0, cols] = buf_ref[:, 1, cols]
        nb_ref[:, 1, cols] = u

    y_ref[...] = _ffn_body(x_ref[...], gpre_ref[...], gpost_ref[...], up_ref, cw_ref, cb_ref, down_ref, hist, keep)


def _ffn_sample(x1, buf, fw):
    names = ("g_pre", "g_post", "up", "conv_w", "conv_b", "down")
    args = (x1, buf, *[fw[k] for k in names])
    return pl.pallas_call(
        _ffn_sample_kernel,
        grid=(1,),
        in_specs=[_const(v.shape) for v in args],
        out_specs=[_const(x1.shape), _const(buf.shape)],
        out_shape=[jax.ShapeDtypeStruct(x1.shape, F32), jax.ShapeDtypeStruct(buf.shape, F32)],
        compiler_params=_params(("arbitrary",)),
    )(*args)


def _block_diag(blocks):
    n, bi, bj = blocks.shape
    eye = jnp.eye(n, dtype=blocks.dtype)
    return jnp.einsum("nij,nm->nimj", blocks, eye).reshape(n * bi, n * bj)


def _layer_weights(norm_pre_mix, norm_post_mix, norm_pre_ffn, norm_post_ffn, w_in,
                   rwkv_mu, rwkv_w0, rwkv_w_up, rwkv_a0, rwkv_a_up, rwkv_g_up, rwkv_k_k, rwkv_k_a,
                   rwkv_r_k, rwkv_lnx_w, rwkv_lnx_b, rwkv_w_out,
                   lru_conv_w, lru_conv_b, lru_gx_w, lru_gx_b, lru_ga_w, lru_ga_b, lru_lambda, lru_w_out,
                   w_o, ffn_up, ffn_conv_w, ffn_conv_b, ffn_down):
    row = lambda x: x.reshape(1, -1)
    w = rwkv_w0.shape[-1]
    head_of = jnp.arange(w) // HEAD_DIM
    t_idx = jnp.arange(CHUNK)
    lo_w, lo_a = rwkv_w_up.shape[0], rwkv_a_up.shape[0]
    lora2 = jnp.zeros((lo_w + lo_a, 2 * w), F32).at[:lo_w, :w].set(rwkv_w_up).at[lo_w:, w:].set(rwkv_a_up)
    rw = dict(mu=row(rwkv_mu), w0=row(rwkv_w0), lora2=lora2.astype(BF16), a0=row(rwkv_a0),
              g_up=rwkv_g_up.astype(BF16), k_k=row(rwkv_k_k), k_a=row(rwkv_k_a), r_k=row(rwkv_r_k),
              lnx_w=row(rwkv_lnx_w), lnx_b=row(rwkv_lnx_b), w_out=rwkv_w_out.astype(BF16),
              seg=(head_of[:, None] == head_of[None, :]).astype(BF16),
              tri=(t_idx[:, None] >= t_idx[None, :]).astype(BF16))
    lw = dict(conv_w=lru_conv_w, conv_b=row(lru_conv_b),
              gate_w=jnp.concatenate([_block_diag(lru_gx_w), _block_diag(lru_ga_w)], axis=1).astype(BF16),
              gate_b=jnp.concatenate([row(lru_gx_b), row(lru_ga_b)], axis=1), lam=row(lru_lambda))
    fw = dict(g_pre=row(norm_pre_ffn), g_post=row(norm_post_ffn), up=ffn_up.astype(BF16),
              conv_w=ffn_conv_w, conv_b=row(ffn_conv_b), down=ffn_down.astype(BF16))
    return dict(g_in=row(norm_pre_mix), w_in=w_in.astype(BF16), rw=rw, lw=lw, fw=fw,
                lru_w_out=lru_w_out.astype(BF16), w_o=w_o.astype(BF16), g_mix=row(norm_post_mix))


def _splits(wts):
    w = wts["rw"]["w0"].shape[-1]
    c_rwkv = wts["rw"]["mu"].shape[-1]
    c_lru = wts["lw"]["lam"].shape[-1]
    return (c_rwkv, c_lru, wts["w_in"].shape[1] - c_rwkv - c_lru)


def _prompt_layer(x, wts):
    b, t, d = x.shape
    x2 = x.reshape(b * t, d)
    tm = min(t, ROW_TILE)
    m, xb, gates, p_tail = _proj_prompt(x2, wts["g_in"], wts["w_in"], wts["rw"]["mu"], _splits(wts), b, t, tm)
    oa, wkv = _rwkv_prompt(m, wts["rw"], b, t, min(t, WKV_ROWS))
    x1, xb_tail, h_tail = _mix_prompt(x2, oa, xb, gates, wts["lw"], wts["lru_w_out"], wts["w_o"], wts["g_mix"],
                                      b, t, tm)
    y, u_tail = _ffn_prompt(x1, wts["fw"], b, t, min(t, FFN_ROWS))
    conv_w = wts["lw"]["conv_w"].shape[0]
    ffn_w = wts["fw"]["conv_w"].shape[0]
    state = (p_tail[:, SUBLANES - 1:], wkv, xb_tail[:, SUBLANES - (conv_w - 1):], h_tail[:, SUBLANES - 1],
             u_tail[:, SUBLANES - (ffn_w - 1):])
    return y.reshape(b, t, d), state


def _sample_layer(x, shift, wkv, lru_buf, h0, ffn_buf, wts):
    n, t, d = x.shape
    x2 = x.reshape(n, d)
    p, xb, gates = _proj(x2, wts["g_in"], wts["w_in"], _splits(wts), n)
    oa, wkv_t = _rwkv_sample(p, shift[:, 0, :], jnp.transpose(wkv, (1, 2, 3, 0)), wts["rw"])
    lru_bufs = [lru_buf[:, j, :] for j in range(lru_buf.shape[1])]
    h = _lru_sample(xb, lru_bufs, h0, wts["lw"])
    x1 = _mix_sample(x2, oa, h, gates, wts["lru_w_out"], wts["w_o"], wts["g_mix"])
    y, ffn_new = _ffn_sample(x1, ffn_buf, wts["fw"])
    state = (p.reshape(n, 1, -1), jnp.transpose(wkv_t, (3, 0, 1, 2)), jnp.stack(lru_bufs[1:] + [xb], axis=1), h,
             ffn_new)
    return y.reshape(n, t, d), state


def kernel(x_prompt, x_sample, state_rwkv_shift, state_rwkv_wkv, state_lru_conv, state_lru_h, state_ffn_conv,
           norm_pre_mix, norm_post_mix, norm_pre_ffn, norm_post_ffn, w_in,
           rwkv_mu, rwkv_w0, rwkv_w_up, rwkv_a0, rwkv_a_up, rwkv_g_up, rwkv_k_k, rwkv_k_a,
           rwkv_r_k, rwkv_lnx_w, rwkv_lnx_b, rwkv_w_out,
           lru_conv_w, lru_conv_b, lru_gx_w, lru_gx_b, lru_ga_w, lru_ga_b, lru_lambda, lru_w_out,
           w_o, ffn_up, ffn_conv_w, ffn_conv_b, ffn_down):
    params = (norm_pre_mix, norm_post_mix, norm_pre_ffn, norm_post_ffn, w_in,
              rwkv_mu, rwkv_w0, rwkv_w_up, rwkv_a0, rwkv_a_up, rwkv_g_up, rwkv_k_k, rwkv_k_a,
              rwkv_r_k, rwkv_lnx_w, rwkv_lnx_b, rwkv_w_out,
              lru_conv_w, lru_conv_b, lru_gx_w, lru_gx_b, lru_ga_w, lru_ga_b, lru_lambda, lru_w_out,
              w_o, ffn_up, ffn_conv_w, ffn_conv_b, ffn_down)
    depth = w_in.shape[0]
    assert depth == 1 and x_sample.shape[1] == 1
    yp, ys = x_prompt, x_sample
    new_p, new_s = [], []
    for l in range(depth):
        wts = _layer_weights(*[q[l] for q in params])
        yp, st_p = _prompt_layer(yp, wts)
        ys, st_s = _sample_layer(ys, state_rwkv_shift[l], state_rwkv_wkv[l], state_lru_conv[l],
                                 state_lru_h[l], state_ffn_conv[l], wts)
        new_p.append(st_p)
        new_s.append(st_s)
    stk = lambda lst, i: jnp.stack([s[i] for s in lst], axis=0)
    return (yp, ys,
            stk(new_p, 0), stk(new_p, 1), stk(new_p, 2), stk(new_p, 3), stk(new_p, 4),
            stk(new_s, 0), stk(new_s, 1), stk(new_s, 2), stk(new_s, 3), stk(new_s, 4))
```

```python
import functools
import math

import jax
import jax.numpy as jnp
from jax import lax
from jax.experimental import pallas as pl
from jax.experimental.pallas import tpu as pltpu

F32 = jnp.float32
BF16 = jnp.bfloat16

NORM_EPS = 1e-6
LN_X_EPS = 64e-5
LRU_C = 8.0
DECAY_SCALE = math.exp(-0.5)
HEAD_DIM = 64
LANES = 128
SUBLANES = 8
MXU_TILE = 256
CHUNK = 64
ROW_TILE = 512
WKV_ROWS = 512
GATE_COLS = 512
SHIFT_COLS = 256
FF_COLS = 256
FFN_ROWS = 256
FFN_AHEAD = 3
VMEM_LIMIT = 56 * 1024 * 1024

_NN = (((1,), (0,)), ((), ()))
_NT = (((1,), (1,)), ((), ()))
_TN = (((0,), (0,)), ((), ()))


def _rms(x, g):
    return x * lax.rsqrt(jnp.mean(x * x, axis=-1, keepdims=True) + NORM_EPS) * g


def _softplus(x):
    return jnp.maximum(x, 0.0) + jnp.log1p(jnp.exp(-jnp.abs(x)))


def _sigmoid(x):
    return 0.5 * jnp.tanh(0.5 * x) + 0.5


def _gelu_tanh(x):
    c = 0.7978845608028654
    hx = 0.5 * x
    return hx + hx * jnp.tanh(x * (c + (c * 0.044715) * (x * x)))


def _mm(a, b):
    return jnp.dot(a.astype(BF16), b.astype(BF16), preferred_element_type=F32)


def _mmd(a, b, dims=_NN):
    return lax.dot_general(a.astype(BF16), b.astype(BF16), dims, preferred_element_type=F32)


def _segsum(x, seg):
    x16 = x.astype(BF16)
    n = seg.shape[0]
    return jnp.concatenate([jnp.dot(x16[:, c:c + n], seg, preferred_element_type=F32)
                            for c in range(0, x.shape[1], n)], axis=1)


def _cumsum_rows(tri, x):
    hi = x.astype(BF16)
    lo = (x - hi.astype(F32)).astype(BF16)
    n = tri.shape[0]
    d = lambda y: jnp.concatenate([jnp.dot(tri, y[r:r + n], preferred_element_type=F32)
                                   for r in range(0, y.shape[0], n)], axis=0)
    return d(hi) + d(lo)


def _shift_rows(x, tail, j):
    xs = pltpu.roll(x, j, axis=0)
    ts = pltpu.roll(tail, j, axis=0)
    row = lax.broadcasted_iota(jnp.int32, ts.shape, 0)
    head = jnp.where(row < j, ts, xs[0:SUBLANES])
    if x.shape[0] == SUBLANES:
        return head
    return jnp.concatenate([head, xs[SUBLANES:]], axis=0)


def _const(shape):
    n = len(shape)
    return pl.BlockSpec(shape, lambda *_: (0,) * n, pipeline_mode=pl.Buffered(1))


def _params(sem):
    return pltpu.CompilerParams(dimension_semantics=sem, vmem_limit_bytes=VMEM_LIMIT)


def _proj_kernel(x_ref, g_ref, w_ref, p_ref, xb_ref, gt_ref, *, splits):
    xn = _rms(x_ref[...], g_ref[...]).astype(BF16)
    c0, c1, c2 = splits
    p_ref[...] = jnp.dot(xn, w_ref[:, 0:c0], preferred_element_type=F32)
    xb_ref[...] = jnp.dot(xn, w_ref[:, c0:c0 + c1], preferred_element_type=F32)
    gt_ref[...] = jnp.dot(xn, w_ref[:, c0 + c1:c0 + c1 + c2], preferred_element_type=F32)


def _proj_prompt_kernel(x_ref, g_ref, w_ref, mu_ref, m_ref, xb_ref, gt_ref, tail_ref, prev_scr, *, splits):
    @pl.when(pl.program_id(1) == 0)
    def _init():
        prev_scr[...] = jnp.zeros_like(prev_scr)

    xn = _rms(x_ref[...], g_ref[...]).astype(BF16)
    c0, c1, c2 = splits
    g0 = c0 + c1
    p = jnp.dot(xn, w_ref[:, 0:c0], preferred_element_type=F32)
    xb_ref[...] = jnp.dot(xn, w_ref[:, c0:g0], preferred_element_type=F32)
    rows = p.shape[0]
    row = lax.broadcasted_iota(jnp.int32, (rows, SHIFT_COLS), 0)
    shift_chunks = list(range(0, c0, SHIFT_COLS))
    gate_chunks = list(range(0, c2, GATE_COLS))
    per_gate = -(-len(shift_chunks) // len(gate_chunks))
    for i, off in enumerate(gate_chunks):
        gt_ref[:, off:off + GATE_COLS] = jnp.dot(xn, w_ref[:, g0 + off:g0 + off + GATE_COLS],
                                                 preferred_element_type=F32)
        for s0 in shift_chunks[i * per_gate:(i + 1) * per_gate]:
            cs = slice(s0, s0 + SHIFT_COLS)
            pc = p[:, cs]
            prev = jnp.where(row == 0, prev_scr[SUBLANES - 1:SUBLANES, cs], pltpu.roll(pc, 1, axis=0))
            m_ref[:, cs] = pc + mu_ref[:, cs] * (prev - pc)
    prev_scr[...] = p[rows - SUBLANES:rows, :]
    tail_ref[0] = p[rows - SUBLANES:rows, :]


def _proj_prompt(x, g, w_in16, mu, splits, b, t, tm):
    n, d = x.shape
    c0, c1, c2 = splits
    nt = t // tm
    rows = lambda c: pl.BlockSpec((tm, c), lambda i, j: (i * nt + j, 0))
    return pl.pallas_call(
        functools.partial(_proj_prompt_kernel, splits=splits),
        grid=(b, nt),
        in_specs=[rows(d), _const((1, d)), _const(w_in16.shape), _const(mu.shape)],
        out_specs=[rows(c0), rows(c1), rows(c2), pl.BlockSpec((1, SUBLANES, c0), lambda i, j: (i, 0, 0))],
        out_shape=[jax.ShapeDtypeStruct((n, c0), F32), jax.ShapeDtypeStruct((n, c1), F32),
                   jax.ShapeDtypeStruct((n, c2), F32), jax.ShapeDtypeStruct((b, SUBLANES, c0), F32)],
        scratch_shapes=[pltpu.VMEM((SUBLANES, c0), F32)],
        compiler_params=_params(("arbitrary", "arbitrary")),
    )(x, g, w_in16, mu)


def _proj(x, g, w_in16, splits, tm):
    n, d = x.shape
    c0, c1, c2 = splits
    return pl.pallas_call(
        functools.partial(_proj_kernel, splits=splits),
        grid=(n // tm,),
        in_specs=[pl.BlockSpec((tm, d), lambda i: (i, 0)), _const((1, d)), _const(w_in16.shape)],
        out_specs=[pl.BlockSpec((tm, c0), lambda i: (i, 0)),
                   pl.BlockSpec((tm, c1), lambda i: (i, 0)),
                   pl.BlockSpec((tm, c2), lambda i: (i, 0))],
        out_shape=[jax.ShapeDtypeStruct((n, c0), F32), jax.ShapeDtypeStruct((n, c1), F32),
                   jax.ShapeDtypeStruct((n, c2), F32)],
        compiler_params=_params(("arbitrary",)),
    )(x, g, w_in16)


def _rwkv_pre(m, w0, lora2, a0, g_up, k_k, k_a, seg):
    w = w0.shape[-1]
    r, k, v = m[:, 0:w], m[:, w:2 * w], m[:, 2 * w:3 * w]
    z = m[:, 3 * w:3 * w + LANES]
    lane = lax.broadcasted_iota(jnp.int32, z.shape, 1)
    lor = _mm(jnp.where(lane < HEAD_DIM, jnp.tanh(z), z), lora2)
    lw = -DECAY_SCALE * _sigmoid(w0 + lor[:, 0:w])
    a_sig = _sigmoid(a0 + lor[:, w:2 * w])
    g = _mm(_sigmoid(m[:, 3 * w + LANES:3 * w + 2 * LANES]), g_up)
    kk = k * k_k
    kkn = kk * lax.rsqrt(jnp.maximum(_segsum(kk * kk, seg), 1e-24))
    k2 = k * (1.0 + (a_sig - 1.0) * k_a)
    return r, lw, k2, v, -kkn, kkn * a_sig, g


def _rwkv_post(y, r, k2, v, g, r_k, lnx_w, lnx_b, w_out, seg):
    inv = 1.0 / HEAD_DIM
    mean = _segsum(y, seg) * inv
    d = y - mean
    var = _segsum(d * d, seg) * inv
    yn = d * lax.rsqrt(var + LN_X_EPS) * lnx_w + lnx_b
    bonus = _segsum(r * k2 * r_k, seg) * v
    return _mm((yn + bonus) * g, w_out)


def _wkv_chunk_summaries(xs, m0b, strict, incl, eye2, eyew):
    ln = xs[0][0].shape[0]
    swap = lambda x: pltpu.roll(x, HEAD_DIM, axis=1)
    h16 = lambda x: x.astype(BF16)
    zero16 = jnp.zeros((), BF16)
    bd = lambda x: jnp.concatenate([jnp.where(m0b, x, zero16), jnp.where(m0b, zero16, x)], axis=0)
    cat = jnp.concatenate
    rts, ats, bts, kts, bhs, khs, vs, wls = zip(*xs)
    at16 = [h16(x) for x in ats]
    v16 = [h16(x) for x in vs]
    bdv = [bd(x) for x in v16]
    gms = [_mmd(cat([bd(a), bd(h16(rt))], axis=0), cat([bt, kt], axis=0), _NT)
           for rt, a, bt, kt in zip(rts, at16, bts, kts)]
    ga0 = [jnp.where(strict, g[0:ln], 0.0) for g in gms]
    ga1 = [jnp.where(strict, g[ln:2 * ln], 0.0) for g in gms]
    gr0 = [jnp.where(incl, g[2 * ln:3 * ln], 0.0) for g in gms]
    gr1 = [jnp.where(incl, g[3 * ln:4 * ln], 0.0) for g in gms]
    n_ab = [jnp.where(m0b, x, swap(y)) for x, y in zip(ga0, ga1)]
    n_ak = [jnp.where(m0b, swap(x), y) for x, y in zip(ga0, ga1)]
    n_rb = [jnp.where(m0b, x, swap(y)) for x, y in zip(gr0, gr1)]
    n_rk = [jnp.where(m0b, swap(x), y) for x, y in zip(gr0, gr1)]
    tinv = [eye2 + n for n in n_ab]
    steps = (ln - 1).bit_length() - 1
    if steps > 0:
        npow = [h16(_mmd(n, bd(n))) for n in (h16(x) for x in n_ab)]
        for _ in range(steps - 1):
            both = [_mmd(cat([n, h16(t)], axis=0), bd(n)) for n, t in zip(npow, tinv)]
            tinv = [t + x[ln:2 * ln] for t, x in zip(tinv, both)]
            npow = [h16(x[0:ln]) for x in both]
        tinv = [t + _mmd(t, bd(n)) for t, n in zip(tinv, npow)]
    kv = [_mmd(cat([nk, nr], axis=0), b) for nk, nr, b in zip(n_ak, n_rk, bdv)]
    akv = [x[0:ln] for x in kv]
    rkv = [x[ln:2 * ln] for x in kv]
    tt = [_mmd(t, cat([bd(a), bd(h16(ak))], axis=1)) for t, a, ak in zip(tinv, at16, akv)]
    ta = [h16(x[:, 0:LANES]) for x in tt]
    tk = [h16(x[:, LANES:2 * LANES]) for x in tt]
    mc = [_mmd(cat([cat([a, k], axis=1), cat([jnp.zeros_like(v), v], axis=1)], axis=0), cat([bh, kh], axis=0), _TN)
          for a, k, v, bh, kh in zip(ta, tk, v16, bhs, khs)]
    ms = [eyew * wl + x[0:LANES] for x, wl in zip(mc, wls)]
    cs = [x[LANES:2 * LANES] for x in mc]
    qy = [_mmd(n, cat([bd(a), bd(k)], axis=1)) for n, a, k in zip(n_rb, ta, tk)]
    qs = [rt + x[:, 0:LANES] for rt, x in zip(rts, qy)]
    y0 = [x[:, LANES:2 * LANES] + r for x, r in zip(qy, rkv)]
    return list(zip(qs, y0, ms, cs))


def _rwkv_prompt_kernel(m_ref, w0_ref, lora_ref, a0_ref, gup_ref, kk_ref, ka_ref, rk_ref,
                        lnw_ref, lnb_ref, wout_ref, seg_ref, tri_ref, oa_ref, wkv_ref, s_scr):
    t = pl.program_id(1)

    @pl.when(t == 0)
    def _init():
        s_scr[...] = jnp.zeros_like(s_scr)

    rows = m_ref.shape[0]
    seg = seg_ref[...]
    r, lw, k2, v, aa, bb, g = _rwkv_pre(m_ref[...], w0_ref[...], lora_ref[...], a0_ref[...],
                                        gup_ref[...], kk_ref[...], ka_ref[...], seg)
    c = _cumsum_rows(tri_ref[...], lw)

    ln = CHUNK
    ri = lax.broadcasted_iota(jnp.int32, (ln, LANES), 0)
    ci = lax.broadcasted_iota(jnp.int32, (ln, LANES), 1)
    cj = jnp.where(ci < HEAD_DIM, ci, ci - HEAD_DIM)
    m0b = ci < HEAD_DIM
    strict = cj < ri
    incl = cj <= ri
    eye2 = (cj == ri).astype(F32)
    r2 = lax.broadcasted_iota(jnp.int32, (LANES, LANES), 0)
    c2 = lax.broadcasted_iota(jnp.int32, (LANES, LANES), 1)
    bdmask = ((r2 < HEAD_DIM) == (c2 < HEAD_DIM)).astype(F32)
    eyew = (r2 == c2).astype(F32)

    npairs = s_scr.shape[0]
    insts = []
    for ch in range(rows // ln):
        rs = slice(ch * ln, (ch + 1) * ln)
        c_, lw_ = c[rs], lw[rs]
        cl = c_[ln - 1:ln, :]
        e_n = jnp.exp(-c_)
        e_l = jnp.exp(cl - c_)
        rt = r[rs] * jnp.exp(c_)
        at = aa[rs] * jnp.exp(c_ - lw_)
        bt, kt = bb[rs] * e_n, k2[rs] * e_n
        bh, kh = bb[rs] * e_l, k2[rs] * e_l
        wl = jnp.exp(cl)
        v_ = v[rs]
        for pr in range(npairs):
            sl = slice(pr * LANES, (pr + 1) * LANES)
            insts.append((rt[:, sl], at[:, sl], bt[:, sl], kt[:, sl], bh[:, sl], kh[:, sl], v_[:, sl], wl[:, sl]))
    summaries = _wkv_chunk_summaries(insts, m0b, strict, incl, eye2, eyew)

    states = [s_scr[pr] for pr in range(npairs)]
    y_rows = []
    for ch in range(rows // ln):
        ys = []
        for pr in range(npairs):
            q, y0, m, cc = summaries[ch * npairs + pr]
            ys.append(_mmd(q, states[pr], _NT) + y0)
            states[pr] = (_mmd(states[pr], m) + cc) * bdmask
        y_rows.append(jnp.concatenate(ys, axis=1))
    for pr in range(npairs):
        s_scr[pr] = states[pr]
    y = jnp.concatenate(y_rows, axis=0)
    oa_ref[...] = _rwkv_post(y, r, k2, v, g, rk_ref[...], lnw_ref[...], lnb_ref[...], wout_ref[...], seg)

    @pl.when(t == pl.num_programs(1) - 1)
    def _fin():
        for pr in range(npairs):
            s = s_scr[pr]
            wkv_ref[0, 2 * pr] = s[0:HEAD_DIM, 0:HEAD_DIM]
            wkv_ref[0, 2 * pr + 1] = pltpu.roll(s[HEAD_DIM:LANES, :], HEAD_DIM, axis=1)[:, 0:HEAD_DIM]


def _rwkv_prompt(m, rw, b, t, tc):
    n, cp = m.shape
    w = rw["w0"].shape[-1]
    heads = w // HEAD_DIM
    nt = t // tc
    dm = rw["w_out"].shape[-1]
    tri = jnp.kron(jnp.eye(min(tc, MXU_TILE) // CHUNK, dtype=BF16), rw["tri"])
    names = ("w0", "lora2", "a0", "g_up", "k_k", "k_a", "r_k", "lnx_w", "lnx_b", "w_out", "seg")
    return pl.pallas_call(
        _rwkv_prompt_kernel,
        grid=(b, nt),
        in_specs=[pl.BlockSpec((tc, cp), lambda i, j: (i * nt + j, 0))] + [_const(rw[k].shape) for k in names]
                 + [_const(tri.shape)],
        out_specs=[pl.BlockSpec((tc, dm), lambda i, j: (i * nt + j, 0)),
                   pl.BlockSpec((1, heads, HEAD_DIM, HEAD_DIM), lambda i, j: (i, 0, 0, 0))],
        out_shape=[jax.ShapeDtypeStruct((n, dm), F32),
                   jax.ShapeDtypeStruct((b, heads, HEAD_DIM, HEAD_DIM), F32)],
        scratch_shapes=[pltpu.VMEM((heads // 2, LANES, LANES), F32)],
        compiler_params=_params(("arbitrary", "arbitrary")),
    )(m, *[rw[k] for k in names], tri)


def _rwkv_sample_pre_kernel(p_ref, prev_ref, mu_ref, w0_ref, lora_ref, a0_ref, gup_ref, kk_ref, ka_ref, seg_ref,
                            rt_ref, wt_ref, kt_ref, vt_ref, at_ref, bt_ref, r_ref, k_ref, v_ref, g_ref):
    p = p_ref[...]
    r, lw, k2, v, aa, bb, g = _rwkv_pre(p + mu_ref[...] * (prev_ref[...] - p), w0_ref[...], lora_ref[...],
                                        a0_ref[...], gup_ref[...], kk_ref[...], ka_ref[...], seg_ref[...])
    rt_ref[...] = r.T
    wt_ref[...] = jnp.exp(lw).T
    kt_ref[...] = k2.T
    vt_ref[...] = v.T
    at_ref[...] = aa.T
    bt_ref[...] = bb.T
    r_ref[...] = r
    k_ref[...] = k2
    v_ref[...] = v
    g_ref[...] = g


def _rwkv_sample_step_kernel(s_ref, r_ref, w_ref, k_ref, v_ref, a_ref, b_ref, y_ref, so_ref):
    r, w, k, a, b = r_ref[...], w_ref[...], k_ref[...], a_ref[...], b_ref[...]

    def body(i, carry):
        base = pl.multiple_of(i * SUBLANES, SUBLANES)
        vrows = v_ref[pl.ds(base, SUBLANES), :]
        ys = []
        for j in range(SUBLANES):
            s = s_ref[0, base + j]
            sa = jnp.sum(s * a, axis=0, keepdims=True)
            s_new = s * w + sa * b + vrows[j:j + 1, :] * k
            so_ref[0, base + j] = s_new
            ys.append(jnp.sum(s_new * r, axis=0, keepdims=True))
        y_ref[pl.ds(base, SUBLANES), :] = jnp.concatenate(ys, axis=0)
        return carry

    lax.fori_loop(0, s_ref.shape[1] // SUBLANES, body, 0)


def _rwkv_sample_post_kernel(yt_ref, r_ref, k_ref, v_ref, g_ref, rk_ref, lnw_ref, lnb_ref, wout_ref, seg_ref, oa_ref):
    oa_ref[...] = _rwkv_post(yt_ref[...].T, r_ref[...], k_ref[...], v_ref[...], g_ref[...], rk_ref[...],
                             lnw_ref[...], lnb_ref[...], wout_ref[...], seg_ref[...])


def _rwkv_sample(p, shift, wkv, rw):
    n, cp = p.shape
    w = rw["w0"].shape[-1]
    heads = w // HEAD_DIM
    dm = rw["w_out"].shape[-1]
    pre_names = ("mu", "w0", "lora2", "a0", "g_up", "k_k", "k_a", "seg")
    vec = jax.ShapeDtypeStruct((n, w), F32)
    vec_t = jax.ShapeDtypeStruct((w, n), F32)
    rt, wt, kt, vt, at, bt, r, k2, v, g = pl.pallas_call(
        _rwkv_sample_pre_kernel,
        grid=(1,),
        in_specs=[_const((n, cp)), _const((n, cp))] + [_const(rw[k].shape) for k in pre_names],
        out_specs=[_const((w, n))] * 6 + [_const((n, w))] * 4,
        out_shape=[vec_t] * 6 + [vec] * 4,
        compiler_params=_params(("arbitrary",)),
    )(p, shift, *[rw[k] for k in pre_names])

    head_spec = pl.BlockSpec((HEAD_DIM, n), lambda h: (h, 0))
    st_spec = pl.BlockSpec((1, HEAD_DIM, HEAD_DIM, n), lambda h: (h, 0, 0, 0))
    yt, wkv_new = pl.pallas_call(
        _rwkv_sample_step_kernel,
        grid=(heads,),
        in_specs=[st_spec] + [head_spec] * 6,
        out_specs=[head_spec, st_spec],
        out_shape=[vec_t, jax.ShapeDtypeStruct(wkv.shape, F32)],
        compiler_params=_params(("arbitrary",)),
    )(wkv, rt, wt, kt, vt, at, bt)

    post_names = ("r_k", "lnx_w", "lnx_b", "w_out", "seg")
    oa = pl.pallas_call(
        _rwkv_sample_post_kernel,
        grid=(1,),
        in_specs=[_const((w, n))] + [_const((n, w))] * 4 + [_const(rw[k].shape) for k in post_names],
        out_specs=_const((n, dm)),
        out_shape=jax.ShapeDtypeStruct((n, dm), F32),
        compiler_params=_params(("arbitrary",)),
    )(yt, r, k2, v, g, *[rw[k] for k in post_names])
    return oa, wkv_new


def _lru_gates(xc, gw, gb, lam):
    c = xc.shape[-1]
    n = gw.shape[1]
    xc16 = xc.astype(BF16)
    parts = [jnp.dot(xc16[:, i * n:(i + 1) * n], gw[i], preferred_element_type=F32) for i in range(c // n)]
    gx = _sigmoid(jnp.concatenate([p[:, 0:n] for p in parts], axis=1) + gb[:, 0:c])
    ga = _sigmoid(jnp.concatenate([p[:, n:2 * n] for p in parts], axis=1) + gb[:, c:2 * c])
    log_a = -LRU_C * ga * _softplus(-lam)
    a = jnp.exp(log_a)
    u = jnp.sqrt(1.0 - a * a) * (gx * xc)
    return a, u


def _scan_rows(a, u, h0):
    rows, c = a.shape
    groups = rows // SUBLANES
    a3 = a.reshape(groups, SUBLANES, c)
    u3 = u.reshape(groups, SUBLANES, c)
    sub = lax.broadcasted_iota(jnp.int32, a3.shape, 1)
    k = 1
    while k < SUBLANES:
        keep = sub >= k
        u3 = u3 + a3 * jnp.where(keep, pltpu.roll(u3, k, axis=1), 0.0)
        a3 = a3 * jnp.where(keep, pltpu.roll(a3, k, axis=1), 1.0)
        k *= 2
    h = h0
    out = []
    for i in range(groups):
        hi = u3[i] + a3[i] * h
        out.append(hi)
        h = hi[SUBLANES - 1:SUBLANES, :]
    return jnp.concatenate(out, axis=0)


def _lru_prompt(x, tail, h0, cw_ref, cb_ref, gw, gb, lam):
    width = cw_ref.shape[0]
    xc = cb_ref[...] + cw_ref[width - 1:width, :] * x
    for j in range(1, width):
        xc = xc + cw_ref[width - 1 - j:width - j, :] * _shift_rows(x, tail, j)
    a, u = _lru_gates(xc, gw, gb, lam)
    return _scan_rows(a, u, h0)


def _lru_sample_kernel(x_ref, *refs):
    *buf_refs, h0_ref, cw_ref, cb_ref, gw_ref, gb_ref, lam_ref, h_ref = refs
    width = cw_ref.shape[0]
    xc = cb_ref[...] + cw_ref[width - 1:width, :] * x_ref[...]
    for j, buf_ref in enumerate(buf_refs):
        xc = xc + cw_ref[j:j + 1, :] * buf_ref[...]
    a, u = _lru_gates(xc, gw_ref[...], gb_ref[...], lam_ref[...])
    h_ref[...] = a * h0_ref[...] + u


def _lru_sample(xb, bufs, h0, lw):
    names = ("conv_w", "conv_b", "gate_w", "gate_b", "lam")
    args = (xb, *bufs, h0, *[lw[k] for k in names])
    return pl.pallas_call(
        _lru_sample_kernel,
        grid=(1,),
        in_specs=[_const(v.shape) for v in args],
        out_specs=_const(xb.shape),
        out_shape=jax.ShapeDtypeStruct(xb.shape, F32),
        compiler_params=_params(("arbitrary",)),
    )(*args)


def _mix_rows(x, oa, hs, gates, lwo, wo, g):
    d = x.shape[-1]
    ob = _mm(hs, lwo)
    mix = _mm(_sigmoid(gates[:, 0:d]) * oa + _sigmoid(gates[:, d:2 * d]) * ob, wo)
    return x + _rms(mix, g)


def _mix_sample_kernel(x_ref, oa_ref, hs_ref, gt_ref, lwo_ref, wo_ref, g_ref, o_ref):
    o_ref[...] = _mix_rows(x_ref[...], oa_ref[...], hs_ref[...], gt_ref[...], lwo_ref[...], wo_ref[...], g_ref[...])


def _mix_sample(x, oa, hs, gates, lwo16, wo16, g):
    args = (x, oa, hs, gates, lwo16, wo16, g)
    return pl.pallas_call(
        _mix_sample_kernel,
        grid=(1,),
        in_specs=[_const(v.shape) for v in args],
        out_specs=_const(x.shape),
        out_shape=jax.ShapeDtypeStruct(x.shape, F32),
        compiler_params=_params(("arbitrary",)),
    )(*args)


def _mix_prompt_kernel(x_ref, oa_ref, xb_ref, gt_ref, cw_ref, cb_ref, gw_ref, gb_ref, lam_ref, lwo_ref, wo_ref, g_ref,
                       o_ref, xtail_ref, htail_ref, xtail_scr, h_scr):
    @pl.when(pl.program_id(1) == 0)
    def _init():
        xtail_scr[...] = jnp.zeros_like(xtail_scr)
        h_scr[...] = jnp.zeros_like(h_scr)

    xb = xb_ref[...]
    rows = xb.shape[0]
    hs = _lru_prompt(xb, xtail_scr[...], h_scr[SUBLANES - 1:SUBLANES, :], cw_ref, cb_ref,
                     gw_ref[...], gb_ref[...], lam_ref[...])
    xtail_scr[...] = xb[rows - SUBLANES:rows, :]
    xtail_ref[0] = xb[rows - SUBLANES:rows, :]
    h_scr[...] = hs[rows - SUBLANES:rows, :]
    htail_ref[0] = hs[rows - SUBLANES:rows, :]
    o_ref[...] = _mix_rows(x_ref[...], oa_ref[...], hs, gt_ref[...], lwo_ref[...], wo_ref[...], g_ref[...])


def _mix_prompt(x, oa, xb, gates, lw, lwo16, wo16, g, b, t, tm):
    n, d = x.shape
    c = xb.shape[1]
    nt = t // tm
    names = ("conv_w", "conv_b", "gate_w", "gate_b", "lam")
    consts = [lw[k] for k in names] + [lwo16, wo16, g]
    rows = lambda w: pl.BlockSpec((tm, w), lambda i, j: (i * nt + j, 0))
    tail = pl.BlockSpec((1, SUBLANES, c), lambda i, j: (i, 0, 0))
    return pl.pallas_call(
        _mix_prompt_kernel,
        grid=(b, nt),
        in_specs=[rows(d), rows(d), rows(c), rows(2 * d)] + [_const(v.shape) for v in consts],
        out_specs=[rows(d), tail, tail],
        out_shape=[jax.ShapeDtypeStruct((n, d), F32), jax.ShapeDtypeStruct((b, SUBLANES, c), F32),
                   jax.ShapeDtypeStruct((b, SUBLANES, c), F32)],
        scratch_shapes=[pltpu.VMEM((SUBLANES, c), F32), pltpu.VMEM((SUBLANES, c), F32)],
        compiler_params=_params(("arbitrary", "arbitrary")),
    )(x, oa, xb, gates, *consts)


def _ffn_body(x1, gpre, gpost, up_ref, cw_ref, cb_ref, down_ref, hist, keep):
    dff = down_ref.shape[0]
    hn = _rms(x1, gpre).astype(BF16)
    starts = list(range(0, dff, FF_COLS))

    def up(c0):
        return [(slice(off, off + FF_COLS), jnp.dot(hn, up_ref[:, off:off + FF_COLS], preferred_element_type=F32))
                for off in (c0, dff + c0)]

    ups = [up(c0) for c0 in starts[:FFN_AHEAD]]
    f = jnp.zeros(x1.shape, F32)
    for i, c0 in enumerate(starts):
        if i + FFN_AHEAD < len(starts):
            ups.append(up(starts[i + FFN_AHEAD]))
        halves = []
        for cols, u in ups[i]:
            u1, u2 = hist(u, cols)
            halves.append(cb_ref[:, cols] + cw_ref[2:3, cols] * u + cw_ref[1:2, cols] * u1 + cw_ref[0:1, cols] * u2)
            keep(u, cols)
        act = _gelu_tanh(halves[0]) * halves[1]
        f = f + jnp.dot(act.astype(BF16), down_ref[c0:c0 + FF_COLS, :], preferred_element_type=F32)
    return x1 + _rms(f, gpost)


def _ffn_prompt_kernel(x_ref, gpre_ref, gpost_ref, up_ref, cw_ref, cb_ref, down_ref, y_ref, tail_ref, tail_scr):
    @pl.when(pl.program_id(1) == 0)
    def _init():
        tail_scr[...] = jnp.zeros_like(tail_scr)

    rows = x_ref.shape[0]

    def hist(u, cols):
        tail = tail_scr[:, cols]
        return _shift_rows(u, tail, 1), _shift_rows(u, tail, 2)

    def keep(u, cols):
        tail_scr[:, cols] = u[rows - SUBLANES:rows, :]
        tail_ref[0, :, cols] = u[rows - SUBLANES:rows, :]

    y_ref[...] = _ffn_body(x_ref[...], gpre_ref[...], gpost_ref[...], up_ref, cw_ref, cb_ref, down_ref, hist, keep)


def _ffn_prompt(x1, fw, b, t, tm):
    n, d = x1.shape
    nt = t // tm
    dff2 = fw["up"].shape[1]
    names = ("g_pre", "g_post", "up", "conv_w", "conv_b", "down")
    rows = pl.BlockSpec((tm, d), lambda i, j: (i * nt + j, 0))
    return pl.pallas_call(
        _ffn_prompt_kernel,
        grid=(b, nt),
        in_specs=[rows] + [_const(fw[k].shape) for k in names],
        out_specs=[rows, pl.BlockSpec((1, SUBLANES, dff2), lambda i, j: (i, 0, 0))],
        out_shape=[jax.ShapeDtypeStruct((n, d), F32), jax.ShapeDtypeStruct((b, SUBLANES, dff2), F32)],
        scratch_shapes=[pltpu.VMEM((SUBLANES, dff2), F32)],
        compiler_params=_params(("arbitrary", "arbitrary")),
    )(x1, *[fw[k] for k in names])


def _ffn_sample_kernel(x_ref, buf_ref, gpre_ref, gpost_ref, up_ref, cw_ref, cb_ref, down_ref, y_ref, nb_ref):
    def hist(u, cols):
        return buf_ref[:, 1, cols], buf_ref[:, 0, cols]

    def keep(u, cols):
        nb_ref[:, 0, cols] = buf_ref[:, 1, cols]
        nb_ref[:, 1, cols] = u

    y_ref[...] = _ffn_body(x_ref[...], gpre_ref[...], gpost_ref[...], up_ref, cw_ref, cb_ref, down_ref, hist, keep)


def _ffn_sample(x1, buf, fw):
    names = ("g_pre", "g_post", "up", "conv_w", "conv_b", "down")
    args = (x1, buf, *[fw[k] for k in names])
    return pl.pallas_call(
        _ffn_sample_kernel,
        grid=(1,),
        in_specs=[_const(v.shape) for v in args],
        out_specs=[_const(x1.shape), _const(buf.shape)],
        out_shape=[jax.ShapeDtypeStruct(x1.shape, F32), jax.ShapeDtypeStruct(buf.shape, F32)],
        compiler_params=_params(("arbitrary",)),
    )(*args)


def _block_diag(blocks):
    n, bi, bj = blocks.shape
    eye = jnp.eye(n, dtype=blocks.dtype)
    return jnp.einsum("nij,nm->nimj", blocks, eye).reshape(n * bi, n * bj)


def _gate_blocks(gx, ga):
    n = min(gx.shape[0], MXU_TILE)
    return jnp.stack([jnp.concatenate([gx[i:i + n, i:i + n], ga[i:i + n, i:i + n]], axis=1)
                      for i in range(0, gx.shape[0], n)], axis=0)


def _layer_weights(norm_pre_mix, norm_post_mix, norm_pre_ffn, norm_post_ffn, w_in,
                   rwkv_mu, rwkv_w0, rwkv_w_up, rwkv_a0, rwkv_a_up, rwkv_g_up, rwkv_k_k, rwkv_k_a,
                   rwkv_r_k, rwkv_lnx_w, rwkv_lnx_b, rwkv_w_out,
                   lru_conv_w, lru_conv_b, lru_gx_w, lru_gx_b, lru_ga_w, lru_ga_b, lru_lambda, lru_w_out,
                   w_o, ffn_up, ffn_conv_w, ffn_conv_b, ffn_down):
    row = lambda x: x.reshape(1, -1)
    w = rwkv_w0.shape[-1]
    head_of = jnp.arange(min(w, MXU_TILE)) // HEAD_DIM
    t_idx = jnp.arange(CHUNK)
    lo_w, lo_a = rwkv_w_up.shape[0], rwkv_a_up.shape[0]
    lora2 = jnp.zeros((lo_w + lo_a, 2 * w), F32).at[:lo_w, :w].set(rwkv_w_up).at[lo_w:, w:].set(rwkv_a_up)
    rw = dict(mu=row(rwkv_mu), w0=row(rwkv_w0), lora2=lora2.astype(BF16), a0=row(rwkv_a0),
              g_up=rwkv_g_up.astype(BF16), k_k=row(rwkv_k_k), k_a=row(rwkv_k_a), r_k=row(rwkv_r_k),
              lnx_w=row(rwkv_lnx_w), lnx_b=row(rwkv_lnx_b), w_out=rwkv_w_out.astype(BF16),
              seg=(head_of[:, None] == head_of[None, :]).astype(BF16),
              tri=(t_idx[:, None] >= t_idx[None, :]).astype(BF16))
    lw = dict(conv_w=lru_conv_w, conv_b=row(lru_conv_b),
              gate_w=_gate_blocks(_block_diag(lru_gx_w), _block_diag(lru_ga_w)).astype(BF16),
              gate_b=jnp.concatenate([row(lru_gx_b), row(lru_ga_b)], axis=1), lam=row(lru_lambda))
    fw = dict(g_pre=row(norm_pre_ffn), g_post=row(norm_post_ffn), up=ffn_up.astype(BF16),
              conv_w=ffn_conv_w, conv_b=row(ffn_conv_b), down=ffn_down.astype(BF16))
    return dict(g_in=row(norm_pre_mix), w_in=w_in.astype(BF16), rw=rw, lw=lw, fw=fw,
                lru_w_out=lru_w_out.astype(BF16), w_o=w_o.astype(BF16), g_mix=row(norm_post_mix))


def _splits(wts):
    w = wts["rw"]["w0"].shape[-1]
    c_rwkv = wts["rw"]["mu"].shape[-1]
    c_lru = wts["lw"]["lam"].shape[-1]
    return (c_rwkv, c_lru, wts["w_in"].shape[1] - c_rwkv - c_lru)


def _prompt_layer(x, wts):
    b, t, d = x.shape
    x2 = x.reshape(b * t, d)
    tm = min(t, ROW_TILE)
    m, xb, gates, p_tail = _proj_prompt(x2, wts["g_in"], wts["w_in"], wts["rw"]["mu"], _splits(wts), b, t, tm)
    oa, wkv = _rwkv_prompt(m, wts["rw"], b, t, min(t, WKV_ROWS))
    x1, xb_tail, h_tail = _mix_prompt(x2, oa, xb, gates, wts["lw"], wts["lru_w_out"], wts["w_o"], wts["g_mix"],
                                      b, t, tm)
    y, u_tail = _ffn_prompt(x1, wts["fw"], b, t, min(t, FFN_ROWS))
    conv_w = wts["lw"]["conv_w"].shape[0]
    ffn_w = wts["fw"]["conv_w"].shape[0]
    state = (p_tail[:, SUBLANES - 1:], wkv, xb_tail[:, SUBLANES - (conv_w - 1):], h_tail[:, SUBLANES - 1],
             u_tail[:, SUBLANES - (ffn_w - 1):])
    return y.reshape(b, t, d), state


def _sample_layer(x, shift, wkv, lru_buf, h0, ffn_buf, wts):
    n, t, d = x.shape
    x2 = x.reshape(n, d)
    p, xb, gates = _proj(x2, wts["g_in"], wts["w_in"], _splits(wts), n)
    oa, wkv_t = _rwkv_sample(p, shift[:, 0, :], jnp.transpose(wkv, (1, 2, 3, 0)), wts["rw"])
    lru_bufs = [lru_buf[:, j, :] for j in range(lru_buf.shape[1])]
    h = _lru_sample(xb, lru_bufs, h0, wts["lw"])
    x1 = _mix_sample(x2, oa, h, gates, wts["lru_w_out"], wts["w_o"], wts["g_mix"])
    y, ffn_new = _ffn_sample(x1, ffn_buf, wts["fw"])
    state = (p.reshape(n, 1, -1), jnp.transpose(wkv_t, (3, 0, 1, 2)), jnp.stack(lru_bufs[1:] + [xb], axis=1), h,
             ffn_new)
    return y.reshape(n, t, d), state


def kernel(x_prompt, x_sample, state_rwkv_shift, state_rwkv_wkv, state_lru_conv, state_lru_h, state_ffn_conv,
           norm_pre_mix, norm_post_mix, norm_pre_ffn, norm_post_ffn, w_in,
           rwkv_mu, rwkv_w0, rwkv_w_up, rwkv_a0, rwkv_a_up, rwkv_g_up, rwkv_k_k, rwkv_k_a,
           rwkv_r_k, rwkv_lnx_w, rwkv_lnx_b, rwkv_w_out,
           lru_conv_w, lru_conv_b, lru_gx_w, lru_gx_b, lru_ga_w, lru_ga_b, lru_lambda, lru_w_out,
           w_o, ffn_up, ffn_conv_w, ffn_conv_b, ffn_down):
    params = (norm_pre_mix, norm_post_mix, norm_pre_ffn, norm_post_ffn, w_in,
              rwkv_mu, rwkv_w0, rwkv_w_up, rwkv_a0, rwkv_a_up, rwkv_g_up, rwkv_k_k, rwkv_k_a,
              rwkv_r_k, rwkv_lnx_w, rwkv_lnx_b, rwkv_w_out,
              lru_conv_w, lru_conv_b, lru_gx_w, lru_gx_b, lru_ga_w, lru_ga_b, lru_lambda, lru_w_out,
              w_o, ffn_up, ffn_conv_w, ffn_conv_b, ffn_down)
    depth = w_in.shape[0]
    assert depth == 1 and x_sample.shape[1] == 1
    yp, ys = x_prompt, x_sample
    new_p, new_s = [], []
    for l in range(depth):
        wts = _layer_weights(*[q[l] for q in params])
        yp, st_p = _prompt_layer(yp, wts)
        ys, st_s = _sample_layer(ys, state_rwkv_shift[l], state_rwkv_wkv[l], state_lru_conv[l],
                                 state_lru_h[l], state_ffn_conv[l], wts)
        new_p.append(st_p)
        new_s.append(st_s)
    stk = lambda lst, i: jnp.stack([s[i] for s in lst], axis=0)
    return (yp, ys,
            stk(new_p, 0), stk(new_p, 1), stk(new_p, 2), stk(new_p, 3), stk(new_p, 4),
            stk(new_s, 0), stk(new_s, 1), stk(new_s, 2), stk(new_s, 3), stk(new_s, 4))
```

```python
import functools
import math

import jax
import jax.numpy as jnp
from jax import lax
from jax.experimental import pallas as pl
from jax.experimental.pallas import tpu as pltpu

F32 = jnp.float32
BF16 = jnp.bfloat16

NORM_EPS = 1e-6
LN_X_EPS = 64e-5
LRU_C = 8.0
DECAY_SCALE = math.exp(-0.5)
HEAD_DIM = 64
LANES = 128
SUBLANES = 8
MXU_TILE = 256
CHUNK = 64
ROW_TILE = 512
WKV_ROWS = 512
GATE_COLS = 512
SHIFT_COLS = 256
FF_COLS = 256
FFN_ROWS = 512
FFN_PARTS = 2
FFN_PART_LEAD = 4
FFN_AHEAD = 3
VMEM_LIMIT = 56 * 1024 * 1024

_NN = (((1,), (0,)), ((), ()))
_NT = (((1,), (1,)), ((), ()))
_TN = (((0,), (0,)), ((), ()))


def _rms(x, g):
    return x * lax.rsqrt(jnp.mean(x * x, axis=-1, keepdims=True) + NORM_EPS) * g


def _softplus(x):
    return jnp.maximum(x, 0.0) + jnp.log1p(jnp.exp(-jnp.abs(x)))


def _sigmoid(x):
    return 0.5 * jnp.tanh(0.5 * x) + 0.5


def _gelu_tanh(x):
    c = 0.7978845608028654
    hx = 0.5 * x
    return hx + hx * jnp.tanh(x * (c + (c * 0.044715) * (x * x)))


def _mm(a, b):
    return jnp.dot(a.astype(BF16), b.astype(BF16), preferred_element_type=F32)


def _mmd(a, b, dims=_NN):
    return lax.dot_general(a.astype(BF16), b.astype(BF16), dims, preferred_element_type=F32)


def _segsum(x, seg):
    x16 = x.astype(BF16)
    n = seg.shape[0]
    return jnp.concatenate([jnp.dot(x16[:, c:c + n], seg, preferred_element_type=F32)
                            for c in range(0, x.shape[1], n)], axis=1)


def _cumsum_rows(tri, x):
    hi = x.astype(BF16)
    lo = (x - hi.astype(F32)).astype(BF16)
    n = tri.shape[0]
    d = lambda y: jnp.concatenate([jnp.dot(tri, y[r:r + n], preferred_element_type=F32)
                                   for r in range(0, y.shape[0], n)], axis=0)
    return d(hi) + d(lo)


def _shift_rows(x, tail, j):
    xs = pltpu.roll(x, j, axis=0)
    ts = pltpu.roll(tail, j, axis=0)
    row = lax.broadcasted_iota(jnp.int32, ts.shape, 0)
    head = jnp.where(row < j, ts, xs[0:SUBLANES])
    if x.shape[0] == SUBLANES:
        return head
    return jnp.concatenate([head, xs[SUBLANES:]], axis=0)


def _const(shape):
    n = len(shape)
    return pl.BlockSpec(shape, lambda *_: (0,) * n, pipeline_mode=pl.Buffered(1))


def _params(sem):
    return pltpu.CompilerParams(dimension_semantics=sem, vmem_limit_bytes=VMEM_LIMIT)


def _proj_kernel(x_ref, g_ref, w_ref, p_ref, xb_ref, gt_ref, *, splits):
    xn = _rms(x_ref[...], g_ref[...]).astype(BF16)
    c0, c1, c2 = splits
    p_ref[...] = jnp.dot(xn, w_ref[:, 0:c0], preferred_element_type=F32)
    xb_ref[...] = jnp.dot(xn, w_ref[:, c0:c0 + c1], preferred_element_type=F32)
    gt_ref[...] = jnp.dot(xn, w_ref[:, c0 + c1:c0 + c1 + c2], preferred_element_type=F32)


def _proj_prompt_kernel(x_ref, g_ref, w_ref, mu_ref, m_ref, xb_ref, gt_ref, tail_ref, prev_scr, *, splits):
    @pl.when(pl.program_id(1) == 0)
    def _init():
        prev_scr[...] = jnp.zeros_like(prev_scr)

    xn = _rms(x_ref[...], g_ref[...]).astype(BF16)
    c0, c1, c2 = splits
    g0 = c0 + c1
    p = jnp.dot(xn, w_ref[:, 0:c0], preferred_element_type=F32)
    xb_ref[...] = jnp.dot(xn, w_ref[:, c0:g0], preferred_element_type=F32)
    rows = p.shape[0]
    row = lax.broadcasted_iota(jnp.int32, (rows, SHIFT_COLS), 0)
    shift_chunks = list(range(0, c0, SHIFT_COLS))
    gate_chunks = list(range(0, c2, GATE_COLS))
    per_gate = -(-len(shift_chunks) // len(gate_chunks))
    for i, off in enumerate(gate_chunks):
        gt_ref[:, off:off + GATE_COLS] = jnp.dot(xn, w_ref[:, g0 + off:g0 + off + GATE_COLS],
                                                 preferred_element_type=F32)
        for s0 in shift_chunks[i * per_gate:(i + 1) * per_gate]:
            cs = slice(s0, s0 + SHIFT_COLS)
            pc = p[:, cs]
            prev = jnp.where(row == 0, prev_scr[SUBLANES - 1:SUBLANES, cs], pltpu.roll(pc, 1, axis=0))
            m_ref[:, cs] = pc + mu_ref[:, cs] * (prev - pc)
    prev_scr[...] = p[rows - SUBLANES:rows, :]
    tail_ref[0] = p[rows - SUBLANES:rows, :]


def _proj_prompt(x, g, w_in16, mu, splits, b, t, tm):
    n, d = x.shape
    c0, c1, c2 = splits
    nt = t // tm
    rows = lambda c: pl.BlockSpec((tm, c), lambda i, j: (i * nt + j, 0))
    return pl.pallas_call(
        functools.partial(_proj_prompt_kernel, splits=splits),
        grid=(b, nt),
        in_specs=[rows(d), _const((1, d)), _const(w_in16.shape), _const(mu.shape)],
        out_specs=[rows(c0), rows(c1), rows(c2), pl.BlockSpec((1, SUBLANES, c0), lambda i, j: (i, 0, 0))],
        out_shape=[jax.ShapeDtypeStruct((n, c0), F32), jax.ShapeDtypeStruct((n, c1), F32),
                   jax.ShapeDtypeStruct((n, c2), F32), jax.ShapeDtypeStruct((b, SUBLANES, c0), F32)],
        scratch_shapes=[pltpu.VMEM((SUBLANES, c0), F32)],
        compiler_params=_params(("arbitrary", "arbitrary")),
    )(x, g, w_in16, mu)


def _proj(x, g, w_in16, splits, tm):
    n, d = x.shape
    c0, c1, c2 = splits
    return pl.pallas_call(
        functools.partial(_proj_kernel, splits=splits),
        grid=(n // tm,),
        in_specs=[pl.BlockSpec((tm, d), lambda i: (i, 0)), _const((1, d)), _const(w_in16.shape)],
        out_specs=[pl.BlockSpec((tm, c0), lambda i: (i, 0)),
                   pl.BlockSpec((tm, c1), lambda i: (i, 0)),
                   pl.BlockSpec((tm, c2), lambda i: (i, 0))],
        out_shape=[jax.ShapeDtypeStruct((n, c0), F32), jax.ShapeDtypeStruct((n, c1), F32),
                   jax.ShapeDtypeStruct((n, c2), F32)],
        compiler_params=_params(("arbitrary",)),
    )(x, g, w_in16)


def _rwkv_pre(m, w0, lora2, a0, g_up, k_k, k_a, seg):
    w = w0.shape[-1]
    r, k, v = m[:, 0:w], m[:, w:2 * w], m[:, 2 * w:3 * w]
    z = m[:, 3 * w:3 * w + LANES]
    lane = lax.broadcasted_iota(jnp.int32, z.shape, 1)
    lor = _mm(jnp.where(lane < HEAD_DIM, jnp.tanh(z), z), lora2)
    lw = -DECAY_SCALE * _sigmoid(w0 + lor[:, 0:w])
    a_sig = _sigmoid(a0 + lor[:, w:2 * w])
    g = _mm(_sigmoid(m[:, 3 * w + LANES:3 * w + 2 * LANES]), g_up)
    kk = k * k_k
    kkn = kk * lax.rsqrt(jnp.maximum(_segsum(kk * kk, seg), 1e-24))
    k2 = k * (1.0 + (a_sig - 1.0) * k_a)
    return r, lw, k2, v, -kkn, kkn * a_sig, g


def _rwkv_post(y, r, k2, v, g, r_k, lnx_w, lnx_b, w_out, seg):
    inv = 1.0 / HEAD_DIM
    mean = _segsum(y, seg) * inv
    d = y - mean
    var = _segsum(d * d, seg) * inv
    yn = d * lax.rsqrt(var + LN_X_EPS) * lnx_w + lnx_b
    bonus = _segsum(r * k2 * r_k, seg) * v
    return _mm((yn + bonus) * g, w_out)


def _wkv_chunk_summaries(xs, m0b, strict, incl, eye2, eyew):
    ln = xs[0][0].shape[0]
    swap = lambda x: pltpu.roll(x, HEAD_DIM, axis=1)
    h16 = lambda x: x.astype(BF16)
    zero16 = jnp.zeros((), BF16)
    bd = lambda x: jnp.concatenate([jnp.where(m0b, x, zero16), jnp.where(m0b, zero16, x)], axis=0)
    cat = jnp.concatenate
    rts, ats, bts, kts, bhs, khs, vs, wls = zip(*xs)
    at16 = [h16(x) for x in ats]
    v16 = [h16(x) for x in vs]
    bdv = [bd(x) for x in v16]
    gms = [_mmd(cat([bd(a), bd(h16(rt))], axis=0), cat([bt, kt], axis=0), _NT)
           for rt, a, bt, kt in zip(rts, at16, bts, kts)]
    ga0 = [jnp.where(strict, g[0:ln], 0.0) for g in gms]
    ga1 = [jnp.where(strict, g[ln:2 * ln], 0.0) for g in gms]
    gr0 = [jnp.where(incl, g[2 * ln:3 * ln], 0.0) for g in gms]
    gr1 = [jnp.where(incl, g[3 * ln:4 * ln], 0.0) for g in gms]
    n_ab = [jnp.where(m0b, x, swap(y)) for x, y in zip(ga0, ga1)]
    n_ak = [jnp.where(m0b, swap(x), y) for x, y in zip(ga0, ga1)]
    n_rb = [jnp.where(m0b, x, swap(y)) for x, y in zip(gr0, gr1)]
    n_rk = [jnp.where(m0b, swap(x), y) for x, y in zip(gr0, gr1)]
    tinv = [eye2 + n for n in n_ab]
    steps = (ln - 1).bit_length() - 1
    if steps > 0:
        npow = [h16(_mmd(n, bd(n))) for n in (h16(x) for x in n_ab)]
        for _ in range(steps - 1):
            both = [_mmd(cat([n, h16(t)], axis=0), bd(n)) for n, t in zip(npow, tinv)]
            tinv = [t + x[ln:2 * ln] for t, x in zip(tinv, both)]
            npow = [h16(x[0:ln]) for x in both]
        tinv = [t + _mmd(t, bd(n)) for t, n in zip(tinv, npow)]
    kv = [_mmd(cat([nk, nr], axis=0), b) for nk, nr, b in zip(n_ak, n_rk, bdv)]
    akv = [x[0:ln] for x in kv]
    rkv = [x[ln:2 * ln] for x in kv]
    tt = [_mmd(t, cat([bd(a), bd(h16(ak))], axis=1)) for t, a, ak in zip(tinv, at16, akv)]
    ta = [h16(x[:, 0:LANES]) for x in tt]
    tk = [h16(x[:, LANES:2 * LANES]) for x in tt]
    mc = [_mmd(cat([cat([a, k], axis=1), cat([jnp.zeros_like(v), v], axis=1)], axis=0), cat([bh, kh], axis=0), _TN)
          for a, k, v, bh, kh in zip(ta, tk, v16, bhs, khs)]
    ms = [eyew * wl + x[0:LANES] for x, wl in zip(mc, wls)]
    cs = [x[LANES:2 * LANES] for x in mc]
    qy = [_mmd(n, cat([bd(a), bd(k)], axis=1)) for n, a, k in zip(n_rb, ta, tk)]
    qs = [rt + x[:, 0:LANES] for rt, x in zip(rts, qy)]
    y0 = [x[:, LANES:2 * LANES] + r for x, r in zip(qy, rkv)]
    return list(zip(qs, y0, ms, cs))


def _rwkv_prompt_kernel(m_ref, w0_ref, lora_ref, a0_ref, gup_ref, kk_ref, ka_ref, rk_ref,
                        lnw_ref, lnb_ref, wout_ref, seg_ref, tri_ref, oa_ref, wkv_ref, s_scr):
    t = pl.program_id(1)

    @pl.when(t == 0)
    def _init():
        s_scr[...] = jnp.zeros_like(s_scr)

    rows = m_ref.shape[0]
    seg = seg_ref[...]
    r, lw, k2, v, aa, bb, g = _rwkv_pre(m_ref[...], w0_ref[...], lora_ref[...], a0_ref[...],
                                        gup_ref[...], kk_ref[...], ka_ref[...], seg)
    c = _cumsum_rows(tri_ref[...], lw)

    ln = CHUNK
    ri = lax.broadcasted_iota(jnp.int32, (ln, LANES), 0)
    ci = lax.broadcasted_iota(jnp.int32, (ln, LANES), 1)
    cj = jnp.where(ci < HEAD_DIM, ci, ci - HEAD_DIM)
    m0b = ci < HEAD_DIM
    strict = cj < ri
    incl = cj <= ri
    eye2 = (cj == ri).astype(F32)
    r2 = lax.broadcasted_iota(jnp.int32, (LANES, LANES), 0)
    c2 = lax.broadcasted_iota(jnp.int32, (LANES, LANES), 1)
    bdmask = ((r2 < HEAD_DIM) == (c2 < HEAD_DIM)).astype(F32)
    eyew = (r2 == c2).astype(F32)

    npairs = s_scr.shape[0]
    insts = []
    for ch in range(rows // ln):
        rs = slice(ch * ln, (ch + 1) * ln)
        c_, lw_ = c[rs], lw[rs]
        cl = c_[ln - 1:ln, :]
        e_n = jnp.exp(-c_)
        e_l = jnp.exp(cl - c_)
        rt = r[rs] * jnp.exp(c_)
        at = aa[rs] * jnp.exp(c_ - lw_)
        bt, kt = bb[rs] * e_n, k2[rs] * e_n
        bh, kh = bb[rs] * e_l, k2[rs] * e_l
        wl = jnp.exp(cl)
        v_ = v[rs]
        for pr in range(npairs):
            sl = slice(pr * LANES, (pr + 1) * LANES)
            insts.append((rt[:, sl], at[:, sl], bt[:, sl], kt[:, sl], bh[:, sl], kh[:, sl], v_[:, sl], wl[:, sl]))
    summaries = _wkv_chunk_summaries(insts, m0b, strict, incl, eye2, eyew)

    states = [s_scr[pr] for pr in range(npairs)]
    y_rows = []
    for ch in range(rows // ln):
        ys = []
        for pr in range(npairs):
            q, y0, m, cc = summaries[ch * npairs + pr]
            ys.append(_mmd(q, states[pr], _NT) + y0)
            states[pr] = (_mmd(states[pr], m) + cc) * bdmask
        y_rows.append(jnp.concatenate(ys, axis=1))
    for pr in range(npairs):
        s_scr[pr] = states[pr]
    y = jnp.concatenate(y_rows, axis=0)
    oa_ref[...] = _rwkv_post(y, r, k2, v, g, rk_ref[...], lnw_ref[...], lnb_ref[...], wout_ref[...], seg)

    @pl.when(t == pl.num_programs(1) - 1)
    def _fin():
        for pr in range(npairs):
            s = s_scr[pr]
            wkv_ref[0, 2 * pr] = s[0:HEAD_DIM, 0:HEAD_DIM]
            wkv_ref[0, 2 * pr + 1] = pltpu.roll(s[HEAD_DIM:LANES, :], HEAD_DIM, axis=1)[:, 0:HEAD_DIM]


def _rwkv_prompt(m, rw, b, t, tc):
    n, cp = m.shape
    w = rw["w0"].shape[-1]
    heads = w // HEAD_DIM
    nt = t // tc
    dm = rw["w_out"].shape[-1]
    tri = jnp.kron(jnp.eye(min(tc, MXU_TILE) // CHUNK, dtype=BF16), rw["tri"])
    names = ("w0", "lora2", "a0", "g_up", "k_k", "k_a", "r_k", "lnx_w", "lnx_b", "w_out", "seg")
    return pl.pallas_call(
        _rwkv_prompt_kernel,
        grid=(b, nt),
        in_specs=[pl.BlockSpec((tc, cp), lambda i, j: (i * nt + j, 0))] + [_const(rw[k].shape) for k in names]
                 + [_const(tri.shape)],
        out_specs=[pl.BlockSpec((tc, dm), lambda i, j: (i * nt + j, 0)),
                   pl.BlockSpec((1, heads, HEAD_DIM, HEAD_DIM), lambda i, j: (i, 0, 0, 0))],
        out_shape=[jax.ShapeDtypeStruct((n, dm), F32),
                   jax.ShapeDtypeStruct((b, heads, HEAD_DIM, HEAD_DIM), F32)],
        scratch_shapes=[pltpu.VMEM((heads // 2, LANES, LANES), F32)],
        compiler_params=_params(("arbitrary", "arbitrary")),
    )(m, *[rw[k] for k in names], tri)


def _rwkv_sample_pre_kernel(p_ref, prev_ref, mu_ref, w0_ref, lora_ref, a0_ref, gup_ref, kk_ref, ka_ref, seg_ref,
                            rt_ref, wt_ref, kt_ref, vt_ref, at_ref, bt_ref, r_ref, k_ref, v_ref, g_ref):
    p = p_ref[...]
    r, lw, k2, v, aa, bb, g = _rwkv_pre(p + mu_ref[...] * (prev_ref[...] - p), w0_ref[...], lora_ref[...],
                                        a0_ref[...], gup_ref[...], kk_ref[...], ka_ref[...], seg_ref[...])
    rt_ref[...] = r.T
    wt_ref[...] = jnp.exp(lw).T
    kt_ref[...] = k2.T
    vt_ref[...] = v.T
    at_ref[...] = aa.T
    bt_ref[...] = bb.T
    r_ref[...] = r
    k_ref[...] = k2
    v_ref[...] = v
    g_ref[...] = g


def _rwkv_sample_step_kernel(s_ref, r_ref, w_ref, k_ref, v_ref, a_ref, b_ref, y_ref, so_ref):
    r, w, k, a, b = r_ref[...], w_ref[...], k_ref[...], a_ref[...], b_ref[...]

    def body(i, carry):
        base = pl.multiple_of(i * SUBLANES, SUBLANES)
        vrows = v_ref[pl.ds(base, SUBLANES), :]
        ys = []
        for j in range(SUBLANES):
            s = s_ref[0, base + j]
            sa = jnp.sum(s * a, axis=0, keepdims=True)
            s_new = s * w + sa * b + vrows[j:j + 1, :] * k
            so_ref[0, base + j] = s_new
            ys.append(jnp.sum(s_new * r, axis=0, keepdims=True))
        y_ref[pl.ds(base, SUBLANES), :] = jnp.concatenate(ys, axis=0)
        return carry

    lax.fori_loop(0, s_ref.shape[1] // SUBLANES, body, 0)


def _rwkv_sample_post_kernel(yt_ref, r_ref, k_ref, v_ref, g_ref, rk_ref, lnw_ref, lnb_ref, wout_ref, seg_ref, oa_ref):
    oa_ref[...] = _rwkv_post(yt_ref[...].T, r_ref[...], k_ref[...], v_ref[...], g_ref[...], rk_ref[...],
                             lnw_ref[...], lnb_ref[...], wout_ref[...], seg_ref[...])


def _rwkv_sample(p, shift, wkv, rw):
    n, cp = p.shape
    w = rw["w0"].shape[-1]
    heads = w // HEAD_DIM
    dm = rw["w_out"].shape[-1]
    pre_names = ("mu", "w0", "lora2", "a0", "g_up", "k_k", "k_a", "seg")
    vec = jax.ShapeDtypeStruct((n, w), F32)
    vec_t = jax.ShapeDtypeStruct((w, n), F32)
    rt, wt, kt, vt, at, bt, r, k2, v, g = pl.pallas_call(
        _rwkv_sample_pre_kernel,
        grid=(1,),
        in_specs=[_const((n, cp)), _const((n, cp))] + [_const(rw[k].shape) for k in pre_names],
        out_specs=[_const((w, n))] * 6 + [_const((n, w))] * 4,
        out_shape=[vec_t] * 6 + [vec] * 4,
        compiler_params=_params(("arbitrary",)),
    )(p, shift, *[rw[k] for k in pre_names])

    head_spec = pl.BlockSpec((HEAD_DIM, n), lambda h: (h, 0))
    st_spec = pl.BlockSpec((1, HEAD_DIM, HEAD_DIM, n), lambda h: (h, 0, 0, 0))
    yt, wkv_new = pl.pallas_call(
        _rwkv_sample_step_kernel,
        grid=(heads,),
        in_specs=[st_spec] + [head_spec] * 6,
        out_specs=[head_spec, st_spec],
        out_shape=[vec_t, jax.ShapeDtypeStruct(wkv.shape, F32)],
        compiler_params=_params(("arbitrary",)),
    )(wkv, rt, wt, kt, vt, at, bt)

    post_names = ("r_k", "lnx_w", "lnx_b", "w_out", "seg")
    oa = pl.pallas_call(
        _rwkv_sample_post_kernel,
        grid=(1,),
        in_specs=[_const((w, n))] + [_const((n, w))] * 4 + [_const(rw[k].shape) for k in post_names],
        out_specs=_const((n, dm)),
        out_shape=jax.ShapeDtypeStruct((n, dm), F32),
        compiler_params=_params(("arbitrary",)),
    )(yt, r, k2, v, g, *[rw[k] for k in post_names])
    return oa, wkv_new


def _lru_gates(xc, gw, gb, lam):
    c = xc.shape[-1]
    n = gw.shape[1]
    xc16 = xc.astype(BF16)
    parts = [jnp.dot(xc16[:, i * n:(i + 1) * n], gw[i], preferred_element_type=F32) for i in range(c // n)]
    gx = _sigmoid(jnp.concatenate([p[:, 0:n] for p in parts], axis=1) + gb[:, 0:c])
    ga = _sigmoid(jnp.concatenate([p[:, n:2 * n] for p in parts], axis=1) + gb[:, c:2 * c])
    log_a = -LRU_C * ga * _softplus(-lam)
    a = jnp.exp(log_a)
    u = jnp.sqrt(1.0 - a * a) * (gx * xc)
    return a, u


def _scan_rows(a, u, h0):
    rows, c = a.shape
    groups = rows // SUBLANES
    a3 = a.reshape(groups, SUBLANES, c)
    u3 = u.reshape(groups, SUBLANES, c)
    sub = lax.broadcasted_iota(jnp.int32, a3.shape, 1)
    k = 1
    while k < SUBLANES:
        keep = sub >= k
        u3 = u3 + a3 * jnp.where(keep, pltpu.roll(u3, k, axis=1), 0.0)
        a3 = a3 * jnp.where(keep, pltpu.roll(a3, k, axis=1), 1.0)
        k *= 2
    h = h0
    out = []
    for i in range(groups):
        hi = u3[i] + a3[i] * h
        out.append(hi)
        h = hi[SUBLANES - 1:SUBLANES, :]
    return jnp.concatenate(out, axis=0)


def _lru_prompt(x, tail, h0, cw_ref, cb_ref, gw, gb, lam):
    width = cw_ref.shape[0]
    xc = cb_ref[...] + cw_ref[width - 1:width, :] * x
    for j in range(1, width):
        xc = xc + cw_ref[width - 1 - j:width - j, :] * _shift_rows(x, tail, j)
    a, u = _lru_gates(xc, gw, gb, lam)
    return _scan_rows(a, u, h0)


def _lru_sample_kernel(x_ref, *refs):
    *buf_refs, h0_ref, cw_ref, cb_ref, gw_ref, gb_ref, lam_ref, h_ref = refs
    width = cw_ref.shape[0]
    xc = cb_ref[...] + cw_ref[width - 1:width, :] * x_ref[...]
    for j, buf_ref in enumerate(buf_refs):
        xc = xc + cw_ref[j:j + 1, :] * buf_ref[...]
    a, u = _lru_gates(xc, gw_ref[...], gb_ref[...], lam_ref[...])
    h_ref[...] = a * h0_ref[...] + u


def _lru_sample(xb, bufs, h0, lw):
    names = ("conv_w", "conv_b", "gate_w", "gate_b", "lam")
    args = (xb, *bufs, h0, *[lw[k] for k in names])
    return pl.pallas_call(
        _lru_sample_kernel,
        grid=(1,),
        in_specs=[_const(v.shape) for v in args],
        out_specs=_const(xb.shape),
        out_shape=jax.ShapeDtypeStruct(xb.shape, F32),
        compiler_params=_params(("arbitrary",)),
    )(*args)


def _mix_rows(x, oa, hs, gates, lwo, wo, g):
    d = x.shape[-1]
    ob = _mm(hs, lwo)
    mix = _mm(_sigmoid(gates[:, 0:d]) * oa + _sigmoid(gates[:, d:2 * d]) * ob, wo)
    return x + _rms(mix, g)


def _mix_sample_kernel(x_ref, oa_ref, hs_ref, gt_ref, lwo_ref, wo_ref, g_ref, o_ref):
    o_ref[...] = _mix_rows(x_ref[...], oa_ref[...], hs_ref[...], gt_ref[...], lwo_ref[...], wo_ref[...], g_ref[...])


def _mix_sample(x, oa, hs, gates, lwo16, wo16, g):
    args = (x, oa, hs, gates, lwo16, wo16, g)
    return pl.pallas_call(
        _mix_sample_kernel,
        grid=(1,),
        in_specs=[_const(v.shape) for v in args],
        out_specs=_const(x.shape),
        out_shape=jax.ShapeDtypeStruct(x.shape, F32),
        compiler_params=_params(("arbitrary",)),
    )(*args)


def _mix_prompt_kernel(x_ref, oa_ref, xb_ref, gt_ref, cw_ref, cb_ref, gw_ref, gb_ref, lam_ref, lwo_ref, wo_ref, g_ref,
                       o_ref, xtail_ref, htail_ref, xtail_scr, h_scr):
    @pl.when(pl.program_id(1) == 0)
    def _init():
        xtail_scr[...] = jnp.zeros_like(xtail_scr)
        h_scr[...] = jnp.zeros_like(h_scr)

    xb = xb_ref[...]
    rows = xb.shape[0]
    hs = _lru_prompt(xb, xtail_scr[...], h_scr[SUBLANES - 1:SUBLANES, :], cw_ref, cb_ref,
                     gw_ref[...], gb_ref[...], lam_ref[...])
    xtail_scr[...] = xb[rows - SUBLANES:rows, :]
    xtail_ref[0] = xb[rows - SUBLANES:rows, :]
    h_scr[...] = hs[rows - SUBLANES:rows, :]
    htail_ref[0] = hs[rows - SUBLANES:rows, :]
    o_ref[...] = _mix_rows(x_ref[...], oa_ref[...], hs, gt_ref[...], lwo_ref[...], wo_ref[...], g_ref[...])


def _mix_prompt(x, oa, xb, gates, lw, lwo16, wo16, g, b, t, tm):
    n, d = x.shape
    c = xb.shape[1]
    nt = t // tm
    names = ("conv_w", "conv_b", "gate_w", "gate_b", "lam")
    consts = [lw[k] for k in names] + [lwo16, wo16, g]
    rows = lambda w: pl.BlockSpec((tm, w), lambda i, j: (i * nt + j, 0))
    tail = pl.BlockSpec((1, SUBLANES, c), lambda i, j: (i, 0, 0))
    return pl.pallas_call(
        _mix_prompt_kernel,
        grid=(b, nt),
        in_specs=[rows(d), rows(d), rows(c), rows(2 * d)] + [_const(v.shape) for v in consts],
        out_specs=[rows(d), tail, tail],
        out_shape=[jax.ShapeDtypeStruct((n, d), F32), jax.ShapeDtypeStruct((b, SUBLANES, c), F32),
                   jax.ShapeDtypeStruct((b, SUBLANES, c), F32)],
        scratch_shapes=[pltpu.VMEM((SUBLANES, c), F32), pltpu.VMEM((SUBLANES, c), F32)],
        compiler_params=_params(("arbitrary", "arbitrary")),
    )(x, oa, xb, gates, *consts)


def _ffn_steps(x1, gpre, gpost, up_ref, cw_ref, cb_ref, down_ref, hist, keep, out):
    dff = down_ref.shape[0]
    hn = _rms(x1, gpre).astype(BF16)
    starts = list(range(0, dff, FF_COLS))

    def up(c0):
        return [(slice(off, off + FF_COLS), jnp.dot(hn, up_ref[:, off:off + FF_COLS], preferred_element_type=F32))
                for off in (c0, dff + c0)]

    ups = [up(c0) for c0 in starts[:FFN_AHEAD]]
    f = jnp.zeros(x1.shape, F32)
    yield
    for i, c0 in enumerate(starts):
        if i + FFN_AHEAD < len(starts):
            ups.append(up(starts[i + FFN_AHEAD]))
        halves = []
        for cols, u in ups[i]:
            u1, u2 = hist(u, cols)
            halves.append(cb_ref[:, cols] + cw_ref[2:3, cols] * u + cw_ref[1:2, cols] * u1 + cw_ref[0:1, cols] * u2)
            keep(u, cols)
        act = _gelu_tanh(halves[0]) * halves[1]
        f = f + jnp.dot(act.astype(BF16), down_ref[c0:c0 + FF_COLS, :], preferred_element_type=F32)
        yield
    out.append(x1 + _rms(f, gpost))


def _ffn_body(x1, gpre, gpost, up_ref, cw_ref, cb_ref, down_ref, hist, keep):
    out = []
    for _ in _ffn_steps(x1, gpre, gpost, up_ref, cw_ref, cb_ref, down_ref, hist, keep, out):
        pass
    return out[0]


def _ffn_prompt_kernel(x_ref, gpre_ref, gpost_ref, up_ref, cw_ref, cb_ref, down_ref, y_ref, tail_ref, tail_scr):
    @pl.when(pl.program_id(1) == 0)
    def _init():
        tail_scr[...] = jnp.zeros_like(tail_scr)

    rows = x_ref.shape[0] // FFN_PARTS

    def hist(u, cols):
        tail = tail_scr[:, cols]
        return _shift_rows(u, tail, 1), _shift_rows(u, tail, 2)

    def keep(u, cols):
        tail_scr[:, cols] = u[rows - SUBLANES:rows, :]
        tail_ref[0, :, cols] = u[rows - SUBLANES:rows, :]

    outs = [[] for _ in range(FFN_PARTS)]
    gens = [_ffn_steps(x_ref[q * rows:(q + 1) * rows, :], gpre_ref[...], gpost_ref[...], up_ref, cw_ref, cb_ref,
                       down_ref, hist, keep, outs[q]) for q in range(FFN_PARTS)]
    nsteps = down_ref.shape[0] // FF_COLS + 1
    live = []
    for q, gen in enumerate(gens):
        live.append(gen)
        lead = nsteps - FFN_PART_LEAD if q + 1 < FFN_PARTS else nsteps + 1
        for _ in range(lead):
            for g in list(live):
                if next(g, StopIteration) is StopIteration:
                    live.remove(g)
    for q in range(FFN_PARTS):
        y_ref[q * rows:(q + 1) * rows, :] = outs[q][0]


def _ffn_prompt(x1, fw, b, t, tm):
    n, d = x1.shape
    nt = t // tm
    dff2 = fw["up"].shape[1]
    names = ("g_pre", "g_post", "up", "conv_w", "conv_b", "down")
    rows = pl.BlockSpec((tm, d), lambda i, j: (i * nt + j, 0))
    return pl.pallas_call(
        _ffn_prompt_kernel,
        grid=(b, nt),
        in_specs=[rows] + [_const(fw[k].shape) for k in names],
        out_specs=[rows, pl.BlockSpec((1, SUBLANES, dff2), lambda i, j: (i, 0, 0))],
        out_shape=[jax.ShapeDtypeStruct((n, d), F32), jax.ShapeDtypeStruct((b, SUBLANES, dff2), F32)],
        scratch_shapes=[pltpu.VMEM((SUBLANES, dff2), F32)],
        compiler_params=_params(("arbitrary", "arbitrary")),
    )(x1, *[fw[k] for k in names])


def _ffn_sample_kernel(x_ref, buf_ref, gpre_ref, gpost_ref, up_ref, cw_ref, cb_ref, down_ref, y_ref, nb_ref):
    def hist(u, cols):
        return buf_ref[:, 1, cols], buf_ref[:, 0, cols]

    def keep(u, cols):
        nb_ref[:, 0, cols] = buf_ref[:, 1, cols]
        nb_ref[:, 1, cols] = u

    y_ref[...] = _ffn_body(x_ref[...], gpre_ref[...], gpost_ref[...], up_ref, cw_ref, cb_ref, down_ref, hist, keep)


def _ffn_sample(x1, buf, fw):
    names = ("g_pre", "g_post", "up", "conv_w", "conv_b", "down")
    args = (x1, buf, *[fw[k] for k in names])
    return pl.pallas_call(
        _ffn_sample_kernel,
        grid=(1,),
        in_specs=[_const(v.shape) for v in args],
        out_specs=[_const(x1.shape), _const(buf.shape)],
        out_shape=[jax.ShapeDtypeStruct(x1.shape, F32), jax.ShapeDtypeStruct(buf.shape, F32)],
        compiler_params=_params(("arbitrary",)),
    )(*args)


def _block_diag(blocks):
    n, bi, bj = blocks.shape
    eye = jnp.eye(n, dtype=blocks.dtype)
    return jnp.einsum("nij,nm->nimj", blocks, eye).reshape(n * bi, n * bj)


def _gate_blocks(gx, ga):
    n = min(gx.shape[0], MXU_TILE)
    return jnp.stack([jnp.concatenate([gx[i:i + n, i:i + n], ga[i:i + n, i:i + n]], axis=1)
                      for i in range(0, gx.shape[0], n)], axis=0)


def _layer_weights(norm_pre_mix, norm_post_mix, norm_pre_ffn, norm_post_ffn, w_in,
                   rwkv_mu, rwkv_w0, rwkv_w_up, rwkv_a0, rwkv_a_up, rwkv_g_up, rwkv_k_k, rwkv_k_a,
                   rwkv_r_k, rwkv_lnx_w, rwkv_lnx_b, rwkv_w_out,
                   lru_conv_w, lru_conv_b, lru_gx_w, lru_gx_b, lru_ga_w, lru_ga_b, lru_lambda, lru_w_out,
                   w_o, ffn_up, ffn_conv_w, ffn_conv_b, ffn_down):
    row = lambda x: x.reshape(1, -1)
    w = rwkv_w0.shape[-1]
    head_of = jnp.arange(min(w, MXU_TILE)) // HEAD_DIM
    t_idx = jnp.arange(CHUNK)
    lo_w, lo_a = rwkv_w_up.shape[0], rwkv_a_up.shape[0]
    lora2 = jnp.zeros((lo_w + lo_a, 2 * w), F32).at[:lo_w, :w].set(rwkv_w_up).at[lo_w:, w:].set(rwkv_a_up)
    rw = dict(mu=row(rwkv_mu), w0=row(rwkv_w0), lora2=lora2.astype(BF16), a0=row(rwkv_a0),
              g_up=rwkv_g_up.astype(BF16), k_k=row(rwkv_k_k), k_a=row(rwkv_k_a), r_k=row(rwkv_r_k),
              lnx_w=row(rwkv_lnx_w), lnx_b=row(rwkv_lnx_b), w_out=rwkv_w_out.astype(BF16),
              seg=(head_of[:, None] == head_of[None, :]).astype(BF16),
              tri=(t_idx[:, None] >= t_idx[None, :]).astype(BF16))
    lw = dict(conv_w=lru_conv_w, conv_b=row(lru_conv_b),
              gate_w=_gate_blocks(_block_diag(lru_gx_w), _block_diag(lru_ga_w)).astype(BF16),
              gate_b=jnp.concatenate([row(lru_gx_b), row(lru_ga_b)], axis=1), lam=row(lru_lambda))
    fw = dict(g_pre=row(norm_pre_ffn), g_post=row(norm_post_ffn), up=ffn_up.astype(BF16),
              conv_w=ffn_conv_w, conv_b=row(ffn_conv_b), down=ffn_down.astype(BF16))
    return dict(g_in=row(norm_pre_mix), w_in=w_in.astype(BF16), rw=rw, lw=lw, fw=fw,
                lru_w_out=lru_w_out.astype(BF16), w_o=w_o.astype(BF16), g_mix=row(norm_post_mix))


def _splits(wts):
    w = wts["rw"]["w0"].shape[-1]
    c_rwkv = wts["rw"]["mu"].shape[-1]
    c_lru = wts["lw"]["lam"].shape[-1]
    return (c_rwkv, c_lru, wts["w_in"].shape[1] - c_rwkv - c_lru)


def _prompt_layer(x, wts):
    b, t, d = x.shape
    x2 = x.reshape(b * t, d)
    tm = min(t, ROW_TILE)
    m, xb, gates, p_tail = _proj_prompt(x2, wts["g_in"], wts["w_in"], wts["rw"]["mu"], _splits(wts), b, t, tm)
    oa, wkv = _rwkv_prompt(m, wts["rw"], b, t, min(t, WKV_ROWS))
    x1, xb_tail, h_tail = _mix_prompt(x2, oa, xb, gates, wts["lw"], wts["lru_w_out"], wts["w_o"], wts["g_mix"],
                                      b, t, tm)
    y, u_tail = _ffn_prompt(x1, wts["fw"], b, t, min(t, FFN_ROWS))
    conv_w = wts["lw"]["conv_w"].shape[0]
    ffn_w = wts["fw"]["conv_w"].shape[0]
    state = (p_tail[:, SUBLANES - 1:], wkv, xb_tail[:, SUBLANES - (conv_w - 1):], h_tail[:, SUBLANES - 1],
             u_tail[:, SUBLANES - (ffn_w - 1):])
    return y.reshape(b, t, d), state


def _sample_layer(x, shift, wkv, lru_buf, h0, ffn_buf, wts):
    n, t, d = x.shape
    x2 = x.reshape(n, d)
    p, xb, gates = _proj(x2, wts["g_in"], wts["w_in"], _splits(wts), n)
    oa, wkv_t = _rwkv_sample(p, shift[:, 0, :], jnp.transpose(wkv, (1, 2, 3, 0)), wts["rw"])
    lru_bufs = [lru_buf[:, j, :] for j in range(lru_buf.shape[1])]
    h = _lru_sample(xb, lru_bufs, h0, wts["lw"])
    x1 = _mix_sample(x2, oa, h, gates, wts["lru_w_out"], wts["w_o"], wts["g_mix"])
    y, ffn_new = _ffn_sample(x1, ffn_buf, wts["fw"])
    state = (p.reshape(n, 1, -1), jnp.transpose(wkv_t, (3, 0, 1, 2)), jnp.stack(lru_bufs[1:] + [xb], axis=1), h,
             ffn_new)
    return y.reshape(n, t, d), state


def kernel(x_prompt, x_sample, state_rwkv_shift, state_rwkv_wkv, state_lru_conv, state_lru_h, state_ffn_conv,
           norm_pre_mix, norm_post_mix, norm_pre_ffn, norm_post_ffn, w_in,
           rwkv_mu, rwkv_w0, rwkv_w_up, rwkv_a0, rwkv_a_up, rwkv_g_up, rwkv_k_k, rwkv_k_a,
           rwkv_r_k, rwkv_lnx_w, rwkv_lnx_b, rwkv_w_out,
           lru_conv_w, lru_conv_b, lru_gx_w, lru_gx_b, lru_ga_w, lru_ga_b, lru_lambda, lru_w_out,
           w_o, ffn_up, ffn_conv_w, ffn_conv_b, ffn_down):
    params = (norm_pre_mix, norm_post_mix, norm_pre_ffn, norm_post_ffn, w_in,
              rwkv_mu, rwkv_w0, rwkv_w_up, rwkv_a0, rwkv_a_up, rwkv_g_up, rwkv_k_k, rwkv_k_a,
              rwkv_r_k, rwkv_lnx_w, rwkv_lnx_b, rwkv_w_out,
              lru_conv_w, lru_conv_b, lru_gx_w, lru_gx_b, lru_ga_w, lru_ga_b, lru_lambda, lru_w_out,
              w_o, ffn_up, ffn_conv_w, ffn_conv_b, ffn_down)
    depth = w_in.shape[0]
    assert depth == 1 and x_sample.shape[1] == 1
    yp, ys = x_prompt, x_sample
    new_p, new_s = [], []
    for l in range(depth):
        wts = _layer_weights(*[q[l] for q in params])
        yp, st_p = _prompt_layer(yp, wts)
        ys, st_s = _sample_layer(ys, state_rwkv_shift[l], state_rwkv_wkv[l], state_lru_conv[l],
                                 state_lru_h[l], state_ffn_conv[l], wts)
        new_p.append(st_p)
        new_s.append(st_s)
    stk = lambda lst, i: jnp.stack([s[i] for s in lst], axis=0)
    return (yp, ys,
            stk(new_p, 0), stk(new_p, 1), stk(new_p, 2), stk(new_p, 3), stk(new_p, 4),
            stk(new_s, 0), stk(new_s, 1), stk(new_s, 2), stk(new_s, 3), stk(new_s, 4))
```

```python
import functools
import math

import jax
import jax.numpy as jnp
from jax import lax
from jax.experimental import pallas as pl
from jax.experimental.pallas import tpu as pltpu

F32 = jnp.float32
BF16 = jnp.bfloat16

NORM_EPS = 1e-6
LN_X_EPS = 64e-5
LRU_C = 8.0
DECAY_SCALE = math.exp(-0.5)
HEAD_DIM = 64
LANES = 128
SUBLANES = 8
MXU_TILE = 256
CHUNK = 64
ROW_TILE = 512
WKV_ROWS = 512
GATE_COLS = 512
SHIFT_COLS = 256
FF_COLS = 256
FFN_ROWS = 512
FFN_PARTS = 2
FFN_PART_LEAD = 4
FFN_AHEAD = 3
VMEM_LIMIT = 56 * 1024 * 1024

_NN = (((1,), (0,)), ((), ()))
_NT = (((1,), (1,)), ((), ()))
_TN = (((0,), (0,)), ((), ()))


def _rms(x, g):
    return x * lax.rsqrt(jnp.mean(x * x, axis=-1, keepdims=True) + NORM_EPS) * g


def _softplus(x):
    return jnp.maximum(x, 0.0) + jnp.log1p(jnp.exp(-jnp.abs(x)))


def _sigmoid(x):
    return 0.5 * jnp.tanh(0.5 * x) + 0.5


def _gelu_tanh(x):
    c = 0.7978845608028654
    hx = 0.5 * x
    return hx + hx * jnp.tanh(x * (c + (c * 0.044715) * (x * x)))


def _mm(a, b):
    return jnp.dot(a.astype(BF16), b.astype(BF16), preferred_element_type=F32)


def _mmd(a, b, dims=_NN):
    return lax.dot_general(a.astype(BF16), b.astype(BF16), dims, preferred_element_type=F32)


def _segsum(x, seg):
    x16 = x.astype(BF16)
    n = seg.shape[0]
    return jnp.concatenate([jnp.dot(x16[:, c:c + n], seg, preferred_element_type=F32)
                            for c in range(0, x.shape[1], n)], axis=1)


def _cumsum_rows(tri, x):
    hi = x.astype(BF16)
    lo = (x - hi.astype(F32)).astype(BF16)
    n = tri.shape[0]
    d = lambda y: jnp.concatenate([jnp.dot(tri, y[r:r + n], preferred_element_type=F32)
                                   for r in range(0, y.shape[0], n)], axis=0)
    return d(hi) + d(lo)


def _shift_rows(x, tail, j):
    xs = pltpu.roll(x, j, axis=0)
    ts = pltpu.roll(tail, j, axis=0)
    row = lax.broadcasted_iota(jnp.int32, ts.shape, 0)
    head = jnp.where(row < j, ts, xs[0:SUBLANES])
    if x.shape[0] == SUBLANES:
        return head
    return jnp.concatenate([head, xs[SUBLANES:]], axis=0)


def _const(shape):
    n = len(shape)
    return pl.BlockSpec(shape, lambda *_: (0,) * n, pipeline_mode=pl.Buffered(1))


def _params(sem):
    return pltpu.CompilerParams(dimension_semantics=sem, vmem_limit_bytes=VMEM_LIMIT)


def _proj_prompt_kernel(x_ref, g_ref, w_ref, mu_ref, m_ref, xb_ref, gt_ref, tail_ref, prev_scr, *, splits):
    @pl.when(pl.program_id(1) == 0)
    def _init():
        prev_scr[...] = jnp.zeros_like(prev_scr)

    xn = _rms(x_ref[...], g_ref[...]).astype(BF16)
    c0, c1, c2 = splits
    g0 = c0 + c1
    p = jnp.dot(xn, w_ref[:, 0:c0], preferred_element_type=F32)
    xb_ref[...] = jnp.dot(xn, w_ref[:, c0:g0], preferred_element_type=F32)
    rows = p.shape[0]
    row = lax.broadcasted_iota(jnp.int32, (rows, SHIFT_COLS), 0)
    shift_chunks = list(range(0, c0, SHIFT_COLS))
    gate_chunks = list(range(0, c2, GATE_COLS))
    per_gate = -(-len(shift_chunks) // len(gate_chunks))
    for i, off in enumerate(gate_chunks):
        gt_ref[:, off:off + GATE_COLS] = jnp.dot(xn, w_ref[:, g0 + off:g0 + off + GATE_COLS],
                                                 preferred_element_type=F32)
        for s0 in shift_chunks[i * per_gate:(i + 1) * per_gate]:
            cs = slice(s0, s0 + SHIFT_COLS)
            pc = p[:, cs]
            prev = jnp.where(row == 0, prev_scr[SUBLANES - 1:SUBLANES, cs], pltpu.roll(pc, 1, axis=0))
            m_ref[:, cs] = pc + mu_ref[:, cs] * (prev - pc)
    prev_scr[...] = p[rows - SUBLANES:rows, :]
    tail_ref[0] = p[rows - SUBLANES:rows, :]


def _proj_prompt(x, g, w_in16, mu, splits, b, t, tm):
    n, d = x.shape
    c0, c1, c2 = splits
    nt = t // tm
    rows = lambda c: pl.BlockSpec((tm, c), lambda i, j: (i * nt + j, 0))
    return pl.pallas_call(
        functools.partial(_proj_prompt_kernel, splits=splits),
        grid=(b, nt),
        in_specs=[rows(d), _const((1, d)), _const(w_in16.shape), _const(mu.shape)],
        out_specs=[rows(c0), rows(c1), rows(c2), pl.BlockSpec((1, SUBLANES, c0), lambda i, j: (i, 0, 0))],
        out_shape=[jax.ShapeDtypeStruct((n, c0), F32), jax.ShapeDtypeStruct((n, c1), F32),
                   jax.ShapeDtypeStruct((n, c2), F32), jax.ShapeDtypeStruct((b, SUBLANES, c0), F32)],
        scratch_shapes=[pltpu.VMEM((SUBLANES, c0), F32)],
        compiler_params=_params(("arbitrary", "arbitrary")),
    )(x, g, w_in16, mu)


def _rwkv_pre(m, w0, lora2, a0, g_up, k_k, k_a, seg):
    w = w0.shape[-1]
    r, k, v = m[:, 0:w], m[:, w:2 * w], m[:, 2 * w:3 * w]
    z = m[:, 3 * w:3 * w + LANES]
    lane = lax.broadcasted_iota(jnp.int32, z.shape, 1)
    lor = _mm(jnp.where(lane < HEAD_DIM, jnp.tanh(z), z), lora2)
    lw = -DECAY_SCALE * _sigmoid(w0 + lor[:, 0:w])
    a_sig = _sigmoid(a0 + lor[:, w:2 * w])
    g = _mm(_sigmoid(m[:, 3 * w + LANES:3 * w + 2 * LANES]), g_up)
    kk = k * k_k
    kkn = kk * lax.rsqrt(jnp.maximum(_segsum(kk * kk, seg), 1e-24))
    k2 = k * (1.0 + (a_sig - 1.0) * k_a)
    return r, lw, k2, v, -kkn, kkn * a_sig, g


def _rwkv_post(y, r, k2, v, g, r_k, lnx_w, lnx_b, w_out, seg):
    inv = 1.0 / HEAD_DIM
    mean = _segsum(y, seg) * inv
    d = y - mean
    var = _segsum(d * d, seg) * inv
    yn = d * lax.rsqrt(var + LN_X_EPS) * lnx_w + lnx_b
    bonus = _segsum(r * k2 * r_k, seg) * v
    return _mm((yn + bonus) * g, w_out)


def _wkv_chunk_summaries(xs, m0b, strict, incl, eye2, eyew):
    ln = xs[0][0].shape[0]
    swap = lambda x: pltpu.roll(x, HEAD_DIM, axis=1)
    h16 = lambda x: x.astype(BF16)
    zero16 = jnp.zeros((), BF16)
    bd = lambda x: jnp.concatenate([jnp.where(m0b, x, zero16), jnp.where(m0b, zero16, x)], axis=0)
    cat = jnp.concatenate
    rts, ats, bts, kts, bhs, khs, vs, wls = zip(*xs)
    at16 = [h16(x) for x in ats]
    v16 = [h16(x) for x in vs]
    bdv = [bd(x) for x in v16]
    gms = [_mmd(cat([bd(a), bd(h16(rt))], axis=0), cat([bt, kt], axis=0), _NT)
           for rt, a, bt, kt in zip(rts, at16, bts, kts)]
    ga0 = [jnp.where(strict, g[0:ln], 0.0) for g in gms]
    ga1 = [jnp.where(strict, g[ln:2 * ln], 0.0) for g in gms]
    gr0 = [jnp.where(incl, g[2 * ln:3 * ln], 0.0) for g in gms]
    gr1 = [jnp.where(incl, g[3 * ln:4 * ln], 0.0) for g in gms]
    n_ab = [jnp.where(m0b, x, swap(y)) for x, y in zip(ga0, ga1)]
    n_ak = [jnp.where(m0b, swap(x), y) for x, y in zip(ga0, ga1)]
    n_rb = [jnp.where(m0b, x, swap(y)) for x, y in zip(gr0, gr1)]
    n_rk = [jnp.where(m0b, swap(x), y) for x, y in zip(gr0, gr1)]
    tinv = [eye2 + n for n in n_ab]
    steps = (ln - 1).bit_length() - 1
    if steps > 0:
        npow = [h16(_mmd(n, bd(n))) for n in (h16(x) for x in n_ab)]
        for _ in range(steps - 1):
            both = [_mmd(cat([n, h16(t)], axis=0), bd(n)) for n, t in zip(npow, tinv)]
            tinv = [t + x[ln:2 * ln] for t, x in zip(tinv, both)]
            npow = [h16(x[0:ln]) for x in both]
        tinv = [t + _mmd(t, bd(n)) for t, n in zip(tinv, npow)]
    kv = [_mmd(cat([nk, nr], axis=0), b) for nk, nr, b in zip(n_ak, n_rk, bdv)]
    akv = [x[0:ln] for x in kv]
    rkv = [x[ln:2 * ln] for x in kv]
    tt = [_mmd(t, cat([bd(a), bd(h16(ak))], axis=1)) for t, a, ak in zip(tinv, at16, akv)]
    ta = [h16(x[:, 0:LANES]) for x in tt]
    tk = [h16(x[:, LANES:2 * LANES]) for x in tt]
    mc = [_mmd(cat([cat([a, k], axis=1), cat([jnp.zeros_like(v), v], axis=1)], axis=0), cat([bh, kh], axis=0), _TN)
          for a, k, v, bh, kh in zip(ta, tk, v16, bhs, khs)]
    ms = [eyew * wl + x[0:LANES] for x, wl in zip(mc, wls)]
    cs = [x[LANES:2 * LANES] for x in mc]
    qy = [_mmd(n, cat([bd(a), bd(k)], axis=1)) for n, a, k in zip(n_rb, ta, tk)]
    qs = [rt + x[:, 0:LANES] for rt, x in zip(rts, qy)]
    y0 = [x[:, LANES:2 * LANES] + r for x, r in zip(qy, rkv)]
    return list(zip(qs, y0, ms, cs))


def _rwkv_prompt_kernel(m_ref, w0_ref, lora_ref, a0_ref, gup_ref, kk_ref, ka_ref, rk_ref,
                        lnw_ref, lnb_ref, wout_ref, seg_ref, tri_ref, oa_ref, wkv_ref, s_scr):
    t = pl.program_id(1)

    @pl.when(t == 0)
    def _init():
        s_scr[...] = jnp.zeros_like(s_scr)

    rows = m_ref.shape[0]
    seg = seg_ref[...]
    r, lw, k2, v, aa, bb, g = _rwkv_pre(m_ref[...], w0_ref[...], lora_ref[...], a0_ref[...],
                                        gup_ref[...], kk_ref[...], ka_ref[...], seg)
    c = _cumsum_rows(tri_ref[...], lw)

    ln = CHUNK
    ri = lax.broadcasted_iota(jnp.int32, (ln, LANES), 0)
    ci = lax.broadcasted_iota(jnp.int32, (ln, LANES), 1)
    cj = jnp.where(ci < HEAD_DIM, ci, ci - HEAD_DIM)
    m0b = ci < HEAD_DIM
    strict = cj < ri
    incl = cj <= ri
    eye2 = (cj == ri).astype(F32)
    r2 = lax.broadcasted_iota(jnp.int32, (LANES, LANES), 0)
    c2 = lax.broadcasted_iota(jnp.int32, (LANES, LANES), 1)
    bdmask = ((r2 < HEAD_DIM) == (c2 < HEAD_DIM)).astype(F32)
    eyew = (r2 == c2).astype(F32)

    npairs = s_scr.shape[0]
    insts = []
    for ch in range(rows // ln):
        rs = slice(ch * ln, (ch + 1) * ln)
        c_, lw_ = c[rs], lw[rs]
        cl = c_[ln - 1:ln, :]
        e_n = jnp.exp(-c_)
        e_l = jnp.exp(cl - c_)
        rt = r[rs] * jnp.exp(c_)
        at = aa[rs] * jnp.exp(c_ - lw_)
        bt, kt = bb[rs] * e_n, k2[rs] * e_n
        bh, kh = bb[rs] * e_l, k2[rs] * e_l
        wl = jnp.exp(cl)
        v_ = v[rs]
        for pr in range(npairs):
            sl = slice(pr * LANES, (pr + 1) * LANES)
            insts.append((rt[:, sl], at[:, sl], bt[:, sl], kt[:, sl], bh[:, sl], kh[:, sl], v_[:, sl], wl[:, sl]))
    summaries = _wkv_chunk_summaries(insts, m0b, strict, incl, eye2, eyew)

    states = [s_scr[pr] for pr in range(npairs)]
    y_rows = []
    for ch in range(rows // ln):
        ys = []
        for pr in range(npairs):
            q, y0, m, cc = summaries[ch * npairs + pr]
            ys.append(_mmd(q, states[pr], _NT) + y0)
            states[pr] = (_mmd(states[pr], m) + cc) * bdmask
        y_rows.append(jnp.concatenate(ys, axis=1))
    for pr in range(npairs):
        s_scr[pr] = states[pr]
    y = jnp.concatenate(y_rows, axis=0)
    oa_ref[...] = _rwkv_post(y, r, k2, v, g, rk_ref[...], lnw_ref[...], lnb_ref[...], wout_ref[...], seg)

    @pl.when(t == pl.num_programs(1) - 1)
    def _fin():
        for pr in range(npairs):
            s = s_scr[pr]
            wkv_ref[0, 2 * pr] = s[0:HEAD_DIM, 0:HEAD_DIM]
            wkv_ref[0, 2 * pr + 1] = pltpu.roll(s[HEAD_DIM:LANES, :], HEAD_DIM, axis=1)[:, 0:HEAD_DIM]


def _rwkv_prompt(m, rw, b, t, tc):
    n, cp = m.shape
    w = rw["w0"].shape[-1]
    heads = w // HEAD_DIM
    nt = t // tc
    dm = rw["w_out"].shape[-1]
    tri = jnp.kron(jnp.eye(min(tc, MXU_TILE) // CHUNK, dtype=BF16), rw["tri"])
    names = ("w0", "lora2", "a0", "g_up", "k_k", "k_a", "r_k", "lnx_w", "lnx_b", "w_out", "seg")
    return pl.pallas_call(
        _rwkv_prompt_kernel,
        grid=(b, nt),
        in_specs=[pl.BlockSpec((tc, cp), lambda i, j: (i * nt + j, 0))] + [_const(rw[k].shape) for k in names]
                 + [_const(tri.shape)],
        out_specs=[pl.BlockSpec((tc, dm), lambda i, j: (i * nt + j, 0)),
                   pl.BlockSpec((1, heads, HEAD_DIM, HEAD_DIM), lambda i, j: (i, 0, 0, 0))],
        out_shape=[jax.ShapeDtypeStruct((n, dm), F32),
                   jax.ShapeDtypeStruct((b, heads, HEAD_DIM, HEAD_DIM), F32)],
        scratch_shapes=[pltpu.VMEM((heads // 2, LANES, LANES), F32)],
        compiler_params=_params(("arbitrary", "arbitrary")),
    )(m, *[rw[k] for k in names], tri)


def _rwkv_sample_step_kernel(s_ref, r_ref, w_ref, k_ref, v_ref, a_ref, b_ref, y_ref, so_ref):
    r, w, k, a, b = r_ref[...], w_ref[...], k_ref[...], a_ref[...], b_ref[...]

    def body(i, carry):
        base = pl.multiple_of(i * SUBLANES, SUBLANES)
        vrows = v_ref[pl.ds(base, SUBLANES), :]
        ys = []
        for j in range(SUBLANES):
            s = s_ref[0, base + j]
            sa = jnp.sum(s * a, axis=0, keepdims=True)
            s_new = s * w + sa * b + vrows[j:j + 1, :] * k
            so_ref[0, base + j] = s_new
            ys.append(jnp.sum(s_new * r, axis=0, keepdims=True))
        y_ref[pl.ds(base, SUBLANES), :] = jnp.concatenate(ys, axis=0)
        return carry

    lax.fori_loop(0, s_ref.shape[1] // SUBLANES, body, 0)


def _lru_gates(xc, gw, gb, lam):
    c = xc.shape[-1]
    n = gw.shape[1]
    xc16 = xc.astype(BF16)
    parts = [jnp.dot(xc16[:, i * n:(i + 1) * n], gw[i], preferred_element_type=F32) for i in range(c // n)]
    gx = _sigmoid(jnp.concatenate([p[:, 0:n] for p in parts], axis=1) + gb[:, 0:c])
    ga = _sigmoid(jnp.concatenate([p[:, n:2 * n] for p in parts], axis=1) + gb[:, c:2 * c])
    log_a = -LRU_C * ga * _softplus(-lam)
    a = jnp.exp(log_a)
    u = jnp.sqrt(1.0 - a * a) * (gx * xc)
    return a, u


def _scan_rows(a, u, h0):
    rows, c = a.shape
    groups = rows // SUBLANES
    a3 = a.reshape(groups, SUBLANES, c)
    u3 = u.reshape(groups, SUBLANES, c)
    sub = lax.broadcasted_iota(jnp.int32, a3.shape, 1)
    k = 1
    while k < SUBLANES:
        keep = sub >= k
        u3 = u3 + a3 * jnp.where(keep, pltpu.roll(u3, k, axis=1), 0.0)
        a3 = a3 * jnp.where(keep, pltpu.roll(a3, k, axis=1), 1.0)
        k *= 2
    h = h0
    out = []
    for i in range(groups):
        hi = u3[i] + a3[i] * h
        out.append(hi)
        h = hi[SUBLANES - 1:SUBLANES, :]
    return jnp.concatenate(out, axis=0)


def _lru_prompt(x, tail, h0, cw_ref, cb_ref, gw, gb, lam):
    width = cw_ref.shape[0]
    xc = cb_ref[...] + cw_ref[width - 1:width, :] * x
    for j in range(1, width):
        xc = xc + cw_ref[width - 1 - j:width - j, :] * _shift_rows(x, tail, j)
    a, u = _lru_gates(xc, gw, gb, lam)
    return _scan_rows(a, u, h0)


def _mix_rows(x, oa, hs, gates, lwo, wo, g):
    d = x.shape[-1]
    ob = _mm(hs, lwo)
    mix = _mm(_sigmoid(gates[:, 0:d]) * oa + _sigmoid(gates[:, d:2 * d]) * ob, wo)
    return x + _rms(mix, g)


def _mix_prompt_kernel(x_ref, oa_ref, xb_ref, gt_ref, cw_ref, cb_ref, gw_ref, gb_ref, lam_ref, lwo_ref, wo_ref, g_ref,
                       o_ref, xtail_ref, htail_ref, xtail_scr, h_scr):
    @pl.when(pl.program_id(1) == 0)
    def _init():
        xtail_scr[...] = jnp.zeros_like(xtail_scr)
        h_scr[...] = jnp.zeros_like(h_scr)

    xb = xb_ref[...]
    rows = xb.shape[0]
    hs = _lru_prompt(xb, xtail_scr[...], h_scr[SUBLANES - 1:SUBLANES, :], cw_ref, cb_ref,
                     gw_ref[...], gb_ref[...], lam_ref[...])
    xtail_scr[...] = xb[rows - SUBLANES:rows, :]
    xtail_ref[0] = xb[rows - SUBLANES:rows, :]
    h_scr[...] = hs[rows - SUBLANES:rows, :]
    htail_ref[0] = hs[rows - SUBLANES:rows, :]
    o_ref[...] = _mix_rows(x_ref[...], oa_ref[...], hs, gt_ref[...], lwo_ref[...], wo_ref[...], g_ref[...])


def _mix_prompt(x, oa, xb, gates, lw, lwo16, wo16, g, b, t, tm):
    n, d = x.shape
    c = xb.shape[1]
    nt = t // tm
    names = ("conv_w", "conv_b", "gate_w", "gate_b", "lam")
    consts = [lw[k] for k in names] + [lwo16, wo16, g]
    rows = lambda w: pl.BlockSpec((tm, w), lambda i, j: (i * nt + j, 0))
    tail = pl.BlockSpec((1, SUBLANES, c), lambda i, j: (i, 0, 0))
    return pl.pallas_call(
        _mix_prompt_kernel,
        grid=(b, nt),
        in_specs=[rows(d), rows(d), rows(c), rows(2 * d)] + [_const(v.shape) for v in consts],
        out_specs=[rows(d), tail, tail],
        out_shape=[jax.ShapeDtypeStruct((n, d), F32), jax.ShapeDtypeStruct((b, SUBLANES, c), F32),
                   jax.ShapeDtypeStruct((b, SUBLANES, c), F32)],
        scratch_shapes=[pltpu.VMEM((SUBLANES, c), F32), pltpu.VMEM((SUBLANES, c), F32)],
        compiler_params=_params(("arbitrary", "arbitrary")),
    )(x, oa, xb, gates, *consts)


def _ffn_steps(x1, gpre, gpost, up_ref, cw_ref, cb_ref, down_ref, hist, keep, out):
    dff = down_ref.shape[0]
    hn = _rms(x1, gpre).astype(BF16)
    starts = list(range(0, dff, FF_COLS))

    def up(c0):
        return [(slice(off, off + FF_COLS), jnp.dot(hn, up_ref[:, off:off + FF_COLS], preferred_element_type=F32))
                for off in (c0, dff + c0)]

    ups = [up(c0) for c0 in starts[:FFN_AHEAD]]
    f = jnp.zeros(x1.shape, F32)
    yield
    for i, c0 in enumerate(starts):
        if i + FFN_AHEAD < len(starts):
            ups.append(up(starts[i + FFN_AHEAD]))
        halves = []
        for cols, u in ups[i]:
            u1, u2 = hist(u, cols)
            halves.append(cb_ref[:, cols] + cw_ref[2:3, cols] * u + cw_ref[1:2, cols] * u1 + cw_ref[0:1, cols] * u2)
            keep(u, cols)
        act = _gelu_tanh(halves[0]) * halves[1]
        f = f + jnp.dot(act.astype(BF16), down_ref[c0:c0 + FF_COLS, :], preferred_element_type=F32)
        yield
    out.append(x1 + _rms(f, gpost))


def _ffn_body(x1, gpre, gpost, up_ref, cw_ref, cb_ref, down_ref, hist, keep):
    out = []
    for _ in _ffn_steps(x1, gpre, gpost, up_ref, cw_ref, cb_ref, down_ref, hist, keep, out):
        pass
    return out[0]


def _ffn_prompt_kernel(x_ref, gpre_ref, gpost_ref, up_ref, cw_ref, cb_ref, down_ref, y_ref, tail_ref, tail_scr):
    @pl.when(pl.program_id(1) == 0)
    def _init():
        tail_scr[...] = jnp.zeros_like(tail_scr)

    rows = x_ref.shape[0] // FFN_PARTS

    def hist(u, cols):
        tail = tail_scr[:, cols]
        return _shift_rows(u, tail, 1), _shift_rows(u, tail, 2)

    def keep(u, cols):
        tail_scr[:, cols] = u[rows - SUBLANES:rows, :]
        tail_ref[0, :, cols] = u[rows - SUBLANES:rows, :]

    outs = [[] for _ in range(FFN_PARTS)]
    gens = [_ffn_steps(x_ref[q * rows:(q + 1) * rows, :], gpre_ref[...], gpost_ref[...], up_ref, cw_ref, cb_ref,
                       down_ref, hist, keep, outs[q]) for q in range(FFN_PARTS)]
    nsteps = down_ref.shape[0] // FF_COLS + 1
    live = []
    for q, gen in enumerate(gens):
        live.append(gen)
        lead = nsteps - FFN_PART_LEAD if q + 1 < FFN_PARTS else nsteps + 1
        for _ in range(lead):
            for g in list(live):
                if next(g, StopIteration) is StopIteration:
                    live.remove(g)
    for q in range(FFN_PARTS):
        y_ref[q * rows:(q + 1) * rows, :] = outs[q][0]


def _ffn_prompt(x1, fw, b, t, tm):
    n, d = x1.shape
    nt = t // tm
    dff2 = fw["up"].shape[1]
    names = ("g_pre", "g_post", "up", "conv_w", "conv_b", "down")
    rows = pl.BlockSpec((tm, d), lambda i, j: (i * nt + j, 0))
    return pl.pallas_call(
        _ffn_prompt_kernel,
        grid=(b, nt),
        in_specs=[rows] + [_const(fw[k].shape) for k in names],
        out_specs=[rows, pl.BlockSpec((1, SUBLANES, dff2), lambda i, j: (i, 0, 0))],
        out_shape=[jax.ShapeDtypeStruct((n, d), F32), jax.ShapeDtypeStruct((b, SUBLANES, dff2), F32)],
        scratch_shapes=[pltpu.VMEM((SUBLANES, dff2), F32)],
        compiler_params=_params(("arbitrary", "arbitrary")),
    )(x1, *[fw[k] for k in names])


def _sample_front_kernel(x_ref, shift_ref, h0_ref, *refs, splits, n_bufs):
    buf_refs, refs = refs[:n_bufs], refs[n_bufs:]
    (gin_ref, w_ref, mu_ref, w0_ref, lora_ref, a0_ref, gup_ref, kk_ref, ka_ref, seg_ref,
     cw_ref, cb_ref, gw_ref, gb_ref, lam_ref,
     p_ref, xb_ref, gt_ref, h_ref, rt_ref, wt_ref, kt_ref, vt_ref, at_ref, bt_ref, r_ref, k_ref, v_ref, g_ref) = refs
    xn = _rms(x_ref[...], gin_ref[...]).astype(BF16)
    c0, c1, c2 = splits
    p = jnp.dot(xn, w_ref[:, 0:c0], preferred_element_type=F32)
    xb = jnp.dot(xn, w_ref[:, c0:c0 + c1], preferred_element_type=F32)
    gt_ref[...] = jnp.dot(xn, w_ref[:, c0 + c1:c0 + c1 + c2], preferred_element_type=F32)
    p_ref[...] = p
    xb_ref[...] = xb
    r, lw, k2, v, aa, bb, g = _rwkv_pre(p + mu_ref[...] * (shift_ref[...] - p), w0_ref[...], lora_ref[...],
                                        a0_ref[...], gup_ref[...], kk_ref[...], ka_ref[...], seg_ref[...])
    rt_ref[...] = r.T
    wt_ref[...] = jnp.exp(lw).T
    kt_ref[...] = k2.T
    vt_ref[...] = v.T
    at_ref[...] = aa.T
    bt_ref[...] = bb.T
    r_ref[...] = r
    k_ref[...] = k2
    v_ref[...] = v
    g_ref[...] = g
    width = cw_ref.shape[0]
    xc = cb_ref[...] + cw_ref[width - 1:width, :] * xb
    for j, buf_ref in enumerate(buf_refs):
        xc = xc + cw_ref[j:j + 1, :] * buf_ref[...]
    a, u = _lru_gates(xc, gw_ref[...], gb_ref[...], lam_ref[...])
    h_ref[...] = a * h0_ref[...] + u


def _sample_back_kernel(yt_ref, r_ref, k_ref, v_ref, g_ref, x_ref, h_ref, gt_ref, buf_ref,
                        rk_ref, lnw_ref, lnb_ref, wout_ref, seg_ref, lwo_ref, wo_ref, gmix_ref,
                        gpre_ref, gpost_ref, up_ref, cw_ref, cb_ref, down_ref, y_ref, nb_ref):
    oa = _rwkv_post(yt_ref[...].T, r_ref[...], k_ref[...], v_ref[...], g_ref[...], rk_ref[...],
                    lnw_ref[...], lnb_ref[...], wout_ref[...], seg_ref[...])
    x1 = _mix_rows(x_ref[...], oa, h_ref[...], gt_ref[...], lwo_ref[...], wo_ref[...], gmix_ref[...])

    def hist(u, cols):
        return buf_ref[:, 1, cols], buf_ref[:, 0, cols]

    def keep(u, cols):
        nb_ref[:, 0, cols] = buf_ref[:, 1, cols]
        nb_ref[:, 1, cols] = u

    y_ref[...] = _ffn_body(x1, gpre_ref[...], gpost_ref[...], up_ref, cw_ref, cb_ref, down_ref, hist, keep)


def _single_step_call(body, args, out_shapes):
    return pl.pallas_call(
        body,
        grid=(1,),
        in_specs=[_const(v.shape) for v in args],
        out_specs=[_const(o.shape) for o in out_shapes],
        out_shape=out_shapes,
        compiler_params=_params(("arbitrary",)),
    )(*args)


def _sample_layer(x, shift, wkv, lru_buf, h0, ffn_buf, wts):
    n, t, d = x.shape
    x2 = x.reshape(n, d)
    rw, lw, fw = wts["rw"], wts["lw"], wts["fw"]
    splits = _splits(wts)
    c0, c1, c2 = splits
    w = rw["w0"].shape[-1]
    heads = w // HEAD_DIM
    lru_bufs = [lru_buf[:, j, :] for j in range(lru_buf.shape[1])]
    f = lambda *shape: jax.ShapeDtypeStruct(shape, F32)
    front_args = (x2, shift[:, 0, :], h0, *lru_bufs, wts["g_in"], wts["w_in"],
                  *[rw[k] for k in ("mu", "w0", "lora2", "a0", "g_up", "k_k", "k_a", "seg")],
                  *[lw[k] for k in ("conv_w", "conv_b", "gate_w", "gate_b", "lam")])
    p, xb, gates, h, rt, wt, kt, vt, at, bt, r, k2, v, g = _single_step_call(
        functools.partial(_sample_front_kernel, splits=splits, n_bufs=len(lru_bufs)), front_args,
        [f(n, c0), f(n, c1), f(n, c2), f(n, c1)] + [f(w, n)] * 6 + [f(n, w)] * 4)

    wkv_t = jnp.transpose(wkv, (1, 2, 3, 0))
    head_spec = pl.BlockSpec((HEAD_DIM, n), lambda i: (i, 0))
    st_spec = pl.BlockSpec((1, HEAD_DIM, HEAD_DIM, n), lambda i: (i, 0, 0, 0))
    yt, wkv_new = pl.pallas_call(
        _rwkv_sample_step_kernel,
        grid=(heads,),
        in_specs=[st_spec] + [head_spec] * 6,
        out_specs=[head_spec, st_spec],
        out_shape=[f(w, n), f(*wkv_t.shape)],
        compiler_params=_params(("arbitrary",)),
    )(wkv_t, rt, wt, kt, vt, at, bt)

    back_args = (yt, r, k2, v, g, x2, h, gates, ffn_buf,
                 *[rw[k] for k in ("r_k", "lnx_w", "lnx_b", "w_out", "seg")],
                 wts["lru_w_out"], wts["w_o"], wts["g_mix"],
                 *[fw[k] for k in ("g_pre", "g_post", "up", "conv_w", "conv_b", "down")])
    y, ffn_new = _single_step_call(_sample_back_kernel, back_args, [f(n, d), f(*ffn_buf.shape)])
    state = (p.reshape(n, 1, -1), jnp.transpose(wkv_new, (3, 0, 1, 2)), jnp.stack(lru_bufs[1:] + [xb], axis=1), h,
             ffn_new)
    return y.reshape(n, t, d), state


def _block_diag(blocks):
    n, bi, bj = blocks.shape
    eye = jnp.eye(n, dtype=blocks.dtype)
    return jnp.einsum("nij,nm->nimj", blocks, eye).reshape(n * bi, n * bj)


def _gate_blocks(gx, ga):
    n = min(gx.shape[0], MXU_TILE)
    return jnp.stack([jnp.concatenate([gx[i:i + n, i:i + n], ga[i:i + n, i:i + n]], axis=1)
                      for i in range(0, gx.shape[0], n)], axis=0)


def _layer_weights(norm_pre_mix, norm_post_mix, norm_pre_ffn, norm_post_ffn, w_in,
                   rwkv_mu, rwkv_w0, rwkv_w_up, rwkv_a0, rwkv_a_up, rwkv_g_up, rwkv_k_k, rwkv_k_a,
                   rwkv_r_k, rwkv_lnx_w, rwkv_lnx_b, rwkv_w_out,
                   lru_conv_w, lru_conv_b, lru_gx_w, lru_gx_b, lru_ga_w, lru_ga_b, lru_lambda, lru_w_out,
                   w_o, ffn_up, ffn_conv_w, ffn_conv_b, ffn_down):
    row = lambda x: x.reshape(1, -1)
    w = rwkv_w0.shape[-1]
    head_of = jnp.arange(min(w, MXU_TILE)) // HEAD_DIM
    t_idx = jnp.arange(CHUNK)
    lo_w, lo_a = rwkv_w_up.shape[0], rwkv_a_up.shape[0]
    lora2 = jnp.zeros((lo_w + lo_a, 2 * w), F32).at[:lo_w, :w].set(rwkv_w_up).at[lo_w:, w:].set(rwkv_a_up)
    rw = dict(mu=row(rwkv_mu), w0=row(rwkv_w0), lora2=lora2.astype(BF16), a0=row(rwkv_a0),
              g_up=rwkv_g_up.astype(BF16), k_k=row(rwkv_k_k), k_a=row(rwkv_k_a), r_k=row(rwkv_r_k),
              lnx_w=row(rwkv_lnx_w), lnx_b=row(rwkv_lnx_b), w_out=rwkv_w_out.astype(BF16),
              seg=(head_of[:, None] == head_of[None, :]).astype(BF16),
              tri=(t_idx[:, None] >= t_idx[None, :]).astype(BF16))
    lw = dict(conv_w=lru_conv_w, conv_b=row(lru_conv_b),
              gate_w=_gate_blocks(_block_diag(lru_gx_w), _block_diag(lru_ga_w)).astype(BF16),
              gate_b=jnp.concatenate([row(lru_gx_b), row(lru_ga_b)], axis=1), lam=row(lru_lambda))
    fw = dict(g_pre=row(norm_pre_ffn), g_post=row(norm_post_ffn), up=ffn_up.astype(BF16),
              conv_w=ffn_conv_w, conv_b=row(ffn_conv_b), down=ffn_down.astype(BF16))
    return dict(g_in=row(norm_pre_mix), w_in=w_in.astype(BF16), rw=rw, lw=lw, fw=fw,
                lru_w_out=lru_w_out.astype(BF16), w_o=w_o.astype(BF16), g_mix=row(norm_post_mix))


def _splits(wts):
    w = wts["rw"]["w0"].shape[-1]
    c_rwkv = wts["rw"]["mu"].shape[-1]
    c_lru = wts["lw"]["lam"].shape[-1]
    return (c_rwkv, c_lru, wts["w_in"].shape[1] - c_rwkv - c_lru)


def _prompt_layer(x, wts):
    b, t, d = x.shape
    x2 = x.reshape(b * t, d)
    tm = min(t, ROW_TILE)
    m, xb, gates, p_tail = _proj_prompt(x2, wts["g_in"], wts["w_in"], wts["rw"]["mu"], _splits(wts), b, t, tm)
    oa, wkv = _rwkv_prompt(m, wts["rw"], b, t, min(t, WKV_ROWS))
    x1, xb_tail, h_tail = _mix_prompt(x2, oa, xb, gates, wts["lw"], wts["lru_w_out"], wts["w_o"], wts["g_mix"],
                                      b, t, tm)
    y, u_tail = _ffn_prompt(x1, wts["fw"], b, t, min(t, FFN_ROWS))
    conv_w = wts["lw"]["conv_w"].shape[0]
    ffn_w = wts["fw"]["conv_w"].shape[0]
    state = (p_tail[:, SUBLANES - 1:], wkv, xb_tail[:, SUBLANES - (conv_w - 1):], h_tail[:, SUBLANES - 1],
             u_tail[:, SUBLANES - (ffn_w - 1):])
    return y.reshape(b, t, d), state


def kernel(x_prompt, x_sample, state_rwkv_shift, state_rwkv_wkv, state_lru_conv, state_lru_h, state_ffn_conv,
           norm_pre_mix, norm_post_mix, norm_pre_ffn, norm_post_ffn, w_in,
           rwkv_mu, rwkv_w0, rwkv_w_up, rwkv_a0, rwkv_a_up, rwkv_g_up, rwkv_k_k, rwkv_k_a,
           rwkv_r_k, rwkv_lnx_w, rwkv_lnx_b, rwkv_w_out,
           lru_conv_w, lru_conv_b, lru_gx_w, lru_gx_b, lru_ga_w, lru_ga_b, lru_lambda, lru_w_out,
           w_o, ffn_up, ffn_conv_w, ffn_conv_b, ffn_down):
    params = (norm_pre_mix, norm_post_mix, norm_pre_ffn, norm_post_ffn, w_in,
              rwkv_mu, rwkv_w0, rwkv_w_up, rwkv_a0, rwkv_a_up, rwkv_g_up, rwkv_k_k, rwkv_k_a,
              rwkv_r_k, rwkv_lnx_w, rwkv_lnx_b, rwkv_w_out,
              lru_conv_w, lru_conv_b, lru_gx_w, lru_gx_b, lru_ga_w, lru_ga_b, lru_lambda, lru_w_out,
              w_o, ffn_up, ffn_conv_w, ffn_conv_b, ffn_down)
    depth = w_in.shape[0]
    assert depth == 1 and x_sample.shape[1] == 1
    yp, ys = x_prompt, x_sample
    new_p, new_s = [], []
    for l in range(depth):
        wts = _layer_weights(*[q[l] for q in params])
        yp, st_p = _prompt_layer(yp, wts)
        ys, st_s = _sample_layer(ys, state_rwkv_shift[l], state_rwkv_wkv[l], state_lru_conv[l],
                                 state_lru_h[l], state_ffn_conv[l], wts)
        new_p.append(st_p)
        new_s.append(st_s)
    stk = lambda lst, i: jnp.stack([s[i] for s in lst], axis=0)
    return (yp, ys,
            stk(new_p, 0), stk(new_p, 1), stk(new_p, 2), stk(new_p, 3), stk(new_p, 4),
            stk(new_s, 0), stk(new_s, 1), stk(new_s, 2), stk(new_s, 3), stk(new_s, 4))
```

```python
import functools
import math

import jax
import jax.numpy as jnp
from jax import lax
from jax.experimental import pallas as pl
from jax.experimental.pallas import tpu as pltpu

F32 = jnp.float32
BF16 = jnp.bfloat16

NORM_EPS = 1e-6
LN_X_EPS = 64e-5
LRU_C = 8.0
DECAY_SCALE = math.exp(-0.5)
HEAD_DIM = 64
LANES = 128
SUBLANES = 8
MXU_TILE = 256
CHUNK = 64
ROW_TILE = 512
WKV_ROWS = 512
GATE_COLS = 512
SHIFT_COLS = 256
FF_COLS = 256
FFN_ROWS = 512
FFN_PARTS = 2
FFN_PART_LEAD = 4
FFN_AHEAD = 3
SAMPLE_FF_COLS = 512
VMEM_LIMIT = 56 * 1024 * 1024

_NN = (((1,), (0,)), ((), ()))
_NT = (((1,), (1,)), ((), ()))
_TN = (((0,), (0,)), ((), ()))


def _rms(x, g):
    return x * lax.rsqrt(jnp.mean(x * x, axis=-1, keepdims=True) + NORM_EPS) * g


def _softplus(x):
    return jnp.maximum(x, 0.0) + jnp.log1p(jnp.exp(-jnp.abs(x)))


def _sigmoid(x):
    return 0.5 * jnp.tanh(0.5 * x) + 0.5


def _gelu_tanh(x):
    c = 0.7978845608028654
    hx = 0.5 * x
    return hx + hx * jnp.tanh(x * (c + (c * 0.044715) * (x * x)))


def _mm(a, b):
    return jnp.dot(a.astype(BF16), b.astype(BF16), preferred_element_type=F32)


def _mmd(a, b, dims=_NN):
    return lax.dot_general(a.astype(BF16), b.astype(BF16), dims, preferred_element_type=F32)


def _segsum(x, seg):
    x16 = x.astype(BF16)
    n = seg.shape[0]
    return jnp.concatenate([jnp.dot(x16[:, c:c + n], seg, preferred_element_type=F32)
                            for c in range(0, x.shape[1], n)], axis=1)


def _cumsum_rows(tri, x):
    hi = x.astype(BF16)
    lo = (x - hi.astype(F32)).astype(BF16)
    n = tri.shape[0]
    d = lambda y: jnp.concatenate([jnp.dot(tri, y[r:r + n], preferred_element_type=F32)
                                   for r in range(0, y.shape[0], n)], axis=0)
    return d(hi) + d(lo)


def _shift_rows(x, tail, j):
    xs = pltpu.roll(x, j, axis=0)
    ts = pltpu.roll(tail, j, axis=0)
    row = lax.broadcasted_iota(jnp.int32, ts.shape, 0)
    head = jnp.where(row < j, ts, xs[0:SUBLANES])
    if x.shape[0] == SUBLANES:
        return head
    return jnp.concatenate([head, xs[SUBLANES:]], axis=0)


def _const(shape):
    n = len(shape)
    return pl.BlockSpec(shape, lambda *_: (0,) * n, pipeline_mode=pl.Buffered(1))


def _params(sem):
    return pltpu.CompilerParams(dimension_semantics=sem, vmem_limit_bytes=VMEM_LIMIT)


def _proj_prompt_kernel(x_ref, g_ref, w_ref, mu_ref, m_ref, xb_ref, gt_ref, tail_ref, prev_scr, *, splits):
    @pl.when(pl.program_id(1) == 0)
    def _init():
        prev_scr[...] = jnp.zeros_like(prev_scr)

    xn = _rms(x_ref[...], g_ref[...]).astype(BF16)
    c0, c1, c2 = splits
    g0 = c0 + c1
    p = jnp.dot(xn, w_ref[:, 0:c0], preferred_element_type=F32)
    xb_ref[...] = jnp.dot(xn, w_ref[:, c0:g0], preferred_element_type=F32)
    rows = p.shape[0]
    row = lax.broadcasted_iota(jnp.int32, (rows, SHIFT_COLS), 0)
    shift_chunks = list(range(0, c0, SHIFT_COLS))
    gate_chunks = list(range(0, c2, GATE_COLS))
    per_gate = -(-len(shift_chunks) // len(gate_chunks))
    for i, off in enumerate(gate_chunks):
        gt_ref[:, off:off + GATE_COLS] = jnp.dot(xn, w_ref[:, g0 + off:g0 + off + GATE_COLS],
                                                 preferred_element_type=F32)
        for s0 in shift_chunks[i * per_gate:(i + 1) * per_gate]:
            cs = slice(s0, s0 + SHIFT_COLS)
            pc = p[:, cs]
            prev = jnp.where(row == 0, prev_scr[SUBLANES - 1:SUBLANES, cs], pltpu.roll(pc, 1, axis=0))
            m_ref[:, cs] = pc + mu_ref[:, cs] * (prev - pc)
    prev_scr[...] = p[rows - SUBLANES:rows, :]
    tail_ref[0] = p[rows - SUBLANES:rows, :]


def _proj_prompt(x, g, w_in16, mu, splits, b, t, tm):
    n, d = x.shape
    c0, c1, c2 = splits
    nt = t // tm
    rows = lambda c: pl.BlockSpec((tm, c), lambda i, j: (i * nt + j, 0))
    return pl.pallas_call(
        functools.partial(_proj_prompt_kernel, splits=splits),
        grid=(b, nt),
        in_specs=[rows(d), _const((1, d)), _const(w_in16.shape), _const(mu.shape)],
        out_specs=[rows(c0), rows(c1), rows(c2), pl.BlockSpec((1, SUBLANES, c0), lambda i, j: (i, 0, 0))],
        out_shape=[jax.ShapeDtypeStruct((n, c0), F32), jax.ShapeDtypeStruct((n, c1), F32),
                   jax.ShapeDtypeStruct((n, c2), F32), jax.ShapeDtypeStruct((b, SUBLANES, c0), F32)],
        scratch_shapes=[pltpu.VMEM((SUBLANES, c0), F32)],
        compiler_params=_params(("arbitrary", "arbitrary")),
    )(x, g, w_in16, mu)


def _rwkv_pre(m, w0, lora2, a0, g_up, k_k, k_a, seg):
    w = w0.shape[-1]
    r, k, v = m[:, 0:w], m[:, w:2 * w], m[:, 2 * w:3 * w]
    z = m[:, 3 * w:3 * w + LANES]
    lane = lax.broadcasted_iota(jnp.int32, z.shape, 1)
    lor = _mm(jnp.where(lane < HEAD_DIM, jnp.tanh(z), z), lora2)
    lw = -DECAY_SCALE * _sigmoid(w0 + lor[:, 0:w])
    a_sig = _sigmoid(a0 + lor[:, w:2 * w])
    g = _mm(_sigmoid(m[:, 3 * w + LANES:3 * w + 2 * LANES]), g_up)
    kk = k * k_k
    kkn = kk * lax.rsqrt(jnp.maximum(_segsum(kk * kk, seg), 1e-24))
    k2 = k * (1.0 + (a_sig - 1.0) * k_a)
    return r, lw, k2, v, -kkn, kkn * a_sig, g


def _rwkv_post(y, r, k2, v, g, r_k, lnx_w, lnx_b, w_out, seg):
    inv = 1.0 / HEAD_DIM
    mean = _segsum(y, seg) * inv
    d = y - mean
    var = _segsum(d * d, seg) * inv
    yn = d * lax.rsqrt(var + LN_X_EPS) * lnx_w + lnx_b
    bonus = _segsum(r * k2 * r_k, seg) * v
    return _mm((yn + bonus) * g, w_out)


def _wkv_chunk_summaries(xs, m0b, strict, incl, eye2, eyew):
    ln = xs[0][0].shape[0]
    swap = lambda x: pltpu.roll(x, HEAD_DIM, axis=1)
    h16 = lambda x: x.astype(BF16)
    zero16 = jnp.zeros((), BF16)
    bd = lambda x: jnp.concatenate([jnp.where(m0b, x, zero16), jnp.where(m0b, zero16, x)], axis=0)
    cat = jnp.concatenate
    rts, ats, bts, kts, bhs, khs, vs, wls = zip(*xs)
    at16 = [h16(x) for x in ats]
    v16 = [h16(x) for x in vs]
    bdv = [bd(x) for x in v16]
    gms = [_mmd(cat([bd(a), bd(h16(rt))], axis=0), cat([bt, kt], axis=0), _NT)
           for rt, a, bt, kt in zip(rts, at16, bts, kts)]
    ga0 = [jnp.where(strict, g[0:ln], 0.0) for g in gms]
    ga1 = [jnp.where(strict, g[ln:2 * ln], 0.0) for g in gms]
    gr0 = [jnp.where(incl, g[2 * ln:3 * ln], 0.0) for g in gms]
    gr1 = [jnp.where(incl, g[3 * ln:4 * ln], 0.0) for g in gms]
    n_ab = [jnp.where(m0b, x, swap(y)) for x, y in zip(ga0, ga1)]
    n_ak = [jnp.where(m0b, swap(x), y) for x, y in zip(ga0, ga1)]
    n_rb = [jnp.where(m0b, x, swap(y)) for x, y in zip(gr0, gr1)]
    n_rk = [jnp.where(m0b, swap(x), y) for x, y in zip(gr0, gr1)]
    tinv = [eye2 + n for n in n_ab]
    steps = (ln - 1).bit_length() - 1
    if steps > 0:
        npow = [h16(_mmd(n, bd(n))) for n in (h16(x) for x in n_ab)]
        for _ in range(steps - 1):
            both = [_mmd(cat([n, h16(t)], axis=0), bd(n)) for n, t in zip(npow, tinv)]
            tinv = [t + x[ln:2 * ln] for t, x in zip(tinv, both)]
            npow = [h16(x[0:ln]) for x in both]
        tinv = [t + _mmd(t, bd(n)) for t, n in zip(tinv, npow)]
    kv = [_mmd(cat([nk, nr], axis=0), b) for nk, nr, b in zip(n_ak, n_rk, bdv)]
    akv = [x[0:ln] for x in kv]
    rkv = [x[ln:2 * ln] for x in kv]
    tt = [_mmd(t, cat([bd(a), bd(h16(ak))], axis=1)) for t, a, ak in zip(tinv, at16, akv)]
    ta = [h16(x[:, 0:LANES]) for x in tt]
    tk = [h16(x[:, LANES:2 * LANES]) for x in tt]
    mc = [_mmd(cat([cat([a, k], axis=1), cat([jnp.zeros_like(v), v], axis=1)], axis=0), cat([bh, kh], axis=0), _TN)
          for a, k, v, bh, kh in zip(ta, tk, v16, bhs, khs)]
    ms = [eyew * wl + x[0:LANES] for x, wl in zip(mc, wls)]
    cs = [x[LANES:2 * LANES] for x in mc]
    qy = [_mmd(n, cat([bd(a), bd(k)], axis=1)) for n, a, k in zip(n_rb, ta, tk)]
    qs = [rt + x[:, 0:LANES] for rt, x in zip(rts, qy)]
    y0 = [x[:, LANES:2 * LANES] + r for x, r in zip(qy, rkv)]
    return list(zip(qs, y0, ms, cs))


def _rwkv_prompt_kernel(m_ref, w0_ref, lora_ref, a0_ref, gup_ref, kk_ref, ka_ref, rk_ref,
                        lnw_ref, lnb_ref, wout_ref, seg_ref, tri_ref, oa_ref, wkv_ref, s_scr):
    t = pl.program_id(1)

    @pl.when(t == 0)
    def _init():
        s_scr[...] = jnp.zeros_like(s_scr)

    rows = m_ref.shape[0]
    seg = seg_ref[...]
    r, lw, k2, v, aa, bb, g = _rwkv_pre(m_ref[...], w0_ref[...], lora_ref[...], a0_ref[...],
                                        gup_ref[...], kk_ref[...], ka_ref[...], seg)
    c = _cumsum_rows(tri_ref[...], lw)

    ln = CHUNK
    ri = lax.broadcasted_iota(jnp.int32, (ln, LANES), 0)
    ci = lax.broadcasted_iota(jnp.int32, (ln, LANES), 1)
    cj = jnp.where(ci < HEAD_DIM, ci, ci - HEAD_DIM)
    m0b = ci < HEAD_DIM
    strict = cj < ri
    incl = cj <= ri
    eye2 = (cj == ri).astype(F32)
    r2 = lax.broadcasted_iota(jnp.int32, (LANES, LANES), 0)
    c2 = lax.broadcasted_iota(jnp.int32, (LANES, LANES), 1)
    bdmask = ((r2 < HEAD_DIM) == (c2 < HEAD_DIM)).astype(F32)
    eyew = (r2 == c2).astype(F32)

    npairs = s_scr.shape[0]
    insts = []
    for ch in range(rows // ln):
        rs = slice(ch * ln, (ch + 1) * ln)
        c_, lw_ = c[rs], lw[rs]
        cl = c_[ln - 1:ln, :]
        e_n = jnp.exp(-c_)
        e_l = jnp.exp(cl - c_)
        rt = r[rs] * jnp.exp(c_)
        at = aa[rs] * jnp.exp(c_ - lw_)
        bt, kt = bb[rs] * e_n, k2[rs] * e_n
        bh, kh = bb[rs] * e_l, k2[rs] * e_l
        wl = jnp.exp(cl)
        v_ = v[rs]
        for pr in range(npairs):
            sl = slice(pr * LANES, (pr + 1) * LANES)
            insts.append((rt[:, sl], at[:, sl], bt[:, sl], kt[:, sl], bh[:, sl], kh[:, sl], v_[:, sl], wl[:, sl]))
    summaries = _wkv_chunk_summaries(insts, m0b, strict, incl, eye2, eyew)

    states = [s_scr[pr] for pr in range(npairs)]
    y_rows = []
    for ch in range(rows // ln):
        ys = []
        for pr in range(npairs):
            q, y0, m, cc = summaries[ch * npairs + pr]
            ys.append(_mmd(q, states[pr], _NT) + y0)
            states[pr] = (_mmd(states[pr], m) + cc) * bdmask
        y_rows.append(jnp.concatenate(ys, axis=1))
    for pr in range(npairs):
        s_scr[pr] = states[pr]
    y = jnp.concatenate(y_rows, axis=0)
    oa_ref[...] = _rwkv_post(y, r, k2, v, g, rk_ref[...], lnw_ref[...], lnb_ref[...], wout_ref[...], seg)

    @pl.when(t == pl.num_programs(1) - 1)
    def _fin():
        for pr in range(npairs):
            s = s_scr[pr]
            wkv_ref[0, 2 * pr] = s[0:HEAD_DIM, 0:HEAD_DIM]
            wkv_ref[0, 2 * pr + 1] = pltpu.roll(s[HEAD_DIM:LANES, :], HEAD_DIM, axis=1)[:, 0:HEAD_DIM]


def _rwkv_prompt(m, rw, b, t, tc):
    n, cp = m.shape
    w = rw["w0"].shape[-1]
    heads = w // HEAD_DIM
    nt = t // tc
    dm = rw["w_out"].shape[-1]
    tri = jnp.kron(jnp.eye(min(tc, MXU_TILE) // CHUNK, dtype=BF16), rw["tri"])
    names = ("w0", "lora2", "a0", "g_up", "k_k", "k_a", "r_k", "lnx_w", "lnx_b", "w_out", "seg")
    return pl.pallas_call(
        _rwkv_prompt_kernel,
        grid=(b, nt),
        in_specs=[pl.BlockSpec((tc, cp), lambda i, j: (i * nt + j, 0))] + [_const(rw[k].shape) for k in names]
                 + [_const(tri.shape)],
        out_specs=[pl.BlockSpec((tc, dm), lambda i, j: (i * nt + j, 0)),
                   pl.BlockSpec((1, heads, HEAD_DIM, HEAD_DIM), lambda i, j: (i, 0, 0, 0))],
        out_shape=[jax.ShapeDtypeStruct((n, dm), F32),
                   jax.ShapeDtypeStruct((b, heads, HEAD_DIM, HEAD_DIM), F32)],
        scratch_shapes=[pltpu.VMEM((heads // 2, LANES, LANES), F32)],
        compiler_params=_params(("arbitrary", "arbitrary")),
    )(m, *[rw[k] for k in names], tri)


def _rwkv_sample_step_kernel(s_ref, r_ref, w_ref, k_ref, v_ref, a_ref, b_ref, y_ref, so_ref):
    r, w, k, a, b = r_ref[...], w_ref[...], k_ref[...], a_ref[...], b_ref[...]

    def body(i, carry):
        base = pl.multiple_of(i * SUBLANES, SUBLANES)
        vrows = v_ref[pl.ds(base, SUBLANES), :]
        ys = []
        for j in range(SUBLANES):
            s = s_ref[0, base + j]
            sa = jnp.sum(s * a, axis=0, keepdims=True)
            s_new = s * w + sa * b + vrows[j:j + 1, :] * k
            so_ref[0, base + j] = s_new
            ys.append(jnp.sum(s_new * r, axis=0, keepdims=True))
        y_ref[pl.ds(base, SUBLANES), :] = jnp.concatenate(ys, axis=0)
        return carry

    lax.fori_loop(0, s_ref.shape[1] // SUBLANES, body, 0)


def _lru_gates(xc, gw, gb, lam):
    c = xc.shape[-1]
    n = gw.shape[1]
    xc16 = xc.astype(BF16)
    parts = [jnp.dot(xc16[:, i * n:(i + 1) * n], gw[i], preferred_element_type=F32) for i in range(c // n)]
    gx = _sigmoid(jnp.concatenate([p[:, 0:n] for p in parts], axis=1) + gb[:, 0:c])
    ga = _sigmoid(jnp.concatenate([p[:, n:2 * n] for p in parts], axis=1) + gb[:, c:2 * c])
    log_a = -LRU_C * ga * _softplus(-lam)
    a = jnp.exp(log_a)
    u = jnp.sqrt(1.0 - a * a) * (gx * xc)
    return a, u


def _scan_rows(a, u, h0):
    rows, c = a.shape
    groups = rows // SUBLANES
    a3 = a.reshape(groups, SUBLANES, c)
    u3 = u.reshape(groups, SUBLANES, c)
    sub = lax.broadcasted_iota(jnp.int32, a3.shape, 1)
    k = 1
    while k < SUBLANES:
        keep = sub >= k
        u3 = u3 + a3 * jnp.where(keep, pltpu.roll(u3, k, axis=1), 0.0)
        a3 = a3 * jnp.where(keep, pltpu.roll(a3, k, axis=1), 1.0)
        k *= 2
    h = h0
    out = []
    for i in range(groups):
        hi = u3[i] + a3[i] * h
        out.append(hi)
        h = hi[SUBLANES - 1:SUBLANES, :]
    return jnp.concatenate(out, axis=0)


def _lru_prompt(x, tail, h0, cw_ref, cb_ref, gw, gb, lam):
    width = cw_ref.shape[0]
    xc = cb_ref[...] + cw_ref[width - 1:width, :] * x
    for j in range(1, width):
        xc = xc + cw_ref[width - 1 - j:width - j, :] * _shift_rows(x, tail, j)
    a, u = _lru_gates(xc, gw, gb, lam)
    return _scan_rows(a, u, h0)


def _mix_rows(x, oa, hs, gates, lwo, wo, g):
    d = x.shape[-1]
    ob = _mm(hs, lwo)
    mix = _mm(_sigmoid(gates[:, 0:d]) * oa + _sigmoid(gates[:, d:2 * d]) * ob, wo)
    return x + _rms(mix, g)


def _mix_prompt_kernel(x_ref, oa_ref, xb_ref, gt_ref, cw_ref, cb_ref, gw_ref, gb_ref, lam_ref, lwo_ref, wo_ref, g_ref,
                       o_ref, xtail_ref, htail_ref, xtail_scr, h_scr):
    @pl.when(pl.program_id(1) == 0)
    def _init():
        xtail_scr[...] = jnp.zeros_like(xtail_scr)
        h_scr[...] = jnp.zeros_like(h_scr)

    xb = xb_ref[...]
    rows = xb.shape[0]
    hs = _lru_prompt(xb, xtail_scr[...], h_scr[SUBLANES - 1:SUBLANES, :], cw_ref, cb_ref,
                     gw_ref[...], gb_ref[...], lam_ref[...])
    xtail_scr[...] = xb[rows - SUBLANES:rows, :]
    xtail_ref[0] = xb[rows - SUBLANES:rows, :]
    h_scr[...] = hs[rows - SUBLANES:rows, :]
    htail_ref[0] = hs[rows - SUBLANES:rows, :]
    o_ref[...] = _mix_rows(x_ref[...], oa_ref[...], hs, gt_ref[...], lwo_ref[...], wo_ref[...], g_ref[...])


def _mix_prompt(x, oa, xb, gates, lw, lwo16, wo16, g, b, t, tm):
    n, d = x.shape
    c = xb.shape[1]
    nt = t // tm
    names = ("conv_w", "conv_b", "gate_w", "gate_b", "lam")
    consts = [lw[k] for k in names] + [lwo16, wo16, g]
    rows = lambda w: pl.BlockSpec((tm, w), lambda i, j: (i * nt + j, 0))
    tail = pl.BlockSpec((1, SUBLANES, c), lambda i, j: (i, 0, 0))
    return pl.pallas_call(
        _mix_prompt_kernel,
        grid=(b, nt),
        in_specs=[rows(d), rows(d), rows(c), rows(2 * d)] + [_const(v.shape) for v in consts],
        out_specs=[rows(d), tail, tail],
        out_shape=[jax.ShapeDtypeStruct((n, d), F32), jax.ShapeDtypeStruct((b, SUBLANES, c), F32),
                   jax.ShapeDtypeStruct((b, SUBLANES, c), F32)],
        scratch_shapes=[pltpu.VMEM((SUBLANES, c), F32), pltpu.VMEM((SUBLANES, c), F32)],
        compiler_params=_params(("arbitrary", "arbitrary")),
    )(x, oa, xb, gates, *consts)


def _ffn_steps(x1, gpre, gpost, up_ref, cw_ref, cb_ref, down_ref, hist, keep, out):
    dff = down_ref.shape[0]
    hn = _rms(x1, gpre).astype(BF16)
    starts = list(range(0, dff, FF_COLS))

    def up(c0):
        return [(slice(off, off + FF_COLS), jnp.dot(hn, up_ref[:, off:off + FF_COLS], preferred_element_type=F32))
                for off in (c0, dff + c0)]

    ups = [up(c0) for c0 in starts[:FFN_AHEAD]]
    f = jnp.zeros(x1.shape, F32)
    yield
    for i, c0 in enumerate(starts):
        if i + FFN_AHEAD < len(starts):
            ups.append(up(starts[i + FFN_AHEAD]))
        halves = []
        for cols, u in ups[i]:
            u1, u2 = hist(u, cols)
            halves.append(cb_ref[:, cols] + cw_ref[2:3, cols] * u + cw_ref[1:2, cols] * u1 + cw_ref[0:1, cols] * u2)
            keep(u, cols)
        act = _gelu_tanh(halves[0]) * halves[1]
        f = f + jnp.dot(act.astype(BF16), down_ref[c0:c0 + FF_COLS, :], preferred_element_type=F32)
        yield
    out.append(x1 + _rms(f, gpost))


def _ffn_prompt_kernel(x_ref, gpre_ref, gpost_ref, up_ref, cw_ref, cb_ref, down_ref, y_ref, tail_ref, tail_scr):
    @pl.when(pl.program_id(1) == 0)
    def _init():
        tail_scr[...] = jnp.zeros_like(tail_scr)

    rows = x_ref.shape[0] // FFN_PARTS

    def hist(u, cols):
        tail = tail_scr[:, cols]
        return _shift_rows(u, tail, 1), _shift_rows(u, tail, 2)

    def keep(u, cols):
        tail_scr[:, cols] = u[rows - SUBLANES:rows, :]
        tail_ref[0, :, cols] = u[rows - SUBLANES:rows, :]

    outs = [[] for _ in range(FFN_PARTS)]
    gens = [_ffn_steps(x_ref[q * rows:(q + 1) * rows, :], gpre_ref[...], gpost_ref[...], up_ref, cw_ref, cb_ref,
                       down_ref, hist, keep, outs[q]) for q in range(FFN_PARTS)]
    nsteps = down_ref.shape[0] // FF_COLS + 1
    live = []
    for q, gen in enumerate(gens):
        live.append(gen)
        lead = nsteps - FFN_PART_LEAD if q + 1 < FFN_PARTS else nsteps + 1
        for _ in range(lead):
            for g in list(live):
                if next(g, StopIteration) is StopIteration:
                    live.remove(g)
    for q in range(FFN_PARTS):
        y_ref[q * rows:(q + 1) * rows, :] = outs[q][0]


def _ffn_prompt(x1, fw, b, t, tm):
    n, d = x1.shape
    nt = t // tm
    dff2 = fw["up"].shape[1]
    names = ("g_pre", "g_post", "up", "conv_w", "conv_b", "down")
    rows = pl.BlockSpec((tm, d), lambda i, j: (i * nt + j, 0))
    return pl.pallas_call(
        _ffn_prompt_kernel,
        grid=(b, nt),
        in_specs=[rows] + [_const(fw[k].shape) for k in names],
        out_specs=[rows, pl.BlockSpec((1, SUBLANES, dff2), lambda i, j: (i, 0, 0))],
        out_shape=[jax.ShapeDtypeStruct((n, d), F32), jax.ShapeDtypeStruct((b, SUBLANES, dff2), F32)],
        scratch_shapes=[pltpu.VMEM((SUBLANES, dff2), F32)],
        compiler_params=_params(("arbitrary", "arbitrary")),
    )(x1, *[fw[k] for k in names])


def _sample_front_kernel(x_ref, shift_ref, h0_ref, *refs, splits, n_bufs):
    buf_refs, refs = refs[:n_bufs], refs[n_bufs:]
    (gin_ref, w_ref, mu_ref, w0_ref, lora_ref, a0_ref, gup_ref, kk_ref, ka_ref, seg_ref,
     cw_ref, cb_ref, gw_ref, gb_ref, lam_ref,
     p_ref, xb_ref, gt_ref, h_ref, rt_ref, wt_ref, kt_ref, vt_ref, at_ref, bt_ref, r_ref, k_ref, v_ref, g_ref) = refs
    xn = _rms(x_ref[...], gin_ref[...]).astype(BF16)
    c0, c1, c2 = splits
    p = jnp.dot(xn, w_ref[:, 0:c0], preferred_element_type=F32)
    xb = jnp.dot(xn, w_ref[:, c0:c0 + c1], preferred_element_type=F32)
    gt_ref[...] = jnp.dot(xn, w_ref[:, c0 + c1:c0 + c1 + c2], preferred_element_type=F32)
    p_ref[...] = p
    xb_ref[...] = xb
    r, lw, k2, v, aa, bb, g = _rwkv_pre(p + mu_ref[...] * (shift_ref[...] - p), w0_ref[...], lora_ref[...],
                                        a0_ref[...], gup_ref[...], kk_ref[...], ka_ref[...], seg_ref[...])
    rt_ref[...] = r.T
    wt_ref[...] = jnp.exp(lw).T
    kt_ref[...] = k2.T
    vt_ref[...] = v.T
    at_ref[...] = aa.T
    bt_ref[...] = bb.T
    r_ref[...] = r
    k_ref[...] = k2
    v_ref[...] = v
    g_ref[...] = g
    width = cw_ref.shape[0]
    xc = cb_ref[...] + cw_ref[width - 1:width, :] * xb
    for j, buf_ref in enumerate(buf_refs):
        xc = xc + cw_ref[j:j + 1, :] * buf_ref[...]
    a, u = _lru_gates(xc, gw_ref[...], gb_ref[...], lam_ref[...])
    h_ref[...] = a * h0_ref[...] + u


def _sample_back_kernel(yt_ref, r_ref, k_ref, v_ref, g_ref, x_ref, h_ref, gt_ref, buf_ref,
                        rk_ref, lnw_ref, lnb_ref, wout_ref, seg_ref, lwo_ref, wo_ref, gmix_ref,
                        gpre_ref, gpost_ref, up_ref, cw_ref, cb_ref, down_ref, y_ref, nb_ref,
                        x1_scr, hn_scr, act_scr, f_scr):
    j = pl.program_id(0)
    nh = act_scr.shape[0]

    @pl.when(j == 0)
    def _first():
        oa = _rwkv_post(yt_ref[...].T, r_ref[...], k_ref[...], v_ref[...], g_ref[...], rk_ref[...],
                        lnw_ref[...], lnb_ref[...], wout_ref[...], seg_ref[...])
        x1 = _mix_rows(x_ref[...], oa, h_ref[...], gt_ref[...], lwo_ref[...], wo_ref[...], gmix_ref[...])
        x1_scr[...] = x1
        hn_scr[...] = _rms(x1, gpre_ref[...]).astype(BF16)
        f_scr[...] = jnp.zeros_like(f_scr)

    u = jnp.dot(hn_scr[...], up_ref[...], preferred_element_type=F32)
    u1, u2 = buf_ref[:, 1, :], buf_ref[:, 0, :]
    conv = cb_ref[...] + cw_ref[2:3, :] * u + cw_ref[1:2, :] * u1 + cw_ref[0:1, :] * u2
    nb_ref[:, 0, :] = u1
    nb_ref[:, 1, :] = u

    @pl.when(j < nh)
    def _gate():
        act_scr[j] = _gelu_tanh(conv)

    @pl.when(j >= nh)
    def _value():
        act = (act_scr[j - nh] * conv).astype(BF16)
        f = f_scr[...]
        for c in range(0, act.shape[1], FF_COLS):
            f = f + jnp.dot(act[:, c:c + FF_COLS], down_ref[c:c + FF_COLS, :], preferred_element_type=F32)
        f_scr[...] = f

    @pl.when(j == 2 * nh - 1)
    def _last():
        y_ref[...] = x1_scr[...] + _rms(f_scr[...], gpost_ref[...])


def _single_step_call(body, args, out_shapes):
    return pl.pallas_call(
        body,
        grid=(1,),
        in_specs=[_const(v.shape) for v in args],
        out_specs=[_const(o.shape) for o in out_shapes],
        out_shape=out_shapes,
        compiler_params=_params(("arbitrary",)),
    )(*args)


def _sample_layer(x, shift, wkv, lru_buf, h0, ffn_buf, wts):
    n, t, d = x.shape
    x2 = x.reshape(n, d)
    rw, lw, fw = wts["rw"], wts["lw"], wts["fw"]
    splits = _splits(wts)
    c0, c1, c2 = splits
    w = rw["w0"].shape[-1]
    heads = w // HEAD_DIM
    lru_bufs = [lru_buf[:, j, :] for j in range(lru_buf.shape[1])]
    f = lambda *shape: jax.ShapeDtypeStruct(shape, F32)
    front_args = (x2, shift[:, 0, :], h0, *lru_bufs, wts["g_in"], wts["w_in"],
                  *[rw[k] for k in ("mu", "w0", "lora2", "a0", "g_up", "k_k", "k_a", "seg")],
                  *[lw[k] for k in ("conv_w", "conv_b", "gate_w", "gate_b", "lam")])
    p, xb, gates, h, rt, wt, kt, vt, at, bt, r, k2, v, g = _single_step_call(
        functools.partial(_sample_front_kernel, splits=splits, n_bufs=len(lru_bufs)), front_args,
        [f(n, c0), f(n, c1), f(n, c2), f(n, c1)] + [f(w, n)] * 6 + [f(n, w)] * 4)

    wkv_t = jnp.transpose(wkv, (1, 2, 3, 0))
    head_spec = pl.BlockSpec((HEAD_DIM, n), lambda i: (i, 0))
    st_spec = pl.BlockSpec((1, HEAD_DIM, HEAD_DIM, n), lambda i: (i, 0, 0, 0))
    yt, wkv_new = pl.pallas_call(
        _rwkv_sample_step_kernel,
        grid=(heads,),
        in_specs=[st_spec] + [head_spec] * 6,
        out_specs=[head_spec, st_spec],
        out_shape=[f(w, n), f(*wkv_t.shape)],
        compiler_params=_params(("arbitrary",)),
    )(wkv_t, rt, wt, kt, vt, at, bt)

    row_args = (yt, r, k2, v, g, x2, h, gates)
    mix_consts = (*[rw[k] for k in ("r_k", "lnx_w", "lnx_b", "w_out", "seg")],
                  wts["lru_w_out"], wts["w_o"], wts["g_mix"], fw["g_pre"], fw["g_post"])
    dff = fw["down"].shape[0]
    cols = SAMPLE_FF_COLS if dff % SAMPLE_FF_COLS == 0 else dff
    nh = dff // cols
    col_block = lambda rows: pl.BlockSpec((rows, cols), lambda j: (0, j))
    buf_spec = pl.BlockSpec((n, ffn_buf.shape[1], cols), lambda j: (0, 0, j))
    y, ffn_new = pl.pallas_call(
        _sample_back_kernel,
        grid=(2 * nh,),
        in_specs=[_const(a.shape) for a in row_args] + [buf_spec] + [_const(a.shape) for a in mix_consts]
                 + [col_block(d), col_block(fw["conv_w"].shape[0]), col_block(1),
                    pl.BlockSpec((cols, d), lambda j: (jnp.maximum(j - nh, 0), 0))],
        out_specs=[_const((n, d)), buf_spec],
        out_shape=[f(n, d), f(*ffn_buf.shape)],
        scratch_shapes=[pltpu.VMEM((n, d), F32), pltpu.VMEM((n, d), BF16), pltpu.VMEM((nh, n, cols), F32),
                        pltpu.VMEM((n, d), F32)],
        compiler_params=_params(("arbitrary",)),
    )(*row_args, ffn_buf, *mix_consts, fw["up"], fw["conv_w"], fw["conv_b"], fw["down"])
    state = (p.reshape(n, 1, -1), jnp.transpose(wkv_new, (3, 0, 1, 2)), jnp.stack(lru_bufs[1:] + [xb], axis=1), h,
             ffn_new)
    return y.reshape(n, t, d), state


def _block_diag(blocks):
    n, bi, bj = blocks.shape
    eye = jnp.eye(n, dtype=blocks.dtype)
    return jnp.einsum("nij,nm->nimj", blocks, eye).reshape(n * bi, n * bj)


def _gate_blocks(gx, ga):
    n = min(gx.shape[0], MXU_TILE)
    return jnp.stack([jnp.concatenate([gx[i:i + n, i:i + n], ga[i:i + n, i:i + n]], axis=1)
                      for i in range(0, gx.shape[0], n)], axis=0)


def _layer_weights(norm_pre_mix, norm_post_mix, norm_pre_ffn, norm_post_ffn, w_in,
                   rwkv_mu, rwkv_w0, rwkv_w_up, rwkv_a0, rwkv_a_up, rwkv_g_up, rwkv_k_k, rwkv_k_a,
                   rwkv_r_k, rwkv_lnx_w, rwkv_lnx_b, rwkv_w_out,
                   lru_conv_w, lru_conv_b, lru_gx_w, lru_gx_b, lru_ga_w, lru_ga_b, lru_lambda, lru_w_out,
                   w_o, ffn_up, ffn_conv_w, ffn_conv_b, ffn_down):
    row = lambda x: x.reshape(1, -1)
    w = rwkv_w0.shape[-1]
    head_of = jnp.arange(min(w, MXU_TILE)) // HEAD_DIM
    t_idx = jnp.arange(CHUNK)
    lo_w, lo_a = rwkv_w_up.shape[0], rwkv_a_up.shape[0]
    lora2 = jnp.zeros((lo_w + lo_a, 2 * w), F32).at[:lo_w, :w].set(rwkv_w_up).at[lo_w:, w:].set(rwkv_a_up)
    rw = dict(mu=row(rwkv_mu), w0=row(rwkv_w0), lora2=lora2.astype(BF16), a0=row(rwkv_a0),
              g_up=rwkv_g_up.astype(BF16), k_k=row(rwkv_k_k), k_a=row(rwkv_k_a), r_k=row(rwkv_r_k),
              lnx_w=row(rwkv_lnx_w), lnx_b=row(rwkv_lnx_b), w_out=rwkv_w_out.astype(BF16),
              seg=(head_of[:, None] == head_of[None, :]).astype(BF16),
              tri=(t_idx[:, None] >= t_idx[None, :]).astype(BF16))
    lw = dict(conv_w=lru_conv_w, conv_b=row(lru_conv_b),
              gate_w=_gate_blocks(_block_diag(lru_gx_w), _block_diag(lru_ga_w)).astype(BF16),
              gate_b=jnp.concatenate([row(lru_gx_b), row(lru_ga_b)], axis=1), lam=row(lru_lambda))
    fw = dict(g_pre=row(norm_pre_ffn), g_post=row(norm_post_ffn), up=ffn_up.astype(BF16),
              conv_w=ffn_conv_w, conv_b=row(ffn_conv_b), down=ffn_down.astype(BF16))
    return dict(g_in=row(norm_pre_mix), w_in=w_in.astype(BF16), rw=rw, lw=lw, fw=fw,
                lru_w_out=lru_w_out.astype(BF16), w_o=w_o.astype(BF16), g_mix=row(norm_post_mix))


def _splits(wts):
    w = wts["rw"]["w0"].shape[-1]
    c_rwkv = wts["rw"]["mu"].shape[-1]
    c_lru = wts["lw"]["lam"].shape[-1]
    return (c_rwkv, c_lru, wts["w_in"].shape[1] - c_rwkv - c_lru)


def _prompt_layer(x, wts):
    b, t, d = x.shape
    x2 = x.reshape(b * t, d)
    tm = min(t, ROW_TILE)
    m, xb, gates, p_tail = _proj_prompt(x2, wts["g_in"], wts["w_in"], wts["rw"]["mu"], _splits(wts), b, t, tm)
    oa, wkv = _rwkv_prompt(m, wts["rw"], b, t, min(t, WKV_ROWS))
    x1, xb_tail, h_tail = _mix_prompt(x2, oa, xb, gates, wts["lw"], wts["lru_w_out"], wts["w_o"], wts["g_mix"],
                                      b, t, tm)
    y, u_tail = _ffn_prompt(x1, wts["fw"], b, t, min(t, FFN_ROWS))
    conv_w = wts["lw"]["conv_w"].shape[0]
    ffn_w = wts["fw"]["conv_w"].shape[0]
    state = (p_tail[:, SUBLANES - 1:], wkv, xb_tail[:, SUBLANES - (conv_w - 1):], h_tail[:, SUBLANES - 1],
             u_tail[:, SUBLANES - (ffn_w - 1):])
    return y.reshape(b, t, d), state


def kernel(x_prompt, x_sample, state_rwkv_shift, state_rwkv_wkv, state_lru_conv, state_lru_h, state_ffn_conv,
           norm_pre_mix, norm_post_mix, norm_pre_ffn, norm_post_ffn, w_in,
           rwkv_mu, rwkv_w0, rwkv_w_up, rwkv_a0, rwkv_a_up, rwkv_g_up, rwkv_k_k, rwkv_k_a,
           rwkv_r_k, rwkv_lnx_w, rwkv_lnx_b, rwkv_w_out,
           lru_conv_w, lru_conv_b, lru_gx_w, lru_gx_b, lru_ga_w, lru_ga_b, lru_lambda, lru_w_out,
           w_o, ffn_up, ffn_conv_w, ffn_conv_b, ffn_down):
    params = (norm_pre_mix, norm_post_mix, norm_pre_ffn, norm_post_ffn, w_in,
              rwkv_mu, rwkv_w0, rwkv_w_up, rwkv_a0, rwkv_a_up, rwkv_g_up, rwkv_k_k, rwkv_k_a,
              rwkv_r_k, rwkv_lnx_w, rwkv_lnx_b, rwkv_w_out,
              lru_conv_w, lru_conv_b, lru_gx_w, lru_gx_b, lru_ga_w, lru_ga_b, lru_lambda, lru_w_out,
              w_o, ffn_up, ffn_conv_w, ffn_conv_b, ffn_down)
    depth = w_in.shape[0]
    assert depth == 1 and x_sample.shape[1] == 1
    yp, ys = x_prompt, x_sample
    new_p, new_s = [], []
    for l in range(depth):
        wts = _layer_weights(*[q[l] for q in params])
        yp, st_p = _prompt_layer(yp, wts)
        ys, st_s = _sample_layer(ys, state_rwkv_shift[l], state_rwkv_wkv[l], state_lru_conv[l],
                                 state_lru_h[l], state_ffn_conv[l], wts)
        new_p.append(st_p)
        new_s.append(st_s)
    stk = lambda lst, i: jnp.stack([s[i] for s in lst], axis=0)
    return (yp, ys,
            stk(new_p, 0), stk(new_p, 1), stk(new_p, 2), stk(new_p, 3), stk(new_p, 4),
            stk(new_s, 0), stk(new_s, 1), stk(new_s, 2), stk(new_s, 3), stk(new_s, 4))
```

```python
import functools
import math

import jax
import jax.numpy as jnp
from jax import lax
from jax.experimental import pallas as pl
from jax.experimental.pallas import tpu as pltpu

F32 = jnp.float32
BF16 = jnp.bfloat16

NORM_EPS = 1e-6
LN_X_EPS = 64e-5
LRU_C = 8.0
DECAY_SCALE = math.exp(-0.5)
HEAD_DIM = 64
LANES = 128
SUBLANES = 8
MXU_TILE = 256
CHUNK = 64
ROW_TILE = 512
WKV_ROWS = 512
GATE_COLS = 512
SHIFT_COLS = 256
FF_COLS = 256
FFN_ROWS = 512
FFN_PARTS = 2
FFN_PART_LEAD = 4
FFN_AHEAD = 3
VMEM_LIMIT = 56 * 1024 * 1024

_NN = (((1,), (0,)), ((), ()))
_NT = (((1,), (1,)), ((), ()))
_TN = (((0,), (0,)), ((), ()))


def _rms(x, g):
    return x * lax.rsqrt(jnp.mean(x * x, axis=-1, keepdims=True) + NORM_EPS) * g


def _softplus(x):
    return jnp.maximum(x, 0.0) + jnp.log1p(jnp.exp(-jnp.abs(x)))


def _sigmoid(x):
    return 0.5 * jnp.tanh(0.5 * x) + 0.5


def _gelu_tanh(x):
    c = 0.7978845608028654
    hx = 0.5 * x
    return hx + hx * jnp.tanh(x * (c + (c * 0.044715) * (x * x)))


def _mm(a, b):
    return jnp.dot(a.astype(BF16), b.astype(BF16), preferred_element_type=F32)


def _mmd(a, b, dims=_NN):
    return lax.dot_general(a.astype(BF16), b.astype(BF16), dims, preferred_element_type=F32)


def _segsum(x, seg):
    x16 = x.astype(BF16)
    n = seg.shape[0]
    return jnp.concatenate([jnp.dot(x16[:, c:c + n], seg, preferred_element_type=F32)
                            for c in range(0, x.shape[1], n)], axis=1)


def _cumsum_rows(tri, x):
    hi = x.astype(BF16)
    lo = (x - hi.astype(F32)).astype(BF16)
    n = tri.shape[0]
    d = lambda y: jnp.concatenate([jnp.dot(tri, y[r:r + n], preferred_element_type=F32)
                                   for r in range(0, y.shape[0], n)], axis=0)
    return d(hi) + d(lo)


def _shift_rows(x, tail, j):
    xs = pltpu.roll(x, j, axis=0)
    ts = pltpu.roll(tail, j, axis=0)
    row = lax.broadcasted_iota(jnp.int32, ts.shape, 0)
    head = jnp.where(row < j, ts, xs[0:SUBLANES])
    if x.shape[0] == SUBLANES:
        return head
    return jnp.concatenate([head, xs[SUBLANES:]], axis=0)


def _const(shape):
    n = len(shape)
    return pl.BlockSpec(shape, lambda *_: (0,) * n, pipeline_mode=pl.Buffered(1))


def _params(sem):
    return pltpu.CompilerParams(dimension_semantics=sem, vmem_limit_bytes=VMEM_LIMIT)


def _proj_prompt_kernel(x_ref, g_ref, w_ref, mu_ref, m_ref, xb_ref, gt_ref, tail_ref, prev_scr, *, splits):
    @pl.when(pl.program_id(1) == 0)
    def _init():
        prev_scr[...] = jnp.zeros_like(prev_scr)

    xn = _rms(x_ref[...], g_ref[...]).astype(BF16)
    c0, c1, c2 = splits
    g0 = c0 + c1
    p = jnp.dot(xn, w_ref[:, 0:c0], preferred_element_type=F32)
    xb_ref[...] = jnp.dot(xn, w_ref[:, c0:g0], preferred_element_type=F32)
    rows = p.shape[0]
    row = lax.broadcasted_iota(jnp.int32, (rows, SHIFT_COLS), 0)
    shift_chunks = list(range(0, c0, SHIFT_COLS))
    gate_chunks = list(range(0, c2, GATE_COLS))
    per_gate = -(-len(shift_chunks) // len(gate_chunks))
    for i, off in enumerate(gate_chunks):
        gt_ref[:, off:off + GATE_COLS] = jnp.dot(xn, w_ref[:, g0 + off:g0 + off + GATE_COLS],
                                                 preferred_element_type=F32)
        for s0 in shift_chunks[i * per_gate:(i + 1) * per_gate]:
            cs = slice(s0, s0 + SHIFT_COLS)
            pc = p[:, cs]
            prev = jnp.where(row == 0, prev_scr[SUBLANES - 1:SUBLANES, cs], pltpu.roll(pc, 1, axis=0))
            m_ref[:, cs] = pc + mu_ref[:, cs] * (prev - pc)
    prev_scr[...] = p[rows - SUBLANES:rows, :]
    tail_ref[0] = p[rows - SUBLANES:rows, :]


def _proj_prompt(x, g, w_in16, mu, splits, b, t, tm):
    n, d = x.shape
    c0, c1, c2 = splits
    nt = t // tm
    rows = lambda c: pl.BlockSpec((tm, c), lambda i, j: (i * nt + j, 0))
    return pl.pallas_call(
        functools.partial(_proj_prompt_kernel, splits=splits),
        grid=(b, nt),
        in_specs=[rows(d), _const((1, d)), _const(w_in16.shape), _const(mu.shape)],
        out_specs=[rows(c0), rows(c1), rows(c2), pl.BlockSpec((1, SUBLANES, c0), lambda i, j: (i, 0, 0))],
        out_shape=[jax.ShapeDtypeStruct((n, c0), F32), jax.ShapeDtypeStruct((n, c1), F32),
                   jax.ShapeDtypeStruct((n, c2), F32), jax.ShapeDtypeStruct((b, SUBLANES, c0), F32)],
        scratch_shapes=[pltpu.VMEM((SUBLANES, c0), F32)],
        compiler_params=_params(("arbitrary", "arbitrary")),
    )(x, g, w_in16, mu)


def _rwkv_pre(m, w0, lora2, a0, g_up, k_k, k_a, seg):
    w = w0.shape[-1]
    r, k, v = m[:, 0:w], m[:, w:2 * w], m[:, 2 * w:3 * w]
    z = m[:, 3 * w:3 * w + LANES]
    lane = lax.broadcasted_iota(jnp.int32, z.shape, 1)
    lor = _mm(jnp.where(lane < HEAD_DIM, jnp.tanh(z), z), lora2)
    lw = -DECAY_SCALE * _sigmoid(w0 + lor[:, 0:w])
    a_sig = _sigmoid(a0 + lor[:, w:2 * w])
    g = _mm(_sigmoid(m[:, 3 * w + LANES:3 * w + 2 * LANES]), g_up)
    kk = k * k_k
    kkn = kk * lax.rsqrt(jnp.maximum(_segsum(kk * kk, seg), 1e-24))
    k2 = k * (1.0 + (a_sig - 1.0) * k_a)
    return r, lw, k2, v, -kkn, kkn * a_sig, g


def _rwkv_post(y, r, k2, v, g, r_k, lnx_w, lnx_b, w_out, seg):
    inv = 1.0 / HEAD_DIM
    mean = _segsum(y, seg) * inv
    d = y - mean
    var = _segsum(d * d, seg) * inv
    yn = d * lax.rsqrt(var + LN_X_EPS) * lnx_w + lnx_b
    bonus = _segsum(r * k2 * r_k, seg) * v
    return _mm((yn + bonus) * g, w_out)


def _wkv_chunk_summaries(xs, m0b, strict, incl, eye2, eyew):
    ln = xs[0][0].shape[0]
    swap = lambda x: pltpu.roll(x, HEAD_DIM, axis=1)
    h16 = lambda x: x.astype(BF16)
    zero16 = jnp.zeros((), BF16)
    bd = lambda x: jnp.concatenate([jnp.where(m0b, x, zero16), jnp.where(m0b, zero16, x)], axis=0)
    cat = jnp.concatenate
    rts, ats, bts, kts, bhs, khs, vs, wls = zip(*xs)
    at16 = [h16(x) for x in ats]
    v16 = [h16(x) for x in vs]
    bdv = [bd(x) for x in v16]
    gms = [_mmd(cat([bd(a), bd(h16(rt))], axis=0), cat([bt, kt], axis=0), _NT)
           for rt, a, bt, kt in zip(rts, at16, bts, kts)]
    ga0 = [jnp.where(strict, g[0:ln], 0.0) for g in gms]
    ga1 = [jnp.where(strict, g[ln:2 * ln], 0.0) for g in gms]
    gr0 = [jnp.where(incl, g[2 * ln:3 * ln], 0.0) for g in gms]
    gr1 = [jnp.where(incl, g[3 * ln:4 * ln], 0.0) for g in gms]
    n_ab = [jnp.where(m0b, x, swap(y)) for x, y in zip(ga0, ga1)]
    n_ak = [jnp.where(m0b, swap(x), y) for x, y in zip(ga0, ga1)]
    n_rb = [jnp.where(m0b, x, swap(y)) for x, y in zip(gr0, gr1)]
    n_rk = [jnp.where(m0b, swap(x), y) for x, y in zip(gr0, gr1)]
    tinv = [eye2 + n for n in n_ab]
    steps = (ln - 1).bit_length() - 1
    if steps > 0:
        npow = [h16(_mmd(n, bd(n))) for n in (h16(x) for x in n_ab)]
        for _ in range(steps - 1):
            both = [_mmd(cat([n, h16(t)], axis=0), bd(n)) for n, t in zip(npow, tinv)]
            tinv = [t + x[ln:2 * ln] for t, x in zip(tinv, both)]
            npow = [h16(x[0:ln]) for x in both]
        tinv = [t + _mmd(t, bd(n)) for t, n in zip(tinv, npow)]
    kv = [_mmd(cat([nk, nr], axis=0), b) for nk, nr, b in zip(n_ak, n_rk, bdv)]
    akv = [x[0:ln] for x in kv]
    rkv = [x[ln:2 * ln] for x in kv]
    tt = [_mmd(t, cat([bd(a), bd(h16(ak))], axis=1)) for t, a, ak in zip(tinv, at16, akv)]
    ta = [h16(x[:, 0:LANES]) for x in tt]
    tk = [h16(x[:, LANES:2 * LANES]) for x in tt]
    mc = [_mmd(cat([cat([a, k], axis=1), cat([jnp.zeros_like(v), v], axis=1)], axis=0), cat([bh, kh], axis=0), _TN)
          for a, k, v, bh, kh in zip(ta, tk, v16, bhs, khs)]
    ms = [eyew * wl + x[0:LANES] for x, wl in zip(mc, wls)]
    cs = [x[LANES:2 * LANES] for x in mc]
    qy = [_mmd(n, cat([bd(a), bd(k)], axis=1)) for n, a, k in zip(n_rb, ta, tk)]
    qs = [rt + x[:, 0:LANES] for rt, x in zip(rts, qy)]
    y0 = [x[:, LANES:2 * LANES] + r for x, r in zip(qy, rkv)]
    return list(zip(qs, y0, ms, cs))


def _rwkv_prompt_kernel(m_ref, w0_ref, lora_ref, a0_ref, gup_ref, kk_ref, ka_ref, rk_ref,
                        lnw_ref, lnb_ref, wout_ref, seg_ref, tri_ref, oa_ref, wkv_ref, s_scr):
    t = pl.program_id(1)

    @pl.when(t == 0)
    def _init():
        s_scr[...] = jnp.zeros_like(s_scr)

    rows = m_ref.shape[0]
    seg = seg_ref[...]
    r, lw, k2, v, aa, bb, g = _rwkv_pre(m_ref[...], w0_ref[...], lora_ref[...], a0_ref[...],
                                        gup_ref[...], kk_ref[...], ka_ref[...], seg)
    c = _cumsum_rows(tri_ref[...], lw)

    ln = CHUNK
    ri = lax.broadcasted_iota(jnp.int32, (ln, LANES), 0)
    ci = lax.broadcasted_iota(jnp.int32, (ln, LANES), 1)
    cj = jnp.where(ci < HEAD_DIM, ci, ci - HEAD_DIM)
    m0b = ci < HEAD_DIM
    strict = cj < ri
    incl = cj <= ri
    eye2 = (cj == ri).astype(F32)
    r2 = lax.broadcasted_iota(jnp.int32, (LANES, LANES), 0)
    c2 = lax.broadcasted_iota(jnp.int32, (LANES, LANES), 1)
    bdmask = ((r2 < HEAD_DIM) == (c2 < HEAD_DIM)).astype(F32)
    eyew = (r2 == c2).astype(F32)

    npairs = s_scr.shape[0]
    insts = []
    for ch in range(rows // ln):
        rs = slice(ch * ln, (ch + 1) * ln)
        c_, lw_ = c[rs], lw[rs]
        cl = c_[ln - 1:ln, :]
        e_n = jnp.exp(-c_)
        e_l = jnp.exp(cl - c_)
        rt = r[rs] * jnp.exp(c_)
        at = aa[rs] * jnp.exp(c_ - lw_)
        bt, kt = bb[rs] * e_n, k2[rs] * e_n
        bh, kh = bb[rs] * e_l, k2[rs] * e_l
        wl = jnp.exp(cl)
        v_ = v[rs]
        for pr in range(npairs):
            sl = slice(pr * LANES, (pr + 1) * LANES)
            insts.append((rt[:, sl], at[:, sl], bt[:, sl], kt[:, sl], bh[:, sl], kh[:, sl], v_[:, sl], wl[:, sl]))
    summaries = _wkv_chunk_summaries(insts, m0b, strict, incl, eye2, eyew)

    states = [s_scr[pr] for pr in range(npairs)]
    y_rows = []
    for ch in range(rows // ln):
        ys = []
        for pr in range(npairs):
            q, y0, m, cc = summaries[ch * npairs + pr]
            ys.append(_mmd(q, states[pr], _NT) + y0)
            states[pr] = (_mmd(states[pr], m) + cc) * bdmask
        y_rows.append(jnp.concatenate(ys, axis=1))
    for pr in range(npairs):
        s_scr[pr] = states[pr]
    y = jnp.concatenate(y_rows, axis=0)
    oa_ref[...] = _rwkv_post(y, r, k2, v, g, rk_ref[...], lnw_ref[...], lnb_ref[...], wout_ref[...], seg)

    @pl.when(t == pl.num_programs(1) - 1)
    def _fin():
        for pr in range(npairs):
            s = s_scr[pr]
            wkv_ref[0, 2 * pr] = s[0:HEAD_DIM, 0:HEAD_DIM]
            wkv_ref[0, 2 * pr + 1] = pltpu.roll(s[HEAD_DIM:LANES, :], HEAD_DIM, axis=1)[:, 0:HEAD_DIM]


def _rwkv_prompt(m, rw, b, t, tc):
    n, cp = m.shape
    w = rw["w0"].shape[-1]
    heads = w // HEAD_DIM
    nt = t // tc
    dm = rw["w_out"].shape[-1]
    tri = jnp.kron(jnp.eye(min(tc, MXU_TILE) // CHUNK, dtype=BF16), rw["tri"])
    names = ("w0", "lora2", "a0", "g_up", "k_k", "k_a", "r_k", "lnx_w", "lnx_b", "w_out", "seg")
    return pl.pallas_call(
        _rwkv_prompt_kernel,
        grid=(b, nt),
        in_specs=[pl.BlockSpec((tc, cp), lambda i, j: (i * nt + j, 0))] + [_const(rw[k].shape) for k in names]
                 + [_const(tri.shape)],
        out_specs=[pl.BlockSpec((tc, dm), lambda i, j: (i * nt + j, 0)),
                   pl.BlockSpec((1, heads, HEAD_DIM, HEAD_DIM), lambda i, j: (i, 0, 0, 0))],
        out_shape=[jax.ShapeDtypeStruct((n, dm), F32),
                   jax.ShapeDtypeStruct((b, heads, HEAD_DIM, HEAD_DIM), F32)],
        scratch_shapes=[pltpu.VMEM((heads // 2, LANES, LANES), F32)],
        compiler_params=_params(("arbitrary", "arbitrary")),
    )(m, *[rw[k] for k in names], tri)


def _rwkv_sample_step_kernel(s_ref, r_ref, w_ref, k_ref, v_ref, a_ref, b_ref, y_ref, so_ref):
    r, w, k, a, b = r_ref[...], w_ref[...], k_ref[...], a_ref[...], b_ref[...]

    def body(i, carry):
        base = pl.multiple_of(i * SUBLANES, SUBLANES)
        vrows = v_ref[pl.ds(base, SUBLANES), :]
        ys = []
        for j in range(SUBLANES):
            s = s_ref[0, base + j]
            sa = jnp.sum(s * a, axis=0, keepdims=True)
            s_new = s * w + sa * b + vrows[j:j + 1, :] * k
            so_ref[0, base + j] = s_new
            ys.append(jnp.sum(s_new * r, axis=0, keepdims=True))
        y_ref[pl.ds(base, SUBLANES), :] = jnp.concatenate(ys, axis=0)
        return carry

    lax.fori_loop(0, s_ref.shape[1] // SUBLANES, body, 0)


def _lru_gates(xc, gw, gb, lam):
    c = xc.shape[-1]
    n = gw.shape[1]
    xc16 = xc.astype(BF16)
    parts = [jnp.dot(xc16[:, i * n:(i + 1) * n], gw[i], preferred_element_type=F32) for i in range(c // n)]
    gx = _sigmoid(jnp.concatenate([p[:, 0:n] for p in parts], axis=1) + gb[:, 0:c])
    ga = _sigmoid(jnp.concatenate([p[:, n:2 * n] for p in parts], axis=1) + gb[:, c:2 * c])
    log_a = -LRU_C * ga * _softplus(-lam)
    a = jnp.exp(log_a)
    u = jnp.sqrt(1.0 - a * a) * (gx * xc)
    return a, u


def _scan_rows(a, u, h0):
    rows, c = a.shape
    groups = rows // SUBLANES
    a3 = a.reshape(groups, SUBLANES, c)
    u3 = u.reshape(groups, SUBLANES, c)
    sub = lax.broadcasted_iota(jnp.int32, a3.shape, 1)
    k = 1
    while k < SUBLANES:
        keep = sub >= k
        u3 = u3 + a3 * jnp.where(keep, pltpu.roll(u3, k, axis=1), 0.0)
        a3 = a3 * jnp.where(keep, pltpu.roll(a3, k, axis=1), 1.0)
        k *= 2
    h = h0
    out = []
    for i in range(groups):
        hi = u3[i] + a3[i] * h
        out.append(hi)
        h = hi[SUBLANES - 1:SUBLANES, :]
    return jnp.concatenate(out, axis=0)


def _lru_prompt(x, tail, h0, cw_ref, cb_ref, gw, gb, lam):
    width = cw_ref.shape[0]
    xc = cb_ref[...] + cw_ref[width - 1:width, :] * x
    for j in range(1, width):
        xc = xc + cw_ref[width - 1 - j:width - j, :] * _shift_rows(x, tail, j)
    a, u = _lru_gates(xc, gw, gb, lam)
    return _scan_rows(a, u, h0)


def _mix_rows(x, oa, hs, gates, lwo, wo, g):
    d = x.shape[-1]
    ob = _mm(hs, lwo)
    mix = _mm(_sigmoid(gates[:, 0:d]) * oa + _sigmoid(gates[:, d:2 * d]) * ob, wo)
    return x + _rms(mix, g)


def _mix_prompt_kernel(x_ref, oa_ref, xb_ref, gt_ref, cw_ref, cb_ref, gw_ref, gb_ref, lam_ref, lwo_ref, wo_ref, g_ref,
                       o_ref, xtail_ref, htail_ref, xtail_scr, h_scr):
    @pl.when(pl.program_id(1) == 0)
    def _init():
        xtail_scr[...] = jnp.zeros_like(xtail_scr)
        h_scr[...] = jnp.zeros_like(h_scr)

    xb = xb_ref[...]
    rows = xb.shape[0]
    hs = _lru_prompt(xb, xtail_scr[...], h_scr[SUBLANES - 1:SUBLANES, :], cw_ref, cb_ref,
                     gw_ref[...], gb_ref[...], lam_ref[...])
    xtail_scr[...] = xb[rows - SUBLANES:rows, :]
    xtail_ref[0] = xb[rows - SUBLANES:rows, :]
    h_scr[...] = hs[rows - SUBLANES:rows, :]
    htail_ref[0] = hs[rows - SUBLANES:rows, :]
    o_ref[...] = _mix_rows(x_ref[...], oa_ref[...], hs, gt_ref[...], lwo_ref[...], wo_ref[...], g_ref[...])


def _mix_prompt(x, oa, xb, gates, lw, lwo16, wo16, g, b, t, tm):
    n, d = x.shape
    c = xb.shape[1]
    nt = t // tm
    names = ("conv_w", "conv_b", "gate_w", "gate_b", "lam")
    consts = [lw[k] for k in names] + [lwo16, wo16, g]
    rows = lambda w: pl.BlockSpec((tm, w), lambda i, j: (i * nt + j, 0))
    tail = pl.BlockSpec((1, SUBLANES, c), lambda i, j: (i, 0, 0))
    return pl.pallas_call(
        _mix_prompt_kernel,
        grid=(b, nt),
        in_specs=[rows(d), rows(d), rows(c), rows(2 * d)] + [_const(v.shape) for v in consts],
        out_specs=[rows(d), tail, tail],
        out_shape=[jax.ShapeDtypeStruct((n, d), F32), jax.ShapeDtypeStruct((b, SUBLANES, c), F32),
                   jax.ShapeDtypeStruct((b, SUBLANES, c), F32)],
        scratch_shapes=[pltpu.VMEM((SUBLANES, c), F32), pltpu.VMEM((SUBLANES, c), F32)],
        compiler_params=_params(("arbitrary", "arbitrary")),
    )(x, oa, xb, gates, *consts)


def _ffn_steps(x1, gpre, gpost, up_ref, cw_ref, cb_ref, down_ref, hist, keep, out):
    dff = down_ref.shape[0]
    hn = _rms(x1, gpre).astype(BF16)
    starts = list(range(0, dff, FF_COLS))

    def up(c0):
        return [(slice(off, off + FF_COLS), jnp.dot(hn, up_ref[:, off:off + FF_COLS], preferred_element_type=F32))
                for off in (c0, dff + c0)]

    ups = [up(c0) for c0 in starts[:FFN_AHEAD]]
    f = jnp.zeros(x1.shape, F32)
    yield
    for i, c0 in enumerate(starts):
        if i + FFN_AHEAD < len(starts):
            ups.append(up(starts[i + FFN_AHEAD]))
        halves = []
        for cols, u in ups[i]:
            u1, u2 = hist(u, cols)
            halves.append(cb_ref[:, cols] + cw_ref[2:3, cols] * u + cw_ref[1:2, cols] * u1 + cw_ref[0:1, cols] * u2)
            keep(u, cols)
        act = _gelu_tanh(halves[0]) * halves[1]
        f = f + jnp.dot(act.astype(BF16), down_ref[c0:c0 + FF_COLS, :], preferred_element_type=F32)
        yield
    out.append(x1 + _rms(f, gpost))


def _ffn_body(x1, gpre, gpost, up_ref, cw_ref, cb_ref, down_ref, hist, keep):
    out = []
    for _ in _ffn_steps(x1, gpre, gpost, up_ref, cw_ref, cb_ref, down_ref, hist, keep, out):
        pass
    return out[0]


def _ffn_prompt_kernel(x_ref, gpre_ref, gpost_ref, up_ref, cw_ref, cb_ref, down_ref, y_ref, tail_ref, tail_scr):
    @pl.when(pl.program_id(1) == 0)
    def _init():
        tail_scr[...] = jnp.zeros_like(tail_scr)

    rows = x_ref.shape[0] // FFN_PARTS

    def hist(u, cols):
        tail = tail_scr[:, cols]
        return _shift_rows(u, tail, 1), _shift_rows(u, tail, 2)

    def keep(u, cols):
        tail_scr[:, cols] = u[rows - SUBLANES:rows, :]
        tail_ref[0, :, cols] = u[rows - SUBLANES:rows, :]

    outs = [[] for _ in range(FFN_PARTS)]
    gens = [_ffn_steps(x_ref[q * rows:(q + 1) * rows, :], gpre_ref[...], gpost_ref[...], up_ref, cw_ref, cb_ref,
                       down_ref, hist, keep, outs[q]) for q in range(FFN_PARTS)]
    nsteps = down_ref.shape[0] // FF_COLS + 1
    live = []
    for q, gen in enumerate(gens):
        live.append(gen)
        lead = nsteps - FFN_PART_LEAD if q + 1 < FFN_PARTS else nsteps + 1
        for _ in range(lead):
            for g in list(live):
                if next(g, StopIteration) is StopIteration:
                    live.remove(g)
    for q in range(FFN_PARTS):
        y_ref[q * rows:(q + 1) * rows, :] = outs[q][0]


def _ffn_prompt(x1, fw, b, t, tm):
    n, d = x1.shape
    nt = t // tm
    dff2 = fw["up"].shape[1]
    names = ("g_pre", "g_post", "up", "conv_w", "conv_b", "down")
    rows = pl.BlockSpec((tm, d), lambda i, j: (i * nt + j, 0))
    return pl.pallas_call(
        _ffn_prompt_kernel,
        grid=(b, nt),
        in_specs=[rows] + [_const(fw[k].shape) for k in names],
        out_specs=[rows, pl.BlockSpec((1, SUBLANES, dff2), lambda i, j: (i, 0, 0))],
        out_shape=[jax.ShapeDtypeStruct((n, d), F32), jax.ShapeDtypeStruct((b, SUBLANES, dff2), F32)],
        scratch_shapes=[pltpu.VMEM((SUBLANES, dff2), F32)],
        compiler_params=_params(("arbitrary", "arbitrary")),
    )(x1, *[fw[k] for k in names])


def _sample_front_kernel(x_ref, shift_ref, h0_ref, buf_ref,
                         gin_ref, w_ref, mu_ref, w0_ref, lora_ref, a0_ref, gup_ref, kk_ref, ka_ref, seg_ref,
                         cw_ref, cb_ref, gw_ref, gb_ref, lam_ref,
                         p_ref, nbuf_ref, gt_ref, h_ref, rt_ref, wt_ref, kt_ref, vt_ref, at_ref, bt_ref,
                         r_ref, k_ref, v_ref, g_ref, *, splits):
    xn = _rms(x_ref[:, 0, :], gin_ref[...]).astype(BF16)
    c0, c1, c2 = splits
    p = jnp.dot(xn, w_ref[:, 0:c0], preferred_element_type=F32)
    xb = jnp.dot(xn, w_ref[:, c0:c0 + c1], preferred_element_type=F32)
    gt_ref[...] = jnp.dot(xn, w_ref[:, c0 + c1:c0 + c1 + c2], preferred_element_type=F32)
    p_ref[:, 0, :] = p
    r, lw, k2, v, aa, bb, g = _rwkv_pre(p + mu_ref[...] * (shift_ref[:, 0, :] - p), w0_ref[...], lora_ref[...],
                                        a0_ref[...], gup_ref[...], kk_ref[...], ka_ref[...], seg_ref[...])
    rt_ref[...] = r.T
    wt_ref[...] = jnp.exp(lw).T
    kt_ref[...] = k2.T
    vt_ref[...] = v.T
    at_ref[...] = aa.T
    bt_ref[...] = bb.T
    r_ref[...] = r
    k_ref[...] = k2
    v_ref[...] = v
    g_ref[...] = g
    width = cw_ref.shape[0]
    xc = cb_ref[...] + cw_ref[width - 1:width, :] * xb
    for j in range(width - 1):
        xc = xc + cw_ref[j:j + 1, :] * buf_ref[:, j, :]
        nbuf_ref[:, j, :] = buf_ref[:, j + 1, :] if j + 2 < width else xb
    a, u = _lru_gates(xc, gw_ref[...], gb_ref[...], lam_ref[...])
    h_ref[...] = a * h0_ref[...] + u


def _sample_back_kernel(yt_ref, r_ref, k_ref, v_ref, g_ref, x_ref, h_ref, gt_ref, buf_ref,
                        rk_ref, lnw_ref, lnb_ref, wout_ref, seg_ref, lwo_ref, wo_ref, gmix_ref,
                        gpre_ref, gpost_ref, up_ref, cw_ref, cb_ref, down_ref, y_ref, nb_ref):
    oa = _rwkv_post(yt_ref[...].T, r_ref[...], k_ref[...], v_ref[...], g_ref[...], rk_ref[...],
                    lnw_ref[...], lnb_ref[...], wout_ref[...], seg_ref[...])
    x1 = _mix_rows(x_ref[:, 0, :], oa, h_ref[...], gt_ref[...], lwo_ref[...], wo_ref[...], gmix_ref[...])

    def hist(u, cols):
        return buf_ref[:, 1, cols], buf_ref[:, 0, cols]

    def keep(u, cols):
        nb_ref[:, 0, cols] = buf_ref[:, 1, cols]
        nb_ref[:, 1, cols] = u

    y_ref[:, 0, :] = _ffn_body(x1, gpre_ref[...], gpost_ref[...], up_ref, cw_ref, cb_ref, down_ref, hist, keep)


def _single_step_call(body, args, out_shapes):
    return pl.pallas_call(
        body,
        grid=(1,),
        in_specs=[_const(v.shape) for v in args],
        out_specs=[_const(o.shape) for o in out_shapes],
        out_shape=out_shapes,
        compiler_params=_params(("arbitrary",)),
    )(*args)


def _sample_layer(x, shift, wkv, lru_buf, h0, ffn_buf, wts):
    n, t, d = x.shape
    rw, lw, fw = wts["rw"], wts["lw"], wts["fw"]
    splits = _splits(wts)
    c0, c1, c2 = splits
    w = rw["w0"].shape[-1]
    heads = w // HEAD_DIM
    f = lambda *shape: jax.ShapeDtypeStruct(shape, F32)
    front_args = (x, shift, h0, lru_buf, wts["g_in"], wts["w_in"],
                  *[rw[k] for k in ("mu", "w0", "lora2", "a0", "g_up", "k_k", "k_a", "seg")],
                  *[lw[k] for k in ("conv_w", "conv_b", "gate_w", "gate_b", "lam")])
    p, lru_new, gates, h, rt, wt, kt, vt, at, bt, r, k2, v, g = _single_step_call(
        functools.partial(_sample_front_kernel, splits=splits), front_args,
        [f(n, t, c0), f(*lru_buf.shape), f(n, c2), f(n, c1)] + [f(w, n)] * 6 + [f(n, w)] * 4)

    wkv_t = jnp.transpose(wkv, (1, 2, 3, 0))
    head_spec = pl.BlockSpec((HEAD_DIM, n), lambda i: (i, 0))
    st_spec = pl.BlockSpec((1, HEAD_DIM, HEAD_DIM, n), lambda i: (i, 0, 0, 0))
    yt, wkv_new = pl.pallas_call(
        _rwkv_sample_step_kernel,
        grid=(heads,),
        in_specs=[st_spec] + [head_spec] * 6,
        out_specs=[head_spec, st_spec],
        out_shape=[f(w, n), f(*wkv_t.shape)],
        compiler_params=_params(("arbitrary",)),
    )(wkv_t, rt, wt, kt, vt, at, bt)

    back_args = (yt, r, k2, v, g, x, h, gates, ffn_buf,
                 *[rw[k] for k in ("r_k", "lnx_w", "lnx_b", "w_out", "seg")],
                 wts["lru_w_out"], wts["w_o"], wts["g_mix"],
                 *[fw[k] for k in ("g_pre", "g_post", "up", "conv_w", "conv_b", "down")])
    y, ffn_new = _single_step_call(_sample_back_kernel, back_args, [f(n, t, d), f(*ffn_buf.shape)])
    return y, (p, jnp.transpose(wkv_new, (3, 0, 1, 2)), lru_new, h, ffn_new)


def _block_diag(blocks):
    n, bi, bj = blocks.shape
    eye = jnp.eye(n, dtype=blocks.dtype)
    return jnp.einsum("nij,nm->nimj", blocks, eye).reshape(n * bi, n * bj)


def _gate_blocks(gx, ga):
    n = min(gx.shape[0], MXU_TILE)
    return jnp.stack([jnp.concatenate([gx[i:i + n, i:i + n], ga[i:i + n, i:i + n]], axis=1)
                      for i in range(0, gx.shape[0], n)], axis=0)


def _layer_weights(norm_pre_mix, norm_post_mix, norm_pre_ffn, norm_post_ffn, w_in,
                   rwkv_mu, rwkv_w0, rwkv_w_up, rwkv_a0, rwkv_a_up, rwkv_g_up, rwkv_k_k, rwkv_k_a,
                   rwkv_r_k, rwkv_lnx_w, rwkv_lnx_b, rwkv_w_out,
                   lru_conv_w, lru_conv_b, lru_gx_w, lru_gx_b, lru_ga_w, lru_ga_b, lru_lambda, lru_w_out,
                   w_o, ffn_up, ffn_conv_w, ffn_conv_b, ffn_down):
    row = lambda x: x.reshape(1, -1)
    w = rwkv_w0.shape[-1]
    head_of = jnp.arange(min(w, MXU_TILE)) // HEAD_DIM
    t_idx = jnp.arange(CHUNK)
    lo_w, lo_a = rwkv_w_up.shape[0], rwkv_a_up.shape[0]
    lora2 = jnp.zeros((lo_w + lo_a, 2 * w), F32).at[:lo_w, :w].set(rwkv_w_up).at[lo_w:, w:].set(rwkv_a_up)
    rw = dict(mu=row(rwkv_mu), w0=row(rwkv_w0), lora2=lora2.astype(BF16), a0=row(rwkv_a0),
              g_up=rwkv_g_up.astype(BF16), k_k=row(rwkv_k_k), k_a=row(rwkv_k_a), r_k=row(rwkv_r_k),
              lnx_w=row(rwkv_lnx_w), lnx_b=row(rwkv_lnx_b), w_out=rwkv_w_out.astype(BF16),
              seg=(head_of[:, None] == head_of[None, :]).astype(BF16),
              tri=(t_idx[:, None] >= t_idx[None, :]).astype(BF16))
    lw = dict(conv_w=lru_conv_w, conv_b=row(lru_conv_b),
              gate_w=_gate_blocks(_block_diag(lru_gx_w), _block_diag(lru_ga_w)).astype(BF16),
              gate_b=jnp.concatenate([row(lru_gx_b), row(lru_ga_b)], axis=1), lam=row(lru_lambda))
    fw = dict(g_pre=row(norm_pre_ffn), g_post=row(norm_post_ffn), up=ffn_up.astype(BF16),
              conv_w=ffn_conv_w, conv_b=row(ffn_conv_b), down=ffn_down.astype(BF16))
    return dict(g_in=row(norm_pre_mix), w_in=w_in.astype(BF16), rw=rw, lw=lw, fw=fw,
                lru_w_out=lru_w_out.astype(BF16), w_o=w_o.astype(BF16), g_mix=row(norm_post_mix))


def _splits(wts):
    w = wts["rw"]["w0"].shape[-1]
    c_rwkv = wts["rw"]["mu"].shape[-1]
    c_lru = wts["lw"]["lam"].shape[-1]
    return (c_rwkv, c_lru, wts["w_in"].shape[1] - c_rwkv - c_lru)


def _prompt_layer(x, wts):
    b, t, d = x.shape
    x2 = x.reshape(b * t, d)
    tm = min(t, ROW_TILE)
    m, xb, gates, p_tail = _proj_prompt(x2, wts["g_in"], wts["w_in"], wts["rw"]["mu"], _splits(wts), b, t, tm)
    oa, wkv = _rwkv_prompt(m, wts["rw"], b, t, min(t, WKV_ROWS))
    x1, xb_tail, h_tail = _mix_prompt(x2, oa, xb, gates, wts["lw"], wts["lru_w_out"], wts["w_o"], wts["g_mix"],
                                      b, t, tm)
    y, u_tail = _ffn_prompt(x1, wts["fw"], b, t, min(t, FFN_ROWS))
    conv_w = wts["lw"]["conv_w"].shape[0]
    ffn_w = wts["fw"]["conv_w"].shape[0]
    state = (p_tail[:, SUBLANES - 1:], wkv, xb_tail[:, SUBLANES - (conv_w - 1):], h_tail[:, SUBLANES - 1],
             u_tail[:, SUBLANES - (ffn_w - 1):])
    return y.reshape(b, t, d), state


def kernel(x_prompt, x_sample, state_rwkv_shift, state_rwkv_wkv, state_lru_conv, state_lru_h, state_ffn_conv,
           norm_pre_mix, norm_post_mix, norm_pre_ffn, norm_post_ffn, w_in,
           rwkv_mu, rwkv_w0, rwkv_w_up, rwkv_a0, rwkv_a_up, rwkv_g_up, rwkv_k_k, rwkv_k_a,
           rwkv_r_k, rwkv_lnx_w, rwkv_lnx_b, rwkv_w_out,
           lru_conv_w, lru_conv_b, lru_gx_w, lru_gx_b, lru_ga_w, lru_ga_b, lru_lambda, lru_w_out,
           w_o, ffn_up, ffn_conv_w, ffn_conv_b, ffn_down):
    params = (norm_pre_mix, norm_post_mix, norm_pre_ffn, norm_post_ffn, w_in,
              rwkv_mu, rwkv_w0, rwkv_w_up, rwkv_a0, rwkv_a_up, rwkv_g_up, rwkv_k_k, rwkv_k_a,
              rwkv_r_k, rwkv_lnx_w, rwkv_lnx_b, rwkv_w_out,
              lru_conv_w, lru_conv_b, lru_gx_w, lru_gx_b, lru_ga_w, lru_ga_b, lru_lambda, lru_w_out,
              w_o, ffn_up, ffn_conv_w, ffn_conv_b, ffn_down)
    depth = w_in.shape[0]
    assert depth == 1 and x_sample.shape[1] == 1
    yp, ys = x_prompt, x_sample
    new_p, new_s = [], []
    for l in range(depth):
        wts = _layer_weights(*[q[l] for q in params])
        yp, st_p = _prompt_layer(yp, wts)
        ys, st_s = _sample_layer(ys, state_rwkv_shift[l], state_rwkv_wkv[l], state_lru_conv[l],
                                 state_lru_h[l], state_ffn_conv[l], wts)
        new_p.append(st_p)
        new_s.append(st_s)
    stk = lambda lst, i: jnp.stack([s[i] for s in lst], axis=0)
    return (yp, ys,
            stk(new_p, 0), stk(new_p, 1), stk(new_p, 2), stk(new_p, 3), stk(new_p, 4),
            stk(new_s, 0), stk(new_s, 1), stk(new_s, 2), stk(new_s, 3), stk(new_s, 4))
```
